```python
import jax, jax.numpy as jnp
from jax import lax
import numpy as np

D_MODEL = 2048
BATCH = 4
SEQ = 2048
DEPTH = 1
DEC_BATCH = 128
DEC_SEQ = 1
PAST_LEN = 16384
PAGE_SIZE = 128

GLA_HEADS = 4
GLA_DK = D_MODEL // 2
GLA_DV = D_MODEL
GLA_HK = GLA_DK // GLA_HEADS
GLA_HV = GLA_DV // GLA_HEADS
GLA_RANK = 16
GLA_GATE_NORM = 16.0
GLA_CHUNK = 64
CONV_DIM = D_MODEL // 2
CONV_K = 3
N_MEM = 256
MEM_HEADS = 4
MEM_DIM = D_MODEL // 2
MEM_HD = MEM_DIM // MEM_HEADS
D_FF = ((8 * D_MODEL // 3) + 127) // 128 * 128
FFN_CONV_K = 3
DN_ALPHA = (2 * DEPTH) ** 0.25
DN_BETA = (8 * DEPTH) ** -0.25
LN_EPS = 1e-5
RMS_EPS = 1e-6
SPLITS = (GLA_DK, GLA_DK, GLA_DV, GLA_DV, GLA_RANK, CONV_DIM, CONV_DIM, CONV_DIM, MEM_DIM, D_MODEL, D_MODEL, D_MODEL)
IN_COLS = sum(SPLITS)
SPLIT_POINTS = tuple(int(s) for s in np.cumsum(SPLITS)[:-1])

kernel_name = 'hybrid_gla_shortconv_memxattn_deepnorm_step'


def _layer_norm(x, g, b):
    xf = x.astype(jnp.float32)
    mu = jnp.mean(xf, -1, keepdims=True)
    var = jnp.mean(jnp.square(xf - mu), -1, keepdims=True)
    y = (xf - mu) * lax.rsqrt(var + LN_EPS)
    return (y * g.astype(jnp.float32) + b.astype(jnp.float32)).astype(x.dtype)


def _causal_dwconv(u, buf, w, bias=None):
    K = w.shape[0]
    L = u.shape[1]
    z = jnp.concatenate([buf.astype(u.dtype), u], axis=1)
    y = z[:, 0:L] * w[0]
    for j in range(1, K):
        y = y + z[:, j:j + L] * w[j]
    if bias is not None:
        y = y + bias
    return y, z[:, L:]


def _gla_recurrence(q, k, v, log_a, s0):
    B, L, H, _ = q.shape
    c = min(GLA_CHUNK, L)
    n = -(-L // c)
    pad = n * c - L
    f32 = jnp.float32

    def chunks(t):
        t = jnp.pad(t.astype(f32), ((0, 0), (0, pad), (0, 0), (0, 0)))
        return t.reshape(B, n, c, H, t.shape[-1]).transpose(1, 0, 3, 2, 4)

    tri = jnp.tril(jnp.ones((c, c), dtype=bool))[:, :, None]

    def step(S, inp):
        qc, kc, vc, ac = inp
        bc = jnp.cumsum(ac, axis=2)
        diff = bc[:, :, :, None, :] - bc[:, :, None, :, :]
        decay = jnp.exp(jnp.where(tri, diff, -jnp.inf))
        scores = jnp.einsum('bhtk,bhsk,bhtsk->bhts', qc, kc, decay)
        o = (jnp.einsum('bhts,bhsv->bhtv', scores, vc)
             + jnp.einsum('bhtk,bhkv->bhtv', qc * jnp.exp(bc), S))
        b_last = bc[:, :, -1]
        S = (jnp.exp(b_last)[..., None] * S
             + jnp.einsum('bhsk,bhsv->bhkv', kc * jnp.exp(b_last[:, :, None] - bc), vc))
        return S, o

    s_fin, o = lax.scan(step, s0.astype(f32), (chunks(q), chunks(k), chunks(v), chunks(log_a)))
    o = o.transpose(1, 0, 3, 2, 4).reshape(B, n * c, H, -1)[:, :L]
    return o, s_fin


def _token_mixing(x, mem_k, mem_v, s_gla, s_conv, w_in, w_gla_a2, b_gla_a2, g_gla_norm, w_gla_out,
                  w_conv, w_conv_out, w_mem_out, w_o):
    B, L, _ = x.shape
    f32 = jnp.float32
    proj = jnp.einsum('bld,de->ble', x, w_in)
    q, k, v, g, a_lr, cb, cc, ch, mq, z_a, z_b, z_m = jnp.split(proj, SPLIT_POINTS, axis=-1)
    log_a = jax.nn.log_sigmoid((a_lr @ w_gla_a2 + b_gla_a2).astype(f32)) / GLA_GATE_NORM
    qh = q.reshape(B, L, GLA_HEADS, GLA_HK) * (GLA_HK ** -0.5)
    kh = k.reshape(B, L, GLA_HEADS, GLA_HK)
    vh = v.reshape(B, L, GLA_HEADS, GLA_HV)
    o, s_gla_new = _gla_recurrence(qh, kh, vh, log_a.reshape(B, L, GLA_HEADS, GLA_HK), s_gla)
    o = o * lax.rsqrt(jnp.mean(jnp.square(o), -1, keepdims=True) + RMS_EPS) * g_gla_norm.astype(f32)
    o = o.reshape(B, L, GLA_DV).astype(x.dtype) * jax.nn.silu(g)
    y_a = o @ w_gla_out
    u, s_conv_new = _causal_dwconv(cc * ch, s_conv, w_conv)
    y_b = (cb * u) @ w_conv_out
    qm = mq.reshape(B, L, MEM_HEADS, MEM_HD)
    logits = jnp.einsum('blhd,bmhd->bhlm', qm, mem_k).astype(f32) * (MEM_HD ** -0.5)
    p = jax.nn.softmax(logits, axis=-1).astype(x.dtype)
    om = jnp.einsum('bhlm,bmhd->blhd', p, mem_v).reshape(B, L, MEM_DIM).astype(x.dtype)
    y_m = om @ w_mem_out
    merged = jax.nn.sigmoid(z_a) * y_a + jax.nn.sigmoid(z_b) * y_b + jax.nn.sigmoid(z_m) * y_m
    return merged @ w_o, s_gla_new.astype(s_gla.dtype), s_conv_new


def _conv_ffn(x, s_ffn, w_ffn_gate, w_ffn_up, w_ffn_conv, b_ffn_conv, w_ffn_down):
    hg = x @ w_ffn_gate
    hu = x @ w_ffn_up
    hc, s_new = _causal_dwconv(hg, s_ffn, w_ffn_conv, b_ffn_conv)
    return (jax.nn.silu(hc) * hu) @ w_ffn_down, s_new


def _layer(x, mem_k, mem_v, s_gla, s_conv, s_ffn, lw):
    (w_in, w_gla_a2, b_gla_a2, g_gla_norm, w_gla_out, w_conv, w_conv_out, w_mem_out, w_o,
     ln1_g, ln1_b, w_ffn_gate, w_ffn_up, w_ffn_conv, b_ffn_conv, w_ffn_down, ln2_g, ln2_b) = lw
    m, s_gla_new, s_conv_new = _token_mixing(x, mem_k, mem_v, s_gla, s_conv, w_in, w_gla_a2, b_gla_a2,
                                             g_gla_norm, w_gla_out, w_conv, w_conv_out, w_mem_out, w_o)
    x = _layer_norm(DN_ALPHA * x + m, ln1_g, ln1_b)
    f, s_ffn_new = _conv_ffn(x, s_ffn, w_ffn_gate, w_ffn_up, w_ffn_conv, b_ffn_conv, w_ffn_down)
    x = _layer_norm(DN_ALPHA * x + f, ln2_g, ln2_b)
    return x, s_gla_new, s_conv_new, s_ffn_new


def setup_inputs(seed: int = 0) -> dict:
    key = jax.random.key(seed)
    ks = jax.random.split(key, 32)
    f32 = jnp.float32

    def nrm(k, shape, scale):
        return jax.random.normal(k, shape, f32) * scale

    D = D_MODEL
    return {
        'x_prompt': nrm(ks[0], (BATCH, SEQ, D), 1.0),
        'x_sample': nrm(ks[1], (DEC_BATCH, DEC_SEQ, D), 1.0),
        'mem_prompt': nrm(ks[2], (BATCH, N_MEM, D), 1.0),
        'cache_mem_k': nrm(ks[3], (DEPTH, DEC_BATCH, N_MEM, MEM_HEADS, MEM_HD), 1.0),
        'cache_mem_v': nrm(ks[4], (DEPTH, DEC_BATCH, N_MEM, MEM_HEADS, MEM_HD), 1.0),
        'state_gla': nrm(ks[5], (DEPTH, DEC_BATCH, GLA_HEADS, GLA_HK, GLA_HV), 0.5),
        'state_conv': nrm(ks[6], (DEPTH, DEC_BATCH, CONV_K - 1, CONV_DIM), 1.0),
        'state_ffn_conv': nrm(ks[7], (DEPTH, DEC_BATCH, FFN_CONV_K - 1, D_FF), 1.0),
        'w_in': nrm(ks[8], (DEPTH, D, IN_COLS), D ** -0.5),
        'w_gla_a2': nrm(ks[9], (DEPTH, GLA_RANK, GLA_DK), GLA_RANK ** -0.5),
        'b_gla_a2': nrm(ks[10], (DEPTH, GLA_DK), 0.1),
        'g_gla_norm': 1.0 + nrm(ks[11], (DEPTH, GLA_HV), 0.02),
        'w_gla_out': nrm(ks[12], (DEPTH, GLA_DV, D), GLA_DV ** -0.5),
        'w_conv': nrm(ks[13], (DEPTH, CONV_K, CONV_DIM), CONV_K ** -0.5),
        'w_conv_out': nrm(ks[14], (DEPTH, CONV_DIM, D), CONV_DIM ** -0.5),
        'w_mem_k': nrm(ks[15], (DEPTH, D, MEM_DIM), D ** -0.5),
        'w_mem_v': nrm(ks[16], (DEPTH, D, MEM_DIM), D ** -0.5),
        'w_mem_out': nrm(ks[17], (DEPTH, MEM_DIM, D), MEM_DIM ** -0.5),
        'w_o': nrm(ks[18], (DEPTH, D, D), DN_BETA * D ** -0.5),
        'ln1_g': 1.0 + nrm(ks[19], (DEPTH, D), 0.02),
        'ln1_b': nrm(ks[20], (DEPTH, D), 0.02),
        'w_ffn_gate': nrm(ks[21], (DEPTH, D, D_FF), D ** -0.5),
        'w_ffn_up': nrm(ks[22], (DEPTH, D, D_FF), D ** -0.5),
        'w_ffn_conv': nrm(ks[23], (DEPTH, FFN_CONV_K, D_FF), FFN_CONV_K ** -0.5),
        'b_ffn_conv': nrm(ks[24], (DEPTH, D_FF), 0.02),
        'w_ffn_down': nrm(ks[25], (DEPTH, D_FF, D), DN_BETA * D_FF ** -0.5),
        'ln2_g': 1.0 + nrm(ks[26], (DEPTH, D), 0.02),
        'ln2_b': nrm(ks[27], (DEPTH, D), 0.02),
    }


def reference(x_prompt, x_sample, mem_prompt, cache_mem_k, cache_mem_v, state_gla, state_conv, state_ffn_conv,
              w_in, w_gla_a2, b_gla_a2, g_gla_norm, w_gla_out, w_conv, w_conv_out, w_mem_k, w_mem_v, w_mem_out,
              w_o, ln1_g, ln1_b, w_ffn_gate, w_ffn_up, w_ffn_conv, b_ffn_conv, w_ffn_down, ln2_g, ln2_b):
    yp = x_prompt
    ys = x_sample
    bp = x_prompt.shape[0]
    dt = x_prompt.dtype
    p_mk, p_mv, p_gla, p_conv, p_ffn = [], [], [], [], []
    s_gla, s_conv, s_ffn = [], [], []
    for l in range(DEPTH):
        lw = (w_in[l], w_gla_a2[l], b_gla_a2[l], g_gla_norm[l], w_gla_out[l], w_conv[l], w_conv_out[l],
              w_mem_out[l], w_o[l], ln1_g[l], ln1_b[l], w_ffn_gate[l], w_ffn_up[l], w_ffn_conv[l],
              b_ffn_conv[l], w_ffn_down[l], ln2_g[l], ln2_b[l])
        mk = jnp.einsum('bmd,de->bme', mem_prompt, w_mem_k[l]).reshape(bp, N_MEM, MEM_HEADS, MEM_HD)
        mv = jnp.einsum('bmd,de->bme', mem_prompt, w_mem_v[l]).reshape(bp, N_MEM, MEM_HEADS, MEM_HD)
        yp, g_new, c_new, f_new = _layer(
            yp, mk, mv,
            jnp.zeros((bp, GLA_HEADS, GLA_HK, GLA_HV), dt),
            jnp.zeros((bp, CONV_K - 1, CONV_DIM), dt),
            jnp.zeros((bp, FFN_CONV_K - 1, D_FF), dt), lw)
        p_mk.append(mk)
        p_mv.append(mv)
        p_gla.append(g_new)
        p_conv.append(c_new)
        p_ffn.append(f_new)
        ys, g2, c2, f2 = _layer(ys, cache_mem_k[l], cache_mem_v[l], state_gla[l], state_conv[l],
                                state_ffn_conv[l], lw)
        s_gla.append(g2)
        s_conv.append(c2)
        s_ffn.append(f2)
    return (yp, ys, jnp.stack(p_mk), jnp.stack(p_mv), jnp.stack(p_gla), jnp.stack(p_conv), jnp.stack(p_ffn),
            jnp.stack(s_gla), jnp.stack(s_conv), jnp.stack(s_ffn))
```

```python
import functools

import jax
import jax.numpy as jnp
from jax import lax
from jax.experimental import pallas as pl
from jax.experimental.pallas import tpu as pltpu

F32 = jnp.float32
BF16 = jnp.bfloat16

D_MODEL = 2048
GLA_HEADS = 4
GLA_HK = 256
GLA_HV = 512
GLA_RANK = 16
GLA_GATE_NORM = 16.0
CONV_DIM = 1024
N_MEM = 256
MEM_HEADS = 4
MEM_HD = 256
MEM_DIM = 1024
D_FF = 5504
DEPTH = 1
DN_ALPHA = (2 * DEPTH) ** 0.25
LN_EPS = 1e-5
RMS_EPS = 1e-6

LANES = 128
FF_TILE = 512
D_FF_PAD = -(-D_FF // FF_TILE) * FF_TILE
COLS_A = 2 * GLA_HEADS * GLA_HK + 2 * GLA_HEADS * GLA_HV
COLS_B = 3 * CONV_DIM + MEM_DIM + 3 * D_MODEL
GLA_CHUNK = 256
VMEM_LIMIT = 56 * 1024 * 1024

NN = (((1,), (0,)), ((), ()))
NT = (((1,), (1,)), ((), ()))
TN = (((0,), (0,)), ((), ()))


def _dot(a, b, dims=NN):
    return lax.dot_general(a, b, dims, preferred_element_type=F32)


def _params(*sem):
    return pltpu.CompilerParams(dimension_semantics=sem, vmem_limit_bytes=VMEM_LIMIT)


def _split2(x):
    hi = x.astype(BF16)
    lo = (x - hi.astype(F32)).astype(BF16)
    return hi, lo


def _split3(x):
    hi = x.astype(BF16)
    r = x - hi.astype(F32)
    mid = r.astype(BF16)
    lo = (r - mid.astype(F32)).astype(BF16)
    return hi, mid, lo


def _sigmoid(z):
    return 1.0 / (1.0 + jnp.exp(-z))


def _layer_norm(y, g, b):
    mu = jnp.mean(y, axis=-1, keepdims=True)
    d = y - mu
    var = jnp.mean(d * d, axis=-1, keepdims=True)
    return d * lax.rsqrt(var + LN_EPS) * g + b


def _log_decay(alr, wa2, ba2):
    ah, al = _split2(alr)
    wh, wl = _split2(wa2)
    z = _dot(ah, wh) + _dot(ah, wl) + _dot(al, wh) + ba2
    return (jnp.minimum(z, 0.0) - jnp.log1p(jnp.exp(-jnp.abs(z)))) * (1.0 / GLA_GATE_NORM)


def _rms_gate(o, gn, g):
    o = o * lax.rsqrt(jnp.mean(o * o, axis=-1, keepdims=True) + RMS_EPS) * gn
    return o * (g * _sigmoid(g))


def _mm_kernel(x_ref, w_ref, o_ref):
    o_ref[...] = _dot(x_ref[...], w_ref[...]).astype(o_ref.dtype)


def _matmul(x, w, out_dtype, name, tn=1024):
    m, k = x.shape
    n = w.shape[1]
    tm = min(m, 1024)
    tn = min(n, tn)
    return pl.pallas_call(
        _mm_kernel,
        grid=(m // tm, n // tn),
        in_specs=[pl.BlockSpec((tm, k), lambda i, j: (i, 0)),
                  pl.BlockSpec((k, tn), lambda i, j: (0, j))],
        out_specs=pl.BlockSpec((tm, tn), lambda i, j: (i, j)),
        out_shape=jax.ShapeDtypeStruct((m, n), out_dtype),
        compiler_params=_params("parallel", "parallel"),
        name=name,
    )(x, w)


def _intra_scores(q, k, la, bc):
    c, dk = q.shape
    ri = lax.broadcasted_iota(jnp.int32, (c, 1), 0)
    rr = lax.broadcasted_iota(jnp.int32, (c, c), 0)
    cc = lax.broadcasted_iota(jnp.int32, (c, c), 1)
    diag = jnp.sum(q * k, axis=1, keepdims=True)
    a = jnp.where(rr == cc, diag, 0.0)
    b = 1
    while b < c:
        upper = (ri // b) % 2 == 1
        if b == 1:
            e = jnp.where(upper, la, 0.0)
        elif b == 2:
            m4 = ri % 4
            la_prev = pltpu.roll(la, 1, 0)
            la_next = pltpu.roll(la, c - 1, 0)
            e = jnp.where(m4 == 2, la, jnp.where(m4 == 3, la + la_prev, jnp.where(m4 == 0, la_next, 0.0)))
        else:
            ref = bc.reshape(c // (2 * b), 2 * b, dk)[:, b - 1:b, :]
            ref = jnp.broadcast_to(ref, (c // (2 * b), 2 * b, dk)).reshape(c, dk)
            e = jnp.where(upper, bc - ref, ref - bc)
        f = jnp.exp(e)
        qt = jnp.where(upper, q * f, 0.0).astype(BF16)
        kt = jnp.where(upper, 0.0, k * f).astype(BF16)
        blk = _dot(qt, kt, NT)
        a = a + jnp.where(rr // (2 * b) == cc // (2 * b), blk, 0.0)
        b *= 2
    return a


def _gla_kernel(q_ref, k_ref, v_ref, g_ref, alr_ref, wa2_ref, ba2_ref, gn_ref, o_ref, sfin_ref, s_scr):
    ci = pl.program_id(2)

    @pl.when(ci == 0)
    def _():
        s_scr[...] = jnp.zeros_like(s_scr)

    c = q_ref.shape[0]
    la = _log_decay(alr_ref[...], wa2_ref[...], ba2_ref[...])
    lh, lm, ll = _split3(la)
    rr = lax.broadcasted_iota(jnp.int32, (c, c), 0)
    cc = lax.broadcasted_iota(jnp.int32, (c, c), 1)
    tri = (rr >= cc).astype(BF16)
    bc = _dot(tri, lh) + _dot(tri, lm) + _dot(tri, ll)
    b_last = bc[c - 1:c, :]
    ones = jnp.ones((c, LANES), BF16)
    bl_col = _dot(lh, ones, TN) + _dot(lm, ones, TN) + _dot(ll, ones, TN)
    dec_col = jnp.exp(bl_col)
    dec_col = jnp.concatenate([dec_col] * (GLA_HV // LANES), axis=1)

    q = q_ref[...] * (GLA_HK ** -0.5)
    k = k_ref[...]
    vb = v_ref[...].astype(BF16)
    s = s_scr[...]
    o = _dot((q * jnp.exp(bc)).astype(BF16), s.astype(BF16))
    a = _intra_scores(q, k, la, bc)
    o = o + _dot(a.astype(BF16), vb)
    kd = (k * jnp.exp(b_last - bc)).astype(BF16)
    s_new = dec_col * s + _dot(kd, vb, TN)
    s_scr[...] = s_new
    o_ref[...] = _rms_gate(o, gn_ref[...], g_ref[...]).astype(o_ref.dtype)

    @pl.when(ci == pl.num_programs(2) - 1)
    def _():
        sfin_ref[...] = s_new


def _gla_prompt(proj_a, alr, wa2p, ba2, gn, nb, seq):
    c = min(GLA_CHUNK, seq)
    nc = seq // c
    h = GLA_HEADS
    row = lambda b, hh, ci: b * nc + ci
    return pl.pallas_call(
        _gla_kernel,
        grid=(nb, h, nc),
        in_specs=[
            pl.BlockSpec((c, GLA_HK), lambda b, hh, ci: (row(b, hh, ci), hh)),
            pl.BlockSpec((c, GLA_HK), lambda b, hh, ci: (row(b, hh, ci), h + hh)),
            pl.BlockSpec((c, GLA_HV), lambda b, hh, ci: (row(b, hh, ci), h + hh)),
            pl.BlockSpec((c, GLA_HV), lambda b, hh, ci: (row(b, hh, ci), 2 * h + hh)),
            pl.BlockSpec((c, LANES), lambda b, hh, ci: (row(b, hh, ci), 0)),
            pl.BlockSpec((LANES, GLA_HK), lambda b, hh, ci: (0, hh)),
            pl.BlockSpec((1, GLA_HK), lambda b, hh, ci: (0, hh)),
            pl.BlockSpec((1, GLA_HV), lambda b, hh, ci: (0, 0)),
        ],
        out_specs=[
            pl.BlockSpec((c, GLA_HV), lambda b, hh, ci: (row(b, hh, ci), hh)),
            pl.BlockSpec((None, None, GLA_HK, GLA_HV), lambda b, hh, ci: (b, hh, 0, 0)),
        ],
        out_shape=[
            jax.ShapeDtypeStruct((nb * seq, h * GLA_HV), BF16),
            jax.ShapeDtypeStruct((nb, h, GLA_HK, GLA_HV), F32),
        ],
        scratch_shapes=[pltpu.VMEM((GLA_HK, GLA_HV), F32)],
        compiler_params=_params("parallel", "parallel", "arbitrary"),
        name="gla_prompt",
    )(proj_a, proj_a, proj_a, proj_a, alr, wa2p, ba2, gn)


GLA_DEC_ROWS = 8


def _gla_dec_kernel(q_ref, k_ref, v_ref, g_ref, alr_ref, wa2_ref, ba2_ref, gn_ref, s_ref, o_ref, so_ref):
    nb = q_ref.shape[0]
    la = _log_decay(alr_ref[...], wa2_ref[...], ba2_ref[...])
    rep = LANES // nb
    a_t = jnp.concatenate([jnp.exp(la)] * rep, axis=0).T
    k_t = jnp.concatenate([k_ref[...]] * rep, axis=0).T
    qb = (q_ref[...] * (GLA_HK ** -0.5)).astype(BF16)
    v = v_ref[...]
    rows = []
    for j in range(nb):
        s_new = a_t[:, j:j + 1] * s_ref[j] + k_t[:, j:j + 1] * v[j:j + 1, :]
        so_ref[j] = s_new
        rows.append(_dot(qb, s_new.astype(BF16))[j:j + 1, :])
    o = jnp.concatenate(rows, axis=0)
    o_ref[...] = _rms_gate(o, gn_ref[...], g_ref[...]).astype(o_ref.dtype)


def _gla_decode(proj_a, alr, wa2p, ba2, gn, state):
    nseq = proj_a.shape[0]
    h = GLA_HEADS
    r = GLA_DEC_ROWS
    return pl.pallas_call(
        _gla_dec_kernel,
        grid=(nseq // r, h),
        in_specs=[
            pl.BlockSpec((r, GLA_HK), lambda b, hh: (b, hh)),
            pl.BlockSpec((r, GLA_HK), lambda b, hh: (b, h + hh)),
            pl.BlockSpec((r, GLA_HV), lambda b, hh: (b, h + hh)),
            pl.BlockSpec((r, GLA_HV), lambda b, hh: (b, 2 * h + hh)),
            pl.BlockSpec((r, LANES), lambda b, hh: (b, 0)),
            pl.BlockSpec((LANES, GLA_HK), lambda b, hh: (0, hh)),
            pl.BlockSpec((1, GLA_HK), lambda b, hh: (0, hh)),
            pl.BlockSpec((1, GLA_HV), lambda b, hh: (0, 0)),
            pl.BlockSpec((r, None, GLA_HK, GLA_HV), lambda b, hh: (b, hh, 0, 0)),
        ],
        out_specs=[
            pl.BlockSpec((r, GLA_HV), lambda b, hh: (b, hh)),
            pl.BlockSpec((r, None, GLA_HK, GLA_HV), lambda b, hh: (b, hh, 0, 0)),
        ],
        out_shape=[
            jax.ShapeDtypeStruct((nseq, h * GLA_HV), F32),
            jax.ShapeDtypeStruct(state.shape, F32),
        ],
        compiler_params=_params("parallel", "parallel"),
        name="gla_decode",
    )(proj_a, proj_a, proj_a, proj_a, alr, wa2p, ba2, gn, state)


def _shift_rows(x, halo, ri):
    p1 = jnp.where(ri == 0, halo[7:8, :], pltpu.roll(x, 1, 0))
    p2 = jnp.where(ri == 0, halo[6:7, :], jnp.where(ri == 1, halo[7:8, :], pltpu.roll(x, 2, 0)))
    return p1, p2


def _conv_kernel(cb_ref, cc_ref, ch_ref, cch_ref, chh_ref, w_ref, o_ref, st_ref):
    tl = cb_ref.shape[0]
    cch = cc_ref[...] * ch_ref[...]
    halo = cch_ref[...] * chh_ref[...]
    halo = jnp.where(pl.program_id(1) == 0, 0.0, halo)
    ri = lax.broadcasted_iota(jnp.int32, (tl, 1), 0)
    p1, p2 = _shift_rows(cch, halo, ri)
    w = w_ref[...]
    u = w[0:1, :] * p2 + w[1:2, :] * p1 + w[2:3, :] * cch
    o_ref[...] = (cb_ref[...] * u).astype(o_ref.dtype)
    st_ref[...] = cch[tl - 2:tl, :]


def _conv_prompt(proj_b, w_conv, nb, seq):
    tl = min(512, seq)
    nt = seq // tl
    t8 = tl // 8
    cd = CONV_DIM
    halo = lambda col: (lambda b, t: (jnp.maximum((b * nt + t) * t8 - 1, 0), col))
    return pl.pallas_call(
        _conv_kernel,
        grid=(nb, nt),
        in_specs=[
            pl.BlockSpec((tl, cd), lambda b, t: (b * nt + t, 0)),
            pl.BlockSpec((tl, cd), lambda b, t: (b * nt + t, 1)),
            pl.BlockSpec((tl, cd), lambda b, t: (b * nt + t, 2)),
            pl.BlockSpec((8, cd), halo(1)),
            pl.BlockSpec((8, cd), halo(2)),
            pl.BlockSpec((3, cd), lambda b, t: (0, 0)),
        ],
        out_specs=[
            pl.BlockSpec((tl, cd), lambda b, t: (b * nt + t, 0)),
            pl.BlockSpec((None, 2, cd), lambda b, t: (b, 0, 0)),
        ],
        out_shape=[
            jax.ShapeDtypeStruct((nb * seq, cd), BF16),
            jax.ShapeDtypeStruct((nb, 2, cd), F32),
        ],
        compiler_params=_params("parallel", "arbitrary"),
        name="conv_prompt",
    )(proj_b, proj_b, proj_b, proj_b, proj_b, w_conv)


def _conv_dec_kernel(cb_ref, cc_ref, ch_ref, s0_ref, s1_ref, w_ref, o_ref, st_ref):
    cd = cb_ref.shape[1]
    cch = cc_ref[...] * ch_ref[...]
    s1 = s1_ref[...]
    w = w_ref[...]
    u = w[0:1, :] * s0_ref[...] + w[1:2, :] * s1 + w[2:3, :] * cch
    o_ref[...] = (cb_ref[...] * u).astype(o_ref.dtype)
    st_ref[:, :cd] = s1
    st_ref[:, cd:] = cch


def _conv_decode(proj_b, w_conv, state2d):
    n = proj_b.shape[0]
    cd = CONV_DIM
    blk = lambda col: pl.BlockSpec((n, cd), lambda i: (0, col))
    return pl.pallas_call(
        _conv_dec_kernel,
        grid=(1,),
        in_specs=[blk(0), blk(1), blk(2), blk(0), blk(1), pl.BlockSpec((3, cd), lambda i: (0, 0))],
        out_specs=[pl.BlockSpec((n, cd), lambda i: (0, 0)), pl.BlockSpec((n, 2 * cd), lambda i: (0, 0))],
        out_shape=[jax.ShapeDtypeStruct((n, cd), BF16), jax.ShapeDtypeStruct((n, 2 * cd), F32)],
        compiler_params=_params("arbitrary"),
        name="conv_decode",
    )(proj_b, proj_b, proj_b, state2d, state2d, w_conv)


def _softmax_rows(logits):
    m = jnp.max(logits, axis=-1, keepdims=True)
    p = jnp.exp(logits - m)
    return p / jnp.sum(p, axis=-1, keepdims=True)


def _memattn_kernel(q_ref, k_ref, v_ref, o_ref):
    logits = _dot(q_ref[...].astype(BF16), k_ref[...].astype(BF16), NT) * (MEM_HD ** -0.5)
    p = _softmax_rows(logits)
    o_ref[...] = _dot(p.astype(BF16), v_ref[...].astype(BF16)).astype(o_ref.dtype)


def _memattn_prompt(proj_b, mk, mv, nb, seq):
    tl = min(512, seq)
    nt = seq // tl
    h = MEM_HEADS
    qcol = 3 * CONV_DIM // MEM_HD
    return pl.pallas_call(
        _memattn_kernel,
        grid=(nb, h, nt),
        in_specs=[
            pl.BlockSpec((tl, MEM_HD), lambda b, hh, t: (b * nt + t, qcol + hh)),
            pl.BlockSpec((N_MEM, MEM_HD), lambda b, hh, t: (b, hh)),
            pl.BlockSpec((N_MEM, MEM_HD), lambda b, hh, t: (b, hh)),
        ],
        out_specs=pl.BlockSpec((tl, MEM_HD), lambda b, hh, t: (b * nt + t, hh)),
        out_shape=jax.ShapeDtypeStruct((nb * seq, MEM_DIM), BF16),
        compiler_params=_params("parallel", "parallel", "parallel"),
        name="memattn_prompt",
    )(proj_b, mk, mv)


MEMATTN_DEC_ROWS = 4


def _memattn_dec_kernel(q_ref, k_ref, v_ref, o_ref):
    for i in range(q_ref.shape[0]):
        qb = q_ref[i].astype(BF16)
        rows = []
        for hh in range(MEM_HEADS):
            sl = slice(hh * MEM_HD, (hh + 1) * MEM_HD)
            logits = _dot(qb, k_ref[i, :, sl].astype(BF16), NT) * (MEM_HD ** -0.5)
            p = _softmax_rows(logits)
            rows.append(_dot(p.astype(BF16), v_ref[i, :, sl].astype(BF16))[hh:hh + 1, :])
        o_ref[i] = jnp.concatenate(rows, axis=0)


def _memattn_decode(q3, k3, v3):
    n = q3.shape[0]
    r = MEMATTN_DEC_ROWS
    return pl.pallas_call(
        _memattn_dec_kernel,
        grid=(n // r,),
        in_specs=[
            pl.BlockSpec((r, MEM_HEADS, MEM_HD), lambda i: (i, 0, 0)),
            pl.BlockSpec((r, N_MEM, MEM_DIM), lambda i: (i, 0, 0)),
            pl.BlockSpec((r, N_MEM, MEM_DIM), lambda i: (i, 0, 0)),
        ],
        out_specs=pl.BlockSpec((r, MEM_HEADS, MEM_HD), lambda i: (i, 0, 0)),
        out_shape=jax.ShapeDtypeStruct((n, MEM_HEADS, MEM_HD), F32),
        compiler_params=_params("parallel"),
        name="memattn_decode",
    )(q3, k3, v3)


def _merge_kernel(a_ref, b_ref, m_ref, wa_ref, wb_ref, wm_ref, za_ref, zb_ref, zm_ref, o_ref):
    ya = _dot(a_ref[...].astype(BF16), wa_ref[...])
    yb = _dot(b_ref[...].astype(BF16), wb_ref[...])
    ym = _dot(m_ref[...].astype(BF16), wm_ref[...])
    merged = _sigmoid(za_ref[...]) * ya + _sigmoid(zb_ref[...]) * yb + _sigmoid(zm_ref[...]) * ym
    o_ref[...] = merged.astype(o_ref.dtype)


def _merge(o_gla, cbu, om, w_gla_out, w_conv_out, w_mem_out, proj_b):
    m = o_gla.shape[0]
    tm = min(512, m)
    tn = 512
    d = D_MODEL
    zoff = (3 * CONV_DIM + MEM_DIM) // tn
    zspec = lambda g: pl.BlockSpec((tm, tn), lambda i, j: (i, zoff + g * (d // tn) + j))
    return pl.pallas_call(
        _merge_kernel,
        grid=(m // tm, d // tn),
        in_specs=[
            pl.BlockSpec((tm, o_gla.shape[1]), lambda i, j: (i, 0)),
            pl.BlockSpec((tm, cbu.shape[1]), lambda i, j: (i, 0)),
            pl.BlockSpec((tm, om.shape[1]), lambda i, j: (i, 0)),
            pl.BlockSpec((w_gla_out.shape[0], tn), lambda i, j: (0, j)),
            pl.BlockSpec((w_conv_out.shape[0], tn), lambda i, j: (0, j)),
            pl.BlockSpec((w_mem_out.shape[0], tn), lambda i, j: (0, j)),
            zspec(0), zspec(1), zspec(2),
        ],
        out_specs=pl.BlockSpec((tm, tn), lambda i, j: (i, j)),
        out_shape=jax.ShapeDtypeStruct((m, d), BF16),
        compiler_params=_params("parallel", "parallel"),
        name="merge",
    )(o_gla, cbu, om, w_gla_out, w_conv_out, w_mem_out, proj_b, proj_b, proj_b)


def _wo_ln_kernel(x_ref, m_ref, w_ref, g_ref, b_ref, o_ref):
    y = DN_ALPHA * x_ref[...] + _dot(m_ref[...], w_ref[...])
    o_ref[...] = _layer_norm(y, g_ref[...], b_ref[...])


def _wo_ln(x, merged, w_o, g, b):
    m, d = x.shape
    tm = min(512, m)
    return pl.pallas_call(
        _wo_ln_kernel,
        grid=(m // tm,),
        in_specs=[
            pl.BlockSpec((tm, d), lambda i: (i, 0)),
            pl.BlockSpec((tm, d), lambda i: (i, 0)),
            pl.BlockSpec((d, d), lambda i: (0, 0)),
            pl.BlockSpec((1, d), lambda i: (0, 0)),
            pl.BlockSpec((1, d), lambda i: (0, 0)),
        ],
        out_specs=pl.BlockSpec((tm, d), lambda i: (i, 0)),
        out_shape=jax.ShapeDtypeStruct((m, d), F32),
        compiler_params=_params("parallel"),
        name="wo_ln",
    )(x, merged, w_o, g, b)


def _ffn_tail(hc, hu, wd_ref, acc_ref, x_ref, g_ref, b_ref, o_ref, fi):
    h = (hc * _sigmoid(hc) * hu).astype(BF16)
    acc_ref[...] += _dot(h, wd_ref[...])

    @pl.when(fi == pl.num_programs(1) - 1)
    def _():
        o_ref[...] = _layer_norm(DN_ALPHA * x_ref[...] + acc_ref[...], g_ref[...], b_ref[...])


def _ffn_kernel(seq, x_ref, xh_ref, wg_ref, wu_ref, wd_ref, wc_ref, bc_ref, g_ref, b_ref, o_ref,
                xb_scr, acc_scr):
    fi = pl.program_id(1)
    tm = x_ref.shape[0]

    @pl.when(fi == 0)
    def _():
        xb_scr[...] = x_ref[...].astype(BF16)
        acc_scr[...] = jnp.zeros_like(acc_scr)

    xb = xb_scr[...]
    wg = wg_ref[...]
    hg = _dot(xb, wg)
    halo = _dot(xh_ref[...].astype(BF16), wg)
    halo = jnp.where((pl.program_id(0) * tm) % seq == 0, 0.0, halo)
    ri = lax.broadcasted_iota(jnp.int32, (tm, 1), 0)
    p1, p2 = _shift_rows(hg, halo, ri)
    wc = wc_ref[...]
    hc = wc[0:1, :] * p2 + wc[1:2, :] * p1 + wc[2:3, :] * hg + bc_ref[...]
    _ffn_tail(hc, _dot(xb, wu_ref[...]), wd_ref, acc_scr, x_ref, g_ref, b_ref, o_ref, fi)


def _ffn_prompt(x1, w_gate, w_up, w_down, w_conv, b_conv, g, b, seq):
    m, d = x1.shape
    tm = min(512, seq)
    tf = FF_TILE
    t8 = tm // 8
    return pl.pallas_call(
        functools.partial(_ffn_kernel, seq),
        grid=(m // tm, D_FF_PAD // tf),
        in_specs=[
            pl.BlockSpec((tm, d), lambda i, f: (i, 0)),
            pl.BlockSpec((8, d), lambda i, f: (jnp.maximum(i * t8 - 1, 0), 0)),
            pl.BlockSpec((d, tf), lambda i, f: (0, f)),
            pl.BlockSpec((d, tf), lambda i, f: (0, f)),
            pl.BlockSpec((tf, d), lambda i, f: (f, 0)),
            pl.BlockSpec((3, tf), lambda i, f: (0, f)),
            pl.BlockSpec((1, tf), lambda i, f: (0, f)),
            pl.BlockSpec((1, d), lambda i, f: (0, 0)),
            pl.BlockSpec((1, d), lambda i, f: (0, 0)),
        ],
        out_specs=pl.BlockSpec((tm, d), lambda i, f: (i, 0)),
        out_shape=jax.ShapeDtypeStruct((m, d), F32),
        scratch_shapes=[pltpu.VMEM((tm, d), BF16), pltpu.VMEM((tm, d), F32)],
        compiler_params=_params("parallel", "arbitrary"),
        name="ffn_prompt",
    )(x1, x1, w_gate, w_up, w_down, w_conv, b_conv, g, b)


def _ffn_dec_kernel(x_ref, s0_ref, s1_ref, wg_ref, wu_ref, wd_ref, wc_ref, bc_ref, g_ref, b_ref,
                    o_ref, hg_ref, xb_scr, acc_scr):
    fi = pl.program_id(1)

    @pl.when(fi == 0)
    def _():
        xb_scr[...] = x_ref[...].astype(BF16)
        acc_scr[...] = jnp.zeros_like(acc_scr)

    xb = xb_scr[...]
    hg = _dot(xb, wg_ref[...])
    hg_ref[...] = hg
    wc = wc_ref[...]
    hc = wc[0:1, :] * s0_ref[...] + wc[1:2, :] * s1_ref[...] + wc[2:3, :] * hg + bc_ref[...]
    _ffn_tail(hc, _dot(xb, wu_ref[...]), wd_ref, acc_scr, x_ref, g_ref, b_ref, o_ref, fi)


def _ffn_decode(x1, s0, s1, w_gate, w_up, w_down, w_conv, b_conv, g, b):
    m, d = x1.shape
    tf = FF_TILE
    return pl.pallas_call(
        _ffn_dec_kernel,
        grid=(1, D_FF_PAD // tf),
        in_specs=[
            pl.BlockSpec((m, d), lambda i, f: (0, 0)),
            pl.BlockSpec((m, tf), lambda i, f: (0, f)),
            pl.BlockSpec((m, tf), lambda i, f: (0, f)),
            pl.BlockSpec((d, tf), lambda i, f: (0, f)),
            pl.BlockSpec((d, tf), lambda i, f: (0, f)),
            pl.BlockSpec((tf, d), lambda i, f: (f, 0)),
            pl.BlockSpec((3, tf), lambda i, f: (0, f)),
            pl.BlockSpec((1, tf), lambda i, f: (0, f)),
            pl.BlockSpec((1, d), lambda i, f: (0, 0)),
            pl.BlockSpec((1, d), lambda i, f: (0, 0)),
        ],
        out_specs=[pl.BlockSpec((m, d), lambda i, f: (0, 0)), pl.BlockSpec((m, tf), lambda i, f: (0, f))],
        out_shape=[jax.ShapeDtypeStruct((m, d), F32), jax.ShapeDtypeStruct((m, D_FF_PAD), F32)],
        scratch_shapes=[pltpu.VMEM((m, d), BF16), pltpu.VMEM((m, d), F32)],
        compiler_params=_params("parallel", "arbitrary"),
        name="ffn_decode",
    )(x1, s0, s1, w_gate, w_up, w_down, w_conv, b_conv, g, b)


def _pad_cols(x, n):
    return jnp.pad(x, ((0, 0), (0, n - x.shape[1])))


def kernel(x_prompt, x_sample, mem_prompt, cache_mem_k, cache_mem_v, state_gla, state_conv, state_ffn_conv, w_in, w_gla_a2, b_gla_a2, g_gla_norm, w_gla_out, w_conv, w_conv_out, w_mem_k, w_mem_v, w_mem_out, w_o, ln1_g, ln1_b, w_ffn_gate, w_ffn_up, w_ffn_conv, b_ffn_conv, w_ffn_down, ln2_g, ln2_b):
    nb, seq, d = x_prompt.shape
    ns = x_sample.shape[0]
    ff_pad = D_FF_PAD - D_FF

    w_a = w_in[0][:, :COLS_A].astype(BF16)
    w_r = _pad_cols(w_in[0][:, COLS_A:COLS_A + GLA_RANK], LANES).astype(BF16)
    w_b = w_in[0][:, COLS_A + GLA_RANK:].astype(BF16)
    wa2p = jnp.pad(w_gla_a2[0], ((0, LANES - GLA_RANK), (0, 0)))
    ba2 = b_gla_a2
    gn = g_gla_norm
    w_go = w_gla_out[0].astype(BF16)
    w_co = w_conv_out[0].astype(BF16)
    w_mo = w_mem_out[0].astype(BF16)
    w_oo = w_o[0].astype(BF16)
    w_fg = _pad_cols(w_ffn_gate[0], D_FF_PAD).astype(BF16)
    w_fu = _pad_cols(w_ffn_up[0], D_FF_PAD).astype(BF16)
    w_fd = jnp.pad(w_ffn_down[0], ((0, ff_pad), (0, 0))).astype(BF16)
    w_fc = _pad_cols(w_ffn_conv[0], D_FF_PAD)
    b_fc = _pad_cols(b_ffn_conv, D_FF_PAD)
    w_cv = w_conv[0]

    def mix_in(x2d, tag):
        xb = x2d.astype(BF16)
        return (_matmul(xb, w_a, F32, "proj_a_" + tag), _matmul(xb, w_r, F32, "proj_r_" + tag),
                _matmul(xb, w_b, F32, "proj_b_" + tag))

    def mix_out(x2d, o_gla, cbu, om, proj_b):
        merged = _merge(o_gla, cbu, om, w_go, w_co, w_mo, proj_b)
        return _wo_ln(x2d, merged, w_oo, ln1_g, ln1_b)

    xp = x_prompt.reshape(nb * seq, d)
    memb = mem_prompt.reshape(nb * N_MEM, d).astype(BF16)
    mk = _matmul(memb, w_mem_k[0].astype(BF16), F32, "mem_k")
    mv = _matmul(memb, w_mem_v[0].astype(BF16), F32, "mem_v")
    pa, pr, pb = mix_in(xp, "p")
    o_gla, p_gla = _gla_prompt(pa, pr, wa2p, ba2, gn, nb, seq)
    cbu, p_conv = _conv_prompt(pb, w_cv, nb, seq)
    om = _memattn_prompt(pb, mk, mv, nb, seq)
    x1 = mix_out(xp, o_gla, cbu, om, pb)
    yp = _ffn_prompt(x1, w_fg, w_fu, w_fd, w_fc, b_fc, ln2_g, ln2_b, seq)
    x1_tail = x1.reshape(nb, seq, d)[:, seq - 2:, :].reshape(nb * 2, d)
    x1_tail = jnp.pad(x1_tail, ((0, 16 - nb * 2), (0, 0))).astype(BF16)
    p_ffn = _matmul(x1_tail, w_fg, F32, "ffn_state_p", tn=FF_TILE)[:nb * 2, :D_FF]

    xs = x_sample.reshape(ns, d)
    sa, sr, sb = mix_in(xs, "s")
    o_gla_s, s_gla = _gla_decode(sa, sr, wa2p, ba2, gn, state_gla[0])
    cbu_s, s_conv = _conv_decode(sb, w_cv, state_conv[0].reshape(ns, 2 * CONV_DIM))
    mq_s = sb[:, 3 * CONV_DIM:3 * CONV_DIM + MEM_DIM].reshape(ns, MEM_HEADS, MEM_HD)
    om_s = _memattn_decode(mq_s, cache_mem_k[0].reshape(ns, N_MEM, MEM_DIM),
                           cache_mem_v[0].reshape(ns, N_MEM, MEM_DIM)).reshape(ns, MEM_DIM)
    x1s = mix_out(xs, o_gla_s, cbu_s, om_s, sb)
    f0 = _pad_cols(state_ffn_conv[0][:, 0, :], D_FF_PAD)
    f1 = _pad_cols(state_ffn_conv[0][:, 1, :], D_FF_PAD)
    ys, hg_s = _ffn_decode(x1s, f0, f1, w_fg, w_fu, w_fd, w_fc, b_fc, ln2_g, ln2_b)
    s_ffn = jnp.stack([state_ffn_conv[0][:, 1, :], hg_s[:, :D_FF]], axis=1)

    return (yp.reshape(nb, seq, d), ys.reshape(ns, 1, d),
            mk.reshape(1, nb, N_MEM, MEM_HEADS, MEM_HD), mv.reshape(1, nb, N_MEM, MEM_HEADS, MEM_HD),
            p_gla[None], p_conv[None], p_ffn.reshape(1, nb, 2, D_FF),
            s_gla[None], s_conv.reshape(1, ns, 2, CONV_DIM), s_ffn[None])
```

```python
import functools

import jax
import jax.numpy as jnp
from jax import lax
from jax.experimental import pallas as pl
from jax.experimental.pallas import tpu as pltpu

F32 = jnp.float32
BF16 = jnp.bfloat16

D_MODEL = 2048
GLA_HEADS = 4
GLA_HK = 256
GLA_HV = 512
GLA_RANK = 16
GLA_GATE_NORM = 16.0
CONV_DIM = 1024
N_MEM = 256
MEM_HEADS = 4
MEM_HD = 256
MEM_DIM = 1024
D_FF = 5504
DEPTH = 1
DN_ALPHA = (2 * DEPTH) ** 0.25
LN_EPS = 1e-5
RMS_EPS = 1e-6

LANES = 128
FF_TILE = 512
D_FF_PAD = -(-D_FF // FF_TILE) * FF_TILE
COLS_A = 2 * GLA_HEADS * GLA_HK + 2 * GLA_HEADS * GLA_HV
COLS_B = 3 * CONV_DIM + MEM_DIM + 3 * D_MODEL
GLA_CHUNK = 256
VMEM_LIMIT = 56 * 1024 * 1024

NN = (((1,), (0,)), ((), ()))
NT = (((1,), (1,)), ((), ()))
TN = (((0,), (0,)), ((), ()))


def _dot(a, b, dims=NN):
    return lax.dot_general(a, b, dims, preferred_element_type=F32)


def _params(*sem):
    return pltpu.CompilerParams(dimension_semantics=sem, vmem_limit_bytes=VMEM_LIMIT)


def _split2(x):
    hi = x.astype(BF16)
    lo = (x - hi.astype(F32)).astype(BF16)
    return hi, lo


def _split3(x):
    hi = x.astype(BF16)
    r = x - hi.astype(F32)
    mid = r.astype(BF16)
    lo = (r - mid.astype(F32)).astype(BF16)
    return hi, mid, lo


def _sigmoid(z):
    return 1.0 / (1.0 + jnp.exp(-z))


def _layer_norm(y, g, b):
    mu = jnp.mean(y, axis=-1, keepdims=True)
    d = y - mu
    var = jnp.mean(d * d, axis=-1, keepdims=True)
    return d * lax.rsqrt(var + LN_EPS) * g + b


def _log_decay(alr, wa2, ba2):
    ah, al = _split2(alr)
    wh, wl = _split2(wa2)
    z = _dot(ah, wh) + _dot(ah, wl) + _dot(al, wh) + ba2
    return (jnp.minimum(z, 0.0) - jnp.log1p(jnp.exp(-jnp.abs(z)))) * (1.0 / GLA_GATE_NORM)


def _rms_gate(o, gn, g):
    o = o * lax.rsqrt(jnp.mean(o * o, axis=-1, keepdims=True) + RMS_EPS) * gn
    return o * (g * _sigmoid(g))


def _mm_kernel(x_ref, w_ref, o_ref):
    o_ref[...] = _dot(x_ref[...].astype(BF16), w_ref[...]).astype(o_ref.dtype)


def _matmul(x, w, out_dtype, name, tn=1024):
    m, k = x.shape
    n = w.shape[1]
    tm = min(m, 1024)
    tn = min(n, tn)
    return pl.pallas_call(
        _mm_kernel,
        grid=(m // tm, n // tn),
        in_specs=[pl.BlockSpec((tm, k), lambda i, j: (i, 0)),
                  pl.BlockSpec((k, tn), lambda i, j: (0, j))],
        out_specs=pl.BlockSpec((tm, tn), lambda i, j: (i, j)),
        out_shape=jax.ShapeDtypeStruct((m, n), out_dtype),
        compiler_params=_params("parallel", "parallel"),
        name=name,
    )(x, w)


def _proj_kernel(shift, x_ref, w_ref, *rest):
    if shift:
        wx_ref, o_ref, wb_scr = rest
    else:
        o_ref, wb_scr = rest

    @pl.when(pl.program_id(1) == 0)
    def _():
        w = w_ref[...]
        if shift:
            tn = w.shape[1]
            wide = jnp.concatenate([w, wx_ref[...]], axis=1)
            w = pltpu.roll(wide, wide.shape[1] - shift, 1)[:, :tn]
        wb_scr[...] = w.astype(BF16)

    o_ref[...] = _dot(x_ref[...].astype(BF16), wb_scr[...]).astype(o_ref.dtype)


def _proj(x, w, name, col0=0, n=None, tn=1024, shift=0):
    m, k = x.shape
    n = w.shape[1] if n is None else n
    tm = min(m, 1024)
    tn = min(n, tn)
    c0 = col0 // tn
    in_specs = [pl.BlockSpec((tm, k), lambda j, i: (i, 0)),
                pl.BlockSpec((k, tn), lambda j, i: (0, c0 + j))]
    args = [x, w]
    if shift:
        in_specs.append(pl.BlockSpec((k, LANES), lambda j, i: (0, col0 // LANES + (j + 1) * (tn // LANES))))
        args.append(w)
    return pl.pallas_call(
        functools.partial(_proj_kernel, shift),
        grid=(n // tn, m // tm),
        in_specs=in_specs,
        out_specs=pl.BlockSpec((tm, tn), lambda j, i: (i, j)),
        out_shape=jax.ShapeDtypeStruct((m, n), F32),
        scratch_shapes=[pltpu.VMEM((k, tn), BF16)],
        compiler_params=_params("parallel", "arbitrary"),
        name=name,
    )(*args)


def _intra_scores(q, k, la, bc):
    c, dk = q.shape
    ri = lax.broadcasted_iota(jnp.int32, (c, 1), 0)
    rr = lax.broadcasted_iota(jnp.int32, (c, c), 0)
    cc = lax.broadcasted_iota(jnp.int32, (c, c), 1)
    diag = jnp.sum(q * k, axis=1, keepdims=True)
    a = jnp.where(rr == cc, diag, 0.0)
    b = 1
    while b < c:
        upper = (ri // b) % 2 == 1
        if b == 1:
            e = jnp.where(upper, la, 0.0)
        elif b == 2:
            m4 = ri % 4
            la_prev = pltpu.roll(la, 1, 0)
            la_next = pltpu.roll(la, c - 1, 0)
            e = jnp.where(m4 == 2, la, jnp.where(m4 == 3, la + la_prev, jnp.where(m4 == 0, la_next, 0.0)))
        else:
            ref = bc.reshape(c // (2 * b), 2 * b, dk)[:, b - 1:b, :]
            ref = jnp.broadcast_to(ref, (c // (2 * b), 2 * b, dk)).reshape(c, dk)
            e = jnp.where(upper, bc - ref, ref - bc)
        f = jnp.exp(e)
        qt = jnp.where(upper, q * f, 0.0).astype(BF16)
        kt = jnp.where(upper, 0.0, k * f).astype(BF16)
        blk = _dot(qt, kt, NT)
        a = a + jnp.where(rr // (2 * b) == cc // (2 * b), blk, 0.0)
        b *= 2
    return a


def _gla_kernel(q_ref, k_ref, v_ref, g_ref, alr_ref, wa2_ref, ba2_ref, gn_ref, o_ref, sfin_ref, s_scr):
    ci = pl.program_id(2)

    @pl.when(ci == 0)
    def _():
        s_scr[...] = jnp.zeros_like(s_scr)

    c = q_ref.shape[0]
    la = _log_decay(alr_ref[...], wa2_ref[...], ba2_ref[...])
    lh, lm, ll = _split3(la)
    rr = lax.broadcasted_iota(jnp.int32, (c, c), 0)
    cc = lax.broadcasted_iota(jnp.int32, (c, c), 1)
    tri = (rr >= cc).astype(BF16)
    bc = _dot(tri, lh) + _dot(tri, lm) + _dot(tri, ll)
    b_last = bc[c - 1:c, :]
    ones = jnp.ones((c, LANES), BF16)
    bl_col = _dot(lh, ones, TN) + _dot(lm, ones, TN) + _dot(ll, ones, TN)
    dec_col = jnp.exp(bl_col)
    dec_col = jnp.concatenate([dec_col] * (GLA_HV // LANES), axis=1)

    q = q_ref[...] * (GLA_HK ** -0.5)
    k = k_ref[...]
    vb = v_ref[...].astype(BF16)
    s = s_scr[...]
    o = _dot((q * jnp.exp(bc)).astype(BF16), s.astype(BF16))
    a = _intra_scores(q, k, la, bc)
    o = o + _dot(a.astype(BF16), vb)
    kd = (k * jnp.exp(b_last - bc)).astype(BF16)
    s_new = dec_col * s + _dot(kd, vb, TN)
    s_scr[...] = s_new
    o_ref[...] = _rms_gate(o, gn_ref[...], g_ref[...]).astype(o_ref.dtype)

    @pl.when(ci == pl.num_programs(2) - 1)
    def _():
        sfin_ref[...] = s_new


def _gla_prompt(proj_a, alr, wa2p, ba2, gn, nb, seq):
    c = min(GLA_CHUNK, seq)
    nc = seq // c
    h = GLA_HEADS
    row = lambda b, hh, ci: b * nc + ci
    return pl.pallas_call(
        _gla_kernel,
        grid=(nb, h, nc),
        in_specs=[
            pl.BlockSpec((c, GLA_HK), lambda b, hh, ci: (row(b, hh, ci), hh)),
            pl.BlockSpec((c, GLA_HK), lambda b, hh, ci: (row(b, hh, ci), h + hh)),
            pl.BlockSpec((c, GLA_HV), lambda b, hh, ci: (row(b, hh, ci), h + hh)),
            pl.BlockSpec((c, GLA_HV), lambda b, hh, ci: (row(b, hh, ci), 2 * h + hh)),
            pl.BlockSpec((c, LANES), lambda b, hh, ci: (row(b, hh, ci), 0)),
            pl.BlockSpec((LANES, GLA_HK), lambda b, hh, ci: (0, hh)),
            pl.BlockSpec((1, GLA_HK), lambda b, hh, ci: (0, hh)),
            pl.BlockSpec((1, GLA_HV), lambda b, hh, ci: (0, 0)),
        ],
        out_specs=[
            pl.BlockSpec((c, GLA_HV), lambda b, hh, ci: (row(b, hh, ci), hh)),
            pl.BlockSpec((None, None, GLA_HK, GLA_HV), lambda b, hh, ci: (b, hh, 0, 0)),
        ],
        out_shape=[
            jax.ShapeDtypeStruct((nb * seq, h * GLA_HV), BF16),
            jax.ShapeDtypeStruct((nb, h, GLA_HK, GLA_HV), F32),
        ],
        scratch_shapes=[pltpu.VMEM((GLA_HK, GLA_HV), F32)],
        compiler_params=_params("parallel", "parallel", "arbitrary"),
        name="gla_prompt",
    )(proj_a, proj_a, proj_a, proj_a, alr, wa2p, ba2, gn)


GLA_DEC_ROWS = 8


def _gla_dec_kernel(q_ref, k_ref, v_ref, g_ref, alr_ref, wa2_ref, ba2_ref, gn_ref, s_ref, o_ref, so_ref):
    nb = q_ref.shape[0]
    la = _log_decay(alr_ref[...], wa2_ref[...], ba2_ref[...])
    rep = LANES // nb
    a_t = jnp.concatenate([jnp.exp(la)] * rep, axis=0).T
    k_t = jnp.concatenate([k_ref[...]] * rep, axis=0).T
    qb = (q_ref[...] * (GLA_HK ** -0.5)).astype(BF16)
    v = v_ref[...]
    rows = []
    for j in range(nb):
        s_new = a_t[:, j:j + 1] * s_ref[j] + k_t[:, j:j + 1] * v[j:j + 1, :]
        so_ref[j] = s_new
        rows.append(_dot(qb, s_new.astype(BF16))[j:j + 1, :])
    o = jnp.concatenate(rows, axis=0)
    o_ref[...] = _rms_gate(o, gn_ref[...], g_ref[...]).astype(o_ref.dtype)


def _gla_decode(proj_a, alr, wa2p, ba2, gn, state):
    nseq = proj_a.shape[0]
    h = GLA_HEADS
    r = GLA_DEC_ROWS
    return pl.pallas_call(
        _gla_dec_kernel,
        grid=(nseq // r, h),
        in_specs=[
            pl.BlockSpec((r, GLA_HK), lambda b, hh: (b, hh)),
            pl.BlockSpec((r, GLA_HK), lambda b, hh: (b, h + hh)),
            pl.BlockSpec((r, GLA_HV), lambda b, hh: (b, h + hh)),
            pl.BlockSpec((r, GLA_HV), lambda b, hh: (b, 2 * h + hh)),
            pl.BlockSpec((r, LANES), lambda b, hh: (b, 0)),
            pl.BlockSpec((LANES, GLA_HK), lambda b, hh: (0, hh)),
            pl.BlockSpec((1, GLA_HK), lambda b, hh: (0, hh)),
            pl.BlockSpec((1, GLA_HV), lambda b, hh: (0, 0)),
            pl.BlockSpec((r, None, GLA_HK, GLA_HV), lambda b, hh: (b, hh, 0, 0)),
        ],
        out_specs=[
            pl.BlockSpec((r, GLA_HV), lambda b, hh: (b, hh)),
            pl.BlockSpec((r, None, GLA_HK, GLA_HV), lambda b, hh: (b, hh, 0, 0)),
        ],
        out_shape=[
            jax.ShapeDtypeStruct((nseq, h * GLA_HV), F32),
            jax.ShapeDtypeStruct(state.shape, F32),
        ],
        compiler_params=_params("parallel", "parallel"),
        name="gla_decode",
    )(proj_a, proj_a, proj_a, proj_a, alr, wa2p, ba2, gn, state)


def _shift_rows(x, halo, ri):
    p1 = jnp.where(ri == 0, halo[7:8, :], pltpu.roll(x, 1, 0))
    p2 = jnp.where(ri == 0, halo[6:7, :], jnp.where(ri == 1, halo[7:8, :], pltpu.roll(x, 2, 0)))
    return p1, p2


def _conv_kernel(cb_ref, cc_ref, ch_ref, cch_ref, chh_ref, w_ref, o_ref, st_ref):
    tl = cb_ref.shape[0]
    cch = cc_ref[...] * ch_ref[...]
    halo = cch_ref[...] * chh_ref[...]
    halo = jnp.where(pl.program_id(1) == 0, 0.0, halo)
    ri = lax.broadcasted_iota(jnp.int32, (tl, 1), 0)
    p1, p2 = _shift_rows(cch, halo, ri)
    w = w_ref[...]
    u = w[0:1, :] * p2 + w[1:2, :] * p1 + w[2:3, :] * cch
    o_ref[...] = (cb_ref[...] * u).astype(o_ref.dtype)
    st_ref[...] = cch[tl - 2:tl, :]


def _conv_prompt(proj_b, w_conv, nb, seq):
    tl = min(512, seq)
    nt = seq // tl
    t8 = tl // 8
    cd = CONV_DIM
    halo = lambda col: (lambda b, t: (jnp.maximum((b * nt + t) * t8 - 1, 0), col))
    return pl.pallas_call(
        _conv_kernel,
        grid=(nb, nt),
        in_specs=[
            pl.BlockSpec((tl, cd), lambda b, t: (b * nt + t, 0)),
            pl.BlockSpec((tl, cd), lambda b, t: (b * nt + t, 1)),
            pl.BlockSpec((tl, cd), lambda b, t: (b * nt + t, 2)),
            pl.BlockSpec((8, cd), halo(1)),
            pl.BlockSpec((8, cd), halo(2)),
            pl.BlockSpec((3, cd), lambda b, t: (0, 0)),
        ],
        out_specs=[
            pl.BlockSpec((tl, cd), lambda b, t: (b * nt + t, 0)),
            pl.BlockSpec((None, 2, cd), lambda b, t: (b, 0, 0)),
        ],
        out_shape=[
            jax.ShapeDtypeStruct((nb * seq, cd), BF16),
            jax.ShapeDtypeStruct((nb, 2, cd), F32),
        ],
        compiler_params=_params("parallel", "arbitrary"),
        name="conv_prompt",
    )(proj_b, proj_b, proj_b, proj_b, proj_b, w_conv)


def _conv_dec_kernel(cb_ref, cc_ref, ch_ref, s0_ref, s1_ref, w_ref, o_ref, st_ref):
    cd = cb_ref.shape[1]
    cch = cc_ref[...] * ch_ref[...]
    s1 = s1_ref[...]
    w = w_ref[...]
    u = w[0:1, :] * s0_ref[...] + w[1:2, :] * s1 + w[2:3, :] * cch
    o_ref[...] = (cb_ref[...] * u).astype(o_ref.dtype)
    st_ref[:, :cd] = s1
    st_ref[:, cd:] = cch


def _conv_decode(proj_b, w_conv, state2d):
    n = proj_b.shape[0]
    cd = CONV_DIM
    blk = lambda col: pl.BlockSpec((n, cd), lambda i: (0, col))
    return pl.pallas_call(
        _conv_dec_kernel,
        grid=(1,),
        in_specs=[blk(0), blk(1), blk(2), blk(0), blk(1), pl.BlockSpec((3, cd), lambda i: (0, 0))],
        out_specs=[pl.BlockSpec((n, cd), lambda i: (0, 0)), pl.BlockSpec((n, 2 * cd), lambda i: (0, 0))],
        out_shape=[jax.ShapeDtypeStruct((n, cd), BF16), jax.ShapeDtypeStruct((n, 2 * cd), F32)],
        compiler_params=_params("arbitrary"),
        name="conv_decode",
    )(proj_b, proj_b, proj_b, state2d, state2d, w_conv)


def _softmax_rows(logits):
    m = jnp.max(logits, axis=-1, keepdims=True)
    p = jnp.exp(logits - m)
    return p / jnp.sum(p, axis=-1, keepdims=True)


def _memattn_kernel(q_ref, k_ref, v_ref, o_ref):
    logits = _dot(q_ref[...].astype(BF16), k_ref[...].astype(BF16), NT) * (MEM_HD ** -0.5)
    p = _softmax_rows(logits)
    o_ref[...] = _dot(p.astype(BF16), v_ref[...].astype(BF16)).astype(o_ref.dtype)


def _memattn_prompt(proj_b, mk, mv, nb, seq):
    tl = min(512, seq)
    nt = seq // tl
    h = MEM_HEADS
    qcol = 3 * CONV_DIM // MEM_HD
    return pl.pallas_call(
        _memattn_kernel,
        grid=(nb, h, nt),
        in_specs=[
            pl.BlockSpec((tl, MEM_HD), lambda b, hh, t: (b * nt + t, qcol + hh)),
            pl.BlockSpec((N_MEM, MEM_HD), lambda b, hh, t: (b, hh)),
            pl.BlockSpec((N_MEM, MEM_HD), lambda b, hh, t: (b, hh)),
        ],
        out_specs=pl.BlockSpec((tl, MEM_HD), lambda b, hh, t: (b * nt + t, hh)),
        out_shape=jax.ShapeDtypeStruct((nb * seq, MEM_DIM), BF16),
        compiler_params=_params("parallel", "parallel", "parallel"),
        name="memattn_prompt",
    )(proj_b, mk, mv)


MEMATTN_DEC_ROWS = 4


def _memattn_dec_kernel(q_ref, k_ref, v_ref, o_ref):
    def body(i, carry):
        q = q_ref[i] * (MEM_HD ** -0.5)
        logits = jnp.sum(k_ref[i] * q[None], axis=-1, keepdims=True)
        e = jnp.exp(logits - jnp.max(logits, axis=0, keepdims=True))
        p = e / jnp.sum(e, axis=0, keepdims=True)
        o_ref[i] = jnp.sum(p * v_ref[i], axis=0)
        return carry

    lax.fori_loop(0, q_ref.shape[0], body, 0)


def _memattn_decode(q3, k4, v4):
    n = q3.shape[0]
    r = MEMATTN_DEC_ROWS
    return pl.pallas_call(
        _memattn_dec_kernel,
        grid=(n // r,),
        in_specs=[
            pl.BlockSpec((r, MEM_HEADS, MEM_HD), lambda i: (i, 0, 0)),
            pl.BlockSpec((r, N_MEM, MEM_HEADS, MEM_HD), lambda i: (i, 0, 0, 0)),
            pl.BlockSpec((r, N_MEM, MEM_HEADS, MEM_HD), lambda i: (i, 0, 0, 0)),
        ],
        out_specs=pl.BlockSpec((r, MEM_HEADS, MEM_HD), lambda i: (i, 0, 0)),
        out_shape=jax.ShapeDtypeStruct((n, MEM_HEADS, MEM_HD), F32),
        compiler_params=_params("parallel"),
        name="memattn_decode",
    )(q3, k4, v4)


def _merge_kernel(a_ref, b_ref, m_ref, wa_ref, wb_ref, wm_ref, za_ref, zb_ref, zm_ref, o_ref):
    ya = _dot(a_ref[...].astype(BF16), wa_ref[...])
    yb = _dot(b_ref[...].astype(BF16), wb_ref[...])
    ym = _dot(m_ref[...].astype(BF16), wm_ref[...])
    merged = _sigmoid(za_ref[...]) * ya + _sigmoid(zb_ref[...]) * yb + _sigmoid(zm_ref[...]) * ym
    o_ref[...] = merged.astype(o_ref.dtype)


def _merge(o_gla, cbu, om, w_gla_out, w_conv_out, w_mem_out, proj_b):
    m = o_gla.shape[0]
    tm = min(512, m)
    tn = 512
    d = D_MODEL
    zoff = (3 * CONV_DIM + MEM_DIM) // tn
    zspec = lambda g: pl.BlockSpec((tm, tn), lambda i, j: (i, zoff + g * (d // tn) + j))
    return pl.pallas_call(
        _merge_kernel,
        grid=(m // tm, d // tn),
        in_specs=[
            pl.BlockSpec((tm, o_gla.shape[1]), lambda i, j: (i, 0)),
            pl.BlockSpec((tm, cbu.shape[1]), lambda i, j: (i, 0)),
            pl.BlockSpec((tm, om.shape[1]), lambda i, j: (i, 0)),
            pl.BlockSpec((w_gla_out.shape[0], tn), lambda i, j: (0, j)),
            pl.BlockSpec((w_conv_out.shape[0], tn), lambda i, j: (0, j)),
            pl.BlockSpec((w_mem_out.shape[0], tn), lambda i, j: (0, j)),
            zspec(0), zspec(1), zspec(2),
        ],
        out_specs=pl.BlockSpec((tm, tn), lambda i, j: (i, j)),
        out_shape=jax.ShapeDtypeStruct((m, d), BF16),
        compiler_params=_params("parallel", "parallel"),
        name="merge",
    )(o_gla, cbu, om, w_gla_out, w_conv_out, w_mem_out, proj_b, proj_b, proj_b)


def _wo_ln_kernel(x_ref, m_ref, w_ref, g_ref, b_ref, o_ref):
    y = DN_ALPHA * x_ref[...] + _dot(m_ref[...], w_ref[...])
    o_ref[...] = _layer_norm(y, g_ref[...], b_ref[...])


def _wo_ln(x, merged, w_o, g, b):
    m, d = x.shape
    tm = min(512, m)
    return pl.pallas_call(
        _wo_ln_kernel,
        grid=(m // tm,),
        in_specs=[
            pl.BlockSpec((tm, d), lambda i: (i, 0)),
            pl.BlockSpec((tm, d), lambda i: (i, 0)),
            pl.BlockSpec((d, d), lambda i: (0, 0)),
            pl.BlockSpec((1, d), lambda i: (0, 0)),
            pl.BlockSpec((1, d), lambda i: (0, 0)),
        ],
        out_specs=pl.BlockSpec((tm, d), lambda i: (i, 0)),
        out_shape=jax.ShapeDtypeStruct((m, d), F32),
        compiler_params=_params("parallel"),
        name="wo_ln",
    )(x, merged, w_o, g, b)


def _ffn_tail(hc, hu, wd_ref, acc_ref, x_ref, g_ref, b_ref, o_ref, fi):
    h = (hc * _sigmoid(hc) * hu).astype(BF16)
    acc_ref[...] += _dot(h, wd_ref[...])

    @pl.when(fi == pl.num_programs(1) - 1)
    def _():
        o_ref[...] = _layer_norm(DN_ALPHA * x_ref[...] + acc_ref[...], g_ref[...], b_ref[...])


def _ffn_kernel(seq, x_ref, xh_ref, wg_ref, wu_ref, wd_ref, wc_ref, bc_ref, g_ref, b_ref, o_ref,
                xb_scr, acc_scr):
    fi = pl.program_id(1)
    tm = x_ref.shape[0]

    @pl.when(fi == 0)
    def _():
        xb_scr[...] = x_ref[...].astype(BF16)
        acc_scr[...] = jnp.zeros_like(acc_scr)

    xb = xb_scr[...]
    wg = wg_ref[...]
    hg = _dot(xb, wg)
    halo = _dot(xh_ref[...].astype(BF16), wg)
    halo = jnp.where((pl.program_id(0) * tm) % seq == 0, 0.0, halo)
    ri = lax.broadcasted_iota(jnp.int32, (tm, 1), 0)
    p1, p2 = _shift_rows(hg, halo, ri)
    wc = wc_ref[...]
    hc = wc[0:1, :] * p2 + wc[1:2, :] * p1 + wc[2:3, :] * hg + bc_ref[...]
    _ffn_tail(hc, _dot(xb, wu_ref[...]), wd_ref, acc_scr, x_ref, g_ref, b_ref, o_ref, fi)


def _ffn_prompt(x1, w_gate, w_up, w_down, w_conv, b_conv, g, b, seq):
    m, d = x1.shape
    tm = min(512, seq)
    tf = FF_TILE
    t8 = tm // 8
    return pl.pallas_call(
        functools.partial(_ffn_kernel, seq),
        grid=(m // tm, D_FF_PAD // tf),
        in_specs=[
            pl.BlockSpec((tm, d), lambda i, f: (i, 0)),
            pl.BlockSpec((8, d), lambda i, f: (jnp.maximum(i * t8 - 1, 0), 0)),
            pl.BlockSpec((d, tf), lambda i, f: (0, f)),
            pl.BlockSpec((d, tf), lambda i, f: (0, f)),
            pl.BlockSpec((tf, d), lambda i, f: (f, 0)),
            pl.BlockSpec((3, tf), lambda i, f: (0, f)),
            pl.BlockSpec((1, tf), lambda i, f: (0, f)),
            pl.BlockSpec((1, d), lambda i, f: (0, 0)),
            pl.BlockSpec((1, d), lambda i, f: (0, 0)),
        ],
        out_specs=pl.BlockSpec((tm, d), lambda i, f: (i, 0)),
        out_shape=jax.ShapeDtypeStruct((m, d), F32),
        scratch_shapes=[pltpu.VMEM((tm, d), BF16), pltpu.VMEM((tm, d), F32)],
        compiler_params=_params("parallel", "arbitrary"),
        name="ffn_prompt",
    )(x1, x1, w_gate, w_up, w_down, w_conv, b_conv, g, b)


def _ffn_dec_kernel(x_ref, s0_ref, s1_ref, wg_ref, wu_ref, wd_ref, wc_ref, bc_ref, g_ref, b_ref,
                    o_ref, hg_ref, xb_scr, acc_scr):
    fi = pl.program_id(1)

    @pl.when(fi == 0)
    def _():
        xb_scr[...] = x_ref[...].astype(BF16)
        acc_scr[...] = jnp.zeros_like(acc_scr)

    xb = xb_scr[...]
    hg = _dot(xb, wg_ref[...])
    hg_ref[...] = hg
    wc = wc_ref[...]
    hc = wc[0:1, :] * s0_ref[...] + wc[1:2, :] * s1_ref[...] + wc[2:3, :] * hg + bc_ref[...]
    _ffn_tail(hc, _dot(xb, wu_ref[...]), wd_ref, acc_scr, x_ref, g_ref, b_ref, o_ref, fi)


def _ffn_decode(x1, s0, s1, w_gate, w_up, w_down, w_conv, b_conv, g, b):
    m, d = x1.shape
    tf = FF_TILE
    return pl.pallas_call(
        _ffn_dec_kernel,
        grid=(1, D_FF_PAD // tf),
        in_specs=[
            pl.BlockSpec((m, d), lambda i, f: (0, 0)),
            pl.BlockSpec((m, tf), lambda i, f: (0, f)),
            pl.BlockSpec((m, tf), lambda i, f: (0, f)),
            pl.BlockSpec((d, tf), lambda i, f: (0, f)),
            pl.BlockSpec((d, tf), lambda i, f: (0, f)),
            pl.BlockSpec((tf, d), lambda i, f: (f, 0)),
            pl.BlockSpec((3, tf), lambda i, f: (0, f)),
            pl.BlockSpec((1, tf), lambda i, f: (0, f)),
            pl.BlockSpec((1, d), lambda i, f: (0, 0)),
            pl.BlockSpec((1, d), lambda i, f: (0, 0)),
        ],
        out_specs=[pl.BlockSpec((m, d), lambda i, f: (0, 0)), pl.BlockSpec((m, tf), lambda i, f: (0, f))],
        out_shape=[jax.ShapeDtypeStruct((m, d), F32), jax.ShapeDtypeStruct((m, D_FF_PAD), F32)],
        scratch_shapes=[pltpu.VMEM((m, d), BF16), pltpu.VMEM((m, d), F32)],
        compiler_params=_params("parallel", "arbitrary"),
        name="ffn_decode",
    )(x1, s0, s1, w_gate, w_up, w_down, w_conv, b_conv, g, b)


def _pad_cols(x, n):
    return jnp.pad(x, ((0, 0), (0, n - x.shape[1])))


def kernel(x_prompt, x_sample, mem_prompt, cache_mem_k, cache_mem_v, state_gla, state_conv, state_ffn_conv, w_in, w_gla_a2, b_gla_a2, g_gla_norm, w_gla_out, w_conv, w_conv_out, w_mem_k, w_mem_v, w_mem_out, w_o, ln1_g, ln1_b, w_ffn_gate, w_ffn_up, w_ffn_conv, b_ffn_conv, w_ffn_down, ln2_g, ln2_b):
    nb, seq, d = x_prompt.shape
    ns = x_sample.shape[0]
    ff_pad = D_FF_PAD - D_FF

    w_in2 = w_in[0]
    wa2p = jnp.pad(w_gla_a2[0], ((0, LANES - GLA_RANK), (0, 0)))
    ba2 = b_gla_a2
    gn = g_gla_norm
    w_go = w_gla_out[0].astype(BF16)
    w_co = w_conv_out[0].astype(BF16)
    w_mo = w_mem_out[0].astype(BF16)
    w_oo = w_o[0].astype(BF16)
    w_fg = _pad_cols(w_ffn_gate[0], D_FF_PAD).astype(BF16)
    w_fu = _pad_cols(w_ffn_up[0], D_FF_PAD).astype(BF16)
    w_fd = jnp.pad(w_ffn_down[0], ((0, ff_pad), (0, 0))).astype(BF16)
    w_fc = _pad_cols(w_ffn_conv[0], D_FF_PAD)
    b_fc = _pad_cols(b_ffn_conv, D_FF_PAD)
    w_cv = w_conv[0]

    def mix_in(x2d, tag):
        xb = x2d.astype(BF16)
        return (_proj(xb, w_in2, "proj_a_" + tag, 0, COLS_A),
                _proj(xb, w_in2, "proj_r_" + tag, COLS_A, LANES, LANES),
                _proj(xb, w_in2, "proj_b_" + tag, COLS_A, COLS_B, shift=GLA_RANK))

    def mix_out(x2d, o_gla, cbu, om, proj_b):
        merged = _merge(o_gla, cbu, om, w_go, w_co, w_mo, proj_b)
        return _wo_ln(x2d, merged, w_oo, ln1_g, ln1_b)

    xp = x_prompt.reshape(nb * seq, d)
    memb = mem_prompt.reshape(nb * N_MEM, d).astype(BF16)
    mk = _proj(memb, w_mem_k[0], "mem_k")
    mv = _proj(memb, w_mem_v[0], "mem_v")
    pa, pr, pb = mix_in(xp, "p")
    o_gla, p_gla = _gla_prompt(pa, pr, wa2p, ba2, gn, nb, seq)
    cbu, p_conv = _conv_prompt(pb, w_cv, nb, seq)
    om = _memattn_prompt(pb, mk, mv, nb, seq)
    x1 = mix_out(xp, o_gla, cbu, om, pb)
    yp = _ffn_prompt(x1, w_fg, w_fu, w_fd, w_fc, b_fc, ln2_g, ln2_b, seq)
    x1_tail = x1.reshape(nb, seq, d)[:, seq - 2:, :].reshape(nb * 2, d)
    p_ffn = _matmul(x1_tail, w_fg, F32, "ffn_state_p", tn=FF_TILE)[:nb * 2, :D_FF]

    xs = x_sample.reshape(ns, d)
    sa, sr, sb = mix_in(xs, "s")
    o_gla_s, s_gla = _gla_decode(sa, sr, wa2p, ba2, gn, state_gla[0])
    cbu_s, s_conv = _conv_decode(sb, w_cv, state_conv[0].reshape(ns, 2 * CONV_DIM))
    mq_s = sb[:, 3 * CONV_DIM:3 * CONV_DIM + MEM_DIM].reshape(ns, MEM_HEADS, MEM_HD)
    om_s = _memattn_decode(mq_s, cache_mem_k[0], cache_mem_v[0]).reshape(ns, MEM_DIM)
    x1s = mix_out(xs, o_gla_s, cbu_s, om_s, sb)
    f0 = _pad_cols(state_ffn_conv[0][:, 0, :], D_FF_PAD)
    f1 = _pad_cols(state_ffn_conv[0][:, 1, :], D_FF_PAD)
    ys, hg_s = _ffn_decode(x1s, f0, f1, w_fg, w_fu, w_fd, w_fc, b_fc, ln2_g, ln2_b)
    s_ffn = jnp.stack([state_ffn_conv[0][:, 1, :], hg_s[:, :D_FF]], axis=1)

    return (yp.reshape(nb, seq, d), ys.reshape(ns, 1, d),
            mk.reshape(1, nb, N_MEM, MEM_HEADS, MEM_HD), mv.reshape(1, nb, N_MEM, MEM_HEADS, MEM_HD),
            p_gla[None], p_conv[None], p_ffn.reshape(1, nb, 2, D_FF),
            s_gla[None], s_conv.reshape(1, ns, 2, CONV_DIM), s_ffn[None])
```

```python
import functools

import jax
import jax.numpy as jnp
from jax import lax
from jax.experimental import pallas as pl
from jax.experimental.pallas import tpu as pltpu

F32 = jnp.float32
BF16 = jnp.bfloat16

D_MODEL = 2048
GLA_HEADS = 4
GLA_HK = 256
GLA_HV = 512
GLA_RANK = 16
GLA_GATE_NORM = 16.0
CONV_DIM = 1024
N_MEM = 256
MEM_HEADS = 4
MEM_HD = 256
MEM_DIM = 1024
D_FF = 5504
DEPTH = 1
DN_ALPHA = (2 * DEPTH) ** 0.25
LN_EPS = 1e-5
RMS_EPS = 1e-6

LANES = 128
FF_TILE = 512
FF_TILES = -(-D_FF // FF_TILE)
COLS_A = 2 * GLA_HEADS * GLA_HK + 2 * GLA_HEADS * GLA_HV
COLS_B = 3 * CONV_DIM + MEM_DIM + 3 * D_MODEL
GLA_CHUNK = 256
VMEM_LIMIT = 56 * 1024 * 1024

NN = (((1,), (0,)), ((), ()))
NT = (((1,), (1,)), ((), ()))
TN = (((0,), (0,)), ((), ()))


def _dot(a, b, dims=NN):
    return lax.dot_general(a, b, dims, preferred_element_type=F32)


def _params(*sem):
    return pltpu.CompilerParams(dimension_semantics=sem, vmem_limit_bytes=VMEM_LIMIT)


def _split2(x):
    hi = x.astype(BF16)
    lo = (x - hi.astype(F32)).astype(BF16)
    return hi, lo


def _split3(x):
    hi = x.astype(BF16)
    r = x - hi.astype(F32)
    mid = r.astype(BF16)
    lo = (r - mid.astype(F32)).astype(BF16)
    return hi, mid, lo


def _sigmoid(z):
    return 1.0 / (1.0 + jnp.exp(-z))


def _layer_norm(y, g, b):
    mu = jnp.mean(y, axis=-1, keepdims=True)
    d = y - mu
    var = jnp.mean(d * d, axis=-1, keepdims=True)
    return d * lax.rsqrt(var + LN_EPS) * g + b


def _log_decay(alr, wa2, ba2):
    ah, al = _split2(alr)
    wh, wl = _split2(wa2)
    z = _dot(ah, wh) + _dot(ah, wl) + _dot(al, wh) + ba2
    return (jnp.minimum(z, 0.0) - jnp.log1p(jnp.exp(-jnp.abs(z)))) * (1.0 / GLA_GATE_NORM)


def _rms_gate(o, gn, g):
    o = o * lax.rsqrt(jnp.mean(o * o, axis=-1, keepdims=True) + RMS_EPS) * gn
    return o * (g * _sigmoid(g))


def _mm_kernel(x_ref, w_ref, o_ref):
    o_ref[...] = _dot(x_ref[...].astype(BF16), w_ref[...]).astype(o_ref.dtype)


def _matmul(x, w, out_dtype, name, tn=1024):
    m, k = x.shape
    n = w.shape[1]
    tm = min(m, 1024)
    tn = min(n, tn)
    return pl.pallas_call(
        _mm_kernel,
        grid=(m // tm, pl.cdiv(n, tn)),
        in_specs=[pl.BlockSpec((tm, k), lambda i, j: (i, 0)),
                  pl.BlockSpec((k, tn), lambda i, j: (0, j))],
        out_specs=pl.BlockSpec((tm, tn), lambda i, j: (i, j)),
        out_shape=jax.ShapeDtypeStruct((m, n), out_dtype),
        compiler_params=_params("parallel", "parallel"),
        name=name,
    )(x, w)


PROJ_SHIFT_ROWS = 16


def _proj_kernel(transposed, shift, x_ref, w_ref, *rest):
    if shift:
        wx_ref, o_ref, wb_scr = rest
    else:
        o_ref, wb_scr = rest

    @pl.when(pl.program_id(1) == 0)
    def _():
        w = w_ref[...]
        if shift:
            w = jnp.concatenate([w[shift:], wx_ref[:shift]], axis=0)
        if transposed:
            w = w.T
        wb_scr[...] = w.astype(BF16)

    o_ref[...] = _dot(x_ref[...].astype(BF16), wb_scr[...]).astype(o_ref.dtype)


def _proj(x, w, name, col0=0, n=None, tn=1024, shift=0, transposed=False):
    m, k = x.shape
    if n is None:
        n = w.shape[0] if transposed else w.shape[1]
    tm = min(m, 1024)
    tn = min(n, tn)
    c0 = col0 // tn
    if transposed:
        w_spec = pl.BlockSpec((tn, k), lambda j, i: (c0 + j, 0))
    else:
        assert not shift
        w_spec = pl.BlockSpec((k, tn), lambda j, i: (0, c0 + j))
    in_specs = [pl.BlockSpec((tm, k), lambda j, i: (i, 0)), w_spec]
    args = [x, w]
    if shift:
        r = PROJ_SHIFT_ROWS
        assert shift <= r and shift % 8 == 0 and tn % r == 0 and col0 % r == 0
        in_specs.append(pl.BlockSpec((r, k), lambda j, i: (col0 // r + (j + 1) * (tn // r), 0)))
        args.append(w)
    return pl.pallas_call(
        functools.partial(_proj_kernel, transposed, shift),
        grid=(n // tn, m // tm),
        in_specs=in_specs,
        out_specs=pl.BlockSpec((tm, tn), lambda j, i: (i, j)),
        out_shape=jax.ShapeDtypeStruct((m, n), F32),
        scratch_shapes=[pltpu.VMEM((k, tn), BF16)],
        compiler_params=_params("parallel", "arbitrary"),
        name=name,
    )(*args)


def _intra_scores(q, k, la, bc):
    c, dk = q.shape
    ri = lax.broadcasted_iota(jnp.int32, (c, 1), 0)
    rr = lax.broadcasted_iota(jnp.int32, (c, c), 0)
    cc = lax.broadcasted_iota(jnp.int32, (c, c), 1)
    diag = jnp.sum(q * k, axis=1, keepdims=True)
    a = jnp.where(rr == cc, diag, 0.0)
    b = 1
    while b < c:
        upper = (ri // b) % 2 == 1
        if b == 1:
            e = jnp.where(upper, la, 0.0)
        elif b == 2:
            m4 = ri % 4
            la_prev = pltpu.roll(la, 1, 0)
            la_next = pltpu.roll(la, c - 1, 0)
            e = jnp.where(m4 == 2, la, jnp.where(m4 == 3, la + la_prev, jnp.where(m4 == 0, la_next, 0.0)))
        else:
            ref = bc.reshape(c // (2 * b), 2 * b, dk)[:, b - 1:b, :]
            ref = jnp.broadcast_to(ref, (c // (2 * b), 2 * b, dk)).reshape(c, dk)
            e = jnp.where(upper, bc - ref, ref - bc)
        f = jnp.exp(e)
        qt = jnp.where(upper, q * f, 0.0).astype(BF16)
        kt = jnp.where(upper, 0.0, k * f).astype(BF16)
        blk = _dot(qt, kt, NT)
        a = a + jnp.where(rr // (2 * b) == cc // (2 * b), blk, 0.0)
        b *= 2
    return a


def _gla_kernel(q_ref, k_ref, v_ref, g_ref, alr_ref, wa2_ref, ba2_ref, gn_ref, o_ref, sfin_ref, s_scr):
    ci = pl.program_id(2)

    @pl.when(ci == 0)
    def _():
        s_scr[...] = jnp.zeros_like(s_scr)

    c = q_ref.shape[0]
    la = _log_decay(alr_ref[...], wa2_ref[...], ba2_ref[...])
    lh, lm, ll = _split3(la)
    rr = lax.broadcasted_iota(jnp.int32, (c, c), 0)
    cc = lax.broadcasted_iota(jnp.int32, (c, c), 1)
    tri = (rr >= cc).astype(BF16)
    bc = _dot(tri, lh) + _dot(tri, lm) + _dot(tri, ll)
    b_last = bc[c - 1:c, :]
    ones = jnp.ones((c, LANES), BF16)
    bl_col = _dot(lh, ones, TN) + _dot(lm, ones, TN) + _dot(ll, ones, TN)
    dec_col = jnp.exp(bl_col)
    dec_col = jnp.concatenate([dec_col] * (GLA_HV // LANES), axis=1)

    q = q_ref[...] * (GLA_HK ** -0.5)
    k = k_ref[...]
    vb = v_ref[...].astype(BF16)
    s = s_scr[...]
    o = _dot((q * jnp.exp(bc)).astype(BF16), s.astype(BF16))
    a = _intra_scores(q, k, la, bc)
    o = o + _dot(a.astype(BF16), vb)
    kd = (k * jnp.exp(b_last - bc)).astype(BF16)
    s_new = dec_col * s + _dot(kd, vb, TN)
    s_scr[...] = s_new
    o_ref[...] = _rms_gate(o, gn_ref[...], g_ref[...]).astype(o_ref.dtype)

    @pl.when(ci == pl.num_programs(2) - 1)
    def _():
        sfin_ref[...] = s_new


def _gla_prompt(proj_a, alr, wa2p, ba2, gn, nb, seq):
    c = min(GLA_CHUNK, seq)
    nc = seq // c
    h = GLA_HEADS
    row = lambda b, hh, ci: b * nc + ci
    return pl.pallas_call(
        _gla_kernel,
        grid=(nb, h, nc),
        in_specs=[
            pl.BlockSpec((c, GLA_HK), lambda b, hh, ci: (row(b, hh, ci), hh)),
            pl.BlockSpec((c, GLA_HK), lambda b, hh, ci: (row(b, hh, ci), h + hh)),
            pl.BlockSpec((c, GLA_HV), lambda b, hh, ci: (row(b, hh, ci), h + hh)),
            pl.BlockSpec((c, GLA_HV), lambda b, hh, ci: (row(b, hh, ci), 2 * h + hh)),
            pl.BlockSpec((c, LANES), lambda b, hh, ci: (row(b, hh, ci), 0)),
            pl.BlockSpec((LANES, GLA_HK), lambda b, hh, ci: (0, hh)),
            pl.BlockSpec((1, GLA_HK), lambda b, hh, ci: (0, hh)),
            pl.BlockSpec((1, GLA_HV), lambda b, hh, ci: (0, 0)),
        ],
        out_specs=[
            pl.BlockSpec((c, GLA_HV), lambda b, hh, ci: (row(b, hh, ci), hh)),
            pl.BlockSpec((None, None, GLA_HK, GLA_HV), lambda b, hh, ci: (b, hh, 0, 0)),
        ],
        out_shape=[
            jax.ShapeDtypeStruct((nb * seq, h * GLA_HV), BF16),
            jax.ShapeDtypeStruct((nb, h, GLA_HK, GLA_HV), F32),
        ],
        scratch_shapes=[pltpu.VMEM((GLA_HK, GLA_HV), F32)],
        compiler_params=_params("parallel", "parallel", "arbitrary"),
        name="gla_prompt",
    )(proj_a, proj_a, proj_a, proj_a, alr, wa2p, ba2, gn)


GLA_DEC_ROWS = 8


def _gla_dec_kernel(q_ref, k_ref, v_ref, g_ref, alr_ref, wa2_ref, ba2_ref, gn_ref, s_ref, o_ref, so_ref):
    nb = q_ref.shape[0]
    la = _log_decay(alr_ref[...], wa2_ref[...], ba2_ref[...])
    rep = LANES // nb
    a_t = jnp.concatenate([jnp.exp(la)] * rep, axis=0).T
    k_t = jnp.concatenate([k_ref[...]] * rep, axis=0).T
    qb = (q_ref[...] * (GLA_HK ** -0.5)).astype(BF16)
    v = v_ref[...]
    rows = []
    for j in range(nb):
        s_new = a_t[:, j:j + 1] * s_ref[j] + k_t[:, j:j + 1] * v[j:j + 1, :]
        so_ref[j] = s_new
        rows.append(_dot(qb, s_new.astype(BF16))[j:j + 1, :])
    o = jnp.concatenate(rows, axis=0)
    o_ref[...] = _rms_gate(o, gn_ref[...], g_ref[...]).astype(o_ref.dtype)


def _gla_decode(proj_a, alr, wa2p, ba2, gn, state):
    nseq = proj_a.shape[0]
    h = GLA_HEADS
    r = GLA_DEC_ROWS
    return pl.pallas_call(
        _gla_dec_kernel,
        grid=(nseq // r, h),
        in_specs=[
            pl.BlockSpec((r, GLA_HK), lambda b, hh: (b, hh)),
            pl.BlockSpec((r, GLA_HK), lambda b, hh: (b, h + hh)),
            pl.BlockSpec((r, GLA_HV), lambda b, hh: (b, h + hh)),
            pl.BlockSpec((r, GLA_HV), lambda b, hh: (b, 2 * h + hh)),
            pl.BlockSpec((r, LANES), lambda b, hh: (b, 0)),
            pl.BlockSpec((LANES, GLA_HK), lambda b, hh: (0, hh)),
            pl.BlockSpec((1, GLA_HK), lambda b, hh: (0, hh)),
            pl.BlockSpec((1, GLA_HV), lambda b, hh: (0, 0)),
            pl.BlockSpec((r, None, GLA_HK, GLA_HV), lambda b, hh: (b, hh, 0, 0)),
        ],
        out_specs=[
            pl.BlockSpec((r, GLA_HV), lambda b, hh: (b, hh)),
            pl.BlockSpec((r, None, GLA_HK, GLA_HV), lambda b, hh: (b, hh, 0, 0)),
        ],
        out_shape=[
            jax.ShapeDtypeStruct((nseq, h * GLA_HV), F32),
            jax.ShapeDtypeStruct(state.shape, F32),
        ],
        compiler_params=_params("parallel", "parallel"),
        name="gla_decode",
    )(proj_a, proj_a, proj_a, proj_a, alr, wa2p, ba2, gn, state)


def _shift_rows(x, halo, ri):
    p1 = jnp.where(ri == 0, halo[7:8, :], pltpu.roll(x, 1, 0))
    p2 = jnp.where(ri == 0, halo[6:7, :], jnp.where(ri == 1, halo[7:8, :], pltpu.roll(x, 2, 0)))
    return p1, p2


def _conv_kernel(cb_ref, cc_ref, ch_ref, cch_ref, chh_ref, w_ref, o_ref, st_ref):
    tl = cb_ref.shape[0]
    cch = cc_ref[...] * ch_ref[...]
    halo = cch_ref[...] * chh_ref[...]
    halo = jnp.where(pl.program_id(1) == 0, 0.0, halo)
    ri = lax.broadcasted_iota(jnp.int32, (tl, 1), 0)
    p1, p2 = _shift_rows(cch, halo, ri)
    w = w_ref[...]
    u = w[0:1, :] * p2 + w[1:2, :] * p1 + w[2:3, :] * cch
    o_ref[...] = (cb_ref[...] * u).astype(o_ref.dtype)
    st_ref[...] = cch[tl - 2:tl, :]


def _conv_prompt(proj_b, w_conv, nb, seq):
    tl = min(512, seq)
    nt = seq // tl
    t8 = tl // 8
    cd = CONV_DIM
    halo = lambda col: (lambda b, t: (jnp.maximum((b * nt + t) * t8 - 1, 0), col))
    return pl.pallas_call(
        _conv_kernel,
        grid=(nb, nt),
        in_specs=[
            pl.BlockSpec((tl, cd), lambda b, t: (b * nt + t, 0)),
            pl.BlockSpec((tl, cd), lambda b, t: (b * nt + t, 1)),
            pl.BlockSpec((tl, cd), lambda b, t: (b * nt + t, 2)),
            pl.BlockSpec((8, cd), halo(1)),
            pl.BlockSpec((8, cd), halo(2)),
            pl.BlockSpec((3, cd), lambda b, t: (0, 0)),
        ],
        out_specs=[
            pl.BlockSpec((tl, cd), lambda b, t: (b * nt + t, 0)),
            pl.BlockSpec((None, 2, cd), lambda b, t: (b, 0, 0)),
        ],
        out_shape=[
            jax.ShapeDtypeStruct((nb * seq, cd), BF16),
            jax.ShapeDtypeStruct((nb, 2, cd), F32),
        ],
        compiler_params=_params("parallel", "arbitrary"),
        name="conv_prompt",
    )(proj_b, proj_b, proj_b, proj_b, proj_b, w_conv)


def _conv_dec_kernel(cb_ref, cc_ref, ch_ref, s0_ref, s1_ref, w_ref, o_ref, st_ref):
    cd = cb_ref.shape[1]
    cch = cc_ref[...] * ch_ref[...]
    s1 = s1_ref[...]
    w = w_ref[...]
    u = w[0:1, :] * s0_ref[...] + w[1:2, :] * s1 + w[2:3, :] * cch
    o_ref[...] = (cb_ref[...] * u).astype(o_ref.dtype)
    st_ref[:, :cd] = s1
    st_ref[:, cd:] = cch


def _conv_decode(proj_b, w_conv, state2d):
    n = proj_b.shape[0]
    cd = CONV_DIM
    blk = lambda col: pl.BlockSpec((n, cd), lambda i: (0, col))
    return pl.pallas_call(
        _conv_dec_kernel,
        grid=(1,),
        in_specs=[blk(0), blk(1), blk(2), blk(0), blk(1), pl.BlockSpec((3, cd), lambda i: (0, 0))],
        out_specs=[pl.BlockSpec((n, cd), lambda i: (0, 0)), pl.BlockSpec((n, 2 * cd), lambda i: (0, 0))],
        out_shape=[jax.ShapeDtypeStruct((n, cd), BF16), jax.ShapeDtypeStruct((n, 2 * cd), F32)],
        compiler_params=_params("arbitrary"),
        name="conv_decode",
    )(proj_b, proj_b, proj_b, state2d, state2d, w_conv)


def _softmax_rows(logits):
    m = jnp.max(logits, axis=-1, keepdims=True)
    p = jnp.exp(logits - m)
    return p / jnp.sum(p, axis=-1, keepdims=True)


def _memattn_kernel(q_ref, k_ref, v_ref, o_ref):
    logits = _dot(q_ref[...].astype(BF16), k_ref[...].astype(BF16), NT) * (MEM_HD ** -0.5)
    p = _softmax_rows(logits)
    o_ref[...] = _dot(p.astype(BF16), v_ref[...].astype(BF16)).astype(o_ref.dtype)


def _memattn_prompt(proj_b, mk, mv, nb, seq):
    tl = min(512, seq)
    nt = seq // tl
    h = MEM_HEADS
    qcol = 3 * CONV_DIM // MEM_HD
    return pl.pallas_call(
        _memattn_kernel,
        grid=(nb, h, nt),
        in_specs=[
            pl.BlockSpec((tl, MEM_HD), lambda b, hh, t: (b * nt + t, qcol + hh)),
            pl.BlockSpec((N_MEM, MEM_HD), lambda b, hh, t: (b, hh)),
            pl.BlockSpec((N_MEM, MEM_HD), lambda b, hh, t: (b, hh)),
        ],
        out_specs=pl.BlockSpec((tl, MEM_HD), lambda b, hh, t: (b * nt + t, hh)),
        out_shape=jax.ShapeDtypeStruct((nb * seq, MEM_DIM), BF16),
        compiler_params=_params("parallel", "parallel", "parallel"),
        name="memattn_prompt",
    )(proj_b, mk, mv)


MEMATTN_DEC_ROWS = 4


def _memattn_dec_kernel(q_ref, k_ref, v_ref, o_ref):
    def body(i, carry):
        q = q_ref[i] * (MEM_HD ** -0.5)
        logits = jnp.sum(k_ref[i] * q[None], axis=-1, keepdims=True)
        e = jnp.exp(logits - jnp.max(logits, axis=0, keepdims=True))
        p = e / jnp.sum(e, axis=0, keepdims=True)
        o_ref[i] = jnp.sum(p * v_ref[i], axis=0)
        return carry

    lax.fori_loop(0, q_ref.shape[0], body, 0)


def _memattn_decode(q3, k4, v4):
    n = q3.shape[0]
    r = MEMATTN_DEC_ROWS
    return pl.pallas_call(
        _memattn_dec_kernel,
        grid=(n // r,),
        in_specs=[
            pl.BlockSpec((r, MEM_HEADS, MEM_HD), lambda i: (i, 0, 0)),
            pl.BlockSpec((r, N_MEM, MEM_HEADS, MEM_HD), lambda i: (i, 0, 0, 0)),
            pl.BlockSpec((r, N_MEM, MEM_HEADS, MEM_HD), lambda i: (i, 0, 0, 0)),
        ],
        out_specs=pl.BlockSpec((r, MEM_HEADS, MEM_HD), lambda i: (i, 0, 0)),
        out_shape=jax.ShapeDtypeStruct((n, MEM_HEADS, MEM_HD), F32),
        compiler_params=_params("parallel"),
        name="memattn_decode",
    )(q3, k4, v4)


def _merge_kernel(a_ref, b_ref, m_ref, wa_ref, wb_ref, wm_ref, za_ref, zb_ref, zm_ref, o_ref):
    ya = _dot(a_ref[...].astype(BF16), wa_ref[...])
    yb = _dot(b_ref[...].astype(BF16), wb_ref[...])
    ym = _dot(m_ref[...].astype(BF16), wm_ref[...])
    merged = _sigmoid(za_ref[...]) * ya + _sigmoid(zb_ref[...]) * yb + _sigmoid(zm_ref[...]) * ym
    o_ref[...] = merged.astype(o_ref.dtype)


def _merge(o_gla, cbu, om, w_gla_out, w_conv_out, w_mem_out, proj_b):
    m = o_gla.shape[0]
    tm = min(512, m)
    tn = 512
    d = D_MODEL
    zoff = (3 * CONV_DIM + MEM_DIM) // tn
    zspec = lambda g: pl.BlockSpec((tm, tn), lambda i, j: (i, zoff + g * (d // tn) + j))
    return pl.pallas_call(
        _merge_kernel,
        grid=(m // tm, d // tn),
        in_specs=[
            pl.BlockSpec((tm, o_gla.shape[1]), lambda i, j: (i, 0)),
            pl.BlockSpec((tm, cbu.shape[1]), lambda i, j: (i, 0)),
            pl.BlockSpec((tm, om.shape[1]), lambda i, j: (i, 0)),
            pl.BlockSpec((w_gla_out.shape[0], tn), lambda i, j: (0, j)),
            pl.BlockSpec((w_conv_out.shape[0], tn), lambda i, j: (0, j)),
            pl.BlockSpec((w_mem_out.shape[0], tn), lambda i, j: (0, j)),
            zspec(0), zspec(1), zspec(2),
        ],
        out_specs=pl.BlockSpec((tm, tn), lambda i, j: (i, j)),
        out_shape=jax.ShapeDtypeStruct((m, d), BF16),
        compiler_params=_params("parallel", "parallel"),
        name="merge",
    )(o_gla, cbu, om, w_gla_out, w_conv_out, w_mem_out, proj_b, proj_b, proj_b)


def _wo_ln_kernel(x_ref, m_ref, w_ref, g_ref, b_ref, o_ref):
    y = DN_ALPHA * x_ref[...] + _dot(m_ref[...], w_ref[...])
    o_ref[...] = _layer_norm(y, g_ref[...], b_ref[...])


def _wo_ln(x, merged, w_o, g, b):
    m, d = x.shape
    tm = min(512, m)
    return pl.pallas_call(
        _wo_ln_kernel,
        grid=(m // tm,),
        in_specs=[
            pl.BlockSpec((tm, d), lambda i: (i, 0)),
            pl.BlockSpec((tm, d), lambda i: (i, 0)),
            pl.BlockSpec((d, d), lambda i: (0, 0)),
            pl.BlockSpec((1, d), lambda i: (0, 0)),
            pl.BlockSpec((1, d), lambda i: (0, 0)),
        ],
        out_specs=pl.BlockSpec((tm, d), lambda i: (i, 0)),
        out_shape=jax.ShapeDtypeStruct((m, d), F32),
        compiler_params=_params("parallel"),
        name="wo_ln",
    )(x, merged, w_o, g, b)


def _ffn_tail(hc, hu, wd_ref, acc_ref, x_ref, g_ref, b_ref, o_ref, fi):
    tf = hc.shape[1]
    valid = D_FF - fi * tf
    h = hc * _sigmoid(hc) * hu
    h = jnp.where(lax.broadcasted_iota(jnp.int32, (1, tf), 1) < valid, h, 0.0).astype(BF16)
    wd = jnp.where(lax.broadcasted_iota(jnp.int32, (tf, 1), 0) < valid, wd_ref[...], 0.0)
    acc_ref[...] += _dot(h, wd)

    @pl.when(fi == pl.num_programs(1) - 1)
    def _():
        o_ref[...] = _layer_norm(DN_ALPHA * x_ref[...] + acc_ref[...], g_ref[...], b_ref[...])


def _ffn_kernel(seq, x_ref, xh_ref, wg_ref, wu_ref, wd_ref, wc_ref, bc_ref, g_ref, b_ref, o_ref,
                xb_scr, acc_scr):
    fi = pl.program_id(1)
    tm = x_ref.shape[0]

    @pl.when(fi == 0)
    def _():
        xb_scr[...] = x_ref[...].astype(BF16)
        acc_scr[...] = jnp.zeros_like(acc_scr)

    xb = xb_scr[...]
    wg = wg_ref[...]
    hg = _dot(xb, wg)
    halo = _dot(xh_ref[...].astype(BF16), wg)
    halo = jnp.where((pl.program_id(0) * tm) % seq == 0, 0.0, halo)
    ri = lax.broadcasted_iota(jnp.int32, (tm, 1), 0)
    p1, p2 = _shift_rows(hg, halo, ri)
    wc = wc_ref[...]
    hc = wc[0:1, :] * p2 + wc[1:2, :] * p1 + wc[2:3, :] * hg + bc_ref[...]
    _ffn_tail(hc, _dot(xb, wu_ref[...]), wd_ref, acc_scr, x_ref, g_ref, b_ref, o_ref, fi)


def _ffn_prompt(x1, w_gate, w_up, w_down, w_conv, b_conv, g, b, seq):
    m, d = x1.shape
    tm = min(512, seq)
    tf = FF_TILE
    t8 = tm // 8
    return pl.pallas_call(
        functools.partial(_ffn_kernel, seq),
        grid=(m // tm, FF_TILES),
        in_specs=[
            pl.BlockSpec((tm, d), lambda i, f: (i, 0)),
            pl.BlockSpec((8, d), lambda i, f: (jnp.maximum(i * t8 - 1, 0), 0)),
            pl.BlockSpec((d, tf), lambda i, f: (0, f)),
            pl.BlockSpec((d, tf), lambda i, f: (0, f)),
            pl.BlockSpec((tf, d), lambda i, f: (f, 0)),
            pl.BlockSpec((3, tf), lambda i, f: (0, f)),
            pl.BlockSpec((1, tf), lambda i, f: (0, f)),
            pl.BlockSpec((1, d), lambda i, f: (0, 0)),
            pl.BlockSpec((1, d), lambda i, f: (0, 0)),
        ],
        out_specs=pl.BlockSpec((tm, d), lambda i, f: (i, 0)),
        out_shape=jax.ShapeDtypeStruct((m, d), F32),
        scratch_shapes=[pltpu.VMEM((tm, d), BF16), pltpu.VMEM((tm, d), F32)],
        compiler_params=_params("parallel", "arbitrary"),
        name="ffn_prompt",
    )(x1, x1, w_gate, w_up, w_down, w_conv, b_conv, g, b)


def _ffn_dec_kernel(x_ref, s0_ref, s1_ref, wg_ref, wu_ref, wd_ref, wc_ref, bc_ref, g_ref, b_ref,
                    o_ref, hg_ref, xb_scr, acc_scr):
    fi = pl.program_id(1)

    @pl.when(fi == 0)
    def _():
        xb_scr[...] = x_ref[...].astype(BF16)
        acc_scr[...] = jnp.zeros_like(acc_scr)

    xb = xb_scr[...]
    hg = _dot(xb, wg_ref[...])
    hg_ref[...] = hg
    wc = wc_ref[...]
    hc = wc[0:1, :] * s0_ref[...] + wc[1:2, :] * s1_ref[...] + wc[2:3, :] * hg + bc_ref[...]
    _ffn_tail(hc, _dot(xb, wu_ref[...]), wd_ref, acc_scr, x_ref, g_ref, b_ref, o_ref, fi)


def _ffn_decode(x1, s0, s1, w_gate, w_up, w_down, w_conv, b_conv, g, b):
    m, d = x1.shape
    tf = FF_TILE
    return pl.pallas_call(
        _ffn_dec_kernel,
        grid=(1, FF_TILES),
        in_specs=[
            pl.BlockSpec((m, d), lambda i, f: (0, 0)),
            pl.BlockSpec((m, tf), lambda i, f: (0, f)),
            pl.BlockSpec((m, tf), lambda i, f: (0, f)),
            pl.BlockSpec((d, tf), lambda i, f: (0, f)),
            pl.BlockSpec((d, tf), lambda i, f: (0, f)),
            pl.BlockSpec((tf, d), lambda i, f: (f, 0)),
            pl.BlockSpec((3, tf), lambda i, f: (0, f)),
            pl.BlockSpec((1, tf), lambda i, f: (0, f)),
            pl.BlockSpec((1, d), lambda i, f: (0, 0)),
            pl.BlockSpec((1, d), lambda i, f: (0, 0)),
        ],
        out_specs=[pl.BlockSpec((m, d), lambda i, f: (0, 0)), pl.BlockSpec((m, tf), lambda i, f: (0, f))],
        out_shape=[jax.ShapeDtypeStruct((m, d), F32), jax.ShapeDtypeStruct((m, D_FF), F32)],
        scratch_shapes=[pltpu.VMEM((m, d), BF16), pltpu.VMEM((m, d), F32)],
        compiler_params=_params("parallel", "arbitrary"),
        name="ffn_decode",
    )(x1, s0, s1, w_gate, w_up, w_down, w_conv, b_conv, g, b)


def kernel(x_prompt, x_sample, mem_prompt, cache_mem_k, cache_mem_v, state_gla, state_conv, state_ffn_conv, w_in, w_gla_a2, b_gla_a2, g_gla_norm, w_gla_out, w_conv, w_conv_out, w_mem_k, w_mem_v, w_mem_out, w_o, ln1_g, ln1_b, w_ffn_gate, w_ffn_up, w_ffn_conv, b_ffn_conv, w_ffn_down, ln2_g, ln2_b):
    nb, seq, d = x_prompt.shape
    ns = x_sample.shape[0]

    w_in_t = jnp.swapaxes(w_in[0], 0, 1)
    wa2p = jnp.pad(w_gla_a2[0], ((0, LANES - GLA_RANK), (0, 0)))
    ba2 = b_gla_a2
    gn = g_gla_norm
    w_go = w_gla_out[0].astype(BF16)
    w_co = w_conv_out[0].astype(BF16)
    w_mo = w_mem_out[0].astype(BF16)
    w_oo = w_o[0].astype(BF16)
    w_fg = w_ffn_gate[0].astype(BF16)
    w_fu = w_ffn_up[0].astype(BF16)
    w_fd = w_ffn_down[0].astype(BF16)
    w_fc = w_ffn_conv[0]
    b_fc = b_ffn_conv
    w_cv = w_conv[0]

    def mix_in(x2d, tag):
        xb = x2d.astype(BF16)
        return (_proj(xb, w_in_t, "proj_a_" + tag, 0, COLS_A, transposed=True),
                _proj(xb, w_in_t, "proj_r_" + tag, COLS_A, LANES, LANES, transposed=True),
                _proj(xb, w_in_t, "proj_b_" + tag, COLS_A, COLS_B, shift=GLA_RANK, transposed=True))

    def mix_out(x2d, o_gla, cbu, om, proj_b):
        merged = _merge(o_gla, cbu, om, w_go, w_co, w_mo, proj_b)
        return _wo_ln(x2d, merged, w_oo, ln1_g, ln1_b)

    xp = x_prompt.reshape(nb * seq, d)
    memb = mem_prompt.reshape(nb * N_MEM, d).astype(BF16)
    mk = _proj(memb, w_mem_k[0], "mem_k")
    mv = _proj(memb, w_mem_v[0], "mem_v")
    pa, pr, pb = mix_in(xp, "p")
    o_gla, p_gla = _gla_prompt(pa, pr, wa2p, ba2, gn, nb, seq)
    cbu, p_conv = _conv_prompt(pb, w_cv, nb, seq)
    om = _memattn_prompt(pb, mk, mv, nb, seq)
    x1 = mix_out(xp, o_gla, cbu, om, pb)
    yp = _ffn_prompt(x1, w_fg, w_fu, w_fd, w_fc, b_fc, ln2_g, ln2_b, seq)
    x1_tail = x1.reshape(nb, seq, d)[:, seq - 2:, :].reshape(nb * 2, d)
    p_ffn = _matmul(x1_tail, w_fg, F32, "ffn_state_p", tn=FF_TILE)

    xs = x_sample.reshape(ns, d)
    sa, sr, sb = mix_in(xs, "s")
    o_gla_s, s_gla = _gla_decode(sa, sr, wa2p, ba2, gn, state_gla[0])
    cbu_s, s_conv = _conv_decode(sb, w_cv, state_conv[0].reshape(ns, 2 * CONV_DIM))
    mq_s = sb[:, 3 * CONV_DIM:3 * CONV_DIM + MEM_DIM].reshape(ns, MEM_HEADS, MEM_HD)
    om_s = _memattn_decode(mq_s, cache_mem_k[0], cache_mem_v[0]).reshape(ns, MEM_DIM)
    x1s = mix_out(xs, o_gla_s, cbu_s, om_s, sb)
    f0 = state_ffn_conv[0][:, 0, :]
    f1 = state_ffn_conv[0][:, 1, :]
    ys, hg_s = _ffn_decode(x1s, f0, f1, w_fg, w_fu, w_fd, w_fc, b_fc, ln2_g, ln2_b)
    s_ffn = jnp.stack([f1, hg_s], axis=1)

    return (yp.reshape(nb, seq, d), ys.reshape(ns, 1, d),
            mk.reshape(1, nb, N_MEM, MEM_HEADS, MEM_HD), mv.reshape(1, nb, N_MEM, MEM_HEADS, MEM_HD),
            p_gla[None], p_conv[None], p_ffn.reshape(1, nb, 2, D_FF),
            s_gla[None], s_conv.reshape(1, ns, 2, CONV_DIM), s_ffn[None])
```

```python
import functools

import jax
import jax.numpy as jnp
from jax import lax
from jax.experimental import pallas as pl
from jax.experimental.pallas import tpu as pltpu

F32 = jnp.float32
BF16 = jnp.bfloat16

D_MODEL = 2048
GLA_HEADS = 4
GLA_HK = 256
GLA_HV = 512
GLA_RANK = 16
GLA_GATE_NORM = 16.0
CONV_DIM = 1024
N_MEM = 256
MEM_HEADS = 4
MEM_HD = 256
MEM_DIM = 1024
D_FF = 5504
DEPTH = 1
DN_ALPHA = (2 * DEPTH) ** 0.25
LN_EPS = 1e-5
RMS_EPS = 1e-6

LANES = 128
FF_TILE = 512
FF_TILES = -(-D_FF // FF_TILE)
COLS_A = 2 * GLA_HEADS * GLA_HK + 2 * GLA_HEADS * GLA_HV
COLS_B = 3 * CONV_DIM + MEM_DIM + 3 * D_MODEL
GLA_CHUNK = 256
VMEM_LIMIT = 56 * 1024 * 1024

NN = (((1,), (0,)), ((), ()))
NT = (((1,), (1,)), ((), ()))
TN = (((0,), (0,)), ((), ()))


def _dot(a, b, dims=NN):
    return lax.dot_general(a, b, dims, preferred_element_type=F32)


def _params(*sem):
    return pltpu.CompilerParams(dimension_semantics=sem, vmem_limit_bytes=VMEM_LIMIT)


def _split2(x):
    hi = x.astype(BF16)
    lo = (x - hi.astype(F32)).astype(BF16)
    return hi, lo


def _split3(x):
    hi = x.astype(BF16)
    r = x - hi.astype(F32)
    mid = r.astype(BF16)
    lo = (r - mid.astype(F32)).astype(BF16)
    return hi, mid, lo


def _sigmoid(z):
    return 0.5 * jnp.tanh(0.5 * z) + 0.5


def _layer_norm(y, g, b):
    mu = jnp.mean(y, axis=-1, keepdims=True)
    d = y - mu
    var = jnp.mean(d * d, axis=-1, keepdims=True)
    return d * lax.rsqrt(var + LN_EPS) * g + b


def _log_decay(alr, wa2, ba2):
    ah, al = _split2(alr)
    wh, wl = _split2(wa2)
    z = _dot(ah, wh) + _dot(ah, wl) + _dot(al, wh) + ba2
    return (jnp.minimum(z, 0.0) - jnp.log(1.0 + jnp.exp(-jnp.abs(z)))) * (1.0 / GLA_GATE_NORM)


def _rms_gate(o, gn, g):
    o = o * lax.rsqrt(jnp.mean(o * o, axis=-1, keepdims=True) + RMS_EPS) * gn
    return o * (g * _sigmoid(g))


def _mm_kernel(x_ref, w_ref, o_ref):
    o_ref[...] = _dot(x_ref[...].astype(BF16), w_ref[...]).astype(o_ref.dtype)


def _matmul(x, w, out_dtype, name, tn=1024):
    m, k = x.shape
    n = w.shape[1]
    tm = min(m, 1024)
    tn = min(n, tn)
    return pl.pallas_call(
        _mm_kernel,
        grid=(m // tm, pl.cdiv(n, tn)),
        in_specs=[pl.BlockSpec((tm, k), lambda i, j: (i, 0)),
                  pl.BlockSpec((k, tn), lambda i, j: (0, j))],
        out_specs=pl.BlockSpec((tm, tn), lambda i, j: (i, j)),
        out_shape=jax.ShapeDtypeStruct((m, n), out_dtype),
        compiler_params=_params("parallel", "parallel"),
        name=name,
    )(x, w)


PROJ_SHIFT_ROWS = 16


def _proj_kernel(transposed, shift, x_ref, w_ref, *rest):
    if shift:
        wx_ref, o_ref, wb_scr = rest
    else:
        o_ref, wb_scr = rest

    @pl.when(pl.program_id(1) == 0)
    def _():
        w = w_ref[...]
        if shift:
            w = jnp.concatenate([w[shift:], wx_ref[:shift]], axis=0)
        if transposed:
            w = w.T
        wb_scr[...] = w.astype(BF16)

    o_ref[...] = _dot(x_ref[...].astype(BF16), wb_scr[...]).astype(o_ref.dtype)


def _proj(x, w, name, col0=0, n=None, tn=1024, shift=0, transposed=False):
    m, k = x.shape
    if n is None:
        n = w.shape[0] if transposed else w.shape[1]
    tm = min(m, 1024)
    tn = min(n, tn)
    c0 = col0 // tn
    if transposed:
        w_spec = pl.BlockSpec((tn, k), lambda j, i: (c0 + j, 0))
    else:
        assert not shift
        w_spec = pl.BlockSpec((k, tn), lambda j, i: (0, c0 + j))
    in_specs = [pl.BlockSpec((tm, k), lambda j, i: (i, 0)), w_spec]
    args = [x, w]
    if shift:
        r = PROJ_SHIFT_ROWS
        assert shift <= r and shift % 8 == 0 and tn % r == 0 and col0 % r == 0
        in_specs.append(pl.BlockSpec((r, k), lambda j, i: (col0 // r + (j + 1) * (tn // r), 0)))
        args.append(w)
    return pl.pallas_call(
        functools.partial(_proj_kernel, transposed, shift),
        grid=(n // tn, m // tm),
        in_specs=in_specs,
        out_specs=pl.BlockSpec((tm, tn), lambda j, i: (i, j)),
        out_shape=jax.ShapeDtypeStruct((m, n), F32),
        scratch_shapes=[pltpu.VMEM((k, tn), BF16)],
        compiler_params=_params("parallel", "arbitrary"),
        name=name,
    )(*args)


def _gla_levels(c):
    return c.bit_length() - 1


def _init_gla_masks(mask_scr, tri_scr):
    c = tri_scr.shape[0]
    rr = lax.broadcasted_iota(jnp.int32, (c, c), 0)
    cc = lax.broadcasted_iota(jnp.int32, (c, c), 1)
    tri_scr[...] = (rr >= cc).astype(BF16)
    mask_scr[0] = (rr == cc).astype(F32)
    for l in range(_gla_levels(c)):
        b = 1 << l
        pair = (rr // (2 * b) == cc // (2 * b)) & ((rr // b) % 2 == 1) & ((cc // b) % 2 == 0)
        mask_scr[1 + l] = pair.astype(F32)


def _intra_scores(q, k, la, bc, mask_scr):
    c, dk = q.shape
    ri = lax.broadcasted_iota(jnp.int32, (c, 1), 0)
    a = jnp.sum(q * k, axis=1, keepdims=True) * mask_scr[0]
    for l in range(_gla_levels(c)):
        b = 1 << l
        upper = (ri // b) % 2 == 1
        if b == 1:
            e = jnp.where(upper, la, 0.0)
        elif b == 2:
            m4 = ri % 4
            la_prev = pltpu.roll(la, 1, 0)
            la_next = pltpu.roll(la, c - 1, 0)
            e = jnp.where(m4 == 2, la, jnp.where(m4 == 3, la + la_prev, jnp.where(m4 == 0, la_next, 0.0)))
        else:
            ref = bc.reshape(c // (2 * b), 2 * b, dk)[:, b - 1:b, :]
            ref = jnp.broadcast_to(ref, (c // (2 * b), 2 * b, dk)).reshape(c, dk)
            dlt = bc - ref
            e = jnp.minimum(dlt, -dlt)
        w = (jnp.where(upper, q, k) * jnp.exp(e)).astype(BF16)
        a = a + _dot(w, w, NT) * mask_scr[1 + l]
    return a


def _gla_kernel(q_ref, k_ref, v_ref, g_ref, alr_ref, wa2_ref, ba2_ref, gn_ref, o_ref, sfin_ref,
                s_scr, mask_scr, tri_scr):
    ci = pl.program_id(2)

    @pl.when(ci == 0)
    def _():
        s_scr[...] = jnp.zeros_like(s_scr)
        _init_gla_masks(mask_scr, tri_scr)

    c = q_ref.shape[0]
    tri = tri_scr[...]
    ones = jnp.ones((c, LANES), BF16)
    alr = alr_ref[...]
    gn = gn_ref[...]
    for hh in range(s_scr.shape[0]):
        ks = slice(hh * GLA_HK, (hh + 1) * GLA_HK)
        vs = slice(hh * GLA_HV, (hh + 1) * GLA_HV)
        la = _log_decay(alr, wa2_ref[:, ks], ba2_ref[:, ks])
        lh, lm, ll = _split3(la)
        bc = _dot(tri, lh) + _dot(tri, lm) + _dot(tri, ll)
        b_last = bc[c - 1:c, :]
        bl_col = _dot(lh, ones, TN) + _dot(lm, ones, TN) + _dot(ll, ones, TN)
        dec_col = jnp.exp(bl_col)
        dec_col = jnp.concatenate([dec_col] * (GLA_HV // LANES), axis=1)

        q = q_ref[:, ks] * (GLA_HK ** -0.5)
        k = k_ref[:, ks]
        vb = v_ref[:, vs].astype(BF16)
        s = s_scr[hh]
        o = _dot((q * jnp.exp(bc)).astype(BF16), s.astype(BF16))
        a = _intra_scores(q, k, la, bc, mask_scr)
        o = o + _dot(a.astype(BF16), vb)
        kd = (k * jnp.exp(b_last - bc)).astype(BF16)
        s_new = dec_col * s + _dot(kd, vb, TN)
        s_scr[hh] = s_new
        o_ref[:, vs] = _rms_gate(o, gn, g_ref[:, vs]).astype(o_ref.dtype)

    @pl.when(ci == pl.num_programs(2) - 1)
    def _():
        sfin_ref[...] = s_scr[...]


GLA_HEADS_PER_STEP = 2


def _gla_prompt(proj_a, alr, wa2p, ba2, gn, nb, seq):
    c = min(GLA_CHUNK, seq)
    nc = seq // c
    hp = GLA_HEADS_PER_STEP
    ng = GLA_HEADS // hp
    wk, wv = hp * GLA_HK, hp * GLA_HV
    row = lambda b, hg, ci: b * nc + ci
    return pl.pallas_call(
        _gla_kernel,
        grid=(nb, ng, nc),
        in_specs=[
            pl.BlockSpec((c, wk), lambda b, hg, ci: (row(b, hg, ci), hg)),
            pl.BlockSpec((c, wk), lambda b, hg, ci: (row(b, hg, ci), ng + hg)),
            pl.BlockSpec((c, wv), lambda b, hg, ci: (row(b, hg, ci), ng + hg)),
            pl.BlockSpec((c, wv), lambda b, hg, ci: (row(b, hg, ci), 2 * ng + hg)),
            pl.BlockSpec((c, LANES), lambda b, hg, ci: (row(b, hg, ci), 0)),
            pl.BlockSpec((LANES, wk), lambda b, hg, ci: (0, hg)),
            pl.BlockSpec((1, wk), lambda b, hg, ci: (0, hg)),
            pl.BlockSpec((1, GLA_HV), lambda b, hg, ci: (0, 0)),
        ],
        out_specs=[
            pl.BlockSpec((c, wv), lambda b, hg, ci: (row(b, hg, ci), hg)),
            pl.BlockSpec((None, hp, GLA_HK, GLA_HV), lambda b, hg, ci: (b, hg, 0, 0)),
        ],
        out_shape=[
            jax.ShapeDtypeStruct((nb * seq, GLA_HEADS * GLA_HV), BF16),
            jax.ShapeDtypeStruct((nb, GLA_HEADS, GLA_HK, GLA_HV), F32),
        ],
        scratch_shapes=[pltpu.VMEM((hp, GLA_HK, GLA_HV), F32),
                        pltpu.VMEM((1 + _gla_levels(c), c, c), F32),
                        pltpu.VMEM((c, c), BF16)],
        compiler_params=_params("parallel", "parallel", "arbitrary"),
        name="gla_prompt",
    )(proj_a, proj_a, proj_a, proj_a, alr, wa2p, ba2, gn)


GLA_DEC_ROWS = 8


def _gla_dec_kernel(q_ref, k_ref, v_ref, g_ref, alr_ref, wa2_ref, ba2_ref, gn_ref, s_ref, o_ref, so_ref):
    nb = q_ref.shape[0]
    la = _log_decay(alr_ref[...], wa2_ref[...], ba2_ref[...])
    rep = LANES // nb
    a_t = jnp.concatenate([jnp.exp(la)] * rep, axis=0).T
    k_t = jnp.concatenate([k_ref[...]] * rep, axis=0).T
    qb = (q_ref[...] * (GLA_HK ** -0.5)).astype(BF16)
    v = v_ref[...]
    rows = []
    for j in range(nb):
        s_new = a_t[:, j:j + 1] * s_ref[j] + k_t[:, j:j + 1] * v[j:j + 1, :]
        so_ref[j] = s_new
        rows.append(_dot(qb, s_new.astype(BF16))[j:j + 1, :])
    o = jnp.concatenate(rows, axis=0)
    o_ref[...] = _rms_gate(o, gn_ref[...], g_ref[...]).astype(o_ref.dtype)


def _gla_decode(proj_a, alr, wa2p, ba2, gn, state):
    nseq = proj_a.shape[0]
    h = GLA_HEADS
    r = GLA_DEC_ROWS
    return pl.pallas_call(
        _gla_dec_kernel,
        grid=(nseq // r, h),
        in_specs=[
            pl.BlockSpec((r, GLA_HK), lambda b, hh: (b, hh)),
            pl.BlockSpec((r, GLA_HK), lambda b, hh: (b, h + hh)),
            pl.BlockSpec((r, GLA_HV), lambda b, hh: (b, h + hh)),
            pl.BlockSpec((r, GLA_HV), lambda b, hh: (b, 2 * h + hh)),
            pl.BlockSpec((r, LANES), lambda b, hh: (b, 0)),
            pl.BlockSpec((LANES, GLA_HK), lambda b, hh: (0, hh)),
            pl.BlockSpec((1, GLA_HK), lambda b, hh: (0, hh)),
            pl.BlockSpec((1, GLA_HV), lambda b, hh: (0, 0)),
            pl.BlockSpec((r, None, GLA_HK, GLA_HV), lambda b, hh: (b, hh, 0, 0)),
        ],
        out_specs=[
            pl.BlockSpec((r, GLA_HV), lambda b, hh: (b, hh)),
            pl.BlockSpec((r, None, GLA_HK, GLA_HV), lambda b, hh: (b, hh, 0, 0)),
        ],
        out_shape=[
            jax.ShapeDtypeStruct((nseq, h * GLA_HV), F32),
            jax.ShapeDtypeStruct(state.shape, F32),
        ],
        compiler_params=_params("parallel", "parallel"),
        name="gla_decode",
    )(proj_a, proj_a, proj_a, proj_a, alr, wa2p, ba2, gn, state)


def _shift_rows(x, halo, ri):
    p1 = jnp.where(ri == 0, halo[7:8, :], pltpu.roll(x, 1, 0))
    p2 = jnp.where(ri == 0, halo[6:7, :], jnp.where(ri == 1, halo[7:8, :], pltpu.roll(x, 2, 0)))
    return p1, p2


def _conv_kernel(cb_ref, cc_ref, ch_ref, cch_ref, chh_ref, w_ref, o_ref, st_ref):
    tl = cb_ref.shape[0]
    cch = cc_ref[...] * ch_ref[...]
    halo = cch_ref[...] * chh_ref[...]
    halo = jnp.where(pl.program_id(1) == 0, 0.0, halo)
    ri = lax.broadcasted_iota(jnp.int32, (tl, 1), 0)
    p1, p2 = _shift_rows(cch, halo, ri)
    w = w_ref[...]
    u = w[0:1, :] * p2 + w[1:2, :] * p1 + w[2:3, :] * cch
    o_ref[...] = (cb_ref[...] * u).astype(o_ref.dtype)
    st_ref[...] = cch[tl - 2:tl, :]


def _conv_prompt(proj_b, w_conv, nb, seq):
    tl = min(512, seq)
    nt = seq // tl
    t8 = tl // 8
    cd = CONV_DIM
    halo = lambda col: (lambda b, t: (jnp.maximum((b * nt + t) * t8 - 1, 0), col))
    return pl.pallas_call(
        _conv_kernel,
        grid=(nb, nt),
        in_specs=[
            pl.BlockSpec((tl, cd), lambda b, t: (b * nt + t, 0)),
            pl.BlockSpec((tl, cd), lambda b, t: (b * nt + t, 1)),
            pl.BlockSpec((tl, cd), lambda b, t: (b * nt + t, 2)),
            pl.BlockSpec((8, cd), halo(1)),
            pl.BlockSpec((8, cd), halo(2)),
            pl.BlockSpec((3, cd), lambda b, t: (0, 0)),
        ],
        out_specs=[
            pl.BlockSpec((tl, cd), lambda b, t: (b * nt + t, 0)),
            pl.BlockSpec((None, 2, cd), lambda b, t: (b, 0, 0)),
        ],
        out_shape=[
            jax.ShapeDtypeStruct((nb * seq, cd), BF16),
            jax.ShapeDtypeStruct((nb, 2, cd), F32),
        ],
        compiler_params=_params("parallel", "arbitrary"),
        name="conv_prompt",
    )(proj_b, proj_b, proj_b, proj_b, proj_b, w_conv)


def _conv_dec_kernel(cb_ref, cc_ref, ch_ref, s0_ref, s1_ref, w_ref, o_ref, st_ref):
    cd = cb_ref.shape[1]
    cch = cc_ref[...] * ch_ref[...]
    s1 = s1_ref[...]
    w = w_ref[...]
    u = w[0:1, :] * s0_ref[...] + w[1:2, :] * s1 + w[2:3, :] * cch
    o_ref[...] = (cb_ref[...] * u).astype(o_ref.dtype)
    st_ref[:, :cd] = s1
    st_ref[:, cd:] = cch


def _conv_decode(proj_b, w_conv, state2d):
    n = proj_b.shape[0]
    cd = CONV_DIM
    blk = lambda col: pl.BlockSpec((n, cd), lambda i: (0, col))
    return pl.pallas_call(
        _conv_dec_kernel,
        grid=(1,),
        in_specs=[blk(0), blk(1), blk(2), blk(0), blk(1), pl.BlockSpec((3, cd), lambda i: (0, 0))],
        out_specs=[pl.BlockSpec((n, cd), lambda i: (0, 0)), pl.BlockSpec((n, 2 * cd), lambda i: (0, 0))],
        out_shape=[jax.ShapeDtypeStruct((n, cd), BF16), jax.ShapeDtypeStruct((n, 2 * cd), F32)],
        compiler_params=_params("arbitrary"),
        name="conv_decode",
    )(proj_b, proj_b, proj_b, state2d, state2d, w_conv)


def _softmax_rows(logits):
    m = jnp.max(logits, axis=-1, keepdims=True)
    p = jnp.exp(logits - m)
    return p / jnp.sum(p, axis=-1, keepdims=True)


def _memattn_kernel(q_ref, k_ref, v_ref, o_ref):
    logits = _dot(q_ref[...].astype(BF16), k_ref[...].astype(BF16), NT) * (MEM_HD ** -0.5)
    p = _softmax_rows(logits)
    o_ref[...] = _dot(p.astype(BF16), v_ref[...].astype(BF16)).astype(o_ref.dtype)


def _memattn_prompt(proj_b, mk, mv, nb, seq):
    tl = min(512, seq)
    nt = seq // tl
    h = MEM_HEADS
    qcol = 3 * CONV_DIM // MEM_HD
    return pl.pallas_call(
        _memattn_kernel,
        grid=(nb, h, nt),
        in_specs=[
            pl.BlockSpec((tl, MEM_HD), lambda b, hh, t: (b * nt + t, qcol + hh)),
            pl.BlockSpec((N_MEM, MEM_HD), lambda b, hh, t: (b, hh)),
            pl.BlockSpec((N_MEM, MEM_HD), lambda b, hh, t: (b, hh)),
        ],
        out_specs=pl.BlockSpec((tl, MEM_HD), lambda b, hh, t: (b * nt + t, hh)),
        out_shape=jax.ShapeDtypeStruct((nb * seq, MEM_DIM), BF16),
        compiler_params=_params("parallel", "parallel", "parallel"),
        name="memattn_prompt",
    )(proj_b, mk, mv)


MEMATTN_DEC_ROWS = 4


def _memattn_dec_kernel(q_ref, k_ref, v_ref, o_ref):
    def body(i, carry):
        q = q_ref[i] * (MEM_HD ** -0.5)
        logits = jnp.sum(k_ref[i] * q[None], axis=-1, keepdims=True)
        e = jnp.exp(logits - jnp.max(logits, axis=0, keepdims=True))
        p = e / jnp.sum(e, axis=0, keepdims=True)
        o_ref[i] = jnp.sum(p * v_ref[i], axis=0)
        return carry

    lax.fori_loop(0, q_ref.shape[0], body, 0)


def _memattn_decode(q3, k4, v4):
    n = q3.shape[0]
    r = MEMATTN_DEC_ROWS
    return pl.pallas_call(
        _memattn_dec_kernel,
        grid=(n // r,),
        in_specs=[
            pl.BlockSpec((r, MEM_HEADS, MEM_HD), lambda i: (i, 0, 0)),
            pl.BlockSpec((r, N_MEM, MEM_HEADS, MEM_HD), lambda i: (i, 0, 0, 0)),
            pl.BlockSpec((r, N_MEM, MEM_HEADS, MEM_HD), lambda i: (i, 0, 0, 0)),
        ],
        out_specs=pl.BlockSpec((r, MEM_HEADS, MEM_HD), lambda i: (i, 0, 0)),
        out_shape=jax.ShapeDtypeStruct((n, MEM_HEADS, MEM_HD), F32),
        compiler_params=_params("parallel"),
        name="memattn_decode",
    )(q3, k4, v4)


def _merge_kernel(a_ref, b_ref, m_ref, wa_ref, wb_ref, wm_ref, za_ref, zb_ref, zm_ref, o_ref):
    ya = _dot(a_ref[...].astype(BF16), wa_ref[...])
    yb = _dot(b_ref[...].astype(BF16), wb_ref[...])
    ym = _dot(m_ref[...].astype(BF16), wm_ref[...])
    merged = _sigmoid(za_ref[...]) * ya + _sigmoid(zb_ref[...]) * yb + _sigmoid(zm_ref[...]) * ym
    o_ref[...] = merged.astype(o_ref.dtype)


def _merge(o_gla, cbu, om, w_gla_out, w_conv_out, w_mem_out, proj_b):
    m = o_gla.shape[0]
    tm = min(1024, m)
    tn = 512
    d = D_MODEL
    zoff = (3 * CONV_DIM + MEM_DIM) // tn
    zspec = lambda g: pl.BlockSpec((tm, tn), lambda i, j: (i, zoff + g * (d // tn) + j))
    return pl.pallas_call(
        _merge_kernel,
        grid=(m // tm, d // tn),
        in_specs=[
            pl.BlockSpec((tm, o_gla.shape[1]), lambda i, j: (i, 0)),
            pl.BlockSpec((tm, cbu.shape[1]), lambda i, j: (i, 0)),
            pl.BlockSpec((tm, om.shape[1]), lambda i, j: (i, 0)),
            pl.BlockSpec((w_gla_out.shape[0], tn), lambda i, j: (0, j)),
            pl.BlockSpec((w_conv_out.shape[0], tn), lambda i, j: (0, j)),
            pl.BlockSpec((w_mem_out.shape[0], tn), lambda i, j: (0, j)),
            zspec(0), zspec(1), zspec(2),
        ],
        out_specs=pl.BlockSpec((tm, tn), lambda i, j: (i, j)),
        out_shape=jax.ShapeDtypeStruct((m, d), BF16),
        compiler_params=_params("parallel", "parallel"),
        name="merge",
    )(o_gla, cbu, om, w_gla_out, w_conv_out, w_mem_out, proj_b, proj_b, proj_b)


def _wo_ln_kernel(x_ref, m_ref, w_ref, g_ref, b_ref, o_ref):
    tm = x_ref.shape[0]
    half = tm // 2 if tm % 256 == 0 else tm
    for r in range(0, tm, half):
        rows = pl.ds(r, half)
        y = DN_ALPHA * x_ref[rows, :] + _dot(m_ref[rows, :], w_ref[...])
        o_ref[rows, :] = _layer_norm(y, g_ref[...], b_ref[...])


def _wo_ln(x, merged, w_o, g, b):
    m, d = x.shape
    tm = min(512, m)
    return pl.pallas_call(
        _wo_ln_kernel,
        grid=(m // tm,),
        in_specs=[
            pl.BlockSpec((tm, d), lambda i: (i, 0)),
            pl.BlockSpec((tm, d), lambda i: (i, 0)),
            pl.BlockSpec((d, d), lambda i: (0, 0)),
            pl.BlockSpec((1, d), lambda i: (0, 0)),
            pl.BlockSpec((1, d), lambda i: (0, 0)),
        ],
        out_specs=pl.BlockSpec((tm, d), lambda i: (i, 0)),
        out_shape=jax.ShapeDtypeStruct((m, d), F32),
        compiler_params=_params("parallel"),
        name="wo_ln",
    )(x, merged, w_o, g, b)


def _ffn_tail(hc, hu, wd_ref, acc_ref, x_ref, g_ref, b_ref, o_ref, fi):
    tf = hc.shape[1]
    valid = D_FF - fi * tf
    h = hc * _sigmoid(hc) * hu
    h = jnp.where(lax.broadcasted_iota(jnp.int32, (1, tf), 1) < valid, h, 0.0).astype(BF16)
    wd = jnp.where(lax.broadcasted_iota(jnp.int32, (tf, 1), 0) < valid, wd_ref[...], 0.0)
    acc_ref[...] += _dot(h, wd)

    @pl.when(fi == pl.num_programs(1) - 1)
    def _():
        o_ref[...] = _layer_norm(DN_ALPHA * x_ref[...] + acc_ref[...], g_ref[...], b_ref[...])


def _ffn_kernel(seq, x_ref, xh_ref, wg_ref, wu_ref, wd_ref, wc_ref, bc_ref, g_ref, b_ref, o_ref,
                xb_scr, acc_scr):
    fi = pl.program_id(1)
    tm = x_ref.shape[0]

    @pl.when(fi == 0)
    def _():
        xb_scr[...] = x_ref[...].astype(BF16)
        acc_scr[...] = jnp.zeros_like(acc_scr)

    xb = xb_scr[...]
    wg = wg_ref[...]
    hg = _dot(xb, wg)
    halo = _dot(xh_ref[...].astype(BF16), wg)
    halo = jnp.where((pl.program_id(0) * tm) % seq == 0, 0.0, halo)
    ri = lax.broadcasted_iota(jnp.int32, (tm, 1), 0)
    p1, p2 = _shift_rows(hg, halo, ri)
    wc = wc_ref[...]
    hc = wc[0:1, :] * p2 + wc[1:2, :] * p1 + wc[2:3, :] * hg + bc_ref[...]
    _ffn_tail(hc, _dot(xb, wu_ref[...]), wd_ref, acc_scr, x_ref, g_ref, b_ref, o_ref, fi)


def _ffn_prompt(x1, w_gate, w_up, w_down, w_conv, b_conv, g, b, seq):
    m, d = x1.shape
    tm = min(512, seq)
    tf = FF_TILE
    t8 = tm // 8
    return pl.pallas_call(
        functools.partial(_ffn_kernel, seq),
        grid=(m // tm, FF_TILES),
        in_specs=[
            pl.BlockSpec((tm, d), lambda i, f: (i, 0)),
            pl.BlockSpec((8, d), lambda i, f: (jnp.maximum(i * t8 - 1, 0), 0)),
            pl.BlockSpec((d, tf), lambda i, f: (0, f)),
            pl.BlockSpec((d, tf), lambda i, f: (0, f)),
            pl.BlockSpec((tf, d), lambda i, f: (f, 0)),
            pl.BlockSpec((3, tf), lambda i, f: (0, f)),
            pl.BlockSpec((1, tf), lambda i, f: (0, f)),
            pl.BlockSpec((1, d), lambda i, f: (0, 0)),
            pl.BlockSpec((1, d), lambda i, f: (0, 0)),
        ],
        out_specs=pl.BlockSpec((tm, d), lambda i, f: (i, 0)),
        out_shape=jax.ShapeDtypeStruct((m, d), F32),
        scratch_shapes=[pltpu.VMEM((tm, d), BF16), pltpu.VMEM((tm, d), F32)],
        compiler_params=_params("parallel", "arbitrary"),
        name="ffn_prompt",
    )(x1, x1, w_gate, w_up, w_down, w_conv, b_conv, g, b)


def _ffn_dec_kernel(x_ref, s0_ref, s1_ref, wg_ref, wu_ref, wd_ref, wc_ref, bc_ref, g_ref, b_ref,
                    o_ref, hg_ref, xb_scr, acc_scr):
    fi = pl.program_id(1)

    @pl.when(fi == 0)
    def _():
        xb_scr[...] = x_ref[...].astype(BF16)
        acc_scr[...] = jnp.zeros_like(acc_scr)

    xb = xb_scr[...]
    hg = _dot(xb, wg_ref[...])
    hg_ref[...] = hg
    wc = wc_ref[...]
    hc = wc[0:1, :] * s0_ref[...] + wc[1:2, :] * s1_ref[...] + wc[2:3, :] * hg + bc_ref[...]
    _ffn_tail(hc, _dot(xb, wu_ref[...]), wd_ref, acc_scr, x_ref, g_ref, b_ref, o_ref, fi)


def _ffn_decode(x1, s0, s1, w_gate, w_up, w_down, w_conv, b_conv, g, b):
    m, d = x1.shape
    tf = FF_TILE
    return pl.pallas_call(
        _ffn_dec_kernel,
        grid=(1, FF_TILES),
        in_specs=[
            pl.BlockSpec((m, d), lambda i, f: (0, 0)),
            pl.BlockSpec((m, tf), lambda i, f: (0, f)),
            pl.BlockSpec((m, tf), lambda i, f: (0, f)),
            pl.BlockSpec((d, tf), lambda i, f: (0, f)),
            pl.BlockSpec((d, tf), lambda i, f: (0, f)),
            pl.BlockSpec((tf, d), lambda i, f: (f, 0)),
            pl.BlockSpec((3, tf), lambda i, f: (0, f)),
            pl.BlockSpec((1, tf), lambda i, f: (0, f)),
            pl.BlockSpec((1, d), lambda i, f: (0, 0)),
            pl.BlockSpec((1, d), lambda i, f: (0, 0)),
        ],
        out_specs=[pl.BlockSpec((m, d), lambda i, f: (0, 0)), pl.BlockSpec((m, tf), lambda i, f: (0, f))],
        out_shape=[jax.ShapeDtypeStruct((m, d), F32), jax.ShapeDtypeStruct((m, D_FF), F32)],
        scratch_shapes=[pltpu.VMEM((m, d), BF16), pltpu.VMEM((m, d), F32)],
        compiler_params=_params("parallel", "arbitrary"),
        name="ffn_decode",
    )(x1, s0, s1, w_gate, w_up, w_down, w_conv, b_conv, g, b)


def kernel(x_prompt, x_sample, mem_prompt, cache_mem_k, cache_mem_v, state_gla, state_conv, state_ffn_conv, w_in, w_gla_a2, b_gla_a2, g_gla_norm, w_gla_out, w_conv, w_conv_out, w_mem_k, w_mem_v, w_mem_out, w_o, ln1_g, ln1_b, w_ffn_gate, w_ffn_up, w_ffn_conv, b_ffn_conv, w_ffn_down, ln2_g, ln2_b):
    nb, seq, d = x_prompt.shape
    ns = x_sample.shape[0]

    w_in_t = jnp.swapaxes(w_in[0], 0, 1)
    wa2p = jnp.pad(w_gla_a2[0], ((0, LANES - GLA_RANK), (0, 0)))
    ba2 = b_gla_a2
    gn = g_gla_norm
    w_go = w_gla_out[0].astype(BF16)
    w_co = w_conv_out[0].astype(BF16)
    w_mo = w_mem_out[0].astype(BF16)
    w_oo = w_o[0].astype(BF16)
    w_fg = w_ffn_gate[0].astype(BF16)
    w_fu = w_ffn_up[0].astype(BF16)
    w_fd = w_ffn_down[0].astype(BF16)
    w_fc = w_ffn_conv[0]
    b_fc = b_ffn_conv
    w_cv = w_conv[0]

    def mix_in(x2d, tag):
        xb = x2d.astype(BF16)
        return (_proj(xb, w_in_t, "proj_a_" + tag, 0, COLS_A, transposed=True),
                _proj(xb, w_in_t, "proj_r_" + tag, COLS_A, LANES, LANES, transposed=True),
                _proj(xb, w_in_t, "proj_b_" + tag, COLS_A, COLS_B, shift=GLA_RANK, transposed=True))

    def mix_out(x2d, o_gla, cbu, om, proj_b):
        merged = _merge(o_gla, cbu, om, w_go, w_co, w_mo, proj_b)
        return _wo_ln(x2d, merged, w_oo, ln1_g, ln1_b)

    xp = x_prompt.reshape(nb * seq, d)
    memb = mem_prompt.reshape(nb * N_MEM, d).astype(BF16)
    mk = _proj(memb, w_mem_k[0], "mem_k")
    mv = _proj(memb, w_mem_v[0], "mem_v")
    pa, pr, pb = mix_in(xp, "p")
    o_gla, p_gla = _gla_prompt(pa, pr, wa2p, ba2, gn, nb, seq)
    cbu, p_conv = _conv_prompt(pb, w_cv, nb, seq)
    om = _memattn_prompt(pb, mk, mv, nb, seq)
    x1 = mix_out(xp, o_gla, cbu, om, pb)
    yp = _ffn_prompt(x1, w_fg, w_fu, w_fd, w_fc, b_fc, ln2_g, ln2_b, seq)
    x1_tail = x1.reshape(nb, seq, d)[:, seq - 2:, :].reshape(nb * 2, d)
    p_ffn = _matmul(x1_tail, w_fg, F32, "ffn_state_p", tn=FF_TILE)

    xs = x_sample.reshape(ns, d)
    sa, sr, sb = mix_in(xs, "s")
    o_gla_s, s_gla = _gla_decode(sa, sr, wa2p, ba2, gn, state_gla[0])
    cbu_s, s_conv = _conv_decode(sb, w_cv, state_conv[0].reshape(ns, 2 * CONV_DIM))
    mq_s = sb[:, 3 * CONV_DIM:3 * CONV_DIM + MEM_DIM].reshape(ns, MEM_HEADS, MEM_HD)
    om_s = _memattn_decode(mq_s, cache_mem_k[0], cache_mem_v[0]).reshape(ns, MEM_DIM)
    x1s = mix_out(xs, o_gla_s, cbu_s, om_s, sb)
    f0 = state_ffn_conv[0][:, 0, :]
    f1 = state_ffn_conv[0][:, 1, :]
    ys, hg_s = _ffn_decode(x1s, f0, f1, w_fg, w_fu, w_fd, w_fc, b_fc, ln2_g, ln2_b)
    s_ffn = jnp.stack([f1, hg_s], axis=1)

    return (yp.reshape(nb, seq, d), ys.reshape(ns, 1, d),
            mk.reshape(1, nb, N_MEM, MEM_HEADS, MEM_HD), mv.reshape(1, nb, N_MEM, MEM_HEADS, MEM_HD),
            p_gla[None], p_conv[None], p_ffn.reshape(1, nb, 2, D_FF),
            s_gla[None], s_conv.reshape(1, ns, 2, CONV_DIM), s_ffn[None])
```

```python
import functools

import jax
import jax.numpy as jnp
from jax import lax
from jax.experimental import pallas as pl
from jax.experimental.pallas import tpu as pltpu

F32 = jnp.float32
BF16 = jnp.bfloat16

D_MODEL = 2048
GLA_HEADS = 4
GLA_HK = 256
GLA_HV = 512
GLA_RANK = 16
GLA_GATE_NORM = 16.0
CONV_DIM = 1024
N_MEM = 256
MEM_HEADS = 4
MEM_HD = 256
MEM_DIM = 1024
D_FF = 5504
DEPTH = 1
DN_ALPHA = (2 * DEPTH) ** 0.25
LN_EPS = 1e-5
RMS_EPS = 1e-6

LANES = 128
SUBLANES = 8
LOG2_E = 1.4426950408889634
FF_TILE = 512
FF_TILES = -(-D_FF // FF_TILE)
COLS_A = 2 * GLA_HEADS * GLA_HK + 2 * GLA_HEADS * GLA_HV
COLS_B = 3 * CONV_DIM + MEM_DIM + 3 * D_MODEL
GLA_CHUNK = 256
VMEM_LIMIT = 56 * 1024 * 1024

NN = (((1,), (0,)), ((), ()))
NT = (((1,), (1,)), ((), ()))
TN = (((0,), (0,)), ((), ()))


def _dot(a, b, dims=NN):
    return lax.dot_general(a, b, dims, preferred_element_type=F32)


def _params(*sem):
    return pltpu.CompilerParams(dimension_semantics=sem, vmem_limit_bytes=VMEM_LIMIT)


def _split2(x):
    hi = x.astype(BF16)
    lo = (x - hi.astype(F32)).astype(BF16)
    return hi, lo


def _split3(x):
    hi = x.astype(BF16)
    r = x - hi.astype(F32)
    mid = r.astype(BF16)
    lo = (r - mid.astype(F32)).astype(BF16)
    return hi, mid, lo


def _sigmoid(z):
    return 0.5 * jnp.tanh(0.5 * z) + 0.5


def _silu(z):
    h = 0.5 * z
    return h * jnp.tanh(h) + h


def _layer_norm(y, g, b):
    mu = jnp.mean(y, axis=-1, keepdims=True)
    d = y - mu
    var = jnp.mean(d * d, axis=-1, keepdims=True)
    return d * lax.rsqrt(var + LN_EPS) * g + b


def _log_decay(alr, wa2, ba2):
    ah, al = _split2(alr)
    wh, wl = _split2(wa2)
    z = _dot(ah, wh) + _dot(ah, wl) + _dot(al, wh) + ba2
    return (jnp.minimum(z, 0.0) - jnp.log(1.0 + jnp.exp(-jnp.abs(z)))) * (LOG2_E / GLA_GATE_NORM)


def _rms_gate(o, gn, g):
    o = o * lax.rsqrt(jnp.mean(o * o, axis=-1, keepdims=True) + RMS_EPS) * gn
    return o * _silu(g)


def _mm_kernel(x_ref, w_ref, o_ref):
    o_ref[...] = _dot(x_ref[...].astype(BF16), w_ref[...]).astype(o_ref.dtype)


def _matmul(x, w, out_dtype, name, tn=1024):
    m, k = x.shape
    n = w.shape[1]
    tm = min(m, 1024)
    tn = min(n, tn)
    return pl.pallas_call(
        _mm_kernel,
        grid=(m // tm, pl.cdiv(n, tn)),
        in_specs=[pl.BlockSpec((tm, k), lambda i, j: (i, 0)),
                  pl.BlockSpec((k, tn), lambda i, j: (0, j))],
        out_specs=pl.BlockSpec((tm, tn), lambda i, j: (i, j)),
        out_shape=jax.ShapeDtypeStruct((m, n), out_dtype),
        compiler_params=_params("parallel", "parallel"),
        name=name,
    )(x, w)


PROJ_SHIFT_ROWS = 16


def _proj_kernel(transposed, shift, x_ref, w_ref, *rest):
    if shift:
        wx_ref, o_ref, wb_scr = rest
    else:
        o_ref, wb_scr = rest

    @pl.when(pl.program_id(1) == 0)
    def _():
        w = w_ref[...]
        if shift:
            w = jnp.concatenate([w[shift:], wx_ref[:shift]], axis=0)
        if transposed:
            w = w.T
        wb_scr[...] = w.astype(BF16)

    o_ref[...] = _dot(x_ref[...].astype(BF16), wb_scr[...]).astype(o_ref.dtype)


def _proj(x, w, name, col0=0, n=None, tn=1024, shift=0, transposed=False):
    m, k = x.shape
    if n is None:
        n = w.shape[0] if transposed else w.shape[1]
    tm = min(m, 1024)
    tn = min(n, tn)
    c0 = col0 // tn
    if transposed:
        w_spec = pl.BlockSpec((tn, k), lambda j, i: (c0 + j, 0))
    else:
        assert not shift
        w_spec = pl.BlockSpec((k, tn), lambda j, i: (0, c0 + j))
    in_specs = [pl.BlockSpec((tm, k), lambda j, i: (i, 0)), w_spec]
    args = [x, w]
    if shift:
        r = PROJ_SHIFT_ROWS
        assert shift <= r and shift % 8 == 0 and tn % r == 0 and col0 % r == 0
        in_specs.append(pl.BlockSpec((r, k), lambda j, i: (col0 // r + (j + 1) * (tn // r), 0)))
        args.append(w)
    return pl.pallas_call(
        functools.partial(_proj_kernel, transposed, shift),
        grid=(n // tn, m // tm),
        in_specs=in_specs,
        out_specs=pl.BlockSpec((tm, tn), lambda j, i: (i, j)),
        out_shape=jax.ShapeDtypeStruct((m, n), F32),
        scratch_shapes=[pltpu.VMEM((k, tn), BF16)],
        compiler_params=_params("parallel", "arbitrary"),
        name=name,
    )(*args)


def _gla_levels(c):
    return c.bit_length() - 1


def _init_gla_masks(mask_scr, tri_scr):
    c = tri_scr.shape[0]
    rr = lax.broadcasted_iota(jnp.int32, (c, c), 0)
    cc = lax.broadcasted_iota(jnp.int32, (c, c), 1)
    tri_scr[...] = (rr >= cc).astype(BF16)
    mask_scr[0] = (rr == cc).astype(F32)
    for l in range(_gla_levels(c)):
        b = 1 << l
        pair = (rr // (2 * b) == cc // (2 * b)) & ((rr // b) % 2 == 1) & ((cc // b) % 2 == 0)
        mask_scr[1 + l] = pair.astype(F32)


def _intra_scores(q, k, la, bc, mask_scr):
    c, dk = q.shape
    ri = lax.broadcasted_iota(jnp.int32, (c, 1), 0)
    a = jnp.sum(q * k, axis=1, keepdims=True) * mask_scr[0]
    for l in range(_gla_levels(c)):
        b = 1 << l
        if b < SUBLANES:
            upper = (ri // b) % 2 == 1
            if b == 1:
                e = jnp.where(upper, la, 0.0)
            elif b == 2:
                m4 = ri % 4
                la_prev = pltpu.roll(la, 1, 0)
                la_next = pltpu.roll(la, c - 1, 0)
                e = jnp.where(m4 == 2, la, jnp.where(m4 == 3, la + la_prev, jnp.where(m4 == 0, la_next, 0.0)))
            else:
                ref = bc.reshape(c // (2 * b), 2 * b, dk)[:, b - 1:b, :]
                ref = jnp.broadcast_to(ref, (c // (2 * b), 2 * b, dk)).reshape(c, dk)
                dlt = bc - ref
                e = jnp.minimum(dlt, -dlt)
            w = jnp.where(upper, q, k) * jnp.exp2(e)
        else:
            parts = []
            for r0 in range(0, c, 2 * b):
                ref = bc[r0 + b - 1:r0 + b, :]
                parts.append(k[r0:r0 + b] * jnp.exp2(ref - bc[r0:r0 + b]))
                parts.append(q[r0 + b:r0 + 2 * b] * jnp.exp2(bc[r0 + b:r0 + 2 * b] - ref))
            w = jnp.concatenate(parts, axis=0)
        w = w.astype(BF16)
        a = a + _dot(w, w, NT) * mask_scr[1 + l]
    return a


def _gla_kernel(q_ref, k_ref, v_ref, g_ref, alr_ref, wa2_ref, ba2_ref, gn_ref, o_ref, sfin_ref,
                s_scr, mask_scr, tri_scr):
    ci = pl.program_id(2)

    @pl.when(ci == 0)
    def _():
        s_scr[...] = jnp.zeros_like(s_scr)
        _init_gla_masks(mask_scr, tri_scr)

    c = q_ref.shape[0]
    tri = tri_scr[...]
    ones = jnp.ones((c, LANES), BF16)
    alr = alr_ref[...]
    gn = gn_ref[...]
    for hh in range(s_scr.shape[0]):
        ks = slice(hh * GLA_HK, (hh + 1) * GLA_HK)
        vs = slice(hh * GLA_HV, (hh + 1) * GLA_HV)
        la = _log_decay(alr, wa2_ref[:, ks], ba2_ref[:, ks])
        lh, lm, ll = _split3(la)
        bc = _dot(tri, lh) + _dot(tri, lm) + _dot(tri, ll)
        b_last = bc[c - 1:c, :]
        bl_col = _dot(lh, ones, TN) + _dot(lm, ones, TN) + _dot(ll, ones, TN)
        dec_col = jnp.exp2(bl_col)
        dec_col = jnp.concatenate([dec_col] * (GLA_HV // LANES), axis=1)

        q = q_ref[:, ks] * (GLA_HK ** -0.5)
        k = k_ref[:, ks]
        vb = v_ref[:, vs].astype(BF16)
        s = s_scr[hh]
        o = _dot((q * jnp.exp2(bc)).astype(BF16), s.astype(BF16))
        a = _intra_scores(q, k, la, bc, mask_scr)
        o = o + _dot(a.astype(BF16), vb)
        kd = (k * jnp.exp2(b_last - bc)).astype(BF16)
        s_new = dec_col * s + _dot(kd, vb, TN)
        s_scr[hh] = s_new
        o_ref[:, vs] = _rms_gate(o, gn, g_ref[:, vs]).astype(o_ref.dtype)

    @pl.when(ci == pl.num_programs(2) - 1)
    def _():
        sfin_ref[...] = s_scr[...]


GLA_HEADS_PER_STEP = 2


def _gla_prompt(proj_a, alr, wa2p, ba2, gn, nb, seq):
    c = min(GLA_CHUNK, seq)
    nc = seq // c
    hp = GLA_HEADS_PER_STEP
    ng = GLA_HEADS // hp
    wk, wv = hp * GLA_HK, hp * GLA_HV
    row = lambda b, hg, ci: b * nc + ci
    return pl.pallas_call(
        _gla_kernel,
        grid=(nb, ng, nc),
        in_specs=[
            pl.BlockSpec((c, wk), lambda b, hg, ci: (row(b, hg, ci), hg)),
            pl.BlockSpec((c, wk), lambda b, hg, ci: (row(b, hg, ci), ng + hg)),
            pl.BlockSpec((c, wv), lambda b, hg, ci: (row(b, hg, ci), ng + hg)),
            pl.BlockSpec((c, wv), lambda b, hg, ci: (row(b, hg, ci), 2 * ng + hg)),
            pl.BlockSpec((c, LANES), lambda b, hg, ci: (row(b, hg, ci), 0)),
            pl.BlockSpec((LANES, wk), lambda b, hg, ci: (0, hg)),
            pl.BlockSpec((1, wk), lambda b, hg, ci: (0, hg)),
            pl.BlockSpec((1, GLA_HV), lambda b, hg, ci: (0, 0)),
        ],
        out_specs=[
            pl.BlockSpec((c, wv), lambda b, hg, ci: (row(b, hg, ci), hg)),
            pl.BlockSpec((None, hp, GLA_HK, GLA_HV), lambda b, hg, ci: (b, hg, 0, 0)),
        ],
        out_shape=[
            jax.ShapeDtypeStruct((nb * seq, GLA_HEADS * GLA_HV), BF16),
            jax.ShapeDtypeStruct((nb, GLA_HEADS, GLA_HK, GLA_HV), F32),
        ],
        scratch_shapes=[pltpu.VMEM((hp, GLA_HK, GLA_HV), F32),
                        pltpu.VMEM((1 + _gla_levels(c), c, c), F32),
                        pltpu.VMEM((c, c), BF16)],
        compiler_params=_params("parallel", "parallel", "arbitrary"),
        name="gla_prompt",
    )(proj_a, proj_a, proj_a, proj_a, alr, wa2p, ba2, gn)


GLA_DEC_ROWS = 8


def _gla_dec_kernel(q_ref, k_ref, v_ref, g_ref, alr_ref, wa2_ref, ba2_ref, gn_ref, s_ref, o_ref, so_ref):
    nb = q_ref.shape[0]
    la = _log_decay(alr_ref[...], wa2_ref[...], ba2_ref[...])
    rep = LANES // nb
    a_t = jnp.concatenate([jnp.exp2(la)] * rep, axis=0).T
    k_t = jnp.concatenate([k_ref[...]] * rep, axis=0).T
    qb = (q_ref[...] * (GLA_HK ** -0.5)).astype(BF16)
    v = v_ref[...]
    rows = []
    for j in range(nb):
        s_new = a_t[:, j:j + 1] * s_ref[j] + k_t[:, j:j + 1] * v[j:j + 1, :]
        so_ref[j] = s_new
        rows.append(_dot(qb, s_new.astype(BF16))[j:j + 1, :])
    o = jnp.concatenate(rows, axis=0)
    o_ref[...] = _rms_gate(o, gn_ref[...], g_ref[...]).astype(o_ref.dtype)


def _gla_decode(proj_a, alr, wa2p, ba2, gn, state):
    nseq = proj_a.shape[0]
    h = GLA_HEADS
    r = GLA_DEC_ROWS
    return pl.pallas_call(
        _gla_dec_kernel,
        grid=(nseq // r, h),
        in_specs=[
            pl.BlockSpec((r, GLA_HK), lambda b, hh: (b, hh)),
            pl.BlockSpec((r, GLA_HK), lambda b, hh: (b, h + hh)),
            pl.BlockSpec((r, GLA_HV), lambda b, hh: (b, h + hh)),
            pl.BlockSpec((r, GLA_HV), lambda b, hh: (b, 2 * h + hh)),
            pl.BlockSpec((r, LANES), lambda b, hh: (b, 0)),
            pl.BlockSpec((LANES, GLA_HK), lambda b, hh: (0, hh)),
            pl.BlockSpec((1, GLA_HK), lambda b, hh: (0, hh)),
            pl.BlockSpec((1, GLA_HV), lambda b, hh: (0, 0)),
            pl.BlockSpec((r, None, GLA_HK, GLA_HV), lambda b, hh: (b, hh, 0, 0)),
        ],
        out_specs=[
            pl.BlockSpec((r, GLA_HV), lambda b, hh: (b, hh)),
            pl.BlockSpec((r, None, GLA_HK, GLA_HV), lambda b, hh: (b, hh, 0, 0)),
        ],
        out_shape=[
            jax.ShapeDtypeStruct((nseq, h * GLA_HV), F32),
            jax.ShapeDtypeStruct(state.shape, F32),
        ],
        compiler_params=_params("parallel", "parallel"),
        name="gla_decode",
    )(proj_a, proj_a, proj_a, proj_a, alr, wa2p, ba2, gn, state)


def _shift_rows(x, halo, ri):
    p1 = jnp.where(ri == 0, halo[7:8, :], pltpu.roll(x, 1, 0))
    p2 = jnp.where(ri == 0, halo[6:7, :], jnp.where(ri == 1, halo[7:8, :], pltpu.roll(x, 2, 0)))
    return p1, p2


def _conv_kernel(cb_ref, cc_ref, ch_ref, cch_ref, chh_ref, w_ref, o_ref, st_ref):
    tl = cb_ref.shape[0]
    cch = cc_ref[...] * ch_ref[...]
    halo = cch_ref[...] * chh_ref[...]
    halo = jnp.where(pl.program_id(1) == 0, 0.0, halo)
    ri = lax.broadcasted_iota(jnp.int32, (tl, 1), 0)
    p1, p2 = _shift_rows(cch, halo, ri)
    w = w_ref[...]
    u = w[0:1, :] * p2 + w[1:2, :] * p1 + w[2:3, :] * cch
    o_ref[...] = (cb_ref[...] * u).astype(o_ref.dtype)
    st_ref[...] = cch[tl - 2:tl, :]


def _conv_prompt(proj_b, w_conv, nb, seq):
    tl = min(512, seq)
    nt = seq // tl
    t8 = tl // 8
    cd = CONV_DIM
    halo = lambda col: (lambda b, t: (jnp.maximum((b * nt + t) * t8 - 1, 0), col))
    return pl.pallas_call(
        _conv_kernel,
        grid=(nb, nt),
        in_specs=[
            pl.BlockSpec((tl, cd), lambda b, t: (b * nt + t, 0)),
            pl.BlockSpec((tl, cd), lambda b, t: (b * nt + t, 1)),
            pl.BlockSpec((tl, cd), lambda b, t: (b * nt + t, 2)),
            pl.BlockSpec((8, cd), halo(1)),
            pl.BlockSpec((8, cd), halo(2)),
            pl.BlockSpec((3, cd), lambda b, t: (0, 0)),
        ],
        out_specs=[
            pl.BlockSpec((tl, cd), lambda b, t: (b * nt + t, 0)),
            pl.BlockSpec((None, 2, cd), lambda b, t: (b, 0, 0)),
        ],
        out_shape=[
            jax.ShapeDtypeStruct((nb * seq, cd), BF16),
            jax.ShapeDtypeStruct((nb, 2, cd), F32),
        ],
        compiler_params=_params("parallel", "arbitrary"),
        name="conv_prompt",
    )(proj_b, proj_b, proj_b, proj_b, proj_b, w_conv)


def _conv_dec_kernel(cb_ref, cc_ref, ch_ref, s0_ref, s1_ref, w_ref, o_ref, st_ref):
    cd = cb_ref.shape[1]
    cch = cc_ref[...] * ch_ref[...]
    s1 = s1_ref[...]
    w = w_ref[...]
    u = w[0:1, :] * s0_ref[...] + w[1:2, :] * s1 + w[2:3, :] * cch
    o_ref[...] = (cb_ref[...] * u).astype(o_ref.dtype)
    st_ref[:, :cd] = s1
    st_ref[:, cd:] = cch


def _conv_decode(proj_b, w_conv, state2d):
    n = proj_b.shape[0]
    cd = CONV_DIM
    blk = lambda col: pl.BlockSpec((n, cd), lambda i: (0, col))
    return pl.pallas_call(
        _conv_dec_kernel,
        grid=(1,),
        in_specs=[blk(0), blk(1), blk(2), blk(0), blk(1), pl.BlockSpec((3, cd), lambda i: (0, 0))],
        out_specs=[pl.BlockSpec((n, cd), lambda i: (0, 0)), pl.BlockSpec((n, 2 * cd), lambda i: (0, 0))],
        out_shape=[jax.ShapeDtypeStruct((n, cd), BF16), jax.ShapeDtypeStruct((n, 2 * cd), F32)],
        compiler_params=_params("arbitrary"),
        name="conv_decode",
    )(proj_b, proj_b, proj_b, state2d, state2d, w_conv)


def _softmax_rows(logits):
    m = jnp.max(logits, axis=-1, keepdims=True)
    p = jnp.exp(logits - m)
    return p / jnp.sum(p, axis=-1, keepdims=True)


def _memattn_kernel(q_ref, k_ref, v_ref, o_ref):
    logits = _dot(q_ref[...].astype(BF16), k_ref[...].astype(BF16), NT) * (MEM_HD ** -0.5)
    p = _softmax_rows(logits)
    o_ref[...] = _dot(p.astype(BF16), v_ref[...].astype(BF16)).astype(o_ref.dtype)


def _memattn_prompt(proj_b, mk, mv, nb, seq):
    tl = min(512, seq)
    nt = seq // tl
    h = MEM_HEADS
    qcol = 3 * CONV_DIM // MEM_HD
    return pl.pallas_call(
        _memattn_kernel,
        grid=(nb, h, nt),
        in_specs=[
            pl.BlockSpec((tl, MEM_HD), lambda b, hh, t: (b * nt + t, qcol + hh)),
            pl.BlockSpec((N_MEM, MEM_HD), lambda b, hh, t: (b, hh)),
            pl.BlockSpec((N_MEM, MEM_HD), lambda b, hh, t: (b, hh)),
        ],
        out_specs=pl.BlockSpec((tl, MEM_HD), lambda b, hh, t: (b * nt + t, hh)),
        out_shape=jax.ShapeDtypeStruct((nb * seq, MEM_DIM), BF16),
        compiler_params=_params("parallel", "parallel", "parallel"),
        name="memattn_prompt",
    )(proj_b, mk, mv)


MEMATTN_DEC_ROWS = 4


MEM_HD_CHUNKS = MEM_HD // LANES
MEM_TILE_ROWS = MEM_HD_CHUNKS * MEM_HEADS
assert MEM_HD_CHUNKS == 2 and MEM_TILE_ROWS == 8


def _heads_to_tile(x):
    lead = x.shape[:-2]
    x = x.reshape(lead + (MEM_HEADS, MEM_HD_CHUNKS, LANES))
    return jnp.swapaxes(x, -3, -2).reshape(lead + (MEM_TILE_ROWS, LANES))


def _tile_to_heads(x):
    lead = x.shape[:-2]
    x = x.reshape(lead + (MEM_HD_CHUNKS, MEM_HEADS, LANES))
    return jnp.swapaxes(x, -3, -2).reshape(lead + (MEM_HEADS, MEM_HD))


def _memattn_dec_kernel(q_ref, k_ref, v_ref, o_ref):
    def body(i, carry):
        q = q_ref[i] * (MEM_HD ** -0.5)
        prod = k_ref[i] * q[None]
        prod = prod + pltpu.roll(prod, MEM_HEADS, 1)
        logits = jnp.sum(prod, axis=-1, keepdims=True)
        e = jnp.exp(logits - jnp.max(logits, axis=0, keepdims=True))
        o_ref[i] = jnp.sum(e * v_ref[i], axis=0) / jnp.sum(e, axis=0)
        return carry

    lax.fori_loop(0, q_ref.shape[0], body, 0, unroll=True)


def _memattn_decode(q_t, k_t, v_t):
    n = q_t.shape[0]
    r = MEMATTN_DEC_ROWS
    tile = (MEM_TILE_ROWS, LANES)
    return pl.pallas_call(
        _memattn_dec_kernel,
        grid=(n // r,),
        in_specs=[
            pl.BlockSpec((r,) + tile, lambda i: (i, 0, 0)),
            pl.BlockSpec((r, N_MEM) + tile, lambda i: (i, 0, 0, 0)),
            pl.BlockSpec((r, N_MEM) + tile, lambda i: (i, 0, 0, 0)),
        ],
        out_specs=pl.BlockSpec((r,) + tile, lambda i: (i, 0, 0)),
        out_shape=jax.ShapeDtypeStruct((n,) + tile, F32),
        compiler_params=_params("parallel"),
        name="memattn_decode",
    )(q_t, k_t, v_t)


def _merge_kernel(a_ref, b_ref, m_ref, wa_ref, wb_ref, wm_ref, za_ref, zb_ref, zm_ref, o_ref):
    ya = _dot(a_ref[...].astype(BF16), wa_ref[...])
    yb = _dot(b_ref[...].astype(BF16), wb_ref[...])
    ym = _dot(m_ref[...].astype(BF16), wm_ref[...])
    merged = _sigmoid(za_ref[...]) * ya + _sigmoid(zb_ref[...]) * yb + _sigmoid(zm_ref[...]) * ym
    o_ref[...] = merged.astype(o_ref.dtype)


def _merge(o_gla, cbu, om, w_gla_out, w_conv_out, w_mem_out, proj_b):
    m = o_gla.shape[0]
    tm = min(1024, m)
    tn = 512
    d = D_MODEL
    zoff = (3 * CONV_DIM + MEM_DIM) // tn
    zspec = lambda g: pl.BlockSpec((tm, tn), lambda i, j: (i, zoff + g * (d // tn) + j))
    return pl.pallas_call(
        _merge_kernel,
        grid=(m // tm, d // tn),
        in_specs=[
            pl.BlockSpec((tm, o_gla.shape[1]), lambda i, j: (i, 0)),
            pl.BlockSpec((tm, cbu.shape[1]), lambda i, j: (i, 0)),
            pl.BlockSpec((tm, om.shape[1]), lambda i, j: (i, 0)),
            pl.BlockSpec((w_gla_out.shape[0], tn), lambda i, j: (0, j)),
            pl.BlockSpec((w_conv_out.shape[0], tn), lambda i, j: (0, j)),
            pl.BlockSpec((w_mem_out.shape[0], tn), lambda i, j: (0, j)),
            zspec(0), zspec(1), zspec(2),
        ],
        out_specs=pl.BlockSpec((tm, tn), lambda i, j: (i, j)),
        out_shape=jax.ShapeDtypeStruct((m, d), BF16),
        compiler_params=_params("parallel", "parallel"),
        name="merge",
    )(o_gla, cbu, om, w_gla_out, w_conv_out, w_mem_out, proj_b, proj_b, proj_b)


def _wo_ln_kernel(x_ref, m_ref, w_ref, g_ref, b_ref, o_ref):
    tm = x_ref.shape[0]
    half = tm // 2 if tm % 256 == 0 else tm
    for r in range(0, tm, half):
        rows = pl.ds(r, half)
        y = DN_ALPHA * x_ref[rows, :] + _dot(m_ref[rows, :], w_ref[...])
        o_ref[rows, :] = _layer_norm(y, g_ref[...], b_ref[...])


def _wo_ln(x, merged, w_o, g, b):
    m, d = x.shape
    tm = min(512, m)
    return pl.pallas_call(
        _wo_ln_kernel,
        grid=(m // tm,),
        in_specs=[
            pl.BlockSpec((tm, d), lambda i: (i, 0)),
            pl.BlockSpec((tm, d), lambda i: (i, 0)),
            pl.BlockSpec((d, d), lambda i: (0, 0)),
            pl.BlockSpec((1, d), lambda i: (0, 0)),
            pl.BlockSpec((1, d), lambda i: (0, 0)),
        ],
        out_specs=pl.BlockSpec((tm, d), lambda i: (i, 0)),
        out_shape=jax.ShapeDtypeStruct((m, d), F32),
        compiler_params=_params("parallel"),
        name="wo_ln",
    )(x, merged, w_o, g, b)


def _ffn_tail(hc, hu, wd_ref, acc_ref, x_ref, g_ref, b_ref, o_ref, fi):
    tf = hc.shape[1]
    valid = D_FF - fi * tf
    h = hc * _sigmoid(hc) * hu
    h = jnp.where(lax.broadcasted_iota(jnp.int32, (1, tf), 1) < valid, h, 0.0).astype(BF16)
    wd = jnp.where(lax.broadcasted_iota(jnp.int32, (tf, 1), 0) < valid, wd_ref[...], 0.0)
    acc_ref[...] += _dot(h, wd)

    @pl.when(fi == pl.num_programs(1) - 1)
    def _():
        o_ref[...] = _layer_norm(DN_ALPHA * x_ref[...] + acc_ref[...], g_ref[...], b_ref[...])


def _ffn_kernel(seq, x_ref, xh_ref, wg_ref, wu_ref, wd_ref, wc_ref, bc_ref, g_ref, b_ref, o_ref,
                xb_scr, acc_scr):
    fi = pl.program_id(1)
    tm = x_ref.shape[0]

    @pl.when(fi == 0)
    def _():
        xb_scr[...] = x_ref[...].astype(BF16)
        acc_scr[...] = jnp.zeros_like(acc_scr)

    xb = xb_scr[...]
    wg = wg_ref[...]
    hg = _dot(xb, wg)
    halo = _dot(xh_ref[...].astype(BF16), wg)
    halo = jnp.where((pl.program_id(0) * tm) % seq == 0, 0.0, halo)
    ri = lax.broadcasted_iota(jnp.int32, (tm, 1), 0)
    p1, p2 = _shift_rows(hg, halo, ri)
    wc = wc_ref[...]
    hc = wc[0:1, :] * p2 + wc[1:2, :] * p1 + wc[2:3, :] * hg + bc_ref[...]
    _ffn_tail(hc, _dot(xb, wu_ref[...]), wd_ref, acc_scr, x_ref, g_ref, b_ref, o_ref, fi)


def _ffn_prompt(x1, w_gate, w_up, w_down, w_conv, b_conv, g, b, seq):
    m, d = x1.shape
    tm = min(512, seq)
    tf = FF_TILE
    t8 = tm // 8
    return pl.pallas_call(
        functools.partial(_ffn_kernel, seq),
        grid=(m // tm, FF_TILES),
        in_specs=[
            pl.BlockSpec((tm, d), lambda i, f: (i, 0)),
            pl.BlockSpec((8, d), lambda i, f: (jnp.maximum(i * t8 - 1, 0), 0)),
            pl.BlockSpec((d, tf), lambda i, f: (0, f)),
            pl.BlockSpec((d, tf), lambda i, f: (0, f)),
            pl.BlockSpec((tf, d), lambda i, f: (f, 0)),
            pl.BlockSpec((3, tf), lambda i, f: (0, f)),
            pl.BlockSpec((1, tf), lambda i, f: (0, f)),
            pl.BlockSpec((1, d), lambda i, f: (0, 0)),
            pl.BlockSpec((1, d), lambda i, f: (0, 0)),
        ],
        out_specs=pl.BlockSpec((tm, d), lambda i, f: (i, 0)),
        out_shape=jax.ShapeDtypeStruct((m, d), F32),
        scratch_shapes=[pltpu.VMEM((tm, d), BF16), pltpu.VMEM((tm, d), F32)],
        compiler_params=_params("parallel", "arbitrary"),
        name="ffn_prompt",
    )(x1, x1, w_gate, w_up, w_down, w_conv, b_conv, g, b)


def _ffn_dec_kernel(x_ref, s0_ref, s1_ref, wg_ref, wu_ref, wd_ref, wc_ref, bc_ref, g_ref, b_ref,
                    o_ref, hg_ref, xb_scr, acc_scr):
    fi = pl.program_id(1)

    @pl.when(fi == 0)
    def _():
        xb_scr[...] = x_ref[...].astype(BF16)
        acc_scr[...] = jnp.zeros_like(acc_scr)

    xb = xb_scr[...]
    hg = _dot(xb, wg_ref[...])
    hg_ref[...] = hg
    wc = wc_ref[...]
    hc = wc[0:1, :] * s0_ref[...] + wc[1:2, :] * s1_ref[...] + wc[2:3, :] * hg + bc_ref[...]
    _ffn_tail(hc, _dot(xb, wu_ref[...]), wd_ref, acc_scr, x_ref, g_ref, b_ref, o_ref, fi)


def _ffn_decode(x1, s0, s1, w_gate, w_up, w_down, w_conv, b_conv, g, b):
    m, d = x1.shape
    tf = FF_TILE
    return pl.pallas_call(
        _ffn_dec_kernel,
        grid=(1, FF_TILES),
        in_specs=[
            pl.BlockSpec((m, d), lambda i, f: (0, 0)),
            pl.BlockSpec((m, tf), lambda i, f: (0, f)),
            pl.BlockSpec((m, tf), lambda i, f: (0, f)),
            pl.BlockSpec((d, tf), lambda i, f: (0, f)),
            pl.BlockSpec((d, tf), lambda i, f: (0, f)),
            pl.BlockSpec((tf, d), lambda i, f: (f, 0)),
            pl.BlockSpec((3, tf), lambda i, f: (0, f)),
            pl.BlockSpec((1, tf), lambda i, f: (0, f)),
            pl.BlockSpec((1, d), lambda i, f: (0, 0)),
            pl.BlockSpec((1, d), lambda i, f: (0, 0)),
        ],
        out_specs=[pl.BlockSpec((m, d), lambda i, f: (0, 0)), pl.BlockSpec((m, tf), lambda i, f: (0, f))],
        out_shape=[jax.ShapeDtypeStruct((m, d), F32), jax.ShapeDtypeStruct((m, D_FF), F32)],
        scratch_shapes=[pltpu.VMEM((m, d), BF16), pltpu.VMEM((m, d), F32)],
        compiler_params=_params("parallel", "arbitrary"),
        name="ffn_decode",
    )(x1, s0, s1, w_gate, w_up, w_down, w_conv, b_conv, g, b)


def kernel(x_prompt, x_sample, mem_prompt, cache_mem_k, cache_mem_v, state_gla, state_conv, state_ffn_conv, w_in, w_gla_a2, b_gla_a2, g_gla_norm, w_gla_out, w_conv, w_conv_out, w_mem_k, w_mem_v, w_mem_out, w_o, ln1_g, ln1_b, w_ffn_gate, w_ffn_up, w_ffn_conv, b_ffn_conv, w_ffn_down, ln2_g, ln2_b):
    nb, seq, d = x_prompt.shape
    ns = x_sample.shape[0]

    w_in_t = jnp.swapaxes(w_in[0], 0, 1)
    wa2p = jnp.pad(w_gla_a2[0], ((0, LANES - GLA_RANK), (0, 0)))
    ba2 = b_gla_a2
    gn = g_gla_norm
    w_go = w_gla_out[0].astype(BF16)
    w_co = w_conv_out[0].astype(BF16)
    w_mo = w_mem_out[0].astype(BF16)
    w_oo = w_o[0].astype(BF16)
    w_fg = w_ffn_gate[0].astype(BF16)
    w_fu = w_ffn_up[0].astype(BF16)
    w_fd = w_ffn_down[0].astype(BF16)
    w_fc = w_ffn_conv[0]
    b_fc = b_ffn_conv
    w_cv = w_conv[0]

    def mix_in(x2d, tag):
        xb = x2d.astype(BF16)
        return (_proj(xb, w_in_t, "proj_a_" + tag, 0, COLS_A, transposed=True),
                _proj(xb, w_in_t, "proj_r_" + tag, COLS_A, LANES, LANES, transposed=True),
                _proj(xb, w_in_t, "proj_b_" + tag, COLS_A, COLS_B, shift=GLA_RANK, transposed=True))

    def mix_out(x2d, o_gla, cbu, om, proj_b):
        merged = _merge(o_gla, cbu, om, w_go, w_co, w_mo, proj_b)
        return _wo_ln(x2d, merged, w_oo, ln1_g, ln1_b)

    xp = x_prompt.reshape(nb * seq, d)
    memb = mem_prompt.reshape(nb * N_MEM, d).astype(BF16)
    mk = _proj(memb, w_mem_k[0], "mem_k")
    mv = _proj(memb, w_mem_v[0], "mem_v")
    pa, pr, pb = mix_in(xp, "p")
    o_gla, p_gla = _gla_prompt(pa, pr, wa2p, ba2, gn, nb, seq)
    cbu, p_conv = _conv_prompt(pb, w_cv, nb, seq)
    om = _memattn_prompt(pb, mk, mv, nb, seq)
    x1 = mix_out(xp, o_gla, cbu, om, pb)
    yp = _ffn_prompt(x1, w_fg, w_fu, w_fd, w_fc, b_fc, ln2_g, ln2_b, seq)
    x1_tail = x1.reshape(nb, seq, d)[:, seq - 2:, :].reshape(nb * 2, d)
    p_ffn = _matmul(x1_tail, w_fg, F32, "ffn_state_p", tn=FF_TILE)

    xs = x_sample.reshape(ns, d)
    sa, sr, sb = mix_in(xs, "s")
    o_gla_s, s_gla = _gla_decode(sa, sr, wa2p, ba2, gn, state_gla[0])
    cbu_s, s_conv = _conv_decode(sb, w_cv, state_conv[0].reshape(ns, 2 * CONV_DIM))
    mq_s = sb[:, 3 * CONV_DIM:3 * CONV_DIM + MEM_DIM].reshape(ns, MEM_HEADS, MEM_HD)
    om_s = _memattn_decode(_heads_to_tile(mq_s), _heads_to_tile(cache_mem_k[0]), _heads_to_tile(cache_mem_v[0]))
    om_s = _tile_to_heads(om_s).reshape(ns, MEM_DIM)
    x1s = mix_out(xs, o_gla_s, cbu_s, om_s, sb)
    f0 = state_ffn_conv[0][:, 0, :]
    f1 = state_ffn_conv[0][:, 1, :]
    ys, hg_s = _ffn_decode(x1s, f0, f1, w_fg, w_fu, w_fd, w_fc, b_fc, ln2_g, ln2_b)
    s_ffn = jnp.stack([f1, hg_s], axis=1)

    return (yp.reshape(nb, seq, d), ys.reshape(ns, 1, d),
            mk.reshape(1, nb, N_MEM, MEM_HEADS, MEM_HD), mv.reshape(1, nb, N_MEM, MEM_HEADS, MEM_HD),
            p_gla[None], p_conv[None], p_ffn.reshape(1, nb, 2, D_FF),
            s_gla[None], s_conv.reshape(1, ns, 2, CONV_DIM), s_ffn[None])
```

```python
import functools

import jax
import jax.numpy as jnp
from jax import lax
from jax.experimental import pallas as pl
from jax.experimental.pallas import tpu as pltpu

F32 = jnp.float32
BF16 = jnp.bfloat16

D_MODEL = 2048
GLA_HEADS = 4
GLA_HK = 256
GLA_HV = 512
GLA_RANK = 16
GLA_GATE_NORM = 16.0
CONV_DIM = 1024
N_MEM = 256
MEM_HEADS = 4
MEM_HD = 256
MEM_DIM = 1024
D_FF = 5504
DEPTH = 1
DN_ALPHA = (2 * DEPTH) ** 0.25
LN_EPS = 1e-5
RMS_EPS = 1e-6

LANES = 128
SUBLANES = 8
LOG2_E = 1.4426950408889634
FF_TILE = 512
FF_TILES = -(-D_FF // FF_TILE)
COLS_A = 2 * GLA_HEADS * GLA_HK + 2 * GLA_HEADS * GLA_HV
COLS_B = 3 * CONV_DIM + MEM_DIM + 3 * D_MODEL
GLA_CHUNK = 256
VMEM_LIMIT = 56 * 1024 * 1024

NN = (((1,), (0,)), ((), ()))
NT = (((1,), (1,)), ((), ()))
TN = (((0,), (0,)), ((), ()))


def _dot(a, b, dims=NN):
    return lax.dot_general(a, b, dims, preferred_element_type=F32)


def _params(*sem):
    return pltpu.CompilerParams(dimension_semantics=sem, vmem_limit_bytes=VMEM_LIMIT)


def _split2(x):
    hi = x.astype(BF16)
    lo = (x - hi.astype(F32)).astype(BF16)
    return hi, lo


def _split3(x):
    hi = x.astype(BF16)
    r = x - hi.astype(F32)
    mid = r.astype(BF16)
    lo = (r - mid.astype(F32)).astype(BF16)
    return hi, mid, lo


def _sigmoid(z):
    return 0.5 * jnp.tanh(0.5 * z) + 0.5


def _silu(z):
    h = 0.5 * z
    return h * jnp.tanh(h) + h


def _layer_norm(y, g, b):
    mu = jnp.mean(y, axis=-1, keepdims=True)
    d = y - mu
    var = jnp.mean(d * d, axis=-1, keepdims=True)
    return d * lax.rsqrt(var + LN_EPS) * g + b


def _log2_decay(z):
    return (jnp.minimum(z, 0.0) - jnp.log(1.0 + jnp.exp(-jnp.abs(z)))) * (LOG2_E / GLA_GATE_NORM)


def _log_decay(alr, wa2, ba2):
    ah, al = _split2(alr)
    wh, wl = _split2(wa2)
    return _log2_decay(_dot(ah, wh) + _dot(ah, wl) + _dot(al, wh) + ba2)


def _rms_gate(o, gn, g):
    o = o * lax.rsqrt(jnp.mean(o * o, axis=-1, keepdims=True) + RMS_EPS) * gn
    return o * _silu(g)


def _mm_kernel(x_ref, w_ref, o_ref):
    o_ref[...] = _dot(x_ref[...].astype(BF16), w_ref[...]).astype(o_ref.dtype)


def _matmul(x, w, out_dtype, name, tn=1024):
    m, k = x.shape
    n = w.shape[1]
    tm = min(m, 1024)
    tn = min(n, tn)
    return pl.pallas_call(
        _mm_kernel,
        grid=(m // tm, pl.cdiv(n, tn)),
        in_specs=[pl.BlockSpec((tm, k), lambda i, j: (i, 0)),
                  pl.BlockSpec((k, tn), lambda i, j: (0, j))],
        out_specs=pl.BlockSpec((tm, tn), lambda i, j: (i, j)),
        out_shape=jax.ShapeDtypeStruct((m, n), out_dtype),
        compiler_params=_params("parallel", "parallel"),
        name=name,
    )(x, w)


PROJ_SHIFT_ROWS = 16


def _proj_kernel(transposed, shift, rider, x_ref, w_ref, *rest):
    if shift:
        wx_ref, rest = rest[0], rest[1:]
    rider_refs = ()
    if rider:
        body, n_in = rider
        rider_refs, rest = rest[:n_in] + rest[n_in + 1:-1], (rest[n_in], rest[-1])
    o_ref, wb_scr = rest

    @pl.when(pl.program_id(1) == 0)
    def _():
        w = w_ref[...]
        if shift:
            w = jnp.concatenate([w[shift:], wx_ref[:shift]], axis=0)
        if transposed:
            w = w.T
        wb_scr[...] = w.astype(BF16)

    if rider:
        body(*rider_refs)
    o_ref[...] = _dot(x_ref[...].astype(BF16), wb_scr[...]).astype(o_ref.dtype)


def _proj(x, w, name, col0=0, n=None, tn=1024, shift=0, transposed=False, rider=None):
    m, k = x.shape
    if n is None:
        n = w.shape[0] if transposed else w.shape[1]
    tm = min(m, 1024)
    tn = min(n, tn)
    c0 = col0 // tn
    if transposed:
        w_spec = pl.BlockSpec((tn, k), lambda j, i: (c0 + j, 0))
    else:
        assert not shift
        w_spec = pl.BlockSpec((k, tn), lambda j, i: (0, c0 + j))
    in_specs = [pl.BlockSpec((tm, k), lambda j, i: (i, 0)), w_spec]
    args = [x, w]
    if shift:
        r = PROJ_SHIFT_ROWS
        assert shift <= r and shift % 8 == 0 and tn % r == 0 and col0 % r == 0
        in_specs.append(pl.BlockSpec((r, k), lambda j, i: (col0 // r + (j + 1) * (tn // r), 0)))
        args.append(w)
    out_specs = [pl.BlockSpec((tm, tn), lambda j, i: (i, j))]
    out_shape = [jax.ShapeDtypeStruct((m, n), F32)]
    kernel_rider = None
    if rider:
        body, specs_fn, operands = rider
        n_items, r_in_specs, r_args, r_out_specs, r_out_shape = specs_fn(lambda j, i: j * (m // tm) + i, *operands)
        assert n_items <= (n // tn) * (m // tm), "not enough projection steps to carry the rider's work items"
        in_specs += r_in_specs
        args += list(r_args)
        out_specs += r_out_specs
        out_shape += r_out_shape
        kernel_rider = (body, len(r_in_specs))
    outs = pl.pallas_call(
        functools.partial(_proj_kernel, transposed, shift, kernel_rider),
        grid=(n // tn, m // tm),
        in_specs=in_specs,
        out_specs=out_specs,
        out_shape=out_shape,
        scratch_shapes=[pltpu.VMEM((k, tn), BF16)],
        compiler_params=_params("arbitrary" if rider else "parallel", "arbitrary"),
        name=name,
    )(*args)
    return outs if rider else outs[0]


def _gla_levels(c):
    return c.bit_length() - 1


def _init_gla_masks(mask_scr, tri_scr):
    c = tri_scr.shape[0]
    rr = lax.broadcasted_iota(jnp.int32, (c, c), 0)
    cc = lax.broadcasted_iota(jnp.int32, (c, c), 1)
    tri_scr[...] = (rr >= cc).astype(BF16)
    mask_scr[0] = (rr == cc).astype(F32)
    for l in range(_gla_levels(c)):
        b = 1 << l
        pair = (rr // (2 * b) == cc // (2 * b)) & ((rr // b) % 2 == 1) & ((cc // b) % 2 == 0)
        mask_scr[1 + l] = pair.astype(F32)


def _intra_scores(q, k, la, bc, mask_scr):
    c, dk = q.shape
    ri = lax.broadcasted_iota(jnp.int32, (c, 1), 0)
    a = jnp.sum(q * k, axis=1, keepdims=True) * mask_scr[0]
    for l in range(_gla_levels(c)):
        b = 1 << l
        if b < SUBLANES:
            upper = (ri // b) % 2 == 1
            if b == 1:
                e = jnp.where(upper, la, 0.0)
            elif b == 2:
                m4 = ri % 4
                la_prev = pltpu.roll(la, 1, 0)
                la_next = pltpu.roll(la, c - 1, 0)
                e = jnp.where(m4 == 2, la, jnp.where(m4 == 3, la + la_prev, jnp.where(m4 == 0, la_next, 0.0)))
            else:
                ref = bc.reshape(c // (2 * b), 2 * b, dk)[:, b - 1:b, :]
                ref = jnp.broadcast_to(ref, (c // (2 * b), 2 * b, dk)).reshape(c, dk)
                dlt = bc - ref
                e = jnp.minimum(dlt, -dlt)
            w = jnp.where(upper, q, k) * jnp.exp2(e)
        else:
            parts = []
            for r0 in range(0, c, 2 * b):
                ref = bc[r0 + b - 1:r0 + b, :]
                parts.append(k[r0:r0 + b] * jnp.exp2(ref - bc[r0:r0 + b]))
                parts.append(q[r0 + b:r0 + 2 * b] * jnp.exp2(bc[r0 + b:r0 + 2 * b] - ref))
            w = jnp.concatenate(parts, axis=0)
        w = w.astype(BF16)
        a = a + _dot(w, w, NT) * mask_scr[1 + l]
    return a


def _gla_kernel(q_ref, k_ref, v_ref, g_ref, alr_ref, wa2_ref, ba2_ref, gn_ref, o_ref, sfin_ref,
                s_scr, mask_scr, tri_scr):
    ci = pl.program_id(2)

    @pl.when(ci == 0)
    def _():
        s_scr[...] = jnp.zeros_like(s_scr)
        _init_gla_masks(mask_scr, tri_scr)

    c = q_ref.shape[0]
    tri = tri_scr[...]
    ones = jnp.ones((c, LANES), BF16)
    alr = alr_ref[...]
    gn = gn_ref[...]
    for hh in range(s_scr.shape[0]):
        ks = slice(hh * GLA_HK, (hh + 1) * GLA_HK)
        vs = slice(hh * GLA_HV, (hh + 1) * GLA_HV)
        la = _log_decay(alr, wa2_ref[:, ks], ba2_ref[:, ks])
        lh, lm, ll = _split3(la)
        bc = _dot(tri, lh) + _dot(tri, lm) + _dot(tri, ll)
        b_last = bc[c - 1:c, :]
        bl_col = _dot(lh, ones, TN) + _dot(lm, ones, TN) + _dot(ll, ones, TN)
        dec_col = jnp.exp2(bl_col)
        dec_col = jnp.concatenate([dec_col] * (GLA_HV // LANES), axis=1)

        q = q_ref[:, ks] * (GLA_HK ** -0.5)
        k = k_ref[:, ks]
        vb = v_ref[:, vs].astype(BF16)
        s = s_scr[hh]
        o = _dot((q * jnp.exp2(bc)).astype(BF16), s.astype(BF16))
        a = _intra_scores(q, k, la, bc, mask_scr)
        o = o + _dot(a.astype(BF16), vb)
        kd = (k * jnp.exp2(b_last - bc)).astype(BF16)
        s_new = dec_col * s + _dot(kd, vb, TN)
        s_scr[hh] = s_new
        o_ref[:, vs] = _rms_gate(o, gn, g_ref[:, vs]).astype(o_ref.dtype)

    @pl.when(ci == pl.num_programs(2) - 1)
    def _():
        sfin_ref[...] = s_scr[...]


GLA_HEADS_PER_STEP = 2


def _gla_prompt(proj_a, alr, wa2p, ba2, gn, nb, seq):
    c = min(GLA_CHUNK, seq)
    nc = seq // c
    hp = GLA_HEADS_PER_STEP
    ng = GLA_HEADS // hp
    wk, wv = hp * GLA_HK, hp * GLA_HV
    row = lambda b, hg, ci: b * nc + ci
    return pl.pallas_call(
        _gla_kernel,
        grid=(nb, ng, nc),
        in_specs=[
            pl.BlockSpec((c, wk), lambda b, hg, ci: (row(b, hg, ci), hg)),
            pl.BlockSpec((c, wk), lambda b, hg, ci: (row(b, hg, ci), ng + hg)),
            pl.BlockSpec((c, wv), lambda b, hg, ci: (row(b, hg, ci), ng + hg)),
            pl.BlockSpec((c, wv), lambda b, hg, ci: (row(b, hg, ci), 2 * ng + hg)),
            pl.BlockSpec((c, LANES), lambda b, hg, ci: (row(b, hg, ci), 0)),
            pl.BlockSpec((LANES, wk), lambda b, hg, ci: (0, hg)),
            pl.BlockSpec((1, wk), lambda b, hg, ci: (0, hg)),
            pl.BlockSpec((1, GLA_HV), lambda b, hg, ci: (0, 0)),
        ],
        out_specs=[
            pl.BlockSpec((c, wv), lambda b, hg, ci: (row(b, hg, ci), hg)),
            pl.BlockSpec((None, hp, GLA_HK, GLA_HV), lambda b, hg, ci: (b, hg, 0, 0)),
        ],
        out_shape=[
            jax.ShapeDtypeStruct((nb * seq, GLA_HEADS * GLA_HV), BF16),
            jax.ShapeDtypeStruct((nb, GLA_HEADS, GLA_HK, GLA_HV), F32),
        ],
        scratch_shapes=[pltpu.VMEM((hp, GLA_HK, GLA_HV), F32),
                        pltpu.VMEM((1 + _gla_levels(c), c, c), F32),
                        pltpu.VMEM((c, c), BF16)],
        compiler_params=_params("parallel", "parallel", "arbitrary"),
        name="gla_prompt",
    )(proj_a, proj_a, proj_a, proj_a, alr, wa2p, ba2, gn)


GLA_DEC_ROWS = 4


def _gla_dec_item(q_ref, k_ref, v_ref, g_ref, alr_ref, wa2_ref, ba2_ref, gn_ref, s_ref, o_ref, so_ref):
    nb = q_ref.shape[0]
    alr = alr_ref[...]
    wa2 = wa2_ref[...]
    z = ba2_ref[...]
    for r in range(GLA_RANK):
        z = z + alr[:, r:r + 1] * wa2[r:r + 1, :]
    la = _log2_decay(z)
    rep = LANES // nb
    to_cols = lambda x: jnp.concatenate([x] * rep, axis=0).T
    a_t = to_cols(jnp.exp2(la))
    k_t = to_cols(k_ref[...])
    q_t = to_cols(q_ref[...] * (GLA_HK ** -0.5))
    v = v_ref[...]
    rows = []
    for j in range(nb):
        s_new = a_t[:, j:j + 1] * s_ref[j] + k_t[:, j:j + 1] * v[j:j + 1, :]
        so_ref[j] = s_new
        rows.append(jnp.sum(q_t[:, j:j + 1] * s_new, axis=0, keepdims=True))
    o = jnp.concatenate(rows, axis=0)
    o_ref[...] = _rms_gate(o, gn_ref[...], g_ref[...]).astype(o_ref.dtype)


GLA_DEC_INPUTS = 9


def _gla_decode_specs(item, proj_a, alr, wa2p, ba2, gn, state):
    nseq = proj_a.shape[0]
    h = GLA_HEADS
    r = GLA_DEC_ROWS
    n_items = (nseq // r) * h
    proj3 = proj_a.reshape(nseq // r, r, proj_a.shape[1])
    alr3 = alr.reshape(nseq // r, r, LANES)

    def at(fn):
        def index_map(*grid_idx):
            it = jnp.minimum(item(*grid_idx), n_items - 1)
            return fn(it // h, it % h)
        return index_map

    in_specs = [
        pl.BlockSpec((None, r, GLA_HK), at(lambda b, hh: (b, 0, hh))),
        pl.BlockSpec((None, r, GLA_HK), at(lambda b, hh: (b, 0, h + hh))),
        pl.BlockSpec((None, r, GLA_HV), at(lambda b, hh: (b, 0, h + hh))),
        pl.BlockSpec((None, r, GLA_HV), at(lambda b, hh: (b, 0, 2 * h + hh))),
        pl.BlockSpec((None, r, LANES), at(lambda b, hh: (b, 0, 0))),
        pl.BlockSpec((LANES, GLA_HK), at(lambda b, hh: (0, hh))),
        pl.BlockSpec((1, GLA_HK), at(lambda b, hh: (0, hh))),
        pl.BlockSpec((1, GLA_HV), at(lambda b, hh: (0, 0))),
        pl.BlockSpec((r, None, GLA_HK, GLA_HV), at(lambda b, hh: (b, hh, 0, 0))),
    ]
    out_specs = [
        pl.BlockSpec((None, r, GLA_HV), at(lambda b, hh: (b, 0, hh))),
        pl.BlockSpec((r, None, GLA_HK, GLA_HV), at(lambda b, hh: (b, hh, 0, 0))),
    ]
    out_shape = [
        jax.ShapeDtypeStruct((nseq // r, r, h * GLA_HV), F32),
        jax.ShapeDtypeStruct(state.shape, F32),
    ]
    args = (proj3, proj3, proj3, proj3, alr3, wa2p, ba2, gn, state)
    assert len(in_specs) == GLA_DEC_INPUTS
    return n_items, in_specs, args, out_specs, out_shape


def _shift_rows(x, halo, ri):
    p1 = jnp.where(ri == 0, halo[7:8, :], pltpu.roll(x, 1, 0))
    p2 = jnp.where(ri == 0, halo[6:7, :], jnp.where(ri == 1, halo[7:8, :], pltpu.roll(x, 2, 0)))
    return p1, p2


def _conv_kernel(cb_ref, cc_ref, ch_ref, cch_ref, chh_ref, w_ref, o_ref, st_ref):
    tl = cb_ref.shape[0]
    cch = cc_ref[...] * ch_ref[...]
    halo = cch_ref[...] * chh_ref[...]
    halo = jnp.where(pl.program_id(1) == 0, 0.0, halo)
    ri = lax.broadcasted_iota(jnp.int32, (tl, 1), 0)
    p1, p2 = _shift_rows(cch, halo, ri)
    w = w_ref[...]
    u = w[0:1, :] * p2 + w[1:2, :] * p1 + w[2:3, :] * cch
    o_ref[...] = (cb_ref[...] * u).astype(o_ref.dtype)
    st_ref[...] = cch[tl - 2:tl, :]


def _conv_prompt(proj_b, w_conv, nb, seq):
    tl = min(512, seq)
    nt = seq // tl
    t8 = tl // 8
    cd = CONV_DIM
    halo = lambda col: (lambda b, t: (jnp.maximum((b * nt + t) * t8 - 1, 0), col))
    return pl.pallas_call(
        _conv_kernel,
        grid=(nb, nt),
        in_specs=[
            pl.BlockSpec((tl, cd), lambda b, t: (b * nt + t, 0)),
            pl.BlockSpec((tl, cd), lambda b, t: (b * nt + t, 1)),
            pl.BlockSpec((tl, cd), lambda b, t: (b * nt + t, 2)),
            pl.BlockSpec((8, cd), halo(1)),
            pl.BlockSpec((8, cd), halo(2)),
            pl.BlockSpec((3, cd), lambda b, t: (0, 0)),
        ],
        out_specs=[
            pl.BlockSpec((tl, cd), lambda b, t: (b * nt + t, 0)),
            pl.BlockSpec((None, 2, cd), lambda b, t: (b, 0, 0)),
        ],
        out_shape=[
            jax.ShapeDtypeStruct((nb * seq, cd), BF16),
            jax.ShapeDtypeStruct((nb, 2, cd), F32),
        ],
        compiler_params=_params("parallel", "arbitrary"),
        name="conv_prompt",
    )(proj_b, proj_b, proj_b, proj_b, proj_b, w_conv)


def _conv_dec_kernel(cb_ref, cc_ref, ch_ref, s0_ref, s1_ref, w_ref, o_ref, st_ref):
    cd = cb_ref.shape[1]
    cch = cc_ref[...] * ch_ref[...]
    s1 = s1_ref[...]
    w = w_ref[...]
    u = w[0:1, :] * s0_ref[...] + w[1:2, :] * s1 + w[2:3, :] * cch
    o_ref[...] = (cb_ref[...] * u).astype(o_ref.dtype)
    st_ref[:, :cd] = s1
    st_ref[:, cd:] = cch


def _conv_decode(proj_b, w_conv, state2d):
    n = proj_b.shape[0]
    cd = CONV_DIM
    blk = lambda col: pl.BlockSpec((n, cd), lambda i: (0, col))
    return pl.pallas_call(
        _conv_dec_kernel,
        grid=(1,),
        in_specs=[blk(0), blk(1), blk(2), blk(0), blk(1), pl.BlockSpec((3, cd), lambda i: (0, 0))],
        out_specs=[pl.BlockSpec((n, cd), lambda i: (0, 0)), pl.BlockSpec((n, 2 * cd), lambda i: (0, 0))],
        out_shape=[jax.ShapeDtypeStruct((n, cd), BF16), jax.ShapeDtypeStruct((n, 2 * cd), F32)],
        compiler_params=_params("arbitrary"),
        name="conv_decode",
    )(proj_b, proj_b, proj_b, state2d, state2d, w_conv)


def _softmax_rows(logits):
    m = jnp.max(logits, axis=-1, keepdims=True)
    p = jnp.exp(logits - m)
    return p / jnp.sum(p, axis=-1, keepdims=True)


def _memattn_kernel(q_ref, k_ref, v_ref, o_ref):
    logits = _dot(q_ref[...].astype(BF16), k_ref[...].astype(BF16), NT) * (MEM_HD ** -0.5)
    p = _softmax_rows(logits)
    o_ref[...] = _dot(p.astype(BF16), v_ref[...].astype(BF16)).astype(o_ref.dtype)


def _memattn_prompt(proj_b, mk, mv, nb, seq):
    tl = min(512, seq)
    nt = seq // tl
    h = MEM_HEADS
    qcol = 3 * CONV_DIM // MEM_HD
    return pl.pallas_call(
        _memattn_kernel,
        grid=(nb, h, nt),
        in_specs=[
            pl.BlockSpec((tl, MEM_HD), lambda b, hh, t: (b * nt + t, qcol + hh)),
            pl.BlockSpec((N_MEM, MEM_HD), lambda b, hh, t: (b, hh)),
            pl.BlockSpec((N_MEM, MEM_HD), lambda b, hh, t: (b, hh)),
        ],
        out_specs=pl.BlockSpec((tl, MEM_HD), lambda b, hh, t: (b * nt + t, hh)),
        out_shape=jax.ShapeDtypeStruct((nb * seq, MEM_DIM), BF16),
        compiler_params=_params("parallel", "parallel", "parallel"),
        name="memattn_prompt",
    )(proj_b, mk, mv)


MEMATTN_DEC_ROWS = 2


MEM_HD_CHUNKS = MEM_HD // LANES
MEM_TILE_ROWS = MEM_HD_CHUNKS * MEM_HEADS
assert MEM_HD_CHUNKS == 2 and MEM_TILE_ROWS == 8


def _heads_to_tile(x):
    lead = x.shape[:-2]
    x = x.reshape(lead + (MEM_HEADS, MEM_HD_CHUNKS, LANES))
    return jnp.swapaxes(x, -3, -2).reshape(lead + (MEM_TILE_ROWS, LANES))


def _tile_to_heads(x):
    lead = x.shape[:-2]
    x = x.reshape(lead + (MEM_HD_CHUNKS, MEM_HEADS, LANES))
    return jnp.swapaxes(x, -3, -2).reshape(lead + (MEM_HEADS, MEM_HD))


def _memattn_dec_kernel(q_ref, k_ref, v_ref, o_ref):
    def body(i, carry):
        q = q_ref[i] * (MEM_HD ** -0.5)
        prod = k_ref[i] * q[None]
        prod = prod + pltpu.roll(prod, MEM_HEADS, 1)
        logits = jnp.sum(prod, axis=-1, keepdims=True)
        e = jnp.exp(logits - jnp.max(logits, axis=0, keepdims=True))
        o_ref[i] = jnp.sum(e * v_ref[i], axis=0) / jnp.sum(e, axis=0)
        return carry

    lax.fori_loop(0, q_ref.shape[0], body, 0, unroll=True)


def _memattn_decode_specs(item, q_t, k_t, v_t):
    n = q_t.shape[0]
    r = MEMATTN_DEC_ROWS
    n_items = n // r
    tile = (MEM_TILE_ROWS, LANES)

    def at(ndim):
        return lambda *grid_idx: (jnp.minimum(item(*grid_idx), n_items - 1),) + (0,) * (ndim - 1)

    in_specs = [
        pl.BlockSpec((r,) + tile, at(3)),
        pl.BlockSpec((r, N_MEM) + tile, at(4)),
        pl.BlockSpec((r, N_MEM) + tile, at(4)),
    ]
    out_specs = [pl.BlockSpec((r,) + tile, at(3))]
    out_shape = [jax.ShapeDtypeStruct((n,) + tile, F32)]
    return n_items, in_specs, (q_t, k_t, v_t), out_specs, out_shape


def _merge_kernel(a_ref, b_ref, m_ref, wa_ref, wb_ref, wm_ref, za_ref, zb_ref, zm_ref, o_ref):
    ya = _dot(a_ref[...].astype(BF16), wa_ref[...])
    yb = _dot(b_ref[...].astype(BF16), wb_ref[...])
    ym = _dot(m_ref[...].astype(BF16), wm_ref[...])
    merged = _sigmoid(za_ref[...]) * ya + _sigmoid(zb_ref[...]) * yb + _sigmoid(zm_ref[...]) * ym
    o_ref[...] = merged.astype(o_ref.dtype)


def _merge(o_gla, cbu, om, w_gla_out, w_conv_out, w_mem_out, proj_b):
    m = o_gla.shape[0]
    tm = min(1024, m)
    tn = 512
    d = D_MODEL
    zoff = (3 * CONV_DIM + MEM_DIM) // tn
    zspec = lambda g: pl.BlockSpec((tm, tn), lambda i, j: (i, zoff + g * (d // tn) + j))
    return pl.pallas_call(
        _merge_kernel,
        grid=(m // tm, d // tn),
        in_specs=[
            pl.BlockSpec((tm, o_gla.shape[1]), lambda i, j: (i, 0)),
            pl.BlockSpec((tm, cbu.shape[1]), lambda i, j: (i, 0)),
            pl.BlockSpec((tm, om.shape[1]), lambda i, j: (i, 0)),
            pl.BlockSpec((w_gla_out.shape[0], tn), lambda i, j: (0, j)),
            pl.BlockSpec((w_conv_out.shape[0], tn), lambda i, j: (0, j)),
            pl.BlockSpec((w_mem_out.shape[0], tn), lambda i, j: (0, j)),
            zspec(0), zspec(1), zspec(2),
        ],
        out_specs=pl.BlockSpec((tm, tn), lambda i, j: (i, j)),
        out_shape=jax.ShapeDtypeStruct((m, d), BF16),
        compiler_params=_params("parallel", "parallel"),
        name="merge",
    )(o_gla, cbu, om, w_gla_out, w_conv_out, w_mem_out, proj_b, proj_b, proj_b)


def _wo_ln_kernel(x_ref, m_ref, w_ref, g_ref, b_ref, o_ref):
    tm = x_ref.shape[0]
    half = tm // 2 if tm % 256 == 0 else tm
    for r in range(0, tm, half):
        rows = pl.ds(r, half)
        y = DN_ALPHA * x_ref[rows, :] + _dot(m_ref[rows, :], w_ref[...])
        o_ref[rows, :] = _layer_norm(y, g_ref[...], b_ref[...])


def _wo_ln(x, merged, w_o, g, b):
    m, d = x.shape
    tm = min(512, m)
    return pl.pallas_call(
        _wo_ln_kernel,
        grid=(m // tm,),
        in_specs=[
            pl.BlockSpec((tm, d), lambda i: (i, 0)),
            pl.BlockSpec((tm, d), lambda i: (i, 0)),
            pl.BlockSpec((d, d), lambda i: (0, 0)),
            pl.BlockSpec((1, d), lambda i: (0, 0)),
            pl.BlockSpec((1, d), lambda i: (0, 0)),
        ],
        out_specs=pl.BlockSpec((tm, d), lambda i: (i, 0)),
        out_shape=jax.ShapeDtypeStruct((m, d), F32),
        compiler_params=_params("parallel"),
        name="wo_ln",
    )(x, merged, w_o, g, b)


def _ffn_tail(hc, hu, wd_ref, acc_ref, x_ref, g_ref, b_ref, o_ref, fi):
    tf = hc.shape[1]
    valid = D_FF - fi * tf
    h = hc * _sigmoid(hc) * hu
    h = jnp.where(lax.broadcasted_iota(jnp.int32, (1, tf), 1) < valid, h, 0.0).astype(BF16)
    wd = jnp.where(lax.broadcasted_iota(jnp.int32, (tf, 1), 0) < valid, wd_ref[...], 0.0)
    acc_ref[...] += _dot(h, wd)

    @pl.when(fi == pl.num_programs(1) - 1)
    def _():
        o_ref[...] = _layer_norm(DN_ALPHA * x_ref[...] + acc_ref[...], g_ref[...], b_ref[...])


def _ffn_kernel(seq, x_ref, xh_ref, wg_ref, wu_ref, wd_ref, wc_ref, bc_ref, g_ref, b_ref, *rest):
    dec_in, (o_ref, od_ref, so_ref, xb_scr, acc_scr) = rest[:GLA_DEC_INPUTS], rest[GLA_DEC_INPUTS:]
    fi = pl.program_id(1)
    tm = x_ref.shape[0]

    @pl.when(fi == 0)
    def _():
        xb_scr[...] = x_ref[...].astype(BF16)
        acc_scr[...] = jnp.zeros_like(acc_scr)

    _gla_dec_item(*dec_in, od_ref, so_ref)
    xb = xb_scr[...]
    wg = wg_ref[...]
    hg = _dot(xb, wg)
    halo = _dot(xh_ref[...].astype(BF16), wg)
    halo = jnp.where((pl.program_id(0) * tm) % seq == 0, 0.0, halo)
    ri = lax.broadcasted_iota(jnp.int32, (tm, 1), 0)
    p1, p2 = _shift_rows(hg, halo, ri)
    wc = wc_ref[...]
    hc = wc[0:1, :] * p2 + wc[1:2, :] * p1 + wc[2:3, :] * hg + bc_ref[...]
    _ffn_tail(hc, _dot(xb, wu_ref[...]), wd_ref, acc_scr, x_ref, g_ref, b_ref, o_ref, fi)


def _ffn_prompt(x1, w_gate, w_up, w_down, w_conv, b_conv, g, b, seq, gla_dec_operands):
    m, d = x1.shape
    tm = min(512, seq)
    tf = FF_TILE
    t8 = tm // 8
    n_items, dec_in_specs, dec_args, dec_out_specs, dec_out_shape = _gla_decode_specs(
        lambda i, f: i * FF_TILES + f, *gla_dec_operands)
    assert n_items <= (m // tm) * FF_TILES, "not enough FFN steps to carry the decode GLA work items"
    return pl.pallas_call(
        functools.partial(_ffn_kernel, seq),
        grid=(m // tm, FF_TILES),
        in_specs=[
            pl.BlockSpec((tm, d), lambda i, f: (i, 0)),
            pl.BlockSpec((8, d), lambda i, f: (jnp.maximum(i * t8 - 1, 0), 0)),
            pl.BlockSpec((d, tf), lambda i, f: (0, f)),
            pl.BlockSpec((d, tf), lambda i, f: (0, f)),
            pl.BlockSpec((tf, d), lambda i, f: (f, 0)),
            pl.BlockSpec((3, tf), lambda i, f: (0, f)),
            pl.BlockSpec((1, tf), lambda i, f: (0, f)),
            pl.BlockSpec((1, d), lambda i, f: (0, 0)),
            pl.BlockSpec((1, d), lambda i, f: (0, 0)),
        ] + dec_in_specs,
        out_specs=[pl.BlockSpec((tm, d), lambda i, f: (i, 0))] + dec_out_specs,
        out_shape=[jax.ShapeDtypeStruct((m, d), F32)] + dec_out_shape,
        scratch_shapes=[pltpu.VMEM((tm, d), BF16), pltpu.VMEM((tm, d), F32)],
        compiler_params=_params("arbitrary", "arbitrary"),
        name="ffn_prompt_gla_decode",
    )(x1, x1, w_gate, w_up, w_down, w_conv, b_conv, g, b, *dec_args)


def _ffn_dec_kernel(x_ref, s0_ref, s1_ref, wg_ref, wu_ref, wd_ref, wc_ref, bc_ref, g_ref, b_ref,
                    o_ref, hg_ref, xb_scr, acc_scr):
    fi = pl.program_id(1)

    @pl.when(fi == 0)
    def _():
        xb_scr[...] = x_ref[...].astype(BF16)
        acc_scr[...] = jnp.zeros_like(acc_scr)

    xb = xb_scr[...]
    hg = _dot(xb, wg_ref[...])
    hg_ref[...] = hg
    wc = wc_ref[...]
    hc = wc[0:1, :] * s0_ref[...] + wc[1:2, :] * s1_ref[...] + wc[2:3, :] * hg + bc_ref[...]
    _ffn_tail(hc, _dot(xb, wu_ref[...]), wd_ref, acc_scr, x_ref, g_ref, b_ref, o_ref, fi)


def _ffn_decode(x1, s0, s1, w_gate, w_up, w_down, w_conv, b_conv, g, b):
    m, d = x1.shape
    tf = FF_TILE
    return pl.pallas_call(
        _ffn_dec_kernel,
        grid=(1, FF_TILES),
        in_specs=[
            pl.BlockSpec((m, d), lambda i, f: (0, 0)),
            pl.BlockSpec((m, tf), lambda i, f: (0, f)),
            pl.BlockSpec((m, tf), lambda i, f: (0, f)),
            pl.BlockSpec((d, tf), lambda i, f: (0, f)),
            pl.BlockSpec((d, tf), lambda i, f: (0, f)),
            pl.BlockSpec((tf, d), lambda i, f: (f, 0)),
            pl.BlockSpec((3, tf), lambda i, f: (0, f)),
            pl.BlockSpec((1, tf), lambda i, f: (0, f)),
            pl.BlockSpec((1, d), lambda i, f: (0, 0)),
            pl.BlockSpec((1, d), lambda i, f: (0, 0)),
        ],
        out_specs=[pl.BlockSpec((m, d), lambda i, f: (0, 0)), pl.BlockSpec((m, tf), lambda i, f: (0, f))],
        out_shape=[jax.ShapeDtypeStruct((m, d), F32), jax.ShapeDtypeStruct((m, D_FF), F32)],
        scratch_shapes=[pltpu.VMEM((m, d), BF16), pltpu.VMEM((m, d), F32)],
        compiler_params=_params("parallel", "arbitrary"),
        name="ffn_decode",
    )(x1, s0, s1, w_gate, w_up, w_down, w_conv, b_conv, g, b)


def kernel(x_prompt, x_sample, mem_prompt, cache_mem_k, cache_mem_v, state_gla, state_conv, state_ffn_conv, w_in, w_gla_a2, b_gla_a2, g_gla_norm, w_gla_out, w_conv, w_conv_out, w_mem_k, w_mem_v, w_mem_out, w_o, ln1_g, ln1_b, w_ffn_gate, w_ffn_up, w_ffn_conv, b_ffn_conv, w_ffn_down, ln2_g, ln2_b):
    nb, seq, d = x_prompt.shape
    ns = x_sample.shape[0]

    w_in_t = jnp.swapaxes(w_in[0], 0, 1)
    wa2p = jnp.pad(w_gla_a2[0], ((0, LANES - GLA_RANK), (0, 0)))
    ba2 = b_gla_a2
    gn = g_gla_norm
    w_go = w_gla_out[0].astype(BF16)
    w_co = w_conv_out[0].astype(BF16)
    w_mo = w_mem_out[0].astype(BF16)
    w_oo = w_o[0].astype(BF16)
    w_fg = w_ffn_gate[0].astype(BF16)
    w_fu = w_ffn_up[0].astype(BF16)
    w_fd = w_ffn_down[0].astype(BF16)
    w_fc = w_ffn_conv[0]
    b_fc = b_ffn_conv
    w_cv = w_conv[0]

    def mix_in(x2d, tag, rider_b=None):
        xb = x2d.astype(BF16)
        return (_proj(xb, w_in_t, "proj_a_" + tag, 0, COLS_A, transposed=True),
                _proj(xb, w_in_t, "proj_r_" + tag, COLS_A, LANES, LANES, transposed=True),
                _proj(xb, w_in_t, "proj_b_" + tag, COLS_A, COLS_B, shift=GLA_RANK, transposed=True,
                      rider=rider_b))

    def mix_out(x2d, o_gla, cbu, om, proj_b):
        merged = _merge(o_gla, cbu, om, w_go, w_co, w_mo, proj_b)
        return _wo_ln(x2d, merged, w_oo, ln1_g, ln1_b)

    xs = x_sample.reshape(ns, d)
    sa, sr, sb = mix_in(xs, "s")
    mq_s = sb[:, 3 * CONV_DIM:3 * CONV_DIM + MEM_DIM].reshape(ns, MEM_HEADS, MEM_HD)
    memattn_dec = (_memattn_dec_kernel, _memattn_decode_specs,
                   (_heads_to_tile(mq_s), _heads_to_tile(cache_mem_k[0]), _heads_to_tile(cache_mem_v[0])))

    xp = x_prompt.reshape(nb * seq, d)
    memb = mem_prompt.reshape(nb * N_MEM, d).astype(BF16)
    mk = _proj(memb, w_mem_k[0], "mem_k")
    mv = _proj(memb, w_mem_v[0], "mem_v")
    pa, pr, (pb, om_s) = mix_in(xp, "p", memattn_dec)
    o_gla, p_gla = _gla_prompt(pa, pr, wa2p, ba2, gn, nb, seq)
    cbu, p_conv = _conv_prompt(pb, w_cv, nb, seq)
    om = _memattn_prompt(pb, mk, mv, nb, seq)
    x1 = mix_out(xp, o_gla, cbu, om, pb)
    yp, o_gla_s, s_gla = _ffn_prompt(x1, w_fg, w_fu, w_fd, w_fc, b_fc, ln2_g, ln2_b, seq,
                                     (sa, sr, wa2p, ba2, gn, state_gla[0]))
    o_gla_s = o_gla_s.reshape(ns, GLA_HEADS * GLA_HV)
    x1_tail = x1.reshape(nb, seq, d)[:, seq - 2:, :].reshape(nb * 2, d)
    p_ffn = _matmul(x1_tail, w_fg, F32, "ffn_state_p", tn=FF_TILE)

    cbu_s, s_conv = _conv_decode(sb, w_cv, state_conv[0].reshape(ns, 2 * CONV_DIM))
    om_s = _tile_to_heads(om_s).reshape(ns, MEM_DIM)
    x1s = mix_out(xs, o_gla_s, cbu_s, om_s, sb)
    f0 = state_ffn_conv[0][:, 0, :]
    f1 = state_ffn_conv[0][:, 1, :]
    ys, hg_s = _ffn_decode(x1s, f0, f1, w_fg, w_fu, w_fd, w_fc, b_fc, ln2_g, ln2_b)
    s_ffn = jnp.stack([f1, hg_s], axis=1)

    return (yp.reshape(nb, seq, d), ys.reshape(ns, 1, d),
            mk.reshape(1, nb, N_MEM, MEM_HEADS, MEM_HD), mv.reshape(1, nb, N_MEM, MEM_HEADS, MEM_HD),
            p_gla[None], p_conv[None], p_ffn.reshape(1, nb, 2, D_FF),
            s_gla[None], s_conv.reshape(1, ns, 2, CONV_DIM), s_ffn[None])
```

```python
import functools

import jax
import jax.numpy as jnp
from jax import lax
from jax.experimental import pallas as pl
from jax.experimental.pallas import tpu as pltpu

F32 = jnp.float32
BF16 = jnp.bfloat16

D_MODEL = 2048
GLA_HEADS = 4
GLA_HK = 256
GLA_HV = 512
GLA_RANK = 16
GLA_GATE_NORM = 16.0
CONV_DIM = 1024
N_MEM = 256
MEM_HEADS = 4
MEM_HD = 256
MEM_DIM = 1024
D_FF = 5504
DEPTH = 1
DN_ALPHA = (2 * DEPTH) ** 0.25
LN_EPS = 1e-5
RMS_EPS = 1e-6

LANES = 128
SUBLANES = 8
LOG2_E = 1.4426950408889634
FF_TILE = 512
FF_TILES = -(-D_FF // FF_TILE)
COLS_A = 2 * GLA_HEADS * GLA_HK + 2 * GLA_HEADS * GLA_HV
COLS_B = 3 * CONV_DIM + MEM_DIM + 3 * D_MODEL
GLA_CHUNK = 256
VMEM_LIMIT = 56 * 1024 * 1024

NN = (((1,), (0,)), ((), ()))
NT = (((1,), (1,)), ((), ()))
TN = (((0,), (0,)), ((), ()))


def _dot(a, b, dims=NN):
    return lax.dot_general(a, b, dims, preferred_element_type=F32)


def _params(*sem):
    return pltpu.CompilerParams(dimension_semantics=sem, vmem_limit_bytes=VMEM_LIMIT)


def _split2(x):
    hi = x.astype(BF16)
    lo = (x - hi.astype(F32)).astype(BF16)
    return hi, lo


def _split3(x):
    hi = x.astype(BF16)
    r = x - hi.astype(F32)
    mid = r.astype(BF16)
    lo = (r - mid.astype(F32)).astype(BF16)
    return hi, mid, lo


def _sigmoid(z):
    return 0.5 * jnp.tanh(0.5 * z) + 0.5


def _silu(z):
    h = 0.5 * z
    return h * jnp.tanh(h) + h


def _layer_norm(y, g, b):
    mu = jnp.mean(y, axis=-1, keepdims=True)
    d = y - mu
    var = jnp.mean(d * d, axis=-1, keepdims=True)
    return d * lax.rsqrt(var + LN_EPS) * g + b


def _log2_decay(z):
    return (jnp.minimum(z, 0.0) - jnp.log(1.0 + jnp.exp(-jnp.abs(z)))) * (LOG2_E / GLA_GATE_NORM)


def _log_decay(alr, wa2, ba2):
    ah, al = _split2(alr)
    wh, wl = _split2(wa2)
    return _log2_decay(_dot(ah, wh) + _dot(ah, wl) + _dot(al, wh) + ba2)


def _rms_gate(o, gn, g):
    o = o * lax.rsqrt(jnp.mean(o * o, axis=-1, keepdims=True) + RMS_EPS) * gn
    return o * _silu(g)


PROJ_SHIFT_ROWS = 16


def _proj_kernel(transposed, shift, rider, x_ref, w_ref, *rest):
    if shift:
        wx_ref, rest = rest[0], rest[1:]
    rider_refs = ()
    if rider:
        body, n_in = rider
        rider_refs, rest = rest[:n_in] + rest[n_in + 1:-1], (rest[n_in], rest[-1])
    o_ref, wb_scr = rest

    @pl.when(pl.program_id(1) == 0)
    def _():
        w = w_ref[...]
        if shift:
            w = jnp.concatenate([w[shift:], wx_ref[:shift]], axis=0)
        if transposed:
            w = w.T
        wb_scr[...] = w.astype(BF16)

    if rider:
        body(*rider_refs)
    o_ref[...] = _dot(x_ref[...].astype(BF16), wb_scr[...]).astype(o_ref.dtype)


def _proj(x, w, name, col0=0, n=None, tn=1024, shift=0, transposed=False, rider=None, out_dtype=F32):
    m, k = x.shape
    if n is None:
        n = w.shape[0] if transposed else w.shape[1]
    tm = min(m, 1024)
    tn = min(n, tn)
    c0 = col0 // tn
    if transposed:
        w_spec = pl.BlockSpec((tn, k), lambda j, i: (c0 + j, 0))
    else:
        assert not shift
        w_spec = pl.BlockSpec((k, tn), lambda j, i: (0, c0 + j))
    in_specs = [pl.BlockSpec((tm, k), lambda j, i: (i, 0)), w_spec]
    args = [x, w]
    if shift:
        r = PROJ_SHIFT_ROWS
        assert shift <= r and shift % 8 == 0 and tn % r == 0 and col0 % r == 0
        in_specs.append(pl.BlockSpec((r, k), lambda j, i: (col0 // r + (j + 1) * (tn // r), 0)))
        args.append(w)
    out_specs = [pl.BlockSpec((tm, tn), lambda j, i: (i, j))]
    out_shape = [jax.ShapeDtypeStruct((m, n), out_dtype)]
    kernel_rider = None
    if rider:
        body, specs_fn, operands = rider
        n_items, r_in_specs, r_args, r_out_specs, r_out_shape = specs_fn(lambda j, i: j * (m // tm) + i, *operands)
        assert n_items <= (n // tn) * (m // tm), "not enough projection steps to carry the rider's work items"
        in_specs += r_in_specs
        args += list(r_args)
        out_specs += r_out_specs
        out_shape += r_out_shape
        kernel_rider = (body, len(r_in_specs))
    outs = pl.pallas_call(
        functools.partial(_proj_kernel, transposed, shift, kernel_rider),
        grid=(n // tn, m // tm),
        in_specs=in_specs,
        out_specs=out_specs,
        out_shape=out_shape,
        scratch_shapes=[pltpu.VMEM((k, tn), BF16)],
        compiler_params=_params("arbitrary" if rider else "parallel", "arbitrary"),
        name=name,
    )(*args)
    return outs if rider else outs[0]


def _gla_levels(c):
    return c.bit_length() - 1


def _init_gla_masks(mask_scr, tri_scr):
    c = tri_scr.shape[0]
    rr = lax.broadcasted_iota(jnp.int32, (c, c), 0)
    cc = lax.broadcasted_iota(jnp.int32, (c, c), 1)
    tri_scr[...] = (rr >= cc).astype(BF16)
    mask_scr[0] = (rr == cc).astype(F32)
    for l in range(_gla_levels(c)):
        b = 1 << l
        pair = (rr // (2 * b) == cc // (2 * b)) & ((rr // b) % 2 == 1) & ((cc // b) % 2 == 0)
        mask_scr[1 + l] = pair.astype(F32)


def _intra_scores(q, k, la, bc, mask_scr):
    c, dk = q.shape
    ri = lax.broadcasted_iota(jnp.int32, (c, 1), 0)
    a = jnp.sum(q * k, axis=1, keepdims=True) * mask_scr[0]
    for l in range(_gla_levels(c)):
        b = 1 << l
        if b < SUBLANES:
            upper = (ri // b) % 2 == 1
            if b == 1:
                e = jnp.where(upper, la, 0.0)
            elif b == 2:
                m4 = ri % 4
                la_prev = pltpu.roll(la, 1, 0)
                la_next = pltpu.roll(la, c - 1, 0)
                e = jnp.where(m4 == 2, la, jnp.where(m4 == 3, la + la_prev, jnp.where(m4 == 0, la_next, 0.0)))
            else:
                ref = bc.reshape(c // (2 * b), 2 * b, dk)[:, b - 1:b, :]
                ref = jnp.broadcast_to(ref, (c // (2 * b), 2 * b, dk)).reshape(c, dk)
                dlt = bc - ref
                e = jnp.minimum(dlt, -dlt)
            w = jnp.where(upper, q, k) * jnp.exp2(e)
        else:
            parts = []
            for r0 in range(0, c, 2 * b):
                ref = bc[r0 + b - 1:r0 + b, :]
                parts.append(k[r0:r0 + b] * jnp.exp2(ref - bc[r0:r0 + b]))
                parts.append(q[r0 + b:r0 + 2 * b] * jnp.exp2(bc[r0 + b:r0 + 2 * b] - ref))
            w = jnp.concatenate(parts, axis=0)
        w = w.astype(BF16)
        a = a + _dot(w, w, NT) * mask_scr[1 + l]
    return a


def _gla_kernel(q_ref, k_ref, v_ref, g_ref, alr_ref, wa2_ref, ba2_ref, gn_ref, o_ref, sfin_ref,
                s_scr, mask_scr, tri_scr):
    ci = pl.program_id(2)

    @pl.when(ci == 0)
    def _():
        s_scr[...] = jnp.zeros_like(s_scr)
        _init_gla_masks(mask_scr, tri_scr)

    c = q_ref.shape[0]
    tri = tri_scr[...]
    ones = jnp.ones((c, LANES), BF16)
    alr = alr_ref[...]
    gn = gn_ref[...]
    for hh in range(s_scr.shape[0]):
        ks = slice(hh * GLA_HK, (hh + 1) * GLA_HK)
        vs = slice(hh * GLA_HV, (hh + 1) * GLA_HV)
        la = _log_decay(alr, wa2_ref[:, ks], ba2_ref[:, ks])
        lh, lm, ll = _split3(la)
        bc = _dot(tri, lh) + _dot(tri, lm) + _dot(tri, ll)
        b_last = bc[c - 1:c, :]
        bl_col = _dot(lh, ones, TN) + _dot(lm, ones, TN) + _dot(ll, ones, TN)
        dec_col = jnp.exp2(bl_col)
        dec_col = jnp.concatenate([dec_col] * (GLA_HV // LANES), axis=1)

        q = q_ref[:, ks].astype(F32) * (GLA_HK ** -0.5)
        k = k_ref[:, ks].astype(F32)
        vb = v_ref[:, vs].astype(BF16)
        s = s_scr[hh]
        o = _dot((q * jnp.exp2(bc)).astype(BF16), s.astype(BF16))
        a = _intra_scores(q, k, la, bc, mask_scr)
        o = o + _dot(a.astype(BF16), vb)
        kd = (k * jnp.exp2(b_last - bc)).astype(BF16)
        s_new = dec_col * s + _dot(kd, vb, TN)
        s_scr[hh] = s_new
        o_ref[:, vs] = _rms_gate(o, gn, g_ref[:, vs].astype(F32)).astype(o_ref.dtype)

    @pl.when(ci == pl.num_programs(2) - 1)
    def _():
        sfin_ref[...] = s_scr[...]


GLA_HEADS_PER_STEP = 2


def _gla_prompt(proj_a, alr, wa2p, ba2, gn, nb, seq):
    c = min(GLA_CHUNK, seq)
    nc = seq // c
    hp = GLA_HEADS_PER_STEP
    ng = GLA_HEADS // hp
    wk, wv = hp * GLA_HK, hp * GLA_HV
    row = lambda b, hg, ci: b * nc + ci
    return pl.pallas_call(
        _gla_kernel,
        grid=(nb, ng, nc),
        in_specs=[
            pl.BlockSpec((c, wk), lambda b, hg, ci: (row(b, hg, ci), hg)),
            pl.BlockSpec((c, wk), lambda b, hg, ci: (row(b, hg, ci), ng + hg)),
            pl.BlockSpec((c, wv), lambda b, hg, ci: (row(b, hg, ci), ng + hg)),
            pl.BlockSpec((c, wv), lambda b, hg, ci: (row(b, hg, ci), 2 * ng + hg)),
            pl.BlockSpec((c, LANES), lambda b, hg, ci: (row(b, hg, ci), 0)),
            pl.BlockSpec((LANES, wk), lambda b, hg, ci: (0, hg)),
            pl.BlockSpec((1, wk), lambda b, hg, ci: (0, hg)),
            pl.BlockSpec((1, GLA_HV), lambda b, hg, ci: (0, 0)),
        ],
        out_specs=[
            pl.BlockSpec((c, wv), lambda b, hg, ci: (row(b, hg, ci), hg)),
            pl.BlockSpec((None, hp, GLA_HK, GLA_HV), lambda b, hg, ci: (b, hg, 0, 0)),
        ],
        out_shape=[
            jax.ShapeDtypeStruct((nb * seq, GLA_HEADS * GLA_HV), BF16),
            jax.ShapeDtypeStruct((nb, GLA_HEADS, GLA_HK, GLA_HV), F32),
        ],
        scratch_shapes=[pltpu.VMEM((hp, GLA_HK, GLA_HV), F32),
                        pltpu.VMEM((1 + _gla_levels(c), c, c), F32),
                        pltpu.VMEM((c, c), BF16)],
        compiler_params=_params("parallel", "parallel", "arbitrary"),
        name="gla_prompt",
    )(proj_a, proj_a, proj_a, proj_a, alr, wa2p, ba2, gn)


GLA_DEC_ROWS = 4


def _gla_dec_item(q_ref, k_ref, v_ref, g_ref, alr_ref, wa2_ref, ba2_ref, gn_ref, s_ref, o_ref, so_ref):
    nb = q_ref.shape[0]
    alr = alr_ref[...]
    wa2 = wa2_ref[...]
    z = ba2_ref[...]
    for r in range(GLA_RANK):
        z = z + alr[:, r:r + 1] * wa2[r:r + 1, :]
    la = _log2_decay(z)
    rep = LANES // nb
    to_cols = lambda x: jnp.concatenate([x] * rep, axis=0).T
    a_t = to_cols(jnp.exp2(la))
    k_t = to_cols(k_ref[...])
    q_t = to_cols(q_ref[...] * (GLA_HK ** -0.5))
    v = v_ref[...]
    rows = []
    for j in range(nb):
        s_new = a_t[:, j:j + 1] * s_ref[j] + k_t[:, j:j + 1] * v[j:j + 1, :]
        so_ref[j] = s_new
        rows.append(jnp.sum(q_t[:, j:j + 1] * s_new, axis=0, keepdims=True))
    o = jnp.concatenate(rows, axis=0)
    o_ref[...] = _rms_gate(o, gn_ref[...], g_ref[...]).astype(o_ref.dtype)


GLA_DEC_INPUTS = 9


def _gla_decode_specs(item, proj_a, alr, wa2p, ba2, gn, state):
    nseq = proj_a.shape[0]
    h = GLA_HEADS
    r = GLA_DEC_ROWS
    n_items = (nseq // r) * h
    proj3 = proj_a.reshape(nseq // r, r, proj_a.shape[1])
    alr3 = alr.reshape(nseq // r, r, LANES)

    def at(fn):
        def index_map(*grid_idx):
            it = jnp.minimum(item(*grid_idx), n_items - 1)
            return fn(it // h, it % h)
        return index_map

    in_specs = [
        pl.BlockSpec((None, r, GLA_HK), at(lambda b, hh: (b, 0, hh))),
        pl.BlockSpec((None, r, GLA_HK), at(lambda b, hh: (b, 0, h + hh))),
        pl.BlockSpec((None, r, GLA_HV), at(lambda b, hh: (b, 0, h + hh))),
        pl.BlockSpec((None, r, GLA_HV), at(lambda b, hh: (b, 0, 2 * h + hh))),
        pl.BlockSpec((None, r, LANES), at(lambda b, hh: (b, 0, 0))),
        pl.BlockSpec((LANES, GLA_HK), at(lambda b, hh: (0, hh))),
        pl.BlockSpec((1, GLA_HK), at(lambda b, hh: (0, hh))),
        pl.BlockSpec((1, GLA_HV), at(lambda b, hh: (0, 0))),
        pl.BlockSpec((r, None, GLA_HK, GLA_HV), at(lambda b, hh: (b, hh, 0, 0))),
    ]
    out_specs = [
        pl.BlockSpec((None, r, GLA_HV), at(lambda b, hh: (b, 0, hh))),
        pl.BlockSpec((r, None, GLA_HK, GLA_HV), at(lambda b, hh: (b, hh, 0, 0))),
    ]
    out_shape = [
        jax.ShapeDtypeStruct((nseq // r, r, h * GLA_HV), F32),
        jax.ShapeDtypeStruct(state.shape, F32),
    ]
    args = (proj3, proj3, proj3, proj3, alr3, wa2p, ba2, gn, state)
    assert len(in_specs) == GLA_DEC_INPUTS
    return n_items, in_specs, args, out_specs, out_shape


HALO_ROWS = 16


def _shift_rows(x, halo, ri):
    h1 = halo[HALO_ROWS - 1:HALO_ROWS, :]
    h2 = halo[HALO_ROWS - 2:HALO_ROWS - 1, :]
    p1 = jnp.where(ri == 0, h1, pltpu.roll(x, 1, 0))
    p2 = jnp.where(ri == 0, h2, jnp.where(ri == 1, h1, pltpu.roll(x, 2, 0)))
    return p1, p2


def _conv_kernel(cb_ref, cc_ref, ch_ref, cch_ref, chh_ref, w_ref, o_ref, st_ref):
    tl = cb_ref.shape[0]
    cch = cc_ref[...].astype(F32) * ch_ref[...].astype(F32)
    halo = cch_ref[...].astype(F32) * chh_ref[...].astype(F32)
    halo = jnp.where(pl.program_id(1) == 0, 0.0, halo)
    ri = lax.broadcasted_iota(jnp.int32, (tl, 1), 0)
    p1, p2 = _shift_rows(cch, halo, ri)
    w = w_ref[...]
    u = w[0:1, :] * p2 + w[1:2, :] * p1 + w[2:3, :] * cch
    o_ref[...] = (cb_ref[...].astype(F32) * u).astype(o_ref.dtype)
    st_ref[...] = cch[tl - 2:tl, :]


def _conv_prompt(proj_b, w_conv, nb, seq):
    tl = min(512, seq)
    nt = seq // tl
    t8 = tl // HALO_ROWS
    cd = CONV_DIM
    halo = lambda col: (lambda b, t: (jnp.maximum((b * nt + t) * t8 - 1, 0), col))
    return pl.pallas_call(
        _conv_kernel,
        grid=(nb, nt),
        in_specs=[
            pl.BlockSpec((tl, cd), lambda b, t: (b * nt + t, 0)),
            pl.BlockSpec((tl, cd), lambda b, t: (b * nt + t, 1)),
            pl.BlockSpec((tl, cd), lambda b, t: (b * nt + t, 2)),
            pl.BlockSpec((HALO_ROWS, cd), halo(1)),
            pl.BlockSpec((HALO_ROWS, cd), halo(2)),
            pl.BlockSpec((3, cd), lambda b, t: (0, 0)),
        ],
        out_specs=[
            pl.BlockSpec((tl, cd), lambda b, t: (b * nt + t, 0)),
            pl.BlockSpec((None, 2, cd), lambda b, t: (b, 0, 0)),
        ],
        out_shape=[
            jax.ShapeDtypeStruct((nb * seq, cd), BF16),
            jax.ShapeDtypeStruct((nb, 2, cd), F32),
        ],
        compiler_params=_params("parallel", "arbitrary"),
        name="conv_prompt",
    )(proj_b, proj_b, proj_b, proj_b, proj_b, w_conv)


def _conv_dec_kernel(cb_ref, cc_ref, ch_ref, s0_ref, s1_ref, w_ref, o_ref, st_ref):
    cd = cb_ref.shape[1]
    cch = cc_ref[...] * ch_ref[...]
    s1 = s1_ref[...]
    w = w_ref[...]
    u = w[0:1, :] * s0_ref[...] + w[1:2, :] * s1 + w[2:3, :] * cch
    o_ref[...] = (cb_ref[...] * u).astype(o_ref.dtype)
    st_ref[:, :cd] = s1
    st_ref[:, cd:] = cch


def _conv_decode(proj_b, w_conv, state2d):
    n = proj_b.shape[0]
    cd = CONV_DIM
    blk = lambda col: pl.BlockSpec((n, cd), lambda i: (0, col))
    return pl.pallas_call(
        _conv_dec_kernel,
        grid=(1,),
        in_specs=[blk(0), blk(1), blk(2), blk(0), blk(1), pl.BlockSpec((3, cd), lambda i: (0, 0))],
        out_specs=[pl.BlockSpec((n, cd), lambda i: (0, 0)), pl.BlockSpec((n, 2 * cd), lambda i: (0, 0))],
        out_shape=[jax.ShapeDtypeStruct((n, cd), BF16), jax.ShapeDtypeStruct((n, 2 * cd), F32)],
        compiler_params=_params("arbitrary"),
        name="conv_decode",
    )(proj_b, proj_b, proj_b, state2d, state2d, w_conv)


def _softmax_rows(logits):
    m = jnp.max(logits, axis=-1, keepdims=True)
    p = jnp.exp(logits - m)
    return p / jnp.sum(p, axis=-1, keepdims=True)


def _memattn_kernel(q_ref, k_ref, v_ref, o_ref):
    logits = _dot(q_ref[...].astype(BF16), k_ref[...].astype(BF16), NT) * (MEM_HD ** -0.5)
    p = _softmax_rows(logits)
    o_ref[...] = _dot(p.astype(BF16), v_ref[...].astype(BF16)).astype(o_ref.dtype)


def _memattn_prompt(proj_b, mk, mv, nb, seq):
    tl = min(512, seq)
    nt = seq // tl
    h = MEM_HEADS
    qcol = 3 * CONV_DIM // MEM_HD
    return pl.pallas_call(
        _memattn_kernel,
        grid=(nb, h, nt),
        in_specs=[
            pl.BlockSpec((tl, MEM_HD), lambda b, hh, t: (b * nt + t, qcol + hh)),
            pl.BlockSpec((N_MEM, MEM_HD), lambda b, hh, t: (b, hh)),
            pl.BlockSpec((N_MEM, MEM_HD), lambda b, hh, t: (b, hh)),
        ],
        out_specs=pl.BlockSpec((tl, MEM_HD), lambda b, hh, t: (b * nt + t, hh)),
        out_shape=jax.ShapeDtypeStruct((nb * seq, MEM_DIM), BF16),
        compiler_params=_params("parallel", "parallel", "parallel"),
        name="memattn_prompt",
    )(proj_b, mk, mv)


MEMATTN_DEC_ROWS = 2


MEM_HD_CHUNKS = MEM_HD // LANES
MEM_TILE_ROWS = MEM_HD_CHUNKS * MEM_HEADS
assert MEM_HD_CHUNKS == 2 and MEM_TILE_ROWS == 8


def _heads_to_tile(x):
    lead = x.shape[:-2]
    x = x.reshape(lead + (MEM_HEADS, MEM_HD_CHUNKS, LANES))
    return jnp.swapaxes(x, -3, -2).reshape(lead + (MEM_TILE_ROWS, LANES))


def _tile_to_heads(x):
    lead = x.shape[:-2]
    x = x.reshape(lead + (MEM_HD_CHUNKS, MEM_HEADS, LANES))
    return jnp.swapaxes(x, -3, -2).reshape(lead + (MEM_HEADS, MEM_HD))


def _memattn_dec_kernel(q_ref, k_ref, v_ref, o_ref):
    def body(i, carry):
        q = q_ref[i] * (MEM_HD ** -0.5)
        prod = k_ref[i] * q[None]
        prod = prod + pltpu.roll(prod, MEM_HEADS, 1)
        logits = jnp.sum(prod, axis=-1, keepdims=True)
        e = jnp.exp(logits - jnp.max(logits, axis=0, keepdims=True))
        o_ref[i] = jnp.sum(e * v_ref[i], axis=0) / jnp.sum(e, axis=0)
        return carry

    lax.fori_loop(0, q_ref.shape[0], body, 0, unroll=True)


def _memattn_decode_specs(item, q_t, k_t, v_t):
    n = q_t.shape[0]
    r = MEMATTN_DEC_ROWS
    n_items = n // r
    tile = (MEM_TILE_ROWS, LANES)

    def at(ndim):
        return lambda *grid_idx: (jnp.minimum(item(*grid_idx), n_items - 1),) + (0,) * (ndim - 1)

    in_specs = [
        pl.BlockSpec((r,) + tile, at(3)),
        pl.BlockSpec((r, N_MEM) + tile, at(4)),
        pl.BlockSpec((r, N_MEM) + tile, at(4)),
    ]
    out_specs = [pl.BlockSpec((r,) + tile, at(3))]
    out_shape = [jax.ShapeDtypeStruct((n,) + tile, F32)]
    return n_items, in_specs, (q_t, k_t, v_t), out_specs, out_shape


def _merge_kernel(a_ref, b_ref, m_ref, wa_ref, wb_ref, wm_ref, za_ref, zb_ref, zm_ref, o_ref):
    ya = _dot(a_ref[...].astype(BF16), wa_ref[...])
    yb = _dot(b_ref[...].astype(BF16), wb_ref[...])
    ym = _dot(m_ref[...].astype(BF16), wm_ref[...])
    gate = lambda z_ref: _sigmoid(z_ref[...].astype(F32))
    merged = gate(za_ref) * ya + gate(zb_ref) * yb + gate(zm_ref) * ym
    o_ref[...] = merged.astype(o_ref.dtype)


def _merge(o_gla, cbu, om, w_gla_out, w_conv_out, w_mem_out, proj_b):
    m = o_gla.shape[0]
    tm = min(1024, m)
    tn = 512
    d = D_MODEL
    zoff = (3 * CONV_DIM + MEM_DIM) // tn
    zspec = lambda g: pl.BlockSpec((tm, tn), lambda i, j: (i, zoff + g * (d // tn) + j))
    return pl.pallas_call(
        _merge_kernel,
        grid=(m // tm, d // tn),
        in_specs=[
            pl.BlockSpec((tm, o_gla.shape[1]), lambda i, j: (i, 0)),
            pl.BlockSpec((tm, cbu.shape[1]), lambda i, j: (i, 0)),
            pl.BlockSpec((tm, om.shape[1]), lambda i, j: (i, 0)),
            pl.BlockSpec((w_gla_out.shape[0], tn), lambda i, j: (0, j)),
            pl.BlockSpec((w_conv_out.shape[0], tn), lambda i, j: (0, j)),
            pl.BlockSpec((w_mem_out.shape[0], tn), lambda i, j: (0, j)),
            zspec(0), zspec(1), zspec(2),
        ],
        out_specs=pl.BlockSpec((tm, tn), lambda i, j: (i, j)),
        out_shape=jax.ShapeDtypeStruct((m, d), BF16),
        compiler_params=_params("parallel", "parallel"),
        name="merge",
    )(o_gla, cbu, om, w_gla_out, w_conv_out, w_mem_out, proj_b, proj_b, proj_b)


def _wo_ln_kernel(x_ref, m_ref, w_ref, g_ref, b_ref, o_ref):
    tm = x_ref.shape[0]
    half = tm // 2 if tm % 256 == 0 else tm
    for r in range(0, tm, half):
        rows = pl.ds(r, half)
        y = DN_ALPHA * x_ref[rows, :] + _dot(m_ref[rows, :], w_ref[...])
        o_ref[rows, :] = _layer_norm(y, g_ref[...], b_ref[...])


def _wo_ln(x, merged, w_o, g, b):
    m, d = x.shape
    tm = min(512, m)
    return pl.pallas_call(
        _wo_ln_kernel,
        grid=(m // tm,),
        in_specs=[
            pl.BlockSpec((tm, d), lambda i: (i, 0)),
            pl.BlockSpec((tm, d), lambda i: (i, 0)),
            pl.BlockSpec((d, d), lambda i: (0, 0)),
            pl.BlockSpec((1, d), lambda i: (0, 0)),
            pl.BlockSpec((1, d), lambda i: (0, 0)),
        ],
        out_specs=pl.BlockSpec((tm, d), lambda i: (i, 0)),
        out_shape=jax.ShapeDtypeStruct((m, d), F32),
        compiler_params=_params("parallel"),
        name="wo_ln",
    )(x, merged, w_o, g, b)


def _ffn_tail(hc, hu, wd_ref, acc_ref, x_ref, g_ref, b_ref, o_ref, fi):
    tf = hc.shape[1]
    valid = D_FF - fi * tf
    h = hc * _sigmoid(hc) * hu
    h = jnp.where(lax.broadcasted_iota(jnp.int32, (1, tf), 1) < valid, h, 0.0).astype(BF16)
    wd = jnp.where(lax.broadcasted_iota(jnp.int32, (tf, 1), 0) < valid, wd_ref[...], 0.0)
    acc_ref[...] += _dot(h, wd)

    @pl.when(fi == pl.num_programs(1) - 1)
    def _():
        o_ref[...] = _layer_norm(DN_ALPHA * x_ref[...] + acc_ref[...], g_ref[...], b_ref[...])


def _ffn_kernel(seq, x_ref, xh_ref, wg_ref, wu_ref, wd_ref, wc_ref, bc_ref, g_ref, b_ref, *rest):
    dec_in, (o_ref, st_ref, od_ref, so_ref, xb_scr, acc_scr) = rest[:GLA_DEC_INPUTS], rest[GLA_DEC_INPUTS:]
    fi = pl.program_id(1)
    tm = x_ref.shape[0]

    @pl.when(fi == 0)
    def _():
        xb_scr[...] = x_ref[...].astype(BF16)
        acc_scr[...] = jnp.zeros_like(acc_scr)

    _gla_dec_item(*dec_in, od_ref, so_ref)
    xb = xb_scr[...]
    wg = wg_ref[...]
    hg = _dot(xb, wg)
    st_ref[...] = hg[tm - 2:tm, :]
    halo = _dot(xh_ref[...].astype(BF16), wg)
    halo = jnp.where((pl.program_id(0) * tm) % seq == 0, 0.0, halo)
    ri = lax.broadcasted_iota(jnp.int32, (tm, 1), 0)
    p1, p2 = _shift_rows(hg, halo, ri)
    wc = wc_ref[...]
    hc = wc[0:1, :] * p2 + wc[1:2, :] * p1 + wc[2:3, :] * hg + bc_ref[...]
    _ffn_tail(hc, _dot(xb, wu_ref[...]), wd_ref, acc_scr, x_ref, g_ref, b_ref, o_ref, fi)


def _ffn_prompt(x1, w_gate, w_up, w_down, w_conv, b_conv, g, b, seq, gla_dec_operands):
    m, d = x1.shape
    tm = min(512, seq)
    tf = FF_TILE
    t8 = tm // HALO_ROWS
    n_items, dec_in_specs, dec_args, dec_out_specs, dec_out_shape = _gla_decode_specs(
        lambda i, f: i * FF_TILES + f, *gla_dec_operands)
    assert n_items <= (m // tm) * FF_TILES, "not enough FFN steps to carry the decode GLA work items"
    return pl.pallas_call(
        functools.partial(_ffn_kernel, seq),
        grid=(m // tm, FF_TILES),
        in_specs=[
            pl.BlockSpec((tm, d), lambda i, f: (i, 0)),
            pl.BlockSpec((HALO_ROWS, d), lambda i, f: (jnp.maximum(i * t8 - 1, 0), 0)),
            pl.BlockSpec((d, tf), lambda i, f: (0, f)),
            pl.BlockSpec((d, tf), lambda i, f: (0, f)),
            pl.BlockSpec((tf, d), lambda i, f: (f, 0)),
            pl.BlockSpec((3, tf), lambda i, f: (0, f)),
            pl.BlockSpec((1, tf), lambda i, f: (0, f)),
            pl.BlockSpec((1, d), lambda i, f: (0, 0)),
            pl.BlockSpec((1, d), lambda i, f: (0, 0)),
        ] + dec_in_specs,
        out_specs=[pl.BlockSpec((tm, d), lambda i, f: (i, 0)),
                   pl.BlockSpec((None, 2, tf), lambda i, f: (i // (seq // tm), 0, f))] + dec_out_specs,
        out_shape=[jax.ShapeDtypeStruct((m, d), F32),
                   jax.ShapeDtypeStruct((m // seq, 2, D_FF), F32)] + dec_out_shape,
        scratch_shapes=[pltpu.VMEM((tm, d), BF16), pltpu.VMEM((tm, d), F32)],
        compiler_params=_params("arbitrary", "arbitrary"),
        name="ffn_prompt_gla_decode",
    )(x1, x1, w_gate, w_up, w_down, w_conv, b_conv, g, b, *dec_args)


def _ffn_dec_kernel(x_ref, s0_ref, s1_ref, wg_ref, wu_ref, wd_ref, wc_ref, bc_ref, g_ref, b_ref,
                    o_ref, hg_ref, xb_scr, acc_scr):
    fi = pl.program_id(1)

    @pl.when(fi == 0)
    def _():
        xb_scr[...] = x_ref[...].astype(BF16)
        acc_scr[...] = jnp.zeros_like(acc_scr)

    xb = xb_scr[...]
    hg = _dot(xb, wg_ref[...])
    hg_ref[...] = hg
    wc = wc_ref[...]
    hc = wc[0:1, :] * s0_ref[...] + wc[1:2, :] * s1_ref[...] + wc[2:3, :] * hg + bc_ref[...]
    _ffn_tail(hc, _dot(xb, wu_ref[...]), wd_ref, acc_scr, x_ref, g_ref, b_ref, o_ref, fi)


def _ffn_decode(x1, s0, s1, w_gate, w_up, w_down, w_conv, b_conv, g, b):
    m, d = x1.shape
    tf = FF_TILE
    return pl.pallas_call(
        _ffn_dec_kernel,
        grid=(1, FF_TILES),
        in_specs=[
            pl.BlockSpec((m, d), lambda i, f: (0, 0)),
            pl.BlockSpec((m, tf), lambda i, f: (0, f)),
            pl.BlockSpec((m, tf), lambda i, f: (0, f)),
            pl.BlockSpec((d, tf), lambda i, f: (0, f)),
            pl.BlockSpec((d, tf), lambda i, f: (0, f)),
            pl.BlockSpec((tf, d), lambda i, f: (f, 0)),
            pl.BlockSpec((3, tf), lambda i, f: (0, f)),
            pl.BlockSpec((1, tf), lambda i, f: (0, f)),
            pl.BlockSpec((1, d), lambda i, f: (0, 0)),
            pl.BlockSpec((1, d), lambda i, f: (0, 0)),
        ],
        out_specs=[pl.BlockSpec((m, d), lambda i, f: (0, 0)), pl.BlockSpec((m, tf), lambda i, f: (0, f))],
        out_shape=[jax.ShapeDtypeStruct((m, d), F32), jax.ShapeDtypeStruct((m, D_FF), F32)],
        scratch_shapes=[pltpu.VMEM((m, d), BF16), pltpu.VMEM((m, d), F32)],
        compiler_params=_params("parallel", "arbitrary"),
        name="ffn_decode",
    )(x1, s0, s1, w_gate, w_up, w_down, w_conv, b_conv, g, b)


def kernel(x_prompt, x_sample, mem_prompt, cache_mem_k, cache_mem_v, state_gla, state_conv, state_ffn_conv, w_in, w_gla_a2, b_gla_a2, g_gla_norm, w_gla_out, w_conv, w_conv_out, w_mem_k, w_mem_v, w_mem_out, w_o, ln1_g, ln1_b, w_ffn_gate, w_ffn_up, w_ffn_conv, b_ffn_conv, w_ffn_down, ln2_g, ln2_b):
    nb, seq, d = x_prompt.shape
    ns = x_sample.shape[0]

    w_in_t = jnp.swapaxes(w_in[0], 0, 1)
    wa2p = jnp.pad(w_gla_a2[0], ((0, LANES - GLA_RANK), (0, 0)))
    ba2 = b_gla_a2
    gn = g_gla_norm
    w_go = w_gla_out[0].astype(BF16)
    w_co = w_conv_out[0].astype(BF16)
    w_mo = w_mem_out[0].astype(BF16)
    w_oo = w_o[0].astype(BF16)
    w_fg = w_ffn_gate[0].astype(BF16)
    w_fu = w_ffn_up[0].astype(BF16)
    w_fd = w_ffn_down[0].astype(BF16)
    w_fc = w_ffn_conv[0]
    b_fc = b_ffn_conv
    w_cv = w_conv[0]

    def mix_in(x2d, tag, dtype, rider_b=None):
        xb = x2d.astype(BF16)
        return (_proj(xb, w_in_t, "proj_a_" + tag, 0, COLS_A, transposed=True, out_dtype=dtype),
                _proj(xb, w_in_t, "proj_r_" + tag, COLS_A, LANES, LANES, transposed=True),
                _proj(xb, w_in_t, "proj_b_" + tag, COLS_A, COLS_B, shift=GLA_RANK, transposed=True,
                      rider=rider_b, out_dtype=dtype))

    def mix_out(x2d, o_gla, cbu, om, proj_b):
        merged = _merge(o_gla, cbu, om, w_go, w_co, w_mo, proj_b)
        return _wo_ln(x2d, merged, w_oo, ln1_g, ln1_b)

    xs = x_sample.reshape(ns, d)
    sa, sr, sb = mix_in(xs, "s", F32)
    mq_s = sb[:, 3 * CONV_DIM:3 * CONV_DIM + MEM_DIM].reshape(ns, MEM_HEADS, MEM_HD)
    memattn_dec = (_memattn_dec_kernel, _memattn_decode_specs,
                   (_heads_to_tile(mq_s), _heads_to_tile(cache_mem_k[0]), _heads_to_tile(cache_mem_v[0])))

    xp = x_prompt.reshape(nb * seq, d)
    memb = mem_prompt.reshape(nb * N_MEM, d).astype(BF16)
    mk = _proj(memb, w_mem_k[0], "mem_k")
    mv = _proj(memb, w_mem_v[0], "mem_v")
    pa, pr, (pb, om_s) = mix_in(xp, "p", BF16, memattn_dec)
    o_gla, p_gla = _gla_prompt(pa, pr, wa2p, ba2, gn, nb, seq)
    cbu, p_conv = _conv_prompt(pb, w_cv, nb, seq)
    om = _memattn_prompt(pb, mk, mv, nb, seq)
    x1 = mix_out(xp, o_gla, cbu, om, pb)
    yp, p_ffn, o_gla_s, s_gla = _ffn_prompt(x1, w_fg, w_fu, w_fd, w_fc, b_fc, ln2_g, ln2_b, seq,
                                            (sa, sr, wa2p, ba2, gn, state_gla[0]))
    o_gla_s = o_gla_s.reshape(ns, GLA_HEADS * GLA_HV)

    cbu_s, s_conv = _conv_decode(sb, w_cv, state_conv[0].reshape(ns, 2 * CONV_DIM))
    om_s = _tile_to_heads(om_s).reshape(ns, MEM_DIM)
    x1s = mix_out(xs, o_gla_s, cbu_s, om_s, sb)
    f0 = state_ffn_conv[0][:, 0, :]
    f1 = state_ffn_conv[0][:, 1, :]
    ys, hg_s = _ffn_decode(x1s, f0, f1, w_fg, w_fu, w_fd, w_fc, b_fc, ln2_g, ln2_b)
    s_ffn = jnp.stack([f1, hg_s], axis=1)

    return (yp.reshape(nb, seq, d), ys.reshape(ns, 1, d),
            mk.reshape(1, nb, N_MEM, MEM_HEADS, MEM_HD), mv.reshape(1, nb, N_MEM, MEM_HEADS, MEM_HD),
            p_gla[None], p_conv[None], p_ffn.reshape(1, nb, 2, D_FF),
            s_gla[None], s_conv.reshape(1, ns, 2, CONV_DIM), s_ffn[None])
```

```python
import functools

import jax
import jax.numpy as jnp
from jax import lax
from jax.experimental import pallas as pl
from jax.experimental.pallas import tpu as pltpu

F32 = jnp.float32
BF16 = jnp.bfloat16

D_MODEL = 2048
GLA_HEADS = 4
GLA_HK = 256
GLA_HV = 512
GLA_RANK = 16
GLA_GATE_NORM = 16.0
CONV_DIM = 1024
N_MEM = 256
MEM_HEADS = 4
MEM_HD = 256
MEM_DIM = 1024
D_FF = 5504
DEPTH = 1
DN_ALPHA = (2 * DEPTH) ** 0.25
LN_EPS = 1e-5
RMS_EPS = 1e-6

LANES = 128
SUBLANES = 8
LOG2_E = 1.4426950408889634
FF_TILE = 512
FF_TILES = -(-D_FF // FF_TILE)
COLS_A = 2 * GLA_HEADS * GLA_HK + 2 * GLA_HEADS * GLA_HV
COLS_B = 3 * CONV_DIM + MEM_DIM + 3 * D_MODEL
GLA_CHUNK = 256
VMEM_LIMIT = 56 * 1024 * 1024

NN = (((1,), (0,)), ((), ()))
NT = (((1,), (1,)), ((), ()))
TN = (((0,), (0,)), ((), ()))


def _dot(a, b, dims=NN):
    return lax.dot_general(a, b, dims, preferred_element_type=F32)


def _params(*sem):
    return pltpu.CompilerParams(dimension_semantics=sem, vmem_limit_bytes=VMEM_LIMIT)


def _split2(x):
    hi = x.astype(BF16)
    lo = (x - hi.astype(F32)).astype(BF16)
    return hi, lo


def _split3(x):
    hi = x.astype(BF16)
    r = x - hi.astype(F32)
    mid = r.astype(BF16)
    lo = (r - mid.astype(F32)).astype(BF16)
    return hi, mid, lo


def _sigmoid(z):
    return 0.5 * jnp.tanh(0.5 * z) + 0.5


def _silu(z):
    h = 0.5 * z
    return h * jnp.tanh(h) + h


def _layer_norm(y, g, b):
    mu = jnp.mean(y, axis=-1, keepdims=True)
    d = y - mu
    var = jnp.mean(d * d, axis=-1, keepdims=True)
    return d * lax.rsqrt(var + LN_EPS) * g + b


def _log2_decay(z):
    return (jnp.minimum(z, 0.0) - jnp.log(1.0 + jnp.exp(-jnp.abs(z)))) * (LOG2_E / GLA_GATE_NORM)


def _log_decay(alr, wa2, ba2):
    ah, al = _split2(alr)
    wh, wl = _split2(wa2)
    return _log2_decay(_dot(ah, wh) + _dot(ah, wl) + _dot(al, wh) + ba2)


def _rms_gate(o, gn, g):
    o = o * lax.rsqrt(jnp.mean(o * o, axis=-1, keepdims=True) + RMS_EPS) * gn
    return o * _silu(g)


PROJ_SHIFT_ROWS = 16


def _proj_kernel(transposed, shift, rider, x_ref, w_ref, *rest):
    if shift:
        wx_ref, rest = rest[0], rest[1:]
    rider_refs = ()
    if rider:
        body, n_in = rider
        rider_refs, rest = rest[:n_in] + rest[n_in + 1:-1], (rest[n_in], rest[-1])
    o_ref, wb_scr = rest

    @pl.when(pl.program_id(1) == 0)
    def _():
        w = w_ref[...]
        if shift:
            w = jnp.concatenate([w[shift:], wx_ref[:shift]], axis=0)
        if transposed:
            w = w.T
        wb_scr[...] = w.astype(BF16)

    if rider:
        body(*rider_refs)
    o_ref[...] = _dot(x_ref[...].astype(BF16), wb_scr[...]).astype(o_ref.dtype)


def _proj(x, w, name, col0=0, n=None, tn=1024, shift=0, transposed=False, rider=None, out_dtype=F32):
    m, k = x.shape
    if n is None:
        n = w.shape[0] if transposed else w.shape[1]
    tm = min(m, 1024)
    tn = min(n, tn)
    c0 = col0 // tn
    if transposed:
        w_spec = pl.BlockSpec((tn, k), lambda j, i: (c0 + j, 0))
    else:
        assert not shift
        w_spec = pl.BlockSpec((k, tn), lambda j, i: (0, c0 + j))
    in_specs = [pl.BlockSpec((tm, k), lambda j, i: (i, 0)), w_spec]
    args = [x, w]
    if shift:
        r = PROJ_SHIFT_ROWS
        assert shift <= r and shift % 8 == 0 and tn % r == 0 and col0 % r == 0
        in_specs.append(pl.BlockSpec((r, k), lambda j, i: (col0 // r + (j + 1) * (tn // r), 0)))
        args.append(w)
    out_specs = [pl.BlockSpec((tm, tn), lambda j, i: (i, j))]
    out_shape = [jax.ShapeDtypeStruct((m, n), out_dtype)]
    kernel_rider = None
    if rider:
        body, specs_fn, operands = rider
        n_items, r_in_specs, r_args, r_out_specs, r_out_shape = specs_fn(lambda j, i: j * (m // tm) + i, *operands)
        assert n_items <= (n // tn) * (m // tm), "not enough projection steps to carry the rider's work items"
        in_specs += r_in_specs
        args += list(r_args)
        out_specs += r_out_specs
        out_shape += r_out_shape
        kernel_rider = (body, len(r_in_specs))
    outs = pl.pallas_call(
        functools.partial(_proj_kernel, transposed, shift, kernel_rider),
        grid=(n // tn, m // tm),
        in_specs=in_specs,
        out_specs=out_specs,
        out_shape=out_shape,
        scratch_shapes=[pltpu.VMEM((k, tn), BF16)],
        compiler_params=_params("arbitrary" if rider else "parallel", "arbitrary"),
        name=name,
    )(*args)
    return outs if rider else outs[0]


def _gla_levels(c):
    return c.bit_length() - 1


def _init_gla_masks(mask_scr, tri_scr):
    c = tri_scr.shape[0]
    rr = lax.broadcasted_iota(jnp.int32, (c, c), 0)
    cc = lax.broadcasted_iota(jnp.int32, (c, c), 1)
    tri_scr[...] = (rr >= cc).astype(BF16)
    mask_scr[0] = (rr == cc).astype(F32)
    for l in range(_gla_levels(c)):
        b = 1 << l
        pair = (rr // (2 * b) == cc // (2 * b)) & ((rr // b) % 2 == 1) & ((cc // b) % 2 == 0)
        mask_scr[1 + l] = pair.astype(F32)


def _intra_scores(q, k, la, bc, mask_scr):
    c, dk = q.shape
    ri = lax.broadcasted_iota(jnp.int32, (c, 1), 0)
    a = jnp.sum(q * k, axis=1, keepdims=True) * mask_scr[0]
    for l in range(_gla_levels(c)):
        b = 1 << l
        if b < SUBLANES:
            upper = (ri // b) % 2 == 1
            if b == 1:
                e = jnp.where(upper, la, 0.0)
            elif b == 2:
                m4 = ri % 4
                la_prev = pltpu.roll(la, 1, 0)
                la_next = pltpu.roll(la, c - 1, 0)
                e = jnp.where(m4 == 2, la, jnp.where(m4 == 3, la + la_prev, jnp.where(m4 == 0, la_next, 0.0)))
            else:
                ref = bc.reshape(c // (2 * b), 2 * b, dk)[:, b - 1:b, :]
                ref = jnp.broadcast_to(ref, (c // (2 * b), 2 * b, dk)).reshape(c, dk)
                dlt = bc - ref
                e = jnp.minimum(dlt, -dlt)
            w = jnp.where(upper, q, k) * jnp.exp2(e)
        else:
            parts = []
            for r0 in range(0, c, 2 * b):
                ref = bc[r0 + b - 1:r0 + b, :]
                parts.append(k[r0:r0 + b] * jnp.exp2(ref - bc[r0:r0 + b]))
                parts.append(q[r0 + b:r0 + 2 * b] * jnp.exp2(bc[r0 + b:r0 + 2 * b] - ref))
            w = jnp.concatenate(parts, axis=0)
        w = w.astype(BF16)
        a = a + _dot(w, w, NT) * mask_scr[1 + l]
    return a


def _gla_kernel(q_ref, k_ref, v_ref, g_ref, alr_ref, wa2_ref, ba2_ref, gn_ref, o_ref, sfin_ref,
                s_scr, mask_scr, tri_scr):
    ci = pl.program_id(2)

    @pl.when(ci == 0)
    def _():
        s_scr[...] = jnp.zeros_like(s_scr)
        _init_gla_masks(mask_scr, tri_scr)

    c = q_ref.shape[0]
    tri = tri_scr[...]
    ones = jnp.ones((c, LANES), BF16)
    alr = alr_ref[...]
    gn = gn_ref[...]
    for hh in range(s_scr.shape[0]):
        ks = slice(hh * GLA_HK, (hh + 1) * GLA_HK)
        vs = slice(hh * GLA_HV, (hh + 1) * GLA_HV)
        la = _log_decay(alr, wa2_ref[:, ks], ba2_ref[:, ks])
        lh, lm, ll = _split3(la)
        bc = _dot(tri, lh) + _dot(tri, lm) + _dot(tri, ll)
        b_last = bc[c - 1:c, :]
        bl_col = _dot(lh, ones, TN) + _dot(lm, ones, TN) + _dot(ll, ones, TN)
        dec_col = jnp.exp2(bl_col)
        dec_col = jnp.concatenate([dec_col] * (GLA_HV // LANES), axis=1)

        q = q_ref[:, ks].astype(F32) * (GLA_HK ** -0.5)
        k = k_ref[:, ks].astype(F32)
        vb = v_ref[:, vs].astype(BF16)
        s = s_scr[hh]
        o = _dot((q * jnp.exp2(bc)).astype(BF16), s.astype(BF16))
        a = _intra_scores(q, k, la, bc, mask_scr)
        o = o + _dot(a.astype(BF16), vb)
        kd = (k * jnp.exp2(b_last - bc)).astype(BF16)
        s_new = dec_col * s + _dot(kd, vb, TN)
        s_scr[hh] = s_new
        o_ref[:, vs] = _rms_gate(o, gn, g_ref[:, vs].astype(F32)).astype(o_ref.dtype)

    @pl.when(ci == pl.num_programs(2) - 1)
    def _():
        sfin_ref[...] = s_scr[...]


GLA_HEADS_PER_STEP = 2


def _gla_prompt(proj_a, alr, wa2p, ba2, gn, nb, seq):
    c = min(GLA_CHUNK, seq)
    nc = seq // c
    hp = GLA_HEADS_PER_STEP
    ng = GLA_HEADS // hp
    wk, wv = hp * GLA_HK, hp * GLA_HV
    row = lambda b, hg, ci: b * nc + ci
    return pl.pallas_call(
        _gla_kernel,
        grid=(nb, ng, nc),
        in_specs=[
            pl.BlockSpec((c, wk), lambda b, hg, ci: (row(b, hg, ci), hg)),
            pl.BlockSpec((c, wk), lambda b, hg, ci: (row(b, hg, ci), ng + hg)),
            pl.BlockSpec((c, wv), lambda b, hg, ci: (row(b, hg, ci), ng + hg)),
            pl.BlockSpec((c, wv), lambda b, hg, ci: (row(b, hg, ci), 2 * ng + hg)),
            pl.BlockSpec((c, LANES), lambda b, hg, ci: (row(b, hg, ci), 0)),
            pl.BlockSpec((LANES, wk), lambda b, hg, ci: (0, hg)),
            pl.BlockSpec((1, wk), lambda b, hg, ci: (0, hg)),
            pl.BlockSpec((1, GLA_HV), lambda b, hg, ci: (0, 0)),
        ],
        out_specs=[
            pl.BlockSpec((c, wv), lambda b, hg, ci: (row(b, hg, ci), hg)),
            pl.BlockSpec((None, hp, GLA_HK, GLA_HV), lambda b, hg, ci: (b, hg, 0, 0)),
        ],
        out_shape=[
            jax.ShapeDtypeStruct((nb * seq, GLA_HEADS * GLA_HV), BF16),
            jax.ShapeDtypeStruct((nb, GLA_HEADS, GLA_HK, GLA_HV), F32),
        ],
        scratch_shapes=[pltpu.VMEM((hp, GLA_HK, GLA_HV), F32),
                        pltpu.VMEM((1 + _gla_levels(c), c, c), F32),
                        pltpu.VMEM((c, c), BF16)],
        compiler_params=_params("parallel", "parallel", "arbitrary"),
        name="gla_prompt",
    )(proj_a, proj_a, proj_a, proj_a, alr, wa2p, ba2, gn)


GLA_DEC_ROWS = 4


def _gla_dec_item(q_ref, k_ref, v_ref, g_ref, alr_ref, wa2_ref, ba2_ref, gn_ref, s_ref, o_ref, so_ref):
    nb = q_ref.shape[0]
    alr = alr_ref[...]
    wa2 = wa2_ref[...]
    z = ba2_ref[...]
    for r in range(GLA_RANK):
        z = z + alr[:, r:r + 1] * wa2[r:r + 1, :]
    la = _log2_decay(z)
    rep = LANES // nb
    to_cols = lambda x: jnp.concatenate([x] * rep, axis=0).T
    a_t = to_cols(jnp.exp2(la))
    k_t = to_cols(k_ref[...])
    q_t = to_cols(q_ref[...] * (GLA_HK ** -0.5))
    v = v_ref[...]
    rows = []
    for j in range(nb):
        s_new = a_t[:, j:j + 1] * s_ref[j] + k_t[:, j:j + 1] * v[j:j + 1, :]
        so_ref[j] = s_new
        rows.append(jnp.sum(q_t[:, j:j + 1] * s_new, axis=0, keepdims=True))
    o = jnp.concatenate(rows, axis=0)
    o_ref[...] = _rms_gate(o, gn_ref[...], g_ref[...]).astype(o_ref.dtype)


GLA_DEC_INPUTS = 9


def _gla_decode_specs(item, proj_a, alr, wa2p, ba2, gn, state):
    nseq = proj_a.shape[0]
    h = GLA_HEADS
    r = GLA_DEC_ROWS
    n_items = (nseq // r) * h
    proj3 = proj_a.reshape(nseq // r, r, proj_a.shape[1])
    alr3 = alr.reshape(nseq // r, r, LANES)

    def at(fn):
        def index_map(*grid_idx):
            it = jnp.minimum(item(*grid_idx), n_items - 1)
            return fn(it // h, it % h)
        return index_map

    in_specs = [
        pl.BlockSpec((None, r, GLA_HK), at(lambda b, hh: (b, 0, hh))),
        pl.BlockSpec((None, r, GLA_HK), at(lambda b, hh: (b, 0, h + hh))),
        pl.BlockSpec((None, r, GLA_HV), at(lambda b, hh: (b, 0, h + hh))),
        pl.BlockSpec((None, r, GLA_HV), at(lambda b, hh: (b, 0, 2 * h + hh))),
        pl.BlockSpec((None, r, LANES), at(lambda b, hh: (b, 0, 0))),
        pl.BlockSpec((LANES, GLA_HK), at(lambda b, hh: (0, hh))),
        pl.BlockSpec((1, GLA_HK), at(lambda b, hh: (0, hh))),
        pl.BlockSpec((1, GLA_HV), at(lambda b, hh: (0, 0))),
        pl.BlockSpec((r, None, GLA_HK, GLA_HV), at(lambda b, hh: (b, hh, 0, 0))),
    ]
    out_specs = [
        pl.BlockSpec((None, r, GLA_HV), at(lambda b, hh: (b, 0, hh))),
        pl.BlockSpec((r, None, GLA_HK, GLA_HV), at(lambda b, hh: (b, hh, 0, 0))),
    ]
    out_shape = [
        jax.ShapeDtypeStruct((nseq // r, r, h * GLA_HV), F32),
        jax.ShapeDtypeStruct(state.shape, F32),
    ]
    args = (proj3, proj3, proj3, proj3, alr3, wa2p, ba2, gn, state)
    assert len(in_specs) == GLA_DEC_INPUTS
    return n_items, in_specs, args, out_specs, out_shape


HALO_ROWS = 16


def _shift_rows(x, halo, ri):
    h1 = halo[HALO_ROWS - 1:HALO_ROWS, :]
    h2 = halo[HALO_ROWS - 2:HALO_ROWS - 1, :]
    p1 = jnp.where(ri == 0, h1, pltpu.roll(x, 1, 0))
    p2 = jnp.where(ri == 0, h2, jnp.where(ri == 1, h1, pltpu.roll(x, 2, 0)))
    return p1, p2


def _conv_kernel(cb_ref, cc_ref, ch_ref, cch_ref, chh_ref, w_ref, o_ref, st_ref):
    tl = cb_ref.shape[0]
    cch = cc_ref[...].astype(F32) * ch_ref[...].astype(F32)
    halo = cch_ref[...].astype(F32) * chh_ref[...].astype(F32)
    halo = jnp.where(pl.program_id(1) == 0, 0.0, halo)
    ri = lax.broadcasted_iota(jnp.int32, (tl, 1), 0)
    p1, p2 = _shift_rows(cch, halo, ri)
    w = w_ref[...]
    u = w[0:1, :] * p2 + w[1:2, :] * p1 + w[2:3, :] * cch
    o_ref[...] = (cb_ref[...].astype(F32) * u).astype(o_ref.dtype)
    st_ref[...] = cch[tl - 2:tl, :]


def _conv_prompt(proj_b, w_conv, nb, seq):
    tl = min(512, seq)
    nt = seq // tl
    t8 = tl // HALO_ROWS
    cd = CONV_DIM
    halo = lambda col: (lambda b, t: (jnp.maximum((b * nt + t) * t8 - 1, 0), col))
    return pl.pallas_call(
        _conv_kernel,
        grid=(nb, nt),
        in_specs=[
            pl.BlockSpec((tl, cd), lambda b, t: (b * nt + t, 0)),
            pl.BlockSpec((tl, cd), lambda b, t: (b * nt + t, 1)),
            pl.BlockSpec((tl, cd), lambda b, t: (b * nt + t, 2)),
            pl.BlockSpec((HALO_ROWS, cd), halo(1)),
            pl.BlockSpec((HALO_ROWS, cd), halo(2)),
            pl.BlockSpec((3, cd), lambda b, t: (0, 0)),
        ],
        out_specs=[
            pl.BlockSpec((tl, cd), lambda b, t: (b * nt + t, 0)),
            pl.BlockSpec((None, 2, cd), lambda b, t: (b, 0, 0)),
        ],
        out_shape=[
            jax.ShapeDtypeStruct((nb * seq, cd), BF16),
            jax.ShapeDtypeStruct((nb, 2, cd), F32),
        ],
        compiler_params=_params("parallel", "arbitrary"),
        name="conv_prompt",
    )(proj_b, proj_b, proj_b, proj_b, proj_b, w_conv)


def _conv_dec_kernel(cb_ref, cc_ref, ch_ref, s0_ref, s1_ref, w_ref, o_ref, st_ref):
    cd = cb_ref.shape[1]
    cch = cc_ref[...] * ch_ref[...]
    s1 = s1_ref[...]
    w = w_ref[...]
    u = w[0:1, :] * s0_ref[...] + w[1:2, :] * s1 + w[2:3, :] * cch
    o_ref[...] = (cb_ref[...] * u).astype(o_ref.dtype)
    st_ref[:, :cd] = s1
    st_ref[:, cd:] = cch


def _conv_decode(proj_b, w_conv, state2d):
    n = proj_b.shape[0]
    cd = CONV_DIM
    blk = lambda col: pl.BlockSpec((n, cd), lambda i: (0, col))
    return pl.pallas_call(
        _conv_dec_kernel,
        grid=(1,),
        in_specs=[blk(0), blk(1), blk(2), blk(0), blk(1), pl.BlockSpec((3, cd), lambda i: (0, 0))],
        out_specs=[pl.BlockSpec((n, cd), lambda i: (0, 0)), pl.BlockSpec((n, 2 * cd), lambda i: (0, 0))],
        out_shape=[jax.ShapeDtypeStruct((n, cd), BF16), jax.ShapeDtypeStruct((n, 2 * cd), F32)],
        compiler_params=_params("arbitrary"),
        name="conv_decode",
    )(proj_b, proj_b, proj_b, state2d, state2d, w_conv)


def _softmax_rows(logits):
    m = jnp.max(logits, axis=-1, keepdims=True)
    p = jnp.exp(logits - m)
    return p / jnp.sum(p, axis=-1, keepdims=True)


def _memattn_kernel(q_ref, k_ref, v_ref, o_ref):
    logits = _dot(q_ref[...].astype(BF16), k_ref[...].astype(BF16), NT) * (MEM_HD ** -0.5)
    p = _softmax_rows(logits)
    o_ref[...] = _dot(p.astype(BF16), v_ref[...].astype(BF16)).astype(o_ref.dtype)


def _memattn_prompt(proj_b, mk, mv, nb, seq):
    tl = min(512, seq)
    nt = seq // tl
    h = MEM_HEADS
    qcol = 3 * CONV_DIM // MEM_HD
    return pl.pallas_call(
        _memattn_kernel,
        grid=(nb, h, nt),
        in_specs=[
            pl.BlockSpec((tl, MEM_HD), lambda b, hh, t: (b * nt + t, qcol + hh)),
            pl.BlockSpec((N_MEM, MEM_HD), lambda b, hh, t: (b, hh)),
            pl.BlockSpec((N_MEM, MEM_HD), lambda b, hh, t: (b, hh)),
        ],
        out_specs=pl.BlockSpec((tl, MEM_HD), lambda b, hh, t: (b * nt + t, hh)),
        out_shape=jax.ShapeDtypeStruct((nb * seq, MEM_DIM), BF16),
        compiler_params=_params("parallel", "parallel", "parallel"),
        name="memattn_prompt",
    )(proj_b, mk, mv)


MEMATTN_DEC_ROWS = 2


MEM_HD_CHUNKS = MEM_HD // LANES
MEM_TILE_ROWS = MEM_HD_CHUNKS * MEM_HEADS
assert MEM_HD_CHUNKS == 2 and MEM_TILE_ROWS == 8


def _heads_to_tile(x):
    lead = x.shape[:-2]
    x = x.reshape(lead + (MEM_HEADS, MEM_HD_CHUNKS, LANES))
    return jnp.swapaxes(x, -3, -2).reshape(lead + (MEM_TILE_ROWS, LANES))


def _tile_to_heads(x):
    lead = x.shape[:-2]
    x = x.reshape(lead + (MEM_HD_CHUNKS, MEM_HEADS, LANES))
    return jnp.swapaxes(x, -3, -2).reshape(lead + (MEM_HEADS, MEM_HD))


def _memattn_dec_kernel(q_ref, k_ref, v_ref, o_ref):
    def body(i, carry):
        q = q_ref[i] * (MEM_HD ** -0.5)
        prod = k_ref[i] * q[None]
        prod = prod + pltpu.roll(prod, MEM_HEADS, 1)
        logits = jnp.sum(prod, axis=-1, keepdims=True)
        e = jnp.exp(logits - jnp.max(logits, axis=0, keepdims=True))
        o_ref[i] = jnp.sum(e * v_ref[i], axis=0) / jnp.sum(e, axis=0)
        return carry

    lax.fori_loop(0, q_ref.shape[0], body, 0, unroll=True)


def _memattn_decode_specs(item, q_t, k_t, v_t):
    n = q_t.shape[0]
    r = MEMATTN_DEC_ROWS
    n_items = n // r
    tile = (MEM_TILE_ROWS, LANES)

    def at(ndim):
        return lambda *grid_idx: (jnp.minimum(item(*grid_idx), n_items - 1),) + (0,) * (ndim - 1)

    in_specs = [
        pl.BlockSpec((r,) + tile, at(3)),
        pl.BlockSpec((r, N_MEM) + tile, at(4)),
        pl.BlockSpec((r, N_MEM) + tile, at(4)),
    ]
    out_specs = [pl.BlockSpec((r,) + tile, at(3))]
    out_shape = [jax.ShapeDtypeStruct((n,) + tile, F32)]
    return n_items, in_specs, (q_t, k_t, v_t), out_specs, out_shape


def _merge_kernel(a_ref, b_ref, m_ref, wa_ref, wb_ref, wm_ref, za_ref, zb_ref, zm_ref, o_ref):
    ya = _dot(a_ref[...].astype(BF16), wa_ref[...])
    yb = _dot(b_ref[...].astype(BF16), wb_ref[...])
    ym = _dot(m_ref[...].astype(BF16), wm_ref[...])
    gate = lambda z_ref: _sigmoid(z_ref[...].astype(F32))
    merged = gate(za_ref) * ya + gate(zb_ref) * yb + gate(zm_ref) * ym
    o_ref[...] = merged.astype(o_ref.dtype)


def _merge(o_gla, cbu, om, w_gla_out, w_conv_out, w_mem_out, proj_b):
    m = o_gla.shape[0]
    tm = min(1024, m)
    tn = 512
    d = D_MODEL
    zoff = (3 * CONV_DIM + MEM_DIM) // tn
    zspec = lambda g: pl.BlockSpec((tm, tn), lambda i, j: (i, zoff + g * (d // tn) + j))
    return pl.pallas_call(
        _merge_kernel,
        grid=(m // tm, d // tn),
        in_specs=[
            pl.BlockSpec((tm, o_gla.shape[1]), lambda i, j: (i, 0)),
            pl.BlockSpec((tm, cbu.shape[1]), lambda i, j: (i, 0)),
            pl.BlockSpec((tm, om.shape[1]), lambda i, j: (i, 0)),
            pl.BlockSpec((w_gla_out.shape[0], tn), lambda i, j: (0, j)),
            pl.BlockSpec((w_conv_out.shape[0], tn), lambda i, j: (0, j)),
            pl.BlockSpec((w_mem_out.shape[0], tn), lambda i, j: (0, j)),
            zspec(0), zspec(1), zspec(2),
        ],
        out_specs=pl.BlockSpec((tm, tn), lambda i, j: (i, j)),
        out_shape=jax.ShapeDtypeStruct((m, d), BF16),
        compiler_params=_params("parallel", "parallel"),
        name="merge",
    )(o_gla, cbu, om, w_gla_out, w_conv_out, w_mem_out, proj_b, proj_b, proj_b)


def _wo_ln_kernel(x_ref, m_ref, w_ref, g_ref, b_ref, o_ref):
    tm = x_ref.shape[0]
    half = tm // 2 if tm % 256 == 0 else tm
    for r in range(0, tm, half):
        rows = pl.ds(r, half)
        y = DN_ALPHA * x_ref[rows, :] + _dot(m_ref[rows, :], w_ref[...])
        o_ref[rows, :] = _layer_norm(y, g_ref[...], b_ref[...])


def _wo_ln(x, merged, w_o, g, b):
    m, d = x.shape
    tm = min(512, m)
    return pl.pallas_call(
        _wo_ln_kernel,
        grid=(m // tm,),
        in_specs=[
            pl.BlockSpec((tm, d), lambda i: (i, 0)),
            pl.BlockSpec((tm, d), lambda i: (i, 0)),
            pl.BlockSpec((d, d), lambda i: (0, 0)),
            pl.BlockSpec((1, d), lambda i: (0, 0)),
            pl.BlockSpec((1, d), lambda i: (0, 0)),
        ],
        out_specs=pl.BlockSpec((tm, d), lambda i: (i, 0)),
        out_shape=jax.ShapeDtypeStruct((m, d), F32),
        compiler_params=_params("parallel"),
        name="wo_ln",
    )(x, merged, w_o, g, b)


def _ffn_tail(hc, hu, wd_ref, acc_ref, x_ref, g_ref, b_ref, o_ref, fi):
    tf = hc.shape[1]
    valid = D_FF - fi * tf
    h = hc * _sigmoid(hc) * hu
    h = jnp.where(lax.broadcasted_iota(jnp.int32, (1, tf), 1) < valid, h, 0.0).astype(BF16)
    wd = jnp.where(lax.broadcasted_iota(jnp.int32, (tf, 1), 0) < valid, wd_ref[...], 0.0)
    acc_ref[...] += _dot(h, wd)

    @pl.when(fi == pl.num_programs(1) - 1)
    def _():
        o_ref[...] = _layer_norm(DN_ALPHA * x_ref[...] + acc_ref[...], g_ref[...], b_ref[...])


def _ffn_kernel(seq, x_ref, xh_ref, wg_ref, wu_ref, wd_ref, wc_ref, bc_ref, g_ref, b_ref, *rest):
    dec_in, (o_ref, st_ref, od_ref, so_ref, xb_scr, acc_scr) = rest[:GLA_DEC_INPUTS], rest[GLA_DEC_INPUTS:]
    fi = pl.program_id(1)
    tm = x_ref.shape[0]

    @pl.when(fi == 0)
    def _():
        xb_scr[...] = x_ref[...].astype(BF16)
        acc_scr[...] = jnp.zeros_like(acc_scr)

    _gla_dec_item(*dec_in, od_ref, so_ref)
    xb = xb_scr[...]
    wg = wg_ref[...]
    hg = _dot(xb, wg)
    st_ref[...] = hg[tm - 2:tm, :]
    halo = _dot(xh_ref[...].astype(BF16), wg)
    halo = jnp.where((pl.program_id(0) * tm) % seq == 0, 0.0, halo)
    ri = lax.broadcasted_iota(jnp.int32, (tm, 1), 0)
    p1, p2 = _shift_rows(hg, halo, ri)
    wc = wc_ref[...]
    hc = wc[0:1, :] * p2 + wc[1:2, :] * p1 + wc[2:3, :] * hg + bc_ref[...]
    _ffn_tail(hc, _dot(xb, wu_ref[...]), wd_ref, acc_scr, x_ref, g_ref, b_ref, o_ref, fi)


def _ffn_prompt(x1, w_gate, w_up, w_down, w_conv, b_conv, g, b, seq, gla_dec_operands):
    m, d = x1.shape
    tm = min(512, seq)
    tf = FF_TILE
    t8 = tm // HALO_ROWS
    n_items, dec_in_specs, dec_args, dec_out_specs, dec_out_shape = _gla_decode_specs(
        lambda i, f: i * FF_TILES + f, *gla_dec_operands)
    assert n_items <= (m // tm) * FF_TILES, "not enough FFN steps to carry the decode GLA work items"
    y, tile_state, o_dec, s_dec = pl.pallas_call(
        functools.partial(_ffn_kernel, seq),
        grid=(m // tm, FF_TILES),
        in_specs=[
            pl.BlockSpec((tm, d), lambda i, f: (i, 0)),
            pl.BlockSpec((HALO_ROWS, d), lambda i, f: (jnp.maximum(i * t8 - 1, 0), 0)),
            pl.BlockSpec((d, tf), lambda i, f: (0, f)),
            pl.BlockSpec((d, tf), lambda i, f: (0, f)),
            pl.BlockSpec((tf, d), lambda i, f: (f, 0)),
            pl.BlockSpec((3, tf), lambda i, f: (0, f)),
            pl.BlockSpec((1, tf), lambda i, f: (0, f)),
            pl.BlockSpec((1, d), lambda i, f: (0, 0)),
            pl.BlockSpec((1, d), lambda i, f: (0, 0)),
        ] + dec_in_specs,
        out_specs=[pl.BlockSpec((tm, d), lambda i, f: (i, 0)),
                   pl.BlockSpec((None, 2, tf), lambda i, f: (i, 0, f))] + dec_out_specs,
        out_shape=[jax.ShapeDtypeStruct((m, d), F32),
                   jax.ShapeDtypeStruct((m // tm, 2, D_FF), F32)] + dec_out_shape,
        scratch_shapes=[pltpu.VMEM((tm, d), BF16), pltpu.VMEM((tm, d), F32)],
        compiler_params=_params("arbitrary", "arbitrary"),
        name="ffn_prompt_gla_decode",
    )(x1, x1, w_gate, w_up, w_down, w_conv, b_conv, g, b, *dec_args)
    tiles_per_seq = seq // tm
    return y, tile_state[tiles_per_seq - 1::tiles_per_seq], o_dec, s_dec


def _ffn_dec_kernel(x_ref, s0_ref, s1_ref, wg_ref, wu_ref, wd_ref, wc_ref, bc_ref, g_ref, b_ref,
                    o_ref, hg_ref, xb_scr, acc_scr):
    fi = pl.program_id(1)

    @pl.when(fi == 0)
    def _():
        xb_scr[...] = x_ref[...].astype(BF16)
        acc_scr[...] = jnp.zeros_like(acc_scr)

    xb = xb_scr[...]
    hg = _dot(xb, wg_ref[...])
    hg_ref[...] = hg
    wc = wc_ref[...]
    hc = wc[0:1, :] * s0_ref[...] + wc[1:2, :] * s1_ref[...] + wc[2:3, :] * hg + bc_ref[...]
    _ffn_tail(hc, _dot(xb, wu_ref[...]), wd_ref, acc_scr, x_ref, g_ref, b_ref, o_ref, fi)


def _ffn_decode(x1, s0, s1, w_gate, w_up, w_down, w_conv, b_conv, g, b):
    m, d = x1.shape
    tf = FF_TILE
    return pl.pallas_call(
        _ffn_dec_kernel,
        grid=(1, FF_TILES),
        in_specs=[
            pl.BlockSpec((m, d), lambda i, f: (0, 0)),
            pl.BlockSpec((m, tf), lambda i, f: (0, f)),
            pl.BlockSpec((m, tf), lambda i, f: (0, f)),
            pl.BlockSpec((d, tf), lambda i, f: (0, f)),
            pl.BlockSpec((d, tf), lambda i, f: (0, f)),
            pl.BlockSpec((tf, d), lambda i, f: (f, 0)),
            pl.BlockSpec((3, tf), lambda i, f: (0, f)),
            pl.BlockSpec((1, tf), lambda i, f: (0, f)),
            pl.BlockSpec((1, d), lambda i, f: (0, 0)),
            pl.BlockSpec((1, d), lambda i, f: (0, 0)),
        ],
        out_specs=[pl.BlockSpec((m, d), lambda i, f: (0, 0)), pl.BlockSpec((m, tf), lambda i, f: (0, f))],
        out_shape=[jax.ShapeDtypeStruct((m, d), F32), jax.ShapeDtypeStruct((m, D_FF), F32)],
        scratch_shapes=[pltpu.VMEM((m, d), BF16), pltpu.VMEM((m, d), F32)],
        compiler_params=_params("parallel", "arbitrary"),
        name="ffn_decode",
    )(x1, s0, s1, w_gate, w_up, w_down, w_conv, b_conv, g, b)


def kernel(x_prompt, x_sample, mem_prompt, cache_mem_k, cache_mem_v, state_gla, state_conv, state_ffn_conv, w_in, w_gla_a2, b_gla_a2, g_gla_norm, w_gla_out, w_conv, w_conv_out, w_mem_k, w_mem_v, w_mem_out, w_o, ln1_g, ln1_b, w_ffn_gate, w_ffn_up, w_ffn_conv, b_ffn_conv, w_ffn_down, ln2_g, ln2_b):
    nb, seq, d = x_prompt.shape
    ns = x_sample.shape[0]

    w_in_t = jnp.swapaxes(w_in[0], 0, 1)
    wa2p = jnp.pad(w_gla_a2[0], ((0, LANES - GLA_RANK), (0, 0)))
    ba2 = b_gla_a2
    gn = g_gla_norm
    w_go = w_gla_out[0].astype(BF16)
    w_co = w_conv_out[0].astype(BF16)
    w_mo = w_mem_out[0].astype(BF16)
    w_oo = w_o[0].astype(BF16)
    w_fg = w_ffn_gate[0].astype(BF16)
    w_fu = w_ffn_up[0].astype(BF16)
    w_fd = w_ffn_down[0].astype(BF16)
    w_fc = w_ffn_conv[0]
    b_fc = b_ffn_conv
    w_cv = w_conv[0]

    def mix_in(x2d, tag, dtype, rider_b=None):
        xb = x2d.astype(BF16)
        return (_proj(xb, w_in_t, "proj_a_" + tag, 0, COLS_A, transposed=True, out_dtype=dtype),
                _proj(xb, w_in_t, "proj_r_" + tag, COLS_A, LANES, LANES, transposed=True),
                _proj(xb, w_in_t, "proj_b_" + tag, COLS_A, COLS_B, shift=GLA_RANK, transposed=True,
                      rider=rider_b, out_dtype=dtype))

    def mix_out(x2d, o_gla, cbu, om, proj_b):
        merged = _merge(o_gla, cbu, om, w_go, w_co, w_mo, proj_b)
        return _wo_ln(x2d, merged, w_oo, ln1_g, ln1_b)

    xs = x_sample.reshape(ns, d)
    sa, sr, sb = mix_in(xs, "s", F32)
    mq_s = sb[:, 3 * CONV_DIM:3 * CONV_DIM + MEM_DIM].reshape(ns, MEM_HEADS, MEM_HD)
    memattn_dec = (_memattn_dec_kernel, _memattn_decode_specs,
                   (_heads_to_tile(mq_s), _heads_to_tile(cache_mem_k[0]), _heads_to_tile(cache_mem_v[0])))

    xp = x_prompt.reshape(nb * seq, d)
    memb = mem_prompt.reshape(nb * N_MEM, d).astype(BF16)
    mk = _proj(memb, w_mem_k[0], "mem_k")
    mv = _proj(memb, w_mem_v[0], "mem_v")
    pa, pr, (pb, om_s) = mix_in(xp, "p", BF16, memattn_dec)
    o_gla, p_gla = _gla_prompt(pa, pr, wa2p, ba2, gn, nb, seq)
    cbu, p_conv = _conv_prompt(pb, w_cv, nb, seq)
    om = _memattn_prompt(pb, mk, mv, nb, seq)
    x1 = mix_out(xp, o_gla, cbu, om, pb)
    yp, p_ffn, o_gla_s, s_gla = _ffn_prompt(x1, w_fg, w_fu, w_fd, w_fc, b_fc, ln2_g, ln2_b, seq,
                                            (sa, sr, wa2p, ba2, gn, state_gla[0]))
    o_gla_s = o_gla_s.reshape(ns, GLA_HEADS * GLA_HV)

    cbu_s, s_conv = _conv_decode(sb, w_cv, state_conv[0].reshape(ns, 2 * CONV_DIM))
    om_s = _tile_to_heads(om_s).reshape(ns, MEM_DIM)
    x1s = mix_out(xs, o_gla_s, cbu_s, om_s, sb)
    f0 = state_ffn_conv[0][:, 0, :]
    f1 = state_ffn_conv[0][:, 1, :]
    ys, hg_s = _ffn_decode(x1s, f0, f1, w_fg, w_fu, w_fd, w_fc, b_fc, ln2_g, ln2_b)
    s_ffn = jnp.stack([f1, hg_s], axis=1)

    return (yp.reshape(nb, seq, d), ys.reshape(ns, 1, d),
            mk.reshape(1, nb, N_MEM, MEM_HEADS, MEM_HD), mv.reshape(1, nb, N_MEM, MEM_HEADS, MEM_HD),
            p_gla[None], p_conv[None], p_ffn.reshape(1, nb, 2, D_FF),
            s_gla[None], s_conv.reshape(1, ns, 2, CONV_DIM), s_ffn[None])
```

```python
import functools

import jax
import jax.numpy as jnp
from jax import lax
from jax.experimental import pallas as pl
from jax.experimental.pallas import tpu as pltpu

F32 = jnp.float32
BF16 = jnp.bfloat16

D_MODEL = 2048
GLA_HEADS = 4
GLA_HK = 256
GLA_HV = 512
GLA_RANK = 16
GLA_GATE_NORM = 16.0
CONV_DIM = 1024
N_MEM = 256
MEM_HEADS = 4
MEM_HD = 256
MEM_DIM = 1024
D_FF = 5504
DEPTH = 1
DN_ALPHA = (2 * DEPTH) ** 0.25
LN_EPS = 1e-5
RMS_EPS = 1e-6

LANES = 128
SUBLANES = 8
LOG2_E = 1.4426950408889634
FF_TILE = 512
FF_TILES = -(-D_FF // FF_TILE)
FFN_GATE_CHUNK = FF_TILE
COLS_A = 2 * GLA_HEADS * GLA_HK + 2 * GLA_HEADS * GLA_HV
COLS_B = 3 * CONV_DIM + MEM_DIM + 3 * D_MODEL
GLA_CHUNK = 256
VMEM_LIMIT = 56 * 1024 * 1024

NN = (((1,), (0,)), ((), ()))
NT = (((1,), (1,)), ((), ()))
TN = (((0,), (0,)), ((), ()))


def _dot(a, b, dims=NN):
    return lax.dot_general(a, b, dims, preferred_element_type=F32)


def _params(*sem):
    return pltpu.CompilerParams(dimension_semantics=sem, vmem_limit_bytes=VMEM_LIMIT)


def _split2(x):
    hi = x.astype(BF16)
    lo = (x - hi.astype(F32)).astype(BF16)
    return hi, lo


def _split3(x):
    hi = x.astype(BF16)
    r = x - hi.astype(F32)
    mid = r.astype(BF16)
    lo = (r - mid.astype(F32)).astype(BF16)
    return hi, mid, lo


def _sigmoid(z):
    return 0.5 * jnp.tanh(0.5 * z) + 0.5


def _silu(z):
    h = 0.5 * z
    return h * jnp.tanh(h) + h


def _layer_norm(y, g, b):
    mu = jnp.mean(y, axis=-1, keepdims=True)
    d = y - mu
    var = jnp.mean(d * d, axis=-1, keepdims=True)
    return d * lax.rsqrt(var + LN_EPS) * g + b


def _log2_decay(z):
    return (jnp.minimum(z, 0.0) - jnp.log(1.0 + jnp.exp(-jnp.abs(z)))) * (LOG2_E / GLA_GATE_NORM)


def _log_decay(alr, wa2, ba2):
    ah, al = _split2(alr)
    wh, wl = _split2(wa2)
    return _log2_decay(_dot(ah, wh) + _dot(ah, wl) + _dot(al, wh) + ba2)


def _rms_gate(o, gn, g):
    o = o * lax.rsqrt(jnp.mean(o * o, axis=-1, keepdims=True) + RMS_EPS) * gn
    return o * _silu(g)


PROJ_SHIFT_ROWS = 16


def _proj_kernel(transposed, shift, rider, emit_x, x_ref, w_ref, *rest):
    rest = list(rest)
    wx_ref = rest.pop(0) if shift else None
    wb_scr = rest.pop()
    n_in = rider[1] if rider else 0
    rider_in, rest = rest[:n_in], rest[n_in:]
    o_ref = rest.pop(0)
    ox_ref = rest.pop(0) if emit_x else None
    rider_out = rest
    item = pl.program_id(0) * pl.num_programs(1) + pl.program_id(1)

    @pl.when(pl.program_id(1) == 0)
    def _():
        w = w_ref[...]
        if shift:
            w = jnp.concatenate([w[shift:], wx_ref[:shift]], axis=0)
        w = w.astype(BF16)
        wb_scr[...] = w.T if transposed else w

    xb = x_ref[...].astype(BF16)
    if emit_x:
        ox_ref[...] = xb
    o_ref[...] = _dot(xb, wb_scr[...]).astype(o_ref.dtype)
    if rider:
        rider[0](item, *rider_in, *rider_out)


def _proj(x, w, name, col0=0, n=None, tn=1024, shift=0, transposed=False, rider=None, out_dtype=F32,
          emit_x=False):
    m, k = x.shape
    if n is None:
        n = w.shape[0] if transposed else w.shape[1]
    tm = min(m, 1024)
    tn = min(n, tn)
    c0 = col0 // tn
    if transposed:
        w_spec = pl.BlockSpec((tn, k), lambda j, i: (c0 + j, 0))
    else:
        assert not shift
        w_spec = pl.BlockSpec((k, tn), lambda j, i: (0, c0 + j))
    in_specs = [pl.BlockSpec((tm, k), lambda j, i: (i, 0)), w_spec]
    args = [x, w]
    if shift:
        r = PROJ_SHIFT_ROWS
        assert shift <= r and shift % 8 == 0 and tn % r == 0 and col0 % r == 0
        in_specs.append(pl.BlockSpec((r, k), lambda j, i: (col0 // r + (j + 1) * (tn // r), 0)))
        args.append(w)
    out_specs = [pl.BlockSpec((tm, tn), lambda j, i: (i, j))]
    out_shape = [jax.ShapeDtypeStruct((m, n), out_dtype)]
    if emit_x:
        assert n == tn, "each x tile must be visited once"
        out_specs.append(pl.BlockSpec((tm, k), lambda j, i: (i, 0)))
        out_shape.append(jax.ShapeDtypeStruct((m, k), BF16))
    kernel_rider = None
    if rider:
        body, specs_fn, operands = rider
        n_items, r_in_specs, r_args, r_out_specs, r_out_shape = specs_fn(lambda j, i: j * (m // tm) + i, *operands)
        assert n_items <= (n // tn) * (m // tm), "not enough projection steps to carry the rider's work items"
        in_specs += r_in_specs
        args += list(r_args)
        out_specs += r_out_specs
        out_shape += r_out_shape
        kernel_rider = (body, len(r_in_specs))
    outs = pl.pallas_call(
        functools.partial(_proj_kernel, transposed, shift, kernel_rider, emit_x),
        grid=(n // tn, m // tm),
        in_specs=in_specs,
        out_specs=out_specs,
        out_shape=out_shape,
        scratch_shapes=[pltpu.VMEM((k, tn), BF16)],
        compiler_params=_params("arbitrary" if rider else "parallel", "arbitrary"),
        name=name,
    )(*args)
    return outs if (rider or emit_x) else outs[0]


def _gla_levels(c):
    return c.bit_length() - 1


def _init_gla_masks(mask_scr, tri_scr):
    c = tri_scr.shape[0]
    rr = lax.broadcasted_iota(jnp.int32, (c, c), 0)
    cc = lax.broadcasted_iota(jnp.int32, (c, c), 1)
    tri_scr[...] = (rr >= cc).astype(BF16)
    mask_scr[0] = (rr == cc).astype(F32)
    for l in range(_gla_levels(c)):
        b = 1 << l
        pair = (rr // (2 * b) == cc // (2 * b)) & ((rr // b) % 2 == 1) & ((cc // b) % 2 == 0)
        mask_scr[1 + l] = pair.astype(F32)


def _intra_scores(q, k, la, bc, mask_scr):
    c, dk = q.shape
    ri = lax.broadcasted_iota(jnp.int32, (c, 1), 0)
    a = jnp.sum(q * k, axis=1, keepdims=True) * mask_scr[0]
    for l in range(_gla_levels(c)):
        b = 1 << l
        if b < SUBLANES:
            upper = (ri // b) % 2 == 1
            if b == 1:
                e = jnp.where(upper, la, 0.0)
            elif b == 2:
                m4 = ri % 4
                la_prev = pltpu.roll(la, 1, 0)
                la_next = pltpu.roll(la, c - 1, 0)
                e = jnp.where(m4 == 2, la, jnp.where(m4 == 3, la + la_prev, jnp.where(m4 == 0, la_next, 0.0)))
            else:
                ref = bc.reshape(c // (2 * b), 2 * b, dk)[:, b - 1:b, :]
                ref = jnp.broadcast_to(ref, (c // (2 * b), 2 * b, dk)).reshape(c, dk)
                dlt = bc - ref
                e = jnp.minimum(dlt, -dlt)
            w = jnp.where(upper, q, k) * jnp.exp2(e)
        else:
            parts = []
            for r0 in range(0, c, 2 * b):
                ref = bc[r0 + b - 1:r0 + b, :]
                parts.append(k[r0:r0 + b] * jnp.exp2(ref - bc[r0:r0 + b]))
                parts.append(q[r0 + b:r0 + 2 * b] * jnp.exp2(bc[r0 + b:r0 + 2 * b] - ref))
            w = jnp.concatenate(parts, axis=0)
        w = w.astype(BF16)
        a = a + _dot(w, w, NT) * mask_scr[1 + l]
    return a


def _gla_kernel(q_ref, k_ref, v_ref, g_ref, alr_ref, wa2_ref, ba2_ref, gn_ref, o_ref, sfin_ref,
                s_scr, mask_scr, tri_scr):
    ci = pl.program_id(2)

    @pl.when(ci == 0)
    def _():
        s_scr[...] = jnp.zeros_like(s_scr)
        _init_gla_masks(mask_scr, tri_scr)

    c = q_ref.shape[0]
    tri = tri_scr[...]
    ones = jnp.ones((c, LANES), BF16)
    alr = alr_ref[...]
    gn = gn_ref[...]
    for hh in range(s_scr.shape[0]):
        ks = slice(hh * GLA_HK, (hh + 1) * GLA_HK)
        vs = slice(hh * GLA_HV, (hh + 1) * GLA_HV)
        la = _log_decay(alr, wa2_ref[:, ks], ba2_ref[:, ks])
        lh, lm, ll = _split3(la)
        bc = _dot(tri, lh) + _dot(tri, lm) + _dot(tri, ll)
        b_last = bc[c - 1:c, :]
        bl_col = _dot(lh, ones, TN) + _dot(lm, ones, TN) + _dot(ll, ones, TN)
        dec_col = jnp.exp2(bl_col)
        dec_col = jnp.concatenate([dec_col] * (GLA_HV // LANES), axis=1)

        q = q_ref[:, ks].astype(F32) * (GLA_HK ** -0.5)
        k = k_ref[:, ks].astype(F32)
        vb = v_ref[:, vs].astype(BF16)
        s = s_scr[hh]
        o = _dot((q * jnp.exp2(bc)).astype(BF16), s.astype(BF16))
        a = _intra_scores(q, k, la, bc, mask_scr)
        o = o + _dot(a.astype(BF16), vb)
        kd = (k * jnp.exp2(b_last - bc)).astype(BF16)
        s_new = dec_col * s + _dot(kd, vb, TN)
        s_scr[hh] = s_new
        o_ref[:, vs] = _rms_gate(o, gn, g_ref[:, vs].astype(F32)).astype(o_ref.dtype)

    @pl.when(ci == pl.num_programs(2) - 1)
    def _():
        sfin_ref[...] = s_scr[...]


GLA_HEADS_PER_STEP = 2


def _gla_prompt(proj_a, alr, wa2p, ba2, gn, nb, seq):
    c = min(GLA_CHUNK, seq)
    nc = seq // c
    hp = GLA_HEADS_PER_STEP
    ng = GLA_HEADS // hp
    wk, wv = hp * GLA_HK, hp * GLA_HV
    row = lambda b, hg, ci: b * nc + ci
    return pl.pallas_call(
        _gla_kernel,
        grid=(nb, ng, nc),
        in_specs=[
            pl.BlockSpec((c, wk), lambda b, hg, ci: (row(b, hg, ci), hg)),
            pl.BlockSpec((c, wk), lambda b, hg, ci: (row(b, hg, ci), ng + hg)),
            pl.BlockSpec((c, wv), lambda b, hg, ci: (row(b, hg, ci), ng + hg)),
            pl.BlockSpec((c, wv), lambda b, hg, ci: (row(b, hg, ci), 2 * ng + hg)),
            pl.BlockSpec((c, LANES), lambda b, hg, ci: (row(b, hg, ci), 0)),
            pl.BlockSpec((LANES, wk), lambda b, hg, ci: (0, hg)),
            pl.BlockSpec((1, wk), lambda b, hg, ci: (0, hg)),
            pl.BlockSpec((1, GLA_HV), lambda b, hg, ci: (0, 0)),
        ],
        out_specs=[
            pl.BlockSpec((c, wv), lambda b, hg, ci: (row(b, hg, ci), hg)),
            pl.BlockSpec((None, hp, GLA_HK, GLA_HV), lambda b, hg, ci: (b, hg, 0, 0)),
        ],
        out_shape=[
            jax.ShapeDtypeStruct((nb * seq, GLA_HEADS * GLA_HV), BF16),
            jax.ShapeDtypeStruct((nb, GLA_HEADS, GLA_HK, GLA_HV), F32),
        ],
        scratch_shapes=[pltpu.VMEM((hp, GLA_HK, GLA_HV), F32),
                        pltpu.VMEM((1 + _gla_levels(c), c, c), F32),
                        pltpu.VMEM((c, c), BF16)],
        compiler_params=_params("parallel", "parallel", "arbitrary"),
        name="gla_prompt",
    )(proj_a, proj_a, proj_a, proj_a, alr, wa2p, ba2, gn)


GLA_DEC_ROWS = 4


def _gla_dec_item(q_ref, k_ref, v_ref, g_ref, alr_ref, wa2_ref, ba2_ref, gn_ref, s_ref, o_ref, so_ref):
    nb = q_ref.shape[0]
    alr = alr_ref[...]
    wa2 = wa2_ref[...]
    z = ba2_ref[...]
    for r in range(GLA_RANK):
        z = z + alr[:, r:r + 1] * wa2[r:r + 1, :]
    la = _log2_decay(z)
    rep = LANES // nb
    to_cols = lambda x: jnp.concatenate([x] * rep, axis=0).T
    a_t = to_cols(jnp.exp2(la))
    k_t = to_cols(k_ref[...])
    q_t = to_cols(q_ref[...] * (GLA_HK ** -0.5))
    v = v_ref[...]
    rows = []
    for j in range(nb):
        s_new = a_t[:, j:j + 1] * s_ref[j] + k_t[:, j:j + 1] * v[j:j + 1, :]
        so_ref[j] = s_new
        rows.append(jnp.sum(q_t[:, j:j + 1] * s_new, axis=0, keepdims=True))
    o = jnp.concatenate(rows, axis=0)
    o_ref[...] = _rms_gate(o, gn_ref[...], g_ref[...]).astype(o_ref.dtype)


GLA_DEC_INPUTS = 9


def _gla_decode_specs(item, proj_a, alr, wa2p, ba2, gn, state):
    nseq = proj_a.shape[0]
    h = GLA_HEADS
    r = GLA_DEC_ROWS
    n_items = (nseq // r) * h
    proj3 = proj_a.reshape(nseq // r, r, proj_a.shape[1])
    alr3 = alr.reshape(nseq // r, r, LANES)

    def at(fn):
        def index_map(*grid_idx):
            it = jnp.minimum(item(*grid_idx), n_items - 1)
            return fn(it // h, it % h)
        return index_map

    in_specs = [
        pl.BlockSpec((None, r, GLA_HK), at(lambda b, hh: (b, 0, hh))),
        pl.BlockSpec((None, r, GLA_HK), at(lambda b, hh: (b, 0, h + hh))),
        pl.BlockSpec((None, r, GLA_HV), at(lambda b, hh: (b, 0, h + hh))),
        pl.BlockSpec((None, r, GLA_HV), at(lambda b, hh: (b, 0, 2 * h + hh))),
        pl.BlockSpec((None, r, LANES), at(lambda b, hh: (b, 0, 0))),
        pl.BlockSpec((LANES, GLA_HK), at(lambda b, hh: (0, hh))),
        pl.BlockSpec((1, GLA_HK), at(lambda b, hh: (0, hh))),
        pl.BlockSpec((1, GLA_HV), at(lambda b, hh: (0, 0))),
        pl.BlockSpec((r, None, GLA_HK, GLA_HV), at(lambda b, hh: (b, hh, 0, 0))),
    ]
    out_specs = [
        pl.BlockSpec((None, r, GLA_HV), at(lambda b, hh: (b, 0, hh))),
        pl.BlockSpec((r, None, GLA_HK, GLA_HV), at(lambda b, hh: (b, hh, 0, 0))),
    ]
    out_shape = [
        jax.ShapeDtypeStruct((nseq // r, r, h * GLA_HV), F32),
        jax.ShapeDtypeStruct(state.shape, F32),
    ]
    args = (proj3, proj3, proj3, proj3, alr3, wa2p, ba2, gn, state)
    assert len(in_specs) == GLA_DEC_INPUTS
    return n_items, in_specs, args, out_specs, out_shape


HALO_ROWS = 16


def _shift_rows(x, halo, ri):
    h1 = halo[HALO_ROWS - 1:HALO_ROWS, :]
    h2 = halo[HALO_ROWS - 2:HALO_ROWS - 1, :]
    p1 = jnp.where(ri == 0, h1, pltpu.roll(x, 1, 0))
    p2 = jnp.where(ri == 0, h2, jnp.where(ri == 1, h1, pltpu.roll(x, 2, 0)))
    return p1, p2


CONV_ITEM_ROWS = 512


def _conv_item(tiles_per_seq, n_items, item, cb_ref, cc_ref, ch_ref, cch_ref, chh_ref, w_ref, o_ref, st_ref):
    tl, cd = cb_ref.shape
    tile = jnp.minimum(item, n_items - 1)
    seq_start = tile % tiles_per_seq == 0
    ri = lax.broadcasted_iota(jnp.int32, (tl, 1), 0)
    for c0 in range(0, cd, LANES):
        cs = slice(c0, c0 + LANES)
        cch = cc_ref[:, cs].astype(F32) * ch_ref[:, cs].astype(F32)
        halo = cch_ref[:, cs].astype(F32) * chh_ref[:, cs].astype(F32)
        halo = jnp.where(seq_start, 0.0, halo)
        p1, p2 = _shift_rows(cch, halo, ri)
        w = w_ref[:, cs]
        u = w[0:1, :] * p2 + w[1:2, :] * p1 + w[2:3, :] * cch
        o_ref[:, cs] = (cb_ref[:, cs].astype(F32) * u).astype(o_ref.dtype)
        st_ref[:, cs] = cch[tl - 2:tl, :]


def _conv_prompt_specs(item, proj_b, w_conv, seq):
    m = proj_b.shape[0]
    tl = CONV_ITEM_ROWS
    assert seq % tl == 0
    n_items = m // tl
    tiles_per_seq = seq // tl
    th = tl // HALO_ROWS
    cd = CONV_DIM
    tile = lambda *g: jnp.minimum(item(*g), n_items - 1)
    in_specs = [
        pl.BlockSpec((tl, cd), lambda *g: (tile(*g), 0)),
        pl.BlockSpec((tl, cd), lambda *g: (tile(*g), 1)),
        pl.BlockSpec((tl, cd), lambda *g: (tile(*g), 2)),
        pl.BlockSpec((HALO_ROWS, cd), lambda *g: (jnp.maximum(tile(*g) * th - 1, 0), 1)),
        pl.BlockSpec((HALO_ROWS, cd), lambda *g: (jnp.maximum(tile(*g) * th - 1, 0), 2)),
        pl.BlockSpec((3, cd), lambda *g: (0, 0)),
    ]
    out_specs = [
        pl.BlockSpec((tl, cd), lambda *g: (tile(*g), 0)),
        pl.BlockSpec((None, 2, cd), lambda *g: (tile(*g) // tiles_per_seq, 0, 0)),
    ]
    out_shape = [
        jax.ShapeDtypeStruct((m, cd), BF16),
        jax.ShapeDtypeStruct((m // seq, 2, cd), F32),
    ]
    return n_items, in_specs, (proj_b, proj_b, proj_b, proj_b, proj_b, w_conv), out_specs, out_shape


def _conv_prompt(proj_b, w_conv, seq):
    n_items, in_specs, args, out_specs, out_shape = _conv_prompt_specs(lambda t: t, proj_b, w_conv, seq)
    tiles_per_seq = seq // CONV_ITEM_ROWS

    def conv_kernel(*refs):
        _conv_item(tiles_per_seq, n_items, pl.program_id(0), *refs)

    return pl.pallas_call(
        conv_kernel,
        grid=(n_items,),
        in_specs=in_specs,
        out_specs=out_specs,
        out_shape=out_shape,
        compiler_params=_params("arbitrary"),
        name="conv_prompt",
    )(*args)


def _conv_dec_kernel(cb_ref, cc_ref, ch_ref, s0_ref, s1_ref, w_ref, o_ref, st_ref):
    cd = cb_ref.shape[1]
    cch = cc_ref[...] * ch_ref[...]
    s1 = s1_ref[...]
    w = w_ref[...]
    u = w[0:1, :] * s0_ref[...] + w[1:2, :] * s1 + w[2:3, :] * cch
    o_ref[...] = (cb_ref[...] * u).astype(o_ref.dtype)
    st_ref[:, :cd] = s1
    st_ref[:, cd:] = cch


def _conv_decode(proj_b, w_conv, state2d):
    n = proj_b.shape[0]
    cd = CONV_DIM
    blk = lambda col: pl.BlockSpec((n, cd), lambda i: (0, col))
    return pl.pallas_call(
        _conv_dec_kernel,
        grid=(1,),
        in_specs=[blk(0), blk(1), blk(2), blk(0), blk(1), pl.BlockSpec((3, cd), lambda i: (0, 0))],
        out_specs=[pl.BlockSpec((n, cd), lambda i: (0, 0)), pl.BlockSpec((n, 2 * cd), lambda i: (0, 0))],
        out_shape=[jax.ShapeDtypeStruct((n, cd), BF16), jax.ShapeDtypeStruct((n, 2 * cd), F32)],
        compiler_params=_params("arbitrary"),
        name="conv_decode",
    )(proj_b, proj_b, proj_b, state2d, state2d, w_conv)


def _softmax_rows(logits):
    m = jnp.max(logits, axis=-1, keepdims=True)
    p = jnp.exp(logits - m)
    return p / jnp.sum(p, axis=-1, keepdims=True)


def _memattn_kernel(q_ref, k_ref, v_ref, o_ref):
    logits = _dot(q_ref[...].astype(BF16), k_ref[...].astype(BF16), NT) * (MEM_HD ** -0.5)
    p = _softmax_rows(logits)
    o_ref[...] = _dot(p.astype(BF16), v_ref[...].astype(BF16)).astype(o_ref.dtype)


def _memattn_prompt(proj_b, mk, mv, nb, seq):
    tl = min(512, seq)
    nt = seq // tl
    h = MEM_HEADS
    qcol = 3 * CONV_DIM // MEM_HD
    return pl.pallas_call(
        _memattn_kernel,
        grid=(nb, h, nt),
        in_specs=[
            pl.BlockSpec((tl, MEM_HD), lambda b, hh, t: (b * nt + t, qcol + hh)),
            pl.BlockSpec((N_MEM, MEM_HD), lambda b, hh, t: (b, hh)),
            pl.BlockSpec((N_MEM, MEM_HD), lambda b, hh, t: (b, hh)),
        ],
        out_specs=pl.BlockSpec((tl, MEM_HD), lambda b, hh, t: (b * nt + t, hh)),
        out_shape=jax.ShapeDtypeStruct((nb * seq, MEM_DIM), BF16),
        compiler_params=_params("parallel", "parallel", "parallel"),
        name="memattn_prompt",
    )(proj_b, mk, mv)


MEMATTN_DEC_ROWS = 2


MEM_HD_CHUNKS = MEM_HD // LANES
MEM_TILE_ROWS = MEM_HD_CHUNKS * MEM_HEADS
assert MEM_HD_CHUNKS == 2 and MEM_TILE_ROWS == 8


def _heads_to_tile(x):
    lead = x.shape[:-2]
    x = x.reshape(lead + (MEM_HEADS, MEM_HD_CHUNKS, LANES))
    return jnp.swapaxes(x, -3, -2).reshape(lead + (MEM_TILE_ROWS, LANES))


def _tile_to_heads(x):
    lead = x.shape[:-2]
    x = x.reshape(lead + (MEM_HD_CHUNKS, MEM_HEADS, LANES))
    return jnp.swapaxes(x, -3, -2).reshape(lead + (MEM_HEADS, MEM_HD))


def _memattn_dec_item(item, q_ref, k_ref, v_ref, o_ref):
    def body(i, carry):
        q = q_ref[i] * (MEM_HD ** -0.5)
        prod = k_ref[i] * q[None]
        prod = prod + pltpu.roll(prod, MEM_HEADS, 1)
        logits = jnp.sum(prod, axis=-1, keepdims=True)
        e = jnp.exp(logits - jnp.max(logits, axis=0, keepdims=True))
        o_ref[i] = jnp.sum(e * v_ref[i], axis=0) / jnp.sum(e, axis=0)
        return carry

    lax.fori_loop(0, q_ref.shape[0], body, 0, unroll=True)


def _memattn_decode_specs(item, q_t, k_t, v_t):
    n = q_t.shape[0]
    r = MEMATTN_DEC_ROWS
    n_items = n // r
    tile = (MEM_TILE_ROWS, LANES)

    def at(ndim):
        return lambda *grid_idx: (jnp.minimum(item(*grid_idx), n_items - 1),) + (0,) * (ndim - 1)

    in_specs = [
        pl.BlockSpec((r,) + tile, at(3)),
        pl.BlockSpec((r, N_MEM) + tile, at(4)),
        pl.BlockSpec((r, N_MEM) + tile, at(4)),
    ]
    out_specs = [pl.BlockSpec((r,) + tile, at(3))]
    out_shape = [jax.ShapeDtypeStruct((n,) + tile, F32)]
    return n_items, in_specs, (q_t, k_t, v_t), out_specs, out_shape


def _merge_kernel(a_ref, b_ref, m_ref, wa_ref, wb_ref, wm_ref, za_ref, zb_ref, zm_ref, o_ref):
    ya = _dot(a_ref[...].astype(BF16), wa_ref[...])
    yb = _dot(b_ref[...].astype(BF16), wb_ref[...])
    ym = _dot(m_ref[...].astype(BF16), wm_ref[...])
    gate = lambda z_ref: _sigmoid(z_ref[...].astype(F32))
    merged = gate(za_ref) * ya + gate(zb_ref) * yb + gate(zm_ref) * ym
    o_ref[...] = merged.astype(o_ref.dtype)


def _merge(o_gla, cbu, om, w_gla_out, w_conv_out, w_mem_out, proj_b):
    m = o_gla.shape[0]
    tm = min(1024, m)
    tn = 512
    d = D_MODEL
    zoff = (3 * CONV_DIM + MEM_DIM) // tn
    zspec = lambda g: pl.BlockSpec((tm, tn), lambda i, j: (i, zoff + g * (d // tn) + j))
    return pl.pallas_call(
        _merge_kernel,
        grid=(m // tm, d // tn),
        in_specs=[
            pl.BlockSpec((tm, o_gla.shape[1]), lambda i, j: (i, 0)),
            pl.BlockSpec((tm, cbu.shape[1]), lambda i, j: (i, 0)),
            pl.BlockSpec((tm, om.shape[1]), lambda i, j: (i, 0)),
            pl.BlockSpec((w_gla_out.shape[0], tn), lambda i, j: (0, j)),
            pl.BlockSpec((w_conv_out.shape[0], tn), lambda i, j: (0, j)),
            pl.BlockSpec((w_mem_out.shape[0], tn), lambda i, j: (0, j)),
            zspec(0), zspec(1), zspec(2),
        ],
        out_specs=pl.BlockSpec((tm, tn), lambda i, j: (i, j)),
        out_shape=jax.ShapeDtypeStruct((m, d), BF16),
        compiler_params=_params("parallel", "parallel"),
        name="merge",
    )(o_gla, cbu, om, w_gla_out, w_conv_out, w_mem_out, proj_b, proj_b, proj_b)


def _wo_ln_kernel(x_ref, m_ref, w_ref, g_ref, b_ref, o_ref):
    tm = x_ref.shape[0]
    half = tm // 4 if tm % 512 == 0 else tm
    for r in range(0, tm, half):
        rows = pl.ds(r, half)
        y = DN_ALPHA * x_ref[rows, :] + _dot(m_ref[rows, :], w_ref[...])
        o_ref[rows, :] = _layer_norm(y, g_ref[...], b_ref[...])


def _wo_ln(x, merged, w_o, g, b):
    m, d = x.shape
    tm = min(512, m)
    return pl.pallas_call(
        _wo_ln_kernel,
        grid=(m // tm,),
        in_specs=[
            pl.BlockSpec((tm, d), lambda i: (i, 0)),
            pl.BlockSpec((tm, d), lambda i: (i, 0)),
            pl.BlockSpec((d, d), lambda i: (0, 0)),
            pl.BlockSpec((1, d), lambda i: (0, 0)),
            pl.BlockSpec((1, d), lambda i: (0, 0)),
        ],
        out_specs=pl.BlockSpec((tm, d), lambda i: (i, 0)),
        out_shape=jax.ShapeDtypeStruct((m, d), F32),
        compiler_params=_params("parallel"),
        name="wo_ln",
    )(x, merged, w_o, g, b)


def _ffn_act(hc, hu, valid, c0=0):
    h = hc * _sigmoid(hc) * hu
    return jnp.where(c0 + lax.broadcasted_iota(jnp.int32, (1, hc.shape[1]), 1) < valid, h, 0.0).astype(BF16)


def _ffn_down(h, valid, wd_ref, acc_ref, x_ref, g_ref, b_ref, o_ref, fi):
    wd = jnp.where(lax.broadcasted_iota(jnp.int32, (h.shape[1], 1), 0) < valid, wd_ref[...], 0.0)
    acc_ref[...] += _dot(h, wd)

    @pl.when(fi == pl.num_programs(1) - 1)
    def _():
        o_ref[...] = _layer_norm(DN_ALPHA * x_ref[...] + acc_ref[...], g_ref[...], b_ref[...])


def _ffn_kernel(seq, x_ref, xh_ref, wg_ref, wu_ref, wd_ref, wc_ref, bc_ref, g_ref, b_ref, *rest):
    dec_in, (o_ref, st_ref, od_ref, so_ref, xb_scr, acc_scr) = rest[:GLA_DEC_INPUTS], rest[GLA_DEC_INPUTS:]
    fi = pl.program_id(1)
    tm = x_ref.shape[0]

    @pl.when(fi == 0)
    def _():
        xb_scr[...] = x_ref[...].astype(BF16)
        acc_scr[...] = jnp.zeros_like(acc_scr)

    _gla_dec_item(*dec_in, od_ref, so_ref)
    xb = xb_scr[...]
    xh = xh_ref[...].astype(BF16)
    tf = wg_ref.shape[1]
    valid = D_FF - fi * tf
    seq_start = (pl.program_id(0) * tm) % seq == 0
    ri = lax.broadcasted_iota(jnp.int32, (tm, 1), 0)
    hs = []
    for c0 in range(0, tf, FFN_GATE_CHUNK):
        cs = slice(c0, c0 + FFN_GATE_CHUNK)
        wg = wg_ref[:, cs]
        hg = _dot(xb, wg)
        st_ref[:, cs] = hg[tm - 2:tm, :]
        halo = jnp.where(seq_start, 0.0, _dot(xh, wg))
        p1, p2 = _shift_rows(hg, halo, ri)
        wc = wc_ref[:, cs]
        hc = wc[0:1, :] * p2 + wc[1:2, :] * p1 + wc[2:3, :] * hg + bc_ref[:, cs]
        hs.append(_ffn_act(hc, _dot(xb, wu_ref[:, cs]), valid, c0))
    _ffn_down(jnp.concatenate(hs, axis=1), valid, wd_ref, acc_scr, x_ref, g_ref, b_ref, o_ref, fi)


def _ffn_prompt(x1, w_gate, w_up, w_down, w_conv, b_conv, g, b, seq, gla_dec_operands):
    m, d = x1.shape
    tm = min(512, seq)
    tf = FF_TILE
    t8 = tm // HALO_ROWS
    n_items, dec_in_specs, dec_args, dec_out_specs, dec_out_shape = _gla_decode_specs(
        lambda i, f: i * FF_TILES + f, *gla_dec_operands)
    assert n_items <= (m // tm) * FF_TILES, "not enough FFN steps to carry the decode GLA work items"
    y, tile_state, o_dec, s_dec = pl.pallas_call(
        functools.partial(_ffn_kernel, seq),
        grid=(m // tm, FF_TILES),
        in_specs=[
            pl.BlockSpec((tm, d), lambda i, f: (i, 0)),
            pl.BlockSpec((HALO_ROWS, d), lambda i, f: (jnp.maximum(i * t8 - 1, 0), 0)),
            pl.BlockSpec((d, tf), lambda i, f: (0, f)),
            pl.BlockSpec((d, tf), lambda i, f: (0, f)),
            pl.BlockSpec((tf, d), lambda i, f: (f, 0)),
            pl.BlockSpec((3, tf), lambda i, f: (0, f)),
            pl.BlockSpec((1, tf), lambda i, f: (0, f)),
            pl.BlockSpec((1, d), lambda i, f: (0, 0)),
            pl.BlockSpec((1, d), lambda i, f: (0, 0)),
        ] + dec_in_specs,
        out_specs=[pl.BlockSpec((tm, d), lambda i, f: (i, 0)),
                   pl.BlockSpec((None, 2, tf), lambda i, f: (i, 0, f))] + dec_out_specs,
        out_shape=[jax.ShapeDtypeStruct((m, d), F32),
                   jax.ShapeDtypeStruct((m // tm, 2, D_FF), F32)] + dec_out_shape,
        scratch_shapes=[pltpu.VMEM((tm, d), BF16), pltpu.VMEM((tm, d), F32)],
        compiler_params=_params("arbitrary", "arbitrary"),
        name="ffn_prompt_gla_decode",
    )(x1, x1, w_gate, w_up, w_down, w_conv, b_conv, g, b, *dec_args)
    tiles_per_seq = seq // tm
    return y, tile_state[tiles_per_seq - 1::tiles_per_seq], o_dec, s_dec


def _ffn_dec_kernel(x_ref, s0_ref, s1_ref, wg_ref, wu_ref, wd_ref, wc_ref, bc_ref, g_ref, b_ref,
                    o_ref, hg_ref, xb_scr, acc_scr):
    fi = pl.program_id(1)

    @pl.when(fi == 0)
    def _():
        xb_scr[...] = x_ref[...].astype(BF16)
        acc_scr[...] = jnp.zeros_like(acc_scr)

    xb = xb_scr[...]
    hg = _dot(xb, wg_ref[...])
    hg_ref[...] = hg
    wc = wc_ref[...]
    hc = wc[0:1, :] * s0_ref[...] + wc[1:2, :] * s1_ref[...] + wc[2:3, :] * hg + bc_ref[...]
    valid = D_FF - fi * hc.shape[1]
    _ffn_down(_ffn_act(hc, _dot(xb, wu_ref[...]), valid), valid, wd_ref, acc_scr, x_ref, g_ref, b_ref, o_ref, fi)


def _ffn_decode(x1, s0, s1, w_gate, w_up, w_down, w_conv, b_conv, g, b):
    m, d = x1.shape
    tf = FF_TILE
    return pl.pallas_call(
        _ffn_dec_kernel,
        grid=(1, FF_TILES),
        in_specs=[
            pl.BlockSpec((m, d), lambda i, f: (0, 0)),
            pl.BlockSpec((m, tf), lambda i, f: (0, f)),
            pl.BlockSpec((m, tf), lambda i, f: (0, f)),
            pl.BlockSpec((d, tf), lambda i, f: (0, f)),
            pl.BlockSpec((d, tf), lambda i, f: (0, f)),
            pl.BlockSpec((tf, d), lambda i, f: (f, 0)),
            pl.BlockSpec((3, tf), lambda i, f: (0, f)),
            pl.BlockSpec((1, tf), lambda i, f: (0, f)),
            pl.BlockSpec((1, d), lambda i, f: (0, 0)),
            pl.BlockSpec((1, d), lambda i, f: (0, 0)),
        ],
        out_specs=[pl.BlockSpec((m, d), lambda i, f: (0, 0)), pl.BlockSpec((m, tf), lambda i, f: (0, f))],
        out_shape=[jax.ShapeDtypeStruct((m, d), F32), jax.ShapeDtypeStruct((m, D_FF), F32)],
        scratch_shapes=[pltpu.VMEM((m, d), BF16), pltpu.VMEM((m, d), F32)],
        compiler_params=_params("parallel", "arbitrary"),
        name="ffn_decode",
    )(x1, s0, s1, w_gate, w_up, w_down, w_conv, b_conv, g, b)


def kernel(x_prompt, x_sample, mem_prompt, cache_mem_k, cache_mem_v, state_gla, state_conv, state_ffn_conv, w_in, w_gla_a2, b_gla_a2, g_gla_norm, w_gla_out, w_conv, w_conv_out, w_mem_k, w_mem_v, w_mem_out, w_o, ln1_g, ln1_b, w_ffn_gate, w_ffn_up, w_ffn_conv, b_ffn_conv, w_ffn_down, ln2_g, ln2_b):
    nb, seq, d = x_prompt.shape
    ns = x_sample.shape[0]

    w_in_t = jnp.swapaxes(w_in[0], 0, 1)
    wa2p = jnp.pad(w_gla_a2[0], ((0, LANES - GLA_RANK), (0, 0)))
    ba2 = b_gla_a2
    gn = g_gla_norm
    w_go = w_gla_out[0].astype(BF16)
    w_co = w_conv_out[0].astype(BF16)
    w_mo = w_mem_out[0].astype(BF16)
    w_oo = w_o[0].astype(BF16)
    w_fg = w_ffn_gate[0].astype(BF16)
    w_fu = w_ffn_up[0].astype(BF16)
    w_fd = w_ffn_down[0].astype(BF16)
    w_fc = w_ffn_conv[0]
    b_fc = b_ffn_conv
    w_cv = w_conv[0]

    def mix_in(x2d, tag, dtype, rider_b=None):
        pr, xb = _proj(x2d, w_in_t, "proj_r_" + tag, COLS_A, LANES, LANES, transposed=True, emit_x=True)
        pb = _proj(xb, w_in_t, "proj_b_" + tag, COLS_A, COLS_B, shift=GLA_RANK, transposed=True,
                   rider=rider_b, out_dtype=dtype)
        pa = _proj(xb, w_in_t, "proj_a_" + tag, 0, COLS_A, transposed=True, out_dtype=dtype)
        return pa, pr, pb

    def mix_out(x2d, o_gla, cbu, om, proj_b):
        merged = _merge(o_gla, cbu, om, w_go, w_co, w_mo, proj_b)
        return _wo_ln(x2d, merged, w_oo, ln1_g, ln1_b)

    xs = x_sample.reshape(ns, d)
    sa, sr, sb = mix_in(xs, "s", F32)
    mq_s = sb[:, 3 * CONV_DIM:3 * CONV_DIM + MEM_DIM].reshape(ns, MEM_HEADS, MEM_HD)
    memattn_dec = (_memattn_dec_item, _memattn_decode_specs,
                   (_heads_to_tile(mq_s), _heads_to_tile(cache_mem_k[0]), _heads_to_tile(cache_mem_v[0])))

    xp = x_prompt.reshape(nb * seq, d)
    memb = mem_prompt.reshape(nb * N_MEM, d).astype(BF16)
    mk = _proj(memb, w_mem_k[0], "mem_k")
    mv = _proj(memb, w_mem_v[0], "mem_v")
    pa, pr, (pb, om_s) = mix_in(xp, "p", BF16, memattn_dec)
    o_gla, p_gla = _gla_prompt(pa, pr, wa2p, ba2, gn, nb, seq)
    cbu, p_conv = _conv_prompt(pb, w_cv, seq)
    om = _memattn_prompt(pb, mk, mv, nb, seq)
    x1 = mix_out(xp, o_gla, cbu, om, pb)
    yp, p_ffn, o_gla_s, s_gla = _ffn_prompt(x1, w_fg, w_fu, w_fd, w_fc, b_fc, ln2_g, ln2_b, seq,
                                            (sa, sr, wa2p, ba2, gn, state_gla[0]))
    o_gla_s = o_gla_s.reshape(ns, GLA_HEADS * GLA_HV)

    cbu_s, s_conv = _conv_decode(sb, w_cv, state_conv[0].reshape(ns, 2 * CONV_DIM))
    om_s = _tile_to_heads(om_s).reshape(ns, MEM_DIM)
    x1s = mix_out(xs, o_gla_s, cbu_s, om_s, sb)
    f0 = state_ffn_conv[0][:, 0, :]
    f1 = state_ffn_conv[0][:, 1, :]
    ys, hg_s = _ffn_decode(x1s, f0, f1, w_fg, w_fu, w_fd, w_fc, b_fc, ln2_g, ln2_b)
    s_ffn = jnp.stack([f1, hg_s], axis=1)

    return (yp.reshape(nb, seq, d), ys.reshape(ns, 1, d),
            mk.reshape(1, nb, N_MEM, MEM_HEADS, MEM_HD), mv.reshape(1, nb, N_MEM, MEM_HEADS, MEM_HD),
            p_gla[None], p_conv[None], p_ffn.reshape(1, nb, 2, D_FF),
            s_gla[None], s_conv.reshape(1, ns, 2, CONV_DIM), s_ffn[None])
```

```python
import functools

import jax
import jax.numpy as jnp
from jax import lax
from jax.experimental import pallas as pl
from jax.experimental.pallas import tpu as pltpu

F32 = jnp.float32
BF16 = jnp.bfloat16

D_MODEL = 2048
GLA_HEADS = 4
GLA_HK = 256
GLA_HV = 512
GLA_RANK = 16
GLA_GATE_NORM = 16.0
CONV_DIM = 1024
N_MEM = 256
MEM_HEADS = 4
MEM_HD = 256
MEM_DIM = 1024
D_FF = 5504
DEPTH = 1
DN_ALPHA = (2 * DEPTH) ** 0.25
LN_EPS = 1e-5
RMS_EPS = 1e-6

LANES = 128
SUBLANES = 8
LOG2_E = 1.4426950408889634
FF_TILE = 512
FF_TILES = -(-D_FF // FF_TILE)
FFN_GATE_CHUNK = FF_TILE
COLS_A = 2 * GLA_HEADS * GLA_HK + 2 * GLA_HEADS * GLA_HV
COLS_B = 3 * CONV_DIM + MEM_DIM + 3 * D_MODEL
GLA_CHUNK = 256
VMEM_LIMIT = 56 * 1024 * 1024

NN = (((1,), (0,)), ((), ()))
NT = (((1,), (1,)), ((), ()))
TN = (((0,), (0,)), ((), ()))


def _dot(a, b, dims=NN):
    return lax.dot_general(a, b, dims, preferred_element_type=F32)


def _params(*sem):
    return pltpu.CompilerParams(dimension_semantics=sem, vmem_limit_bytes=VMEM_LIMIT)


def _split2(x):
    hi = x.astype(BF16)
    lo = (x - hi.astype(F32)).astype(BF16)
    return hi, lo


def _split3(x):
    hi = x.astype(BF16)
    r = x - hi.astype(F32)
    mid = r.astype(BF16)
    lo = (r - mid.astype(F32)).astype(BF16)
    return hi, mid, lo


def _sigmoid(z):
    return 0.5 * jnp.tanh(0.5 * z) + 0.5


def _silu(z):
    h = 0.5 * z
    return h * jnp.tanh(h) + h


def _layer_norm(y, g, b):
    mu = jnp.mean(y, axis=-1, keepdims=True)
    d = y - mu
    var = jnp.mean(d * d, axis=-1, keepdims=True)
    return d * lax.rsqrt(var + LN_EPS) * g + b


def _log2_decay(z):
    return (jnp.minimum(z, 0.0) - jnp.log(1.0 + jnp.exp(-jnp.abs(z)))) * (LOG2_E / GLA_GATE_NORM)


def _log_decay(alr, wa2, ba2):
    ah, al = _split2(alr)
    wh, wl = _split2(wa2)
    return _log2_decay(_dot(ah, wh) + _dot(ah, wl) + _dot(al, wh) + ba2)


def _rms_gate(o, gn, g):
    o = o * lax.rsqrt(jnp.mean(o * o, axis=-1, keepdims=True) + RMS_EPS) * gn
    return o * _silu(g)


PROJ_SHIFT_ROWS = 16
PROJ_TILE = 1024
BF16_SUBLANES = 16


def _proj_kernel(transposed, shift, riders, emit_x, x_ref, w_ref, *rest):
    rest = list(rest)
    wx_ref = rest.pop(0) if shift else None
    wb_scr = rest.pop()
    rider_in = [[rest.pop(0) for _ in range(n_in)] for _, n_in, _ in riders]
    o_ref = rest.pop(0)
    ox_ref = rest.pop(0) if emit_x else None
    rider_out = [[rest.pop(0) for _ in range(n_out)] for _, _, n_out in riders]
    item = pl.program_id(0) * pl.num_programs(1) + pl.program_id(1)

    @pl.when(pl.program_id(1) == 0)
    def _():
        w = w_ref[...]
        if shift:
            w = jnp.concatenate([w[shift:], wx_ref[:shift]], axis=0)
        w = w.astype(BF16)
        wb_scr[...] = w.T if transposed else w

    xb = x_ref[...].astype(BF16)
    if emit_x:
        ox_ref[...] = xb
    o_ref[...] = _dot(xb, wb_scr[...]).astype(o_ref.dtype)
    for (body, _, _), r_in, r_out in zip(riders, rider_in, rider_out):
        body(item, *r_in, *r_out)


def _proj(x, w, name, col0=0, n=None, tn=PROJ_TILE, shift=0, transposed=False, riders=(), out_dtype=F32,
          emit_x=False):
    m, k = x.shape
    if n is None:
        n = w.shape[0] if transposed else w.shape[1]
    tm = min(m, PROJ_TILE)
    tn = min(n, tn)
    c0 = col0 // tn
    if transposed:
        w_spec = pl.BlockSpec((tn, k), lambda j, i: (c0 + j, 0))
    else:
        assert not shift
        w_spec = pl.BlockSpec((k, tn), lambda j, i: (0, c0 + j))
    in_specs = [pl.BlockSpec((tm, k), lambda j, i: (i, 0)), w_spec]
    args = [x, w]
    if shift:
        r = PROJ_SHIFT_ROWS
        assert shift <= r and shift % 8 == 0 and tn % r == 0 and col0 % r == 0
        in_specs.append(pl.BlockSpec((r, k), lambda j, i: (col0 // r + (j + 1) * (tn // r), 0)))
        args.append(w)
    out_specs = [pl.BlockSpec((tm, tn), lambda j, i: (i, j))]
    out_shape = [jax.ShapeDtypeStruct((m, n), out_dtype)]
    if emit_x:
        assert n == tn, "each x tile must be visited once"
        out_specs.append(pl.BlockSpec((tm, k), lambda j, i: (i, 0)))
        out_shape.append(jax.ShapeDtypeStruct((m, k), BF16))
    kernel_riders = []
    for body, specs_fn, operands in riders:
        n_items, r_in_specs, r_args, r_out_specs, r_out_shape = specs_fn(lambda j, i: j * (m // tm) + i, *operands)
        assert n_items <= (n // tn) * (m // tm), "not enough projection steps to carry the rider's work items"
        in_specs += r_in_specs
        args += list(r_args)
        out_specs += r_out_specs
        out_shape += r_out_shape
        kernel_riders.append((body, len(r_in_specs), len(r_out_specs)))
    outs = pl.pallas_call(
        functools.partial(_proj_kernel, transposed, shift, tuple(kernel_riders), emit_x),
        grid=(n // tn, m // tm),
        in_specs=in_specs,
        out_specs=out_specs,
        out_shape=out_shape,
        scratch_shapes=[pltpu.VMEM((k, tn), BF16)],
        compiler_params=_params("arbitrary" if riders else "parallel", "arbitrary"),
        name=name,
    )(*args)
    return outs if len(outs) > 1 else outs[0]


def _proj_steps(m, n):
    return (n // min(n, PROJ_TILE)) * (m // min(m, PROJ_TILE))


def _cast_item(item, x_ref, o_ref):
    o_ref[...] = x_ref[...].astype(o_ref.dtype)


def _cast_specs(item, w, n_steps):
    r, c = w.shape
    rows = -(-r // n_steps)
    rows = -(-rows // BF16_SUBLANES) * BF16_SUBLANES
    n_items = -(-r // rows)
    at = lambda *g: (jnp.minimum(item(*g), n_items - 1), 0)
    return (n_items, [pl.BlockSpec((rows, c), at)], (w,), [pl.BlockSpec((rows, c), at)],
            [jax.ShapeDtypeStruct((r, c), BF16)])


def _cast_rider(w, n_steps):
    return (_cast_item, _cast_specs, (w, n_steps))


def _gla_levels(c):
    return c.bit_length() - 1


def _init_gla_masks(mask_scr, tri_scr):
    c = tri_scr.shape[0]
    rr = lax.broadcasted_iota(jnp.int32, (c, c), 0)
    cc = lax.broadcasted_iota(jnp.int32, (c, c), 1)
    tri_scr[...] = (rr >= cc).astype(BF16)
    mask_scr[0] = (rr == cc).astype(F32)
    for l in range(_gla_levels(c)):
        b = 1 << l
        pair = (rr // (2 * b) == cc // (2 * b)) & ((rr // b) % 2 == 1) & ((cc // b) % 2 == 0)
        mask_scr[1 + l] = pair.astype(F32)


def _intra_scores(q, k, la, bc, mask_scr):
    c, dk = q.shape
    ri = lax.broadcasted_iota(jnp.int32, (c, 1), 0)
    a = jnp.sum(q * k, axis=1, keepdims=True) * mask_scr[0]
    for l in range(_gla_levels(c)):
        b = 1 << l
        if b < SUBLANES:
            upper = (ri // b) % 2 == 1
            if b == 1:
                e = jnp.where(upper, la, 0.0)
            elif b == 2:
                m4 = ri % 4
                la_prev = pltpu.roll(la, 1, 0)
                la_next = pltpu.roll(la, c - 1, 0)
                e = jnp.where(m4 == 2, la, jnp.where(m4 == 3, la + la_prev, jnp.where(m4 == 0, la_next, 0.0)))
            else:
                ref = bc.reshape(c // (2 * b), 2 * b, dk)[:, b - 1:b, :]
                ref = jnp.broadcast_to(ref, (c // (2 * b), 2 * b, dk)).reshape(c, dk)
                dlt = bc - ref
                e = jnp.minimum(dlt, -dlt)
            w = jnp.where(upper, q, k) * jnp.exp2(e)
        else:
            parts = []
            for r0 in range(0, c, 2 * b):
                ref = bc[r0 + b - 1:r0 + b, :]
                parts.append(k[r0:r0 + b] * jnp.exp2(ref - bc[r0:r0 + b]))
                parts.append(q[r0 + b:r0 + 2 * b] * jnp.exp2(bc[r0 + b:r0 + 2 * b] - ref))
            w = jnp.concatenate(parts, axis=0)
        w = w.astype(BF16)
        a = a + _dot(w, w, NT) * mask_scr[1 + l]
    return a


def _gla_kernel(q_ref, k_ref, v_ref, g_ref, alr_ref, wa2_ref, ba2_ref, gn_ref, o_ref, sfin_ref,
                s_scr, mask_scr, tri_scr):
    ci = pl.program_id(2)

    @pl.when(ci == 0)
    def _():
        s_scr[...] = jnp.zeros_like(s_scr)
        _init_gla_masks(mask_scr, tri_scr)

    c = q_ref.shape[0]
    tri = tri_scr[...]
    ones = jnp.ones((c, LANES), BF16)
    alr = alr_ref[...]
    gn = gn_ref[...]
    for hh in range(s_scr.shape[0]):
        ks = slice(hh * GLA_HK, (hh + 1) * GLA_HK)
        vs = slice(hh * GLA_HV, (hh + 1) * GLA_HV)
        la = _log_decay(alr, wa2_ref[:, ks], ba2_ref[:, ks])
        lh, lm, ll = _split3(la)
        bc = _dot(tri, lh) + _dot(tri, lm) + _dot(tri, ll)
        b_last = bc[c - 1:c, :]
        bl_col = _dot(lh, ones, TN) + _dot(lm, ones, TN) + _dot(ll, ones, TN)
        dec_col = jnp.exp2(bl_col)
        dec_col = jnp.concatenate([dec_col] * (GLA_HV // LANES), axis=1)

        q = q_ref[:, ks].astype(F32) * (GLA_HK ** -0.5)
        k = k_ref[:, ks].astype(F32)
        vb = v_ref[:, vs].astype(BF16)
        s = s_scr[hh]
        o = _dot((q * jnp.exp2(bc)).astype(BF16), s.astype(BF16))
        a = _intra_scores(q, k, la, bc, mask_scr)
        o = o + _dot(a.astype(BF16), vb)
        kd = (k * jnp.exp2(b_last - bc)).astype(BF16)
        s_new = dec_col * s + _dot(kd, vb, TN)
        s_scr[hh] = s_new
        o_ref[:, vs] = _rms_gate(o, gn, g_ref[:, vs].astype(F32)).astype(o_ref.dtype)

    @pl.when(ci == pl.num_programs(2) - 1)
    def _():
        sfin_ref[...] = s_scr[...]


GLA_HEADS_PER_STEP = 2


def _gla_prompt(proj_a, alr, wa2p, ba2, gn, nb, seq):
    c = min(GLA_CHUNK, seq)
    nc = seq // c
    hp = GLA_HEADS_PER_STEP
    ng = GLA_HEADS // hp
    wk, wv = hp * GLA_HK, hp * GLA_HV
    row = lambda b, hg, ci: b * nc + ci
    return pl.pallas_call(
        _gla_kernel,
        grid=(nb, ng, nc),
        in_specs=[
            pl.BlockSpec((c, wk), lambda b, hg, ci: (row(b, hg, ci), hg)),
            pl.BlockSpec((c, wk), lambda b, hg, ci: (row(b, hg, ci), ng + hg)),
            pl.BlockSpec((c, wv), lambda b, hg, ci: (row(b, hg, ci), ng + hg)),
            pl.BlockSpec((c, wv), lambda b, hg, ci: (row(b, hg, ci), 2 * ng + hg)),
            pl.BlockSpec((c, LANES), lambda b, hg, ci: (row(b, hg, ci), 0)),
            pl.BlockSpec((LANES, wk), lambda b, hg, ci: (0, hg)),
            pl.BlockSpec((1, wk), lambda b, hg, ci: (0, hg)),
            pl.BlockSpec((1, GLA_HV), lambda b, hg, ci: (0, 0)),
        ],
        out_specs=[
            pl.BlockSpec((c, wv), lambda b, hg, ci: (row(b, hg, ci), hg)),
            pl.BlockSpec((None, hp, GLA_HK, GLA_HV), lambda b, hg, ci: (b, hg, 0, 0)),
        ],
        out_shape=[
            jax.ShapeDtypeStruct((nb * seq, GLA_HEADS * GLA_HV), BF16),
            jax.ShapeDtypeStruct((nb, GLA_HEADS, GLA_HK, GLA_HV), F32),
        ],
        scratch_shapes=[pltpu.VMEM((hp, GLA_HK, GLA_HV), F32),
                        pltpu.VMEM((1 + _gla_levels(c), c, c), F32),
                        pltpu.VMEM((c, c), BF16)],
        compiler_params=_params("parallel", "parallel", "arbitrary"),
        name="gla_prompt",
    )(proj_a, proj_a, proj_a, proj_a, alr, wa2p, ba2, gn)


GLA_DEC_ROWS = 4


def _gla_dec_item(q_ref, k_ref, v_ref, g_ref, alr_ref, wa2_ref, ba2_ref, gn_ref, s_ref, o_ref, so_ref):
    nb = q_ref.shape[0]
    alr = alr_ref[...]
    wa2 = wa2_ref[...]
    z = ba2_ref[...]
    for r in range(GLA_RANK):
        z = z + alr[:, r:r + 1] * wa2[r:r + 1, :]
    la = _log2_decay(z)
    rep = LANES // nb
    to_cols = lambda x: jnp.concatenate([x] * rep, axis=0).T
    a_t = to_cols(jnp.exp2(la))
    k_t = to_cols(k_ref[...])
    q_t = to_cols(q_ref[...] * (GLA_HK ** -0.5))
    v = v_ref[...]
    rows = []
    for j in range(nb):
        s_new = a_t[:, j:j + 1] * s_ref[j] + k_t[:, j:j + 1] * v[j:j + 1, :]
        so_ref[j] = s_new
        rows.append(jnp.sum(q_t[:, j:j + 1] * s_new, axis=0, keepdims=True))
    o = jnp.concatenate(rows, axis=0)
    o_ref[...] = _rms_gate(o, gn_ref[...], g_ref[...]).astype(o_ref.dtype)


GLA_DEC_INPUTS = 9


def _gla_decode_specs(item, proj_a, alr, wa2p, ba2, gn, state):
    nseq = proj_a.shape[0]
    h = GLA_HEADS
    r = GLA_DEC_ROWS
    n_items = (nseq // r) * h
    proj3 = proj_a.reshape(nseq // r, r, proj_a.shape[1])
    alr3 = alr.reshape(nseq // r, r, LANES)

    def at(fn):
        def index_map(*grid_idx):
            it = jnp.minimum(item(*grid_idx), n_items - 1)
            return fn(it // h, it % h)
        return index_map

    in_specs = [
        pl.BlockSpec((None, r, GLA_HK), at(lambda b, hh: (b, 0, hh))),
        pl.BlockSpec((None, r, GLA_HK), at(lambda b, hh: (b, 0, h + hh))),
        pl.BlockSpec((None, r, GLA_HV), at(lambda b, hh: (b, 0, h + hh))),
        pl.BlockSpec((None, r, GLA_HV), at(lambda b, hh: (b, 0, 2 * h + hh))),
        pl.BlockSpec((None, r, LANES), at(lambda b, hh: (b, 0, 0))),
        pl.BlockSpec((LANES, GLA_HK), at(lambda b, hh: (0, hh))),
        pl.BlockSpec((1, GLA_HK), at(lambda b, hh: (0, hh))),
        pl.BlockSpec((1, GLA_HV), at(lambda b, hh: (0, 0))),
        pl.BlockSpec((r, None, GLA_HK, GLA_HV), at(lambda b, hh: (b, hh, 0, 0))),
    ]
    out_specs = [
        pl.BlockSpec((None, r, GLA_HV), at(lambda b, hh: (b, 0, hh))),
        pl.BlockSpec((r, None, GLA_HK, GLA_HV), at(lambda b, hh: (b, hh, 0, 0))),
    ]
    out_shape = [
        jax.ShapeDtypeStruct((nseq // r, r, h * GLA_HV), F32),
        jax.ShapeDtypeStruct(state.shape, F32),
    ]
    args = (proj3, proj3, proj3, proj3, alr3, wa2p, ba2, gn, state)
    assert len(in_specs) == GLA_DEC_INPUTS
    return n_items, in_specs, args, out_specs, out_shape


HALO_ROWS = 16


def _shift_rows(x, halo, ri):
    h1 = halo[HALO_ROWS - 1:HALO_ROWS, :]
    h2 = halo[HALO_ROWS - 2:HALO_ROWS - 1, :]
    p1 = jnp.where(ri == 0, h1, pltpu.roll(x, 1, 0))
    p2 = jnp.where(ri == 0, h2, jnp.where(ri == 1, h1, pltpu.roll(x, 2, 0)))
    return p1, p2


CONV_ITEM_ROWS = 512


def _conv_item(tiles_per_seq, n_items, item, cb_ref, cc_ref, ch_ref, cch_ref, chh_ref, w_ref, o_ref, st_ref):
    tl, cd = cb_ref.shape
    tile = jnp.minimum(item, n_items - 1)
    seq_start = tile % tiles_per_seq == 0
    ri = lax.broadcasted_iota(jnp.int32, (tl, 1), 0)
    for c0 in range(0, cd, LANES):
        cs = slice(c0, c0 + LANES)
        cch = cc_ref[:, cs].astype(F32) * ch_ref[:, cs].astype(F32)
        halo = cch_ref[:, cs].astype(F32) * chh_ref[:, cs].astype(F32)
        halo = jnp.where(seq_start, 0.0, halo)
        p1, p2 = _shift_rows(cch, halo, ri)
        w = w_ref[:, cs]
        u = w[0:1, :] * p2 + w[1:2, :] * p1 + w[2:3, :] * cch
        o_ref[:, cs] = (cb_ref[:, cs].astype(F32) * u).astype(o_ref.dtype)
        st_ref[:, cs] = cch[tl - 2:tl, :]


def _conv_prompt_specs(item, proj_b, w_conv, seq):
    m = proj_b.shape[0]
    tl = CONV_ITEM_ROWS
    assert seq % tl == 0
    n_items = m // tl
    tiles_per_seq = seq // tl
    th = tl // HALO_ROWS
    cd = CONV_DIM
    tile = lambda *g: jnp.minimum(item(*g), n_items - 1)
    in_specs = [
        pl.BlockSpec((tl, cd), lambda *g: (tile(*g), 0)),
        pl.BlockSpec((tl, cd), lambda *g: (tile(*g), 1)),
        pl.BlockSpec((tl, cd), lambda *g: (tile(*g), 2)),
        pl.BlockSpec((HALO_ROWS, cd), lambda *g: (jnp.maximum(tile(*g) * th - 1, 0), 1)),
        pl.BlockSpec((HALO_ROWS, cd), lambda *g: (jnp.maximum(tile(*g) * th - 1, 0), 2)),
        pl.BlockSpec((3, cd), lambda *g: (0, 0)),
    ]
    out_specs = [
        pl.BlockSpec((tl, cd), lambda *g: (tile(*g), 0)),
        pl.BlockSpec((None, 2, cd), lambda *g: (tile(*g) // tiles_per_seq, 0, 0)),
    ]
    out_shape = [
        jax.ShapeDtypeStruct((m, cd), BF16),
        jax.ShapeDtypeStruct((m // seq, 2, cd), F32),
    ]
    return n_items, in_specs, (proj_b, proj_b, proj_b, proj_b, proj_b, w_conv), out_specs, out_shape


def _conv_prompt(proj_b, w_conv, seq):
    n_items, in_specs, args, out_specs, out_shape = _conv_prompt_specs(lambda t: t, proj_b, w_conv, seq)
    tiles_per_seq = seq // CONV_ITEM_ROWS

    def conv_kernel(*refs):
        _conv_item(tiles_per_seq, n_items, pl.program_id(0), *refs)

    return pl.pallas_call(
        conv_kernel,
        grid=(n_items,),
        in_specs=in_specs,
        out_specs=out_specs,
        out_shape=out_shape,
        compiler_params=_params("arbitrary"),
        name="conv_prompt",
    )(*args)


def _conv_dec_kernel(cb_ref, cc_ref, ch_ref, s0_ref, s1_ref, w_ref, o_ref, st_ref):
    cd = cb_ref.shape[1]
    cch = cc_ref[...] * ch_ref[...]
    s1 = s1_ref[...]
    w = w_ref[...]
    u = w[0:1, :] * s0_ref[...] + w[1:2, :] * s1 + w[2:3, :] * cch
    o_ref[...] = (cb_ref[...] * u).astype(o_ref.dtype)
    st_ref[:, :cd] = s1
    st_ref[:, cd:] = cch


def _conv_decode(proj_b, w_conv, state2d):
    n = proj_b.shape[0]
    cd = CONV_DIM
    blk = lambda col: pl.BlockSpec((n, cd), lambda i: (0, col))
    return pl.pallas_call(
        _conv_dec_kernel,
        grid=(1,),
        in_specs=[blk(0), blk(1), blk(2), blk(0), blk(1), pl.BlockSpec((3, cd), lambda i: (0, 0))],
        out_specs=[pl.BlockSpec((n, cd), lambda i: (0, 0)), pl.BlockSpec((n, 2 * cd), lambda i: (0, 0))],
        out_shape=[jax.ShapeDtypeStruct((n, cd), BF16), jax.ShapeDtypeStruct((n, 2 * cd), F32)],
        compiler_params=_params("arbitrary"),
        name="conv_decode",
    )(proj_b, proj_b, proj_b, state2d, state2d, w_conv)


def _softmax_rows(logits):
    m = jnp.max(logits, axis=-1, keepdims=True)
    p = jnp.exp(logits - m)
    return p / jnp.sum(p, axis=-1, keepdims=True)


def _memattn_kernel(q_ref, k_ref, v_ref, o_ref):
    logits = _dot(q_ref[...].astype(BF16), k_ref[...].astype(BF16), NT) * (MEM_HD ** -0.5)
    p = _softmax_rows(logits)
    o_ref[...] = _dot(p.astype(BF16), v_ref[...].astype(BF16)).astype(o_ref.dtype)


def _memattn_prompt(proj_b, mk, mv, nb, seq):
    tl = min(512, seq)
    nt = seq // tl
    h = MEM_HEADS
    qcol = 3 * CONV_DIM // MEM_HD
    return pl.pallas_call(
        _memattn_kernel,
        grid=(nb, h, nt),
        in_specs=[
            pl.BlockSpec((tl, MEM_HD), lambda b, hh, t: (b * nt + t, qcol + hh)),
            pl.BlockSpec((N_MEM, MEM_HD), lambda b, hh, t: (b, hh)),
            pl.BlockSpec((N_MEM, MEM_HD), lambda b, hh, t: (b, hh)),
        ],
        out_specs=pl.BlockSpec((tl, MEM_HD), lambda b, hh, t: (b * nt + t, hh)),
        out_shape=jax.ShapeDtypeStruct((nb * seq, MEM_DIM), BF16),
        compiler_params=_params("parallel", "parallel", "parallel"),
        name="memattn_prompt",
    )(proj_b, mk, mv)


MEMATTN_DEC_ROWS = 2


MEM_HD_CHUNKS = MEM_HD // LANES
MEM_TILE_ROWS = MEM_HD_CHUNKS * MEM_HEADS
assert MEM_HD_CHUNKS == 2 and MEM_TILE_ROWS == 8


def _heads_to_tile(x):
    lead = x.shape[:-2]
    x = x.reshape(lead + (MEM_HEADS, MEM_HD_CHUNKS, LANES))
    return jnp.swapaxes(x, -3, -2).reshape(lead + (MEM_TILE_ROWS, LANES))


def _tile_to_heads(x):
    lead = x.shape[:-2]
    x = x.reshape(lead + (MEM_HD_CHUNKS, MEM_HEADS, LANES))
    return jnp.swapaxes(x, -3, -2).reshape(lead + (MEM_HEADS, MEM_HD))


def _memattn_dec_item(item, q_ref, k_ref, v_ref, o_ref):
    def body(i, carry):
        q = q_ref[i] * (MEM_HD ** -0.5)
        prod = k_ref[i] * q[None]
        prod = prod + pltpu.roll(prod, MEM_HEADS, 1)
        logits = jnp.sum(prod, axis=-1, keepdims=True)
        e = jnp.exp(logits - jnp.max(logits, axis=0, keepdims=True))
        o_ref[i] = jnp.sum(e * v_ref[i], axis=0) / jnp.sum(e, axis=0)
        return carry

    lax.fori_loop(0, q_ref.shape[0], body, 0, unroll=True)


def _memattn_decode_specs(item, q_t, k_t, v_t):
    n = q_t.shape[0]
    r = MEMATTN_DEC_ROWS
    n_items = n // r
    tile = (MEM_TILE_ROWS, LANES)

    def at(ndim):
        return lambda *grid_idx: (jnp.minimum(item(*grid_idx), n_items - 1),) + (0,) * (ndim - 1)

    in_specs = [
        pl.BlockSpec((r,) + tile, at(3)),
        pl.BlockSpec((r, N_MEM) + tile, at(4)),
        pl.BlockSpec((r, N_MEM) + tile, at(4)),
    ]
    out_specs = [pl.BlockSpec((r,) + tile, at(3))]
    out_shape = [jax.ShapeDtypeStruct((n,) + tile, F32)]
    return n_items, in_specs, (q_t, k_t, v_t), out_specs, out_shape


def _merge_kernel(a_ref, b_ref, m_ref, wa_ref, wb_ref, wm_ref, za_ref, zb_ref, zm_ref, o_ref):
    ya = _dot(a_ref[...].astype(BF16), wa_ref[...])
    yb = _dot(b_ref[...].astype(BF16), wb_ref[...])
    ym = _dot(m_ref[...].astype(BF16), wm_ref[...])
    gate = lambda z_ref: _sigmoid(z_ref[...].astype(F32))
    merged = gate(za_ref) * ya + gate(zb_ref) * yb + gate(zm_ref) * ym
    o_ref[...] = merged.astype(o_ref.dtype)


def _merge(o_gla, cbu, om, w_gla_out, w_conv_out, w_mem_out, proj_b):
    m = o_gla.shape[0]
    tm = min(1024, m)
    tn = 512
    d = D_MODEL
    zoff = (3 * CONV_DIM + MEM_DIM) // tn
    zspec = lambda g: pl.BlockSpec((tm, tn), lambda i, j: (i, zoff + g * (d // tn) + j))
    return pl.pallas_call(
        _merge_kernel,
        grid=(m // tm, d // tn),
        in_specs=[
            pl.BlockSpec((tm, o_gla.shape[1]), lambda i, j: (i, 0)),
            pl.BlockSpec((tm, cbu.shape[1]), lambda i, j: (i, 0)),
            pl.BlockSpec((tm, om.shape[1]), lambda i, j: (i, 0)),
            pl.BlockSpec((w_gla_out.shape[0], tn), lambda i, j: (0, j)),
            pl.BlockSpec((w_conv_out.shape[0], tn), lambda i, j: (0, j)),
            pl.BlockSpec((w_mem_out.shape[0], tn), lambda i, j: (0, j)),
            zspec(0), zspec(1), zspec(2),
        ],
        out_specs=pl.BlockSpec((tm, tn), lambda i, j: (i, j)),
        out_shape=jax.ShapeDtypeStruct((m, d), BF16),
        compiler_params=_params("parallel", "parallel"),
        name="merge",
    )(o_gla, cbu, om, w_gla_out, w_conv_out, w_mem_out, proj_b, proj_b, proj_b)


def _wo_ln_kernel(x_ref, m_ref, w_ref, g_ref, b_ref, o_ref):
    tm = x_ref.shape[0]
    half = tm // 4 if tm % 512 == 0 else tm
    for r in range(0, tm, half):
        rows = pl.ds(r, half)
        y = DN_ALPHA * x_ref[rows, :] + _dot(m_ref[rows, :], w_ref[...])
        o_ref[rows, :] = _layer_norm(y, g_ref[...], b_ref[...])


def _wo_ln(x, merged, w_o, g, b):
    m, d = x.shape
    tm = min(512, m)
    return pl.pallas_call(
        _wo_ln_kernel,
        grid=(m // tm,),
        in_specs=[
            pl.BlockSpec((tm, d), lambda i: (i, 0)),
            pl.BlockSpec((tm, d), lambda i: (i, 0)),
            pl.BlockSpec((d, d), lambda i: (0, 0)),
            pl.BlockSpec((1, d), lambda i: (0, 0)),
            pl.BlockSpec((1, d), lambda i: (0, 0)),
        ],
        out_specs=pl.BlockSpec((tm, d), lambda i: (i, 0)),
        out_shape=jax.ShapeDtypeStruct((m, d), F32),
        compiler_params=_params("parallel"),
        name="wo_ln",
    )(x, merged, w_o, g, b)


def _ffn_act(hc, hu, valid, c0=0):
    h = hc * _sigmoid(hc) * hu
    return jnp.where(c0 + lax.broadcasted_iota(jnp.int32, (1, hc.shape[1]), 1) < valid, h, 0.0).astype(BF16)


def _ffn_down(h, valid, wd_ref, acc_ref, x_ref, g_ref, b_ref, o_ref, fi):
    wd = jnp.where(lax.broadcasted_iota(jnp.int32, (h.shape[1], 1), 0) < valid, wd_ref[...], 0.0)
    acc_ref[...] += _dot(h, wd)

    @pl.when(fi == pl.num_programs(1) - 1)
    def _():
        o_ref[...] = _layer_norm(DN_ALPHA * x_ref[...] + acc_ref[...], g_ref[...], b_ref[...])


def _ffn_kernel(seq, x_ref, xh_ref, wg_ref, wu_ref, wd_ref, wc_ref, bc_ref, g_ref, b_ref, *rest):
    dec_in, (o_ref, st_ref, od_ref, so_ref, xb_scr, acc_scr) = rest[:GLA_DEC_INPUTS], rest[GLA_DEC_INPUTS:]
    fi = pl.program_id(1)
    tm = x_ref.shape[0]

    @pl.when(fi == 0)
    def _():
        xb_scr[...] = x_ref[...].astype(BF16)
        acc_scr[...] = jnp.zeros_like(acc_scr)

    _gla_dec_item(*dec_in, od_ref, so_ref)
    xb = xb_scr[...]
    xh = xh_ref[...].astype(BF16)
    tf = wg_ref.shape[1]
    valid = D_FF - fi * tf
    seq_start = (pl.program_id(0) * tm) % seq == 0
    ri = lax.broadcasted_iota(jnp.int32, (tm, 1), 0)
    hs = []
    for c0 in range(0, tf, FFN_GATE_CHUNK):
        cs = slice(c0, c0 + FFN_GATE_CHUNK)
        wg = wg_ref[:, cs]
        hg = _dot(xb, wg)
        st_ref[:, cs] = hg[tm - 2:tm, :]
        halo = jnp.where(seq_start, 0.0, _dot(xh, wg))
        p1, p2 = _shift_rows(hg, halo, ri)
        wc = wc_ref[:, cs]
        hc = wc[0:1, :] * p2 + wc[1:2, :] * p1 + wc[2:3, :] * hg + bc_ref[:, cs]
        hs.append(_ffn_act(hc, _dot(xb, wu_ref[:, cs]), valid, c0))
    _ffn_down(jnp.concatenate(hs, axis=1), valid, wd_ref, acc_scr, x_ref, g_ref, b_ref, o_ref, fi)


def _ffn_prompt(x1, w_gate, w_up, w_down, w_conv, b_conv, g, b, seq, gla_dec_operands):
    m, d = x1.shape
    tm = min(512, seq)
    tf = FF_TILE
    t8 = tm // HALO_ROWS
    n_items, dec_in_specs, dec_args, dec_out_specs, dec_out_shape = _gla_decode_specs(
        lambda i, f: i * FF_TILES + f, *gla_dec_operands)
    assert n_items <= (m // tm) * FF_TILES, "not enough FFN steps to carry the decode GLA work items"
    y, tile_state, o_dec, s_dec = pl.pallas_call(
        functools.partial(_ffn_kernel, seq),
        grid=(m // tm, FF_TILES),
        in_specs=[
            pl.BlockSpec((tm, d), lambda i, f: (i, 0)),
            pl.BlockSpec((HALO_ROWS, d), lambda i, f: (jnp.maximum(i * t8 - 1, 0), 0)),
            pl.BlockSpec((d, tf), lambda i, f: (0, f)),
            pl.BlockSpec((d, tf), lambda i, f: (0, f)),
            pl.BlockSpec((tf, d), lambda i, f: (f, 0)),
            pl.BlockSpec((3, tf), lambda i, f: (0, f)),
            pl.BlockSpec((1, tf), lambda i, f: (0, f)),
            pl.BlockSpec((1, d), lambda i, f: (0, 0)),
            pl.BlockSpec((1, d), lambda i, f: (0, 0)),
        ] + dec_in_specs,
        out_specs=[pl.BlockSpec((tm, d), lambda i, f: (i, 0)),
                   pl.BlockSpec((None, 2, tf), lambda i, f: (i, 0, f))] + dec_out_specs,
        out_shape=[jax.ShapeDtypeStruct((m, d), F32),
                   jax.ShapeDtypeStruct((m // tm, 2, D_FF), F32)] + dec_out_shape,
        scratch_shapes=[pltpu.VMEM((tm, d), BF16), pltpu.VMEM((tm, d), F32)],
        compiler_params=_params("arbitrary", "arbitrary"),
        name="ffn_prompt_gla_decode",
    )(x1, x1, w_gate, w_up, w_down, w_conv, b_conv, g, b, *dec_args)
    tiles_per_seq = seq // tm
    return y, tile_state[tiles_per_seq - 1::tiles_per_seq], o_dec, s_dec


def _ffn_dec_kernel(x_ref, s0_ref, s1_ref, wg_ref, wu_ref, wd_ref, wc_ref, bc_ref, g_ref, b_ref,
                    o_ref, hg_ref, xb_scr, acc_scr):
    fi = pl.program_id(1)

    @pl.when(fi == 0)
    def _():
        xb_scr[...] = x_ref[...].astype(BF16)
        acc_scr[...] = jnp.zeros_like(acc_scr)

    xb = xb_scr[...]
    hg = _dot(xb, wg_ref[...])
    hg_ref[...] = hg
    wc = wc_ref[...]
    hc = wc[0:1, :] * s0_ref[...] + wc[1:2, :] * s1_ref[...] + wc[2:3, :] * hg + bc_ref[...]
    valid = D_FF - fi * hc.shape[1]
    _ffn_down(_ffn_act(hc, _dot(xb, wu_ref[...]), valid), valid, wd_ref, acc_scr, x_ref, g_ref, b_ref, o_ref, fi)


def _ffn_decode(x1, s0, s1, w_gate, w_up, w_down, w_conv, b_conv, g, b):
    m, d = x1.shape
    tf = FF_TILE
    return pl.pallas_call(
        _ffn_dec_kernel,
        grid=(1, FF_TILES),
        in_specs=[
            pl.BlockSpec((m, d), lambda i, f: (0, 0)),
            pl.BlockSpec((m, tf), lambda i, f: (0, f)),
            pl.BlockSpec((m, tf), lambda i, f: (0, f)),
            pl.BlockSpec((d, tf), lambda i, f: (0, f)),
            pl.BlockSpec((d, tf), lambda i, f: (0, f)),
            pl.BlockSpec((tf, d), lambda i, f: (f, 0)),
            pl.BlockSpec((3, tf), lambda i, f: (0, f)),
            pl.BlockSpec((1, tf), lambda i, f: (0, f)),
            pl.BlockSpec((1, d), lambda i, f: (0, 0)),
            pl.BlockSpec((1, d), lambda i, f: (0, 0)),
        ],
        out_specs=[pl.BlockSpec((m, d), lambda i, f: (0, 0)), pl.BlockSpec((m, tf), lambda i, f: (0, f))],
        out_shape=[jax.ShapeDtypeStruct((m, d), F32), jax.ShapeDtypeStruct((m, D_FF), F32)],
        scratch_shapes=[pltpu.VMEM((m, d), BF16), pltpu.VMEM((m, d), F32)],
        compiler_params=_params("parallel", "arbitrary"),
        name="ffn_decode",
    )(x1, s0, s1, w_gate, w_up, w_down, w_conv, b_conv, g, b)


def kernel(x_prompt, x_sample, mem_prompt, cache_mem_k, cache_mem_v, state_gla, state_conv, state_ffn_conv, w_in, w_gla_a2, b_gla_a2, g_gla_norm, w_gla_out, w_conv, w_conv_out, w_mem_k, w_mem_v, w_mem_out, w_o, ln1_g, ln1_b, w_ffn_gate, w_ffn_up, w_ffn_conv, b_ffn_conv, w_ffn_down, ln2_g, ln2_b):
    nb, seq, d = x_prompt.shape
    ns = x_sample.shape[0]

    w_in_t = jnp.swapaxes(w_in[0], 0, 1)
    wa2p = jnp.pad(w_gla_a2[0], ((0, LANES - GLA_RANK), (0, 0)))
    ba2 = b_gla_a2
    gn = g_gla_norm
    w_fc = w_ffn_conv[0]
    b_fc = b_ffn_conv
    w_cv = w_conv[0]

    def mix_in(x2d, tag, dtype, riders_b=(), riders_a=()):
        pr, xb = _proj(x2d, w_in_t, "proj_r_" + tag, COLS_A, LANES, LANES, transposed=True, emit_x=True)
        pb = _proj(xb, w_in_t, "proj_b_" + tag, COLS_A, COLS_B, shift=GLA_RANK, transposed=True,
                   riders=riders_b, out_dtype=dtype)
        pa = _proj(xb, w_in_t, "proj_a_" + tag, 0, COLS_A, transposed=True, riders=riders_a, out_dtype=dtype)
        return pa, pr, pb

    xs = x_sample.reshape(ns, d)
    sa, sr, sb = mix_in(xs, "s", F32)
    mq_s = sb[:, 3 * CONV_DIM:3 * CONV_DIM + MEM_DIM].reshape(ns, MEM_HEADS, MEM_HD)
    memattn_dec = (_memattn_dec_item, _memattn_decode_specs,
                   (_heads_to_tile(mq_s), _heads_to_tile(cache_mem_k[0]), _heads_to_tile(cache_mem_v[0])))

    xp = x_prompt.reshape(nb * seq, d)
    memb = mem_prompt.reshape(nb * N_MEM, d).astype(BF16)
    mk = _proj(memb, w_mem_k[0], "mem_k")
    mv = _proj(memb, w_mem_v[0], "mem_v")
    steps_a, steps_b = _proj_steps(nb * seq, COLS_A), _proj_steps(nb * seq, COLS_B)
    riders_b = (memattn_dec, _cast_rider(w_ffn_gate[0], steps_b), _cast_rider(w_ffn_up[0], steps_b))
    riders_a = tuple(_cast_rider(w[0], steps_a) for w in (w_ffn_down, w_gla_out, w_conv_out, w_mem_out, w_o))
    (pa, w_fd, w_go, w_co, w_mo, w_oo), pr, (pb, om_s, w_fg, w_fu) = mix_in(xp, "p", BF16, riders_b, riders_a)

    def mix_out(x2d, o_gla, cbu, om, proj_b):
        merged = _merge(o_gla, cbu, om, w_go, w_co, w_mo, proj_b)
        return _wo_ln(x2d, merged, w_oo, ln1_g, ln1_b)

    o_gla, p_gla = _gla_prompt(pa, pr, wa2p, ba2, gn, nb, seq)
    cbu, p_conv = _conv_prompt(pb, w_cv, seq)
    om = _memattn_prompt(pb, mk, mv, nb, seq)
    x1 = mix_out(xp, o_gla, cbu, om, pb)
    yp, p_ffn, o_gla_s, s_gla = _ffn_prompt(x1, w_fg, w_fu, w_fd, w_fc, b_fc, ln2_g, ln2_b, seq,
                                            (sa, sr, wa2p, ba2, gn, state_gla[0]))
    o_gla_s = o_gla_s.reshape(ns, GLA_HEADS * GLA_HV)

    cbu_s, s_conv = _conv_decode(sb, w_cv, state_conv[0].reshape(ns, 2 * CONV_DIM))
    om_s = _tile_to_heads(om_s).reshape(ns, MEM_DIM)
    x1s = mix_out(xs, o_gla_s, cbu_s, om_s, sb)
    f0 = state_ffn_conv[0][:, 0, :]
    f1 = state_ffn_conv[0][:, 1, :]
    ys, hg_s = _ffn_decode(x1s, f0, f1, w_fg, w_fu, w_fd, w_fc, b_fc, ln2_g, ln2_b)
    s_ffn = jnp.stack([f1, hg_s], axis=1)

    return (yp.reshape(nb, seq, d), ys.reshape(ns, 1, d),
            mk.reshape(1, nb, N_MEM, MEM_HEADS, MEM_HD), mv.reshape(1, nb, N_MEM, MEM_HEADS, MEM_HD),
            p_gla[None], p_conv[None], p_ffn.reshape(1, nb, 2, D_FF),
            s_gla[None], s_conv.reshape(1, ns, 2, CONV_DIM), s_ffn[None])
```

```python
import functools

import jax
import jax.numpy as jnp
from jax import lax
from jax.experimental import pallas as pl
from jax.experimental.pallas import tpu as pltpu

F32 = jnp.float32
BF16 = jnp.bfloat16

D_MODEL = 2048
GLA_HEADS = 4
GLA_HK = 256
GLA_HV = 512
GLA_RANK = 16
GLA_GATE_NORM = 16.0
CONV_DIM = 1024
N_MEM = 256
MEM_HEADS = 4
MEM_HD = 256
MEM_DIM = 1024
D_FF = 5504
DEPTH = 1
DN_ALPHA = (2 * DEPTH) ** 0.25
LN_EPS = 1e-5
RMS_EPS = 1e-6

LANES = 128
SUBLANES = 8
LOG2_E = 1.4426950408889634
FF_TILE = 512
FF_TILES = -(-D_FF // FF_TILE)
FFN_GATE_CHUNK = FF_TILE
COLS_A = 2 * GLA_HEADS * GLA_HK + 2 * GLA_HEADS * GLA_HV
COLS_B = 3 * CONV_DIM + MEM_DIM + 3 * D_MODEL
GLA_CHUNK = 256
VMEM_LIMIT = 56 * 1024 * 1024

NN = (((1,), (0,)), ((), ()))
NT = (((1,), (1,)), ((), ()))
TN = (((0,), (0,)), ((), ()))


def _dot(a, b, dims=NN):
    return lax.dot_general(a, b, dims, preferred_element_type=F32)


def _params(*sem):
    return pltpu.CompilerParams(dimension_semantics=sem, vmem_limit_bytes=VMEM_LIMIT)


def _split2(x):
    hi = x.astype(BF16)
    lo = (x - hi.astype(F32)).astype(BF16)
    return hi, lo


def _split3(x):
    hi = x.astype(BF16)
    r = x - hi.astype(F32)
    mid = r.astype(BF16)
    lo = (r - mid.astype(F32)).astype(BF16)
    return hi, mid, lo


def _sigmoid(z):
    return 0.5 * jnp.tanh(0.5 * z) + 0.5


def _silu(z):
    h = 0.5 * z
    return h * jnp.tanh(h) + h


def _layer_norm(y, g, b):
    mu = jnp.mean(y, axis=-1, keepdims=True)
    d = y - mu
    var = jnp.mean(d * d, axis=-1, keepdims=True)
    return d * lax.rsqrt(var + LN_EPS) * g + b


def _log2_decay(z):
    return (jnp.minimum(z, 0.0) - jnp.log(1.0 + jnp.exp(-jnp.abs(z)))) * (LOG2_E / GLA_GATE_NORM)


def _log_decay(alr, wa2, ba2):
    ah, al = _split2(alr)
    wh, wl = _split2(wa2)
    return _log2_decay(_dot(ah, wh) + _dot(ah, wl) + _dot(al, wh) + ba2)


def _rms_gate(o, gn, g):
    o = o * lax.rsqrt(jnp.mean(o * o, axis=-1, keepdims=True) + RMS_EPS) * gn
    return o * _silu(g)


PROJ_SHIFT_ROWS = 16
PROJ_TILE = 1024
BF16_SUBLANES = 16


def _proj_kernel(transposed, shift, riders, emit_x, x_ref, w_ref, *rest):
    rest = list(rest)
    wx_ref = rest.pop(0) if shift else None
    wb_scr = rest.pop()
    rider_in = [[rest.pop(0) for _ in range(n_in)] for _, n_in, _ in riders]
    o_ref = rest.pop(0)
    ox_ref = rest.pop(0) if emit_x else None
    rider_out = [[rest.pop(0) for _ in range(n_out)] for _, _, n_out in riders]
    item = pl.program_id(0) * pl.num_programs(1) + pl.program_id(1)

    @pl.when(pl.program_id(1) == 0)
    def _():
        w = w_ref[...]
        if shift:
            w = jnp.concatenate([w[shift:], wx_ref[:shift]], axis=0)
        w = w.astype(BF16)
        wb_scr[...] = w

    xb = x_ref[...].astype(BF16)
    if emit_x:
        ox_ref[...] = xb
    o_ref[...] = _dot(xb, wb_scr[...], NT if transposed else NN).astype(o_ref.dtype)
    for (body, _, _), r_in, r_out in zip(riders, rider_in, rider_out):
        body(item, *r_in, *r_out)


def _proj(x, w, name, col0=0, n=None, tn=PROJ_TILE, shift=0, transposed=False, riders=(), out_dtype=F32,
          emit_x=False):
    m, k = x.shape
    if n is None:
        n = w.shape[0] if transposed else w.shape[1]
    tm = min(m, PROJ_TILE)
    tn = min(n, tn)
    c0 = col0 // tn
    if transposed:
        w_spec = pl.BlockSpec((tn, k), lambda j, i: (c0 + j, 0))
    else:
        assert not shift
        w_spec = pl.BlockSpec((k, tn), lambda j, i: (0, c0 + j))
    in_specs = [pl.BlockSpec((tm, k), lambda j, i: (i, 0)), w_spec]
    args = [x, w]
    if shift:
        r = PROJ_SHIFT_ROWS
        assert shift <= r and shift % 8 == 0 and tn % r == 0 and col0 % r == 0
        in_specs.append(pl.BlockSpec((r, k), lambda j, i: (col0 // r + (j + 1) * (tn // r), 0)))
        args.append(w)
    out_specs = [pl.BlockSpec((tm, tn), lambda j, i: (i, j))]
    out_shape = [jax.ShapeDtypeStruct((m, n), out_dtype)]
    if emit_x:
        assert n == tn, "each x tile must be visited once"
        out_specs.append(pl.BlockSpec((tm, k), lambda j, i: (i, 0)))
        out_shape.append(jax.ShapeDtypeStruct((m, k), BF16))
    kernel_riders = []
    for body, specs_fn, operands in riders:
        n_items, r_in_specs, r_args, r_out_specs, r_out_shape = specs_fn(lambda j, i: j * (m // tm) + i, *operands)
        assert n_items <= (n // tn) * (m // tm), "not enough projection steps to carry the rider's work items"
        in_specs += r_in_specs
        args += list(r_args)
        out_specs += r_out_specs
        out_shape += r_out_shape
        kernel_riders.append((body, len(r_in_specs), len(r_out_specs)))
    outs = pl.pallas_call(
        functools.partial(_proj_kernel, transposed, shift, tuple(kernel_riders), emit_x),
        grid=(n // tn, m // tm),
        in_specs=in_specs,
        out_specs=out_specs,
        out_shape=out_shape,
        scratch_shapes=[pltpu.VMEM((tn, k) if transposed else (k, tn), BF16)],
        compiler_params=_params("arbitrary" if riders else "parallel", "arbitrary"),
        name=name,
    )(*args)
    return outs if len(outs) > 1 else outs[0]


def _proj_steps(m, n):
    return (n // min(n, PROJ_TILE)) * (m // min(m, PROJ_TILE))


def _cast_item(item, x_ref, o_ref):
    o_ref[...] = x_ref[...].astype(o_ref.dtype)


def _cast_specs(item, w, n_steps):
    r, c = w.shape
    rows = -(-r // n_steps)
    rows = -(-rows // BF16_SUBLANES) * BF16_SUBLANES
    n_items = -(-r // rows)
    at = lambda *g: (jnp.minimum(item(*g), n_items - 1), 0)
    return (n_items, [pl.BlockSpec((rows, c), at)], (w,), [pl.BlockSpec((rows, c), at)],
            [jax.ShapeDtypeStruct((r, c), BF16)])


def _cast_rider(w, n_steps):
    return (_cast_item, _cast_specs, (w, n_steps))


def _gla_levels(c):
    return c.bit_length() - 1


def _init_gla_masks(mask_scr, tri_scr):
    c = tri_scr.shape[0]
    rr = lax.broadcasted_iota(jnp.int32, (c, c), 0)
    cc = lax.broadcasted_iota(jnp.int32, (c, c), 1)
    tri_scr[...] = (rr >= cc).astype(BF16)
    mask_scr[0] = (rr == cc).astype(F32)
    for l in range(_gla_levels(c)):
        b = 1 << l
        pair = (rr // (2 * b) == cc // (2 * b)) & ((rr // b) % 2 == 1) & ((cc // b) % 2 == 0)
        mask_scr[1 + l] = pair.astype(F32)


def _intra_scores(q, k, la, bc, mask_scr):
    c, dk = q.shape
    ri = lax.broadcasted_iota(jnp.int32, (c, 1), 0)
    a = jnp.sum(q * k, axis=1, keepdims=True) * mask_scr[0]
    for l in range(_gla_levels(c)):
        b = 1 << l
        if b < SUBLANES:
            upper = (ri // b) % 2 == 1
            if b == 1:
                e = jnp.where(upper, la, 0.0)
            elif b == 2:
                m4 = ri % 4
                la_prev = pltpu.roll(la, 1, 0)
                la_next = pltpu.roll(la, c - 1, 0)
                e = jnp.where(m4 == 2, la, jnp.where(m4 == 3, la + la_prev, jnp.where(m4 == 0, la_next, 0.0)))
            else:
                ref = bc.reshape(c // (2 * b), 2 * b, dk)[:, b - 1:b, :]
                ref = jnp.broadcast_to(ref, (c // (2 * b), 2 * b, dk)).reshape(c, dk)
                dlt = bc - ref
                e = jnp.minimum(dlt, -dlt)
            w = jnp.where(upper, q, k) * jnp.exp2(e)
        else:
            parts = []
            for r0 in range(0, c, 2 * b):
                ref = bc[r0 + b - 1:r0 + b, :]
                parts.append(k[r0:r0 + b] * jnp.exp2(ref - bc[r0:r0 + b]))
                parts.append(q[r0 + b:r0 + 2 * b] * jnp.exp2(bc[r0 + b:r0 + 2 * b] - ref))
            w = jnp.concatenate(parts, axis=0)
        w = w.astype(BF16)
        a = a + _dot(w, w, NT) * mask_scr[1 + l]
    return a


def _gla_kernel(q_ref, k_ref, v_ref, g_ref, alr_ref, wa2_ref, ba2_ref, gn_ref, o_ref, sfin_ref,
                s_scr, mask_scr, tri_scr):
    ci = pl.program_id(2)

    @pl.when(ci == 0)
    def _():
        s_scr[...] = jnp.zeros_like(s_scr)
        _init_gla_masks(mask_scr, tri_scr)

    c = q_ref.shape[0]
    tri = tri_scr[...]
    ones = jnp.ones((c, LANES), BF16)
    alr = alr_ref[...]
    gn = gn_ref[...]
    for hh in range(s_scr.shape[0]):
        ks = slice(hh * GLA_HK, (hh + 1) * GLA_HK)
        vs = slice(hh * GLA_HV, (hh + 1) * GLA_HV)
        la = _log_decay(alr, wa2_ref[:, ks], ba2_ref[:, ks])
        lh, lm, ll = _split3(la)
        bc = _dot(tri, lh) + _dot(tri, lm) + _dot(tri, ll)
        b_last = bc[c - 1:c, :]
        bl_col = _dot(lh, ones, TN) + _dot(lm, ones, TN) + _dot(ll, ones, TN)
        dec_col = jnp.exp2(bl_col)
        dec_col = jnp.concatenate([dec_col] * (GLA_HV // LANES), axis=1)

        q = q_ref[:, ks].astype(F32) * (GLA_HK ** -0.5)
        k = k_ref[:, ks].astype(F32)
        vb = v_ref[:, vs].astype(BF16)
        s = s_scr[hh]
        o = _dot((q * jnp.exp2(bc)).astype(BF16), s.astype(BF16))
        a = _intra_scores(q, k, la, bc, mask_scr)
        o = o + _dot(a.astype(BF16), vb)
        kd = (k * jnp.exp2(b_last - bc)).astype(BF16)
        s_new = dec_col * s + _dot(kd, vb, TN)
        s_scr[hh] = s_new
        o_ref[:, vs] = _rms_gate(o, gn, g_ref[:, vs].astype(F32)).astype(o_ref.dtype)

    @pl.when(ci == pl.num_programs(2) - 1)
    def _():
        sfin_ref[...] = s_scr[...]


GLA_HEADS_PER_STEP = 2


def _gla_prompt(proj_a, alr, wa2p, ba2, gn, nb, seq):
    c = min(GLA_CHUNK, seq)
    nc = seq // c
    hp = GLA_HEADS_PER_STEP
    ng = GLA_HEADS // hp
    wk, wv = hp * GLA_HK, hp * GLA_HV
    row = lambda b, hg, ci: b * nc + ci
    return pl.pallas_call(
        _gla_kernel,
        grid=(nb, ng, nc),
        in_specs=[
            pl.BlockSpec((c, wk), lambda b, hg, ci: (row(b, hg, ci), hg)),
            pl.BlockSpec((c, wk), lambda b, hg, ci: (row(b, hg, ci), ng + hg)),
            pl.BlockSpec((c, wv), lambda b, hg, ci: (row(b, hg, ci), ng + hg)),
            pl.BlockSpec((c, wv), lambda b, hg, ci: (row(b, hg, ci), 2 * ng + hg)),
            pl.BlockSpec((c, LANES), lambda b, hg, ci: (row(b, hg, ci), 0)),
            pl.BlockSpec((LANES, wk), lambda b, hg, ci: (0, hg)),
            pl.BlockSpec((1, wk), lambda b, hg, ci: (0, hg)),
            pl.BlockSpec((1, GLA_HV), lambda b, hg, ci: (0, 0)),
        ],
        out_specs=[
            pl.BlockSpec((c, wv), lambda b, hg, ci: (row(b, hg, ci), hg)),
            pl.BlockSpec((None, hp, GLA_HK, GLA_HV), lambda b, hg, ci: (b, hg, 0, 0)),
        ],
        out_shape=[
            jax.ShapeDtypeStruct((nb * seq, GLA_HEADS * GLA_HV), BF16),
            jax.ShapeDtypeStruct((nb, GLA_HEADS, GLA_HK, GLA_HV), F32),
        ],
        scratch_shapes=[pltpu.VMEM((hp, GLA_HK, GLA_HV), F32),
                        pltpu.VMEM((1 + _gla_levels(c), c, c), F32),
                        pltpu.VMEM((c, c), BF16)],
        compiler_params=_params("parallel", "parallel", "arbitrary"),
        name="gla_prompt",
    )(proj_a, proj_a, proj_a, proj_a, alr, wa2p, ba2, gn)


GLA_DEC_ROWS = 4


def _gla_dec_item(q_ref, k_ref, v_ref, g_ref, alr_ref, wa2_ref, ba2_ref, gn_ref, s_ref, o_ref, so_ref):
    nb = q_ref.shape[0]
    alr = alr_ref[...]
    wa2 = wa2_ref[...]
    z = ba2_ref[...]
    for r in range(GLA_RANK):
        z = z + alr[:, r:r + 1] * wa2[r:r + 1, :]
    la = _log2_decay(z)
    rep = LANES // nb
    to_cols = lambda x: jnp.concatenate([x] * rep, axis=0).T
    a_t = to_cols(jnp.exp2(la))
    k_t = to_cols(k_ref[...])
    q_t = to_cols(q_ref[...] * (GLA_HK ** -0.5))
    v = v_ref[...]
    rows = []
    for j in range(nb):
        s_new = a_t[:, j:j + 1] * s_ref[j] + k_t[:, j:j + 1] * v[j:j + 1, :]
        so_ref[j] = s_new
        rows.append(jnp.sum(q_t[:, j:j + 1] * s_new, axis=0, keepdims=True))
    o = jnp.concatenate(rows, axis=0)
    o_ref[...] = _rms_gate(o, gn_ref[...], g_ref[...]).astype(o_ref.dtype)


GLA_DEC_INPUTS = 9


def _gla_decode_specs(item, proj_a, alr, wa2p, ba2, gn, state):
    nseq = proj_a.shape[0]
    h = GLA_HEADS
    r = GLA_DEC_ROWS
    n_items = (nseq // r) * h
    proj3 = proj_a.reshape(nseq // r, r, proj_a.shape[1])
    alr3 = alr.reshape(nseq // r, r, LANES)

    def at(fn):
        def index_map(*grid_idx):
            it = jnp.minimum(item(*grid_idx), n_items - 1)
            return fn(it // h, it % h)
        return index_map

    in_specs = [
        pl.BlockSpec((None, r, GLA_HK), at(lambda b, hh: (b, 0, hh))),
        pl.BlockSpec((None, r, GLA_HK), at(lambda b, hh: (b, 0, h + hh))),
        pl.BlockSpec((None, r, GLA_HV), at(lambda b, hh: (b, 0, h + hh))),
        pl.BlockSpec((None, r, GLA_HV), at(lambda b, hh: (b, 0, 2 * h + hh))),
        pl.BlockSpec((None, r, LANES), at(lambda b, hh: (b, 0, 0))),
        pl.BlockSpec((LANES, GLA_HK), at(lambda b, hh: (0, hh))),
        pl.BlockSpec((1, GLA_HK), at(lambda b, hh: (0, hh))),
        pl.BlockSpec((1, GLA_HV), at(lambda b, hh: (0, 0))),
        pl.BlockSpec((r, None, GLA_HK, GLA_HV), at(lambda b, hh: (b, hh, 0, 0))),
    ]
    out_specs = [
        pl.BlockSpec((None, r, GLA_HV), at(lambda b, hh: (b, 0, hh))),
        pl.BlockSpec((r, None, GLA_HK, GLA_HV), at(lambda b, hh: (b, hh, 0, 0))),
    ]
    out_shape = [
        jax.ShapeDtypeStruct((nseq // r, r, h * GLA_HV), F32),
        jax.ShapeDtypeStruct(state.shape, F32),
    ]
    args = (proj3, proj3, proj3, proj3, alr3, wa2p, ba2, gn, state)
    assert len(in_specs) == GLA_DEC_INPUTS
    return n_items, in_specs, args, out_specs, out_shape


HALO_ROWS = 16


def _shift_rows(x, halo, ri):
    h1 = halo[HALO_ROWS - 1:HALO_ROWS, :]
    h2 = halo[HALO_ROWS - 2:HALO_ROWS - 1, :]
    p1 = jnp.where(ri == 0, h1, pltpu.roll(x, 1, 0))
    p2 = jnp.where(ri == 0, h2, jnp.where(ri == 1, h1, pltpu.roll(x, 2, 0)))
    return p1, p2


CONV_ITEM_ROWS = 512


def _conv_item(tiles_per_seq, n_items, item, cb_ref, cc_ref, ch_ref, cch_ref, chh_ref, w_ref, o_ref, st_ref):
    tl, cd = cb_ref.shape
    tile = jnp.minimum(item, n_items - 1)
    seq_start = tile % tiles_per_seq == 0
    ri = lax.broadcasted_iota(jnp.int32, (tl, 1), 0)
    for c0 in range(0, cd, LANES):
        cs = slice(c0, c0 + LANES)
        cch = cc_ref[:, cs].astype(F32) * ch_ref[:, cs].astype(F32)
        halo = cch_ref[:, cs].astype(F32) * chh_ref[:, cs].astype(F32)
        halo = jnp.where(seq_start, 0.0, halo)
        p1, p2 = _shift_rows(cch, halo, ri)
        w = w_ref[:, cs]
        u = w[0:1, :] * p2 + w[1:2, :] * p1 + w[2:3, :] * cch
        o_ref[:, cs] = (cb_ref[:, cs].astype(F32) * u).astype(o_ref.dtype)
        st_ref[:, cs] = cch[tl - 2:tl, :]


def _conv_prompt_specs(item, proj_b, w_conv, seq):
    m = proj_b.shape[0]
    tl = CONV_ITEM_ROWS
    assert seq % tl == 0
    n_items = m // tl
    tiles_per_seq = seq // tl
    th = tl // HALO_ROWS
    cd = CONV_DIM
    tile = lambda *g: jnp.minimum(item(*g), n_items - 1)
    in_specs = [
        pl.BlockSpec((tl, cd), lambda *g: (tile(*g), 0)),
        pl.BlockSpec((tl, cd), lambda *g: (tile(*g), 1)),
        pl.BlockSpec((tl, cd), lambda *g: (tile(*g), 2)),
        pl.BlockSpec((HALO_ROWS, cd), lambda *g: (jnp.maximum(tile(*g) * th - 1, 0), 1)),
        pl.BlockSpec((HALO_ROWS, cd), lambda *g: (jnp.maximum(tile(*g) * th - 1, 0), 2)),
        pl.BlockSpec((3, cd), lambda *g: (0, 0)),
    ]
    out_specs = [
        pl.BlockSpec((tl, cd), lambda *g: (tile(*g), 0)),
        pl.BlockSpec((None, 2, cd), lambda *g: (tile(*g) // tiles_per_seq, 0, 0)),
    ]
    out_shape = [
        jax.ShapeDtypeStruct((m, cd), BF16),
        jax.ShapeDtypeStruct((m // seq, 2, cd), F32),
    ]
    return n_items, in_specs, (proj_b, proj_b, proj_b, proj_b, proj_b, w_conv), out_specs, out_shape


def _conv_prompt(proj_b, w_conv, seq):
    n_items, in_specs, args, out_specs, out_shape = _conv_prompt_specs(lambda t: t, proj_b, w_conv, seq)
    tiles_per_seq = seq // CONV_ITEM_ROWS

    def conv_kernel(*refs):
        _conv_item(tiles_per_seq, n_items, pl.program_id(0), *refs)

    return pl.pallas_call(
        conv_kernel,
        grid=(n_items,),
        in_specs=in_specs,
        out_specs=out_specs,
        out_shape=out_shape,
        compiler_params=_params("arbitrary"),
        name="conv_prompt",
    )(*args)


def _conv_dec_kernel(cb_ref, cc_ref, ch_ref, s0_ref, s1_ref, w_ref, o_ref, st_ref):
    cd = cb_ref.shape[1]
    cch = cc_ref[...] * ch_ref[...]
    s1 = s1_ref[...]
    w = w_ref[...]
    u = w[0:1, :] * s0_ref[...] + w[1:2, :] * s1 + w[2:3, :] * cch
    o_ref[...] = (cb_ref[...] * u).astype(o_ref.dtype)
    st_ref[:, :cd] = s1
    st_ref[:, cd:] = cch


def _conv_decode(proj_b, w_conv, state2d):
    n = proj_b.shape[0]
    cd = CONV_DIM
    blk = lambda col: pl.BlockSpec((n, cd), lambda i: (0, col))
    return pl.pallas_call(
        _conv_dec_kernel,
        grid=(1,),
        in_specs=[blk(0), blk(1), blk(2), blk(0), blk(1), pl.BlockSpec((3, cd), lambda i: (0, 0))],
        out_specs=[pl.BlockSpec((n, cd), lambda i: (0, 0)), pl.BlockSpec((n, 2 * cd), lambda i: (0, 0))],
        out_shape=[jax.ShapeDtypeStruct((n, cd), BF16), jax.ShapeDtypeStruct((n, 2 * cd), F32)],
        compiler_params=_params("arbitrary"),
        name="conv_decode",
    )(proj_b, proj_b, proj_b, state2d, state2d, w_conv)


def _softmax_rows(logits):
    m = jnp.max(logits, axis=-1, keepdims=True)
    p = jnp.exp(logits - m)
    return p / jnp.sum(p, axis=-1, keepdims=True)


def _memattn_kernel(q_ref, k_ref, v_ref, o_ref):
    logits = _dot(q_ref[...].astype(BF16), k_ref[...].astype(BF16), NT) * (MEM_HD ** -0.5)
    p = _softmax_rows(logits)
    o_ref[...] = _dot(p.astype(BF16), v_ref[...].astype(BF16)).astype(o_ref.dtype)


def _memattn_prompt(proj_b, mk, mv, nb, seq):
    tl = min(1024, seq)
    nt = seq // tl
    h = MEM_HEADS
    qcol = 3 * CONV_DIM // MEM_HD
    return pl.pallas_call(
        _memattn_kernel,
        grid=(nb, h, nt),
        in_specs=[
            pl.BlockSpec((tl, MEM_HD), lambda b, hh, t: (b * nt + t, qcol + hh)),
            pl.BlockSpec((N_MEM, MEM_HD), lambda b, hh, t: (b, hh)),
            pl.BlockSpec((N_MEM, MEM_HD), lambda b, hh, t: (b, hh)),
        ],
        out_specs=pl.BlockSpec((tl, MEM_HD), lambda b, hh, t: (b * nt + t, hh)),
        out_shape=jax.ShapeDtypeStruct((nb * seq, MEM_DIM), BF16),
        compiler_params=_params("parallel", "parallel", "parallel"),
        name="memattn_prompt",
    )(proj_b, mk, mv)


MEMATTN_DEC_ROWS = 2


MEM_HD_CHUNKS = MEM_HD // LANES
MEM_TILE_ROWS = MEM_HD_CHUNKS * MEM_HEADS
assert MEM_HD_CHUNKS == 2 and MEM_TILE_ROWS == 8


def _heads_to_tile(x):
    lead = x.shape[:-2]
    x = x.reshape(lead + (MEM_HEADS, MEM_HD_CHUNKS, LANES))
    return jnp.swapaxes(x, -3, -2).reshape(lead + (MEM_TILE_ROWS, LANES))


def _tile_to_heads(x):
    lead = x.shape[:-2]
    x = x.reshape(lead + (MEM_HD_CHUNKS, MEM_HEADS, LANES))
    return jnp.swapaxes(x, -3, -2).reshape(lead + (MEM_HEADS, MEM_HD))


def _memattn_dec_item(item, q_ref, k_ref, v_ref, o_ref):
    def body(i, carry):
        q = q_ref[i] * (MEM_HD ** -0.5)
        prod = k_ref[i] * q[None]
        prod = prod + pltpu.roll(prod, MEM_HEADS, 1)
        logits = jnp.sum(prod, axis=-1, keepdims=True)
        e = jnp.exp(logits - jnp.max(logits, axis=0, keepdims=True))
        o_ref[i] = jnp.sum(e * v_ref[i], axis=0) / jnp.sum(e, axis=0)
        return carry

    lax.fori_loop(0, q_ref.shape[0], body, 0, unroll=True)


def _memattn_decode_specs(item, q_t, k_t, v_t):
    n = q_t.shape[0]
    r = MEMATTN_DEC_ROWS
    n_items = n // r
    tile = (MEM_TILE_ROWS, LANES)

    def at(ndim):
        return lambda *grid_idx: (jnp.minimum(item(*grid_idx), n_items - 1),) + (0,) * (ndim - 1)

    in_specs = [
        pl.BlockSpec((r,) + tile, at(3)),
        pl.BlockSpec((r, N_MEM) + tile, at(4)),
        pl.BlockSpec((r, N_MEM) + tile, at(4)),
    ]
    out_specs = [pl.BlockSpec((r,) + tile, at(3))]
    out_shape = [jax.ShapeDtypeStruct((n,) + tile, F32)]
    return n_items, in_specs, (q_t, k_t, v_t), out_specs, out_shape


def _merge_kernel(a_ref, b_ref, m_ref, wa_ref, wb_ref, wm_ref, za_ref, zb_ref, zm_ref, o_ref):
    ya = _dot(a_ref[...].astype(BF16), wa_ref[...])
    yb = _dot(b_ref[...].astype(BF16), wb_ref[...])
    ym = _dot(m_ref[...].astype(BF16), wm_ref[...])
    gate = lambda z_ref: _sigmoid(z_ref[...].astype(F32))
    merged = gate(za_ref) * ya + gate(zb_ref) * yb + gate(zm_ref) * ym
    o_ref[...] = merged.astype(o_ref.dtype)


def _merge(o_gla, cbu, om, w_gla_out, w_conv_out, w_mem_out, proj_b):
    m = o_gla.shape[0]
    tm = min(1024, m)
    tn = 512
    d = D_MODEL
    zoff = (3 * CONV_DIM + MEM_DIM) // tn
    zspec = lambda g: pl.BlockSpec((tm, tn), lambda i, j: (i, zoff + g * (d // tn) + j))
    return pl.pallas_call(
        _merge_kernel,
        grid=(m // tm, d // tn),
        in_specs=[
            pl.BlockSpec((tm, o_gla.shape[1]), lambda i, j: (i, 0)),
            pl.BlockSpec((tm, cbu.shape[1]), lambda i, j: (i, 0)),
            pl.BlockSpec((tm, om.shape[1]), lambda i, j: (i, 0)),
            pl.BlockSpec((w_gla_out.shape[0], tn), lambda i, j: (0, j)),
            pl.BlockSpec((w_conv_out.shape[0], tn), lambda i, j: (0, j)),
            pl.BlockSpec((w_mem_out.shape[0], tn), lambda i, j: (0, j)),
            zspec(0), zspec(1), zspec(2),
        ],
        out_specs=pl.BlockSpec((tm, tn), lambda i, j: (i, j)),
        out_shape=jax.ShapeDtypeStruct((m, d), BF16),
        compiler_params=_params("parallel", "parallel"),
        name="merge",
    )(o_gla, cbu, om, w_gla_out, w_conv_out, w_mem_out, proj_b, proj_b, proj_b)


def _wo_ln_kernel(x_ref, m_ref, w_ref, g_ref, b_ref, o_ref):
    tm = x_ref.shape[0]
    half = tm // 4 if tm % 512 == 0 else tm
    for r in range(0, tm, half):
        rows = pl.ds(r, half)
        y = DN_ALPHA * x_ref[rows, :] + _dot(m_ref[rows, :], w_ref[...])
        o_ref[rows, :] = _layer_norm(y, g_ref[...], b_ref[...])


def _wo_ln(x, merged, w_o, g, b):
    m, d = x.shape
    tm = min(512, m)
    return pl.pallas_call(
        _wo_ln_kernel,
        grid=(m // tm,),
        in_specs=[
            pl.BlockSpec((tm, d), lambda i: (i, 0)),
            pl.BlockSpec((tm, d), lambda i: (i, 0)),
            pl.BlockSpec((d, d), lambda i: (0, 0)),
            pl.BlockSpec((1, d), lambda i: (0, 0)),
            pl.BlockSpec((1, d), lambda i: (0, 0)),
        ],
        out_specs=pl.BlockSpec((tm, d), lambda i: (i, 0)),
        out_shape=jax.ShapeDtypeStruct((m, d), F32),
        compiler_params=_params("parallel"),
        name="wo_ln",
    )(x, merged, w_o, g, b)


def _ffn_act(hc, hu, valid, c0=0):
    h = hc * _sigmoid(hc) * hu
    return jnp.where(c0 + lax.broadcasted_iota(jnp.int32, (1, hc.shape[1]), 1) < valid, h, 0.0).astype(BF16)


def _ffn_down(h, valid, wd_ref, acc_ref, x_ref, g_ref, b_ref, o_ref, fi):
    wd = jnp.where(lax.broadcasted_iota(jnp.int32, (h.shape[1], 1), 0) < valid, wd_ref[...], 0.0)
    acc_ref[...] += _dot(h, wd)

    @pl.when(fi == pl.num_programs(1) - 1)
    def _():
        o_ref[...] = _layer_norm(DN_ALPHA * x_ref[...] + acc_ref[...], g_ref[...], b_ref[...])


def _ffn_kernel(seq, x_ref, xh_ref, wg_ref, wu_ref, wd_ref, wc_ref, bc_ref, g_ref, b_ref, *rest):
    dec_in, (o_ref, st_ref, od_ref, so_ref, xb_scr, acc_scr) = rest[:GLA_DEC_INPUTS], rest[GLA_DEC_INPUTS:]
    fi = pl.program_id(1)
    tm = x_ref.shape[0]

    @pl.when(fi == 0)
    def _():
        xb_scr[...] = x_ref[...].astype(BF16)
        acc_scr[...] = jnp.zeros_like(acc_scr)

    _gla_dec_item(*dec_in, od_ref, so_ref)
    xb = xb_scr[...]
    xh = xh_ref[...].astype(BF16)
    tf = wg_ref.shape[1]
    valid = D_FF - fi * tf
    seq_start = (pl.program_id(0) * tm) % seq == 0
    ri = lax.broadcasted_iota(jnp.int32, (tm, 1), 0)
    hs = []
    for c0 in range(0, tf, FFN_GATE_CHUNK):
        cs = slice(c0, c0 + FFN_GATE_CHUNK)
        wg = wg_ref[:, cs]
        hg = _dot(xb, wg)
        st_ref[:, cs] = hg[tm - 2:tm, :]
        halo = jnp.where(seq_start, 0.0, _dot(xh, wg))
        p1, p2 = _shift_rows(hg, halo, ri)
        wc = wc_ref[:, cs]
        hc = wc[0:1, :] * p2 + wc[1:2, :] * p1 + wc[2:3, :] * hg + bc_ref[:, cs]
        hs.append(_ffn_act(hc, _dot(xb, wu_ref[:, cs]), valid, c0))
    _ffn_down(jnp.concatenate(hs, axis=1), valid, wd_ref, acc_scr, x_ref, g_ref, b_ref, o_ref, fi)


def _ffn_prompt(x1, w_gate, w_up, w_down, w_conv, b_conv, g, b, seq, gla_dec_operands):
    m, d = x1.shape
    tm = min(512, seq)
    tf = FF_TILE
    t8 = tm // HALO_ROWS
    n_items, dec_in_specs, dec_args, dec_out_specs, dec_out_shape = _gla_decode_specs(
        lambda i, f: i * FF_TILES + f, *gla_dec_operands)
    assert n_items <= (m // tm) * FF_TILES, "not enough FFN steps to carry the decode GLA work items"
    y, tile_state, o_dec, s_dec = pl.pallas_call(
        functools.partial(_ffn_kernel, seq),
        grid=(m // tm, FF_TILES),
        in_specs=[
            pl.BlockSpec((tm, d), lambda i, f: (i, 0)),
            pl.BlockSpec((HALO_ROWS, d), lambda i, f: (jnp.maximum(i * t8 - 1, 0), 0)),
            pl.BlockSpec((d, tf), lambda i, f: (0, f)),
            pl.BlockSpec((d, tf), lambda i, f: (0, f)),
            pl.BlockSpec((tf, d), lambda i, f: (f, 0)),
            pl.BlockSpec((3, tf), lambda i, f: (0, f)),
            pl.BlockSpec((1, tf), lambda i, f: (0, f)),
            pl.BlockSpec((1, d), lambda i, f: (0, 0)),
            pl.BlockSpec((1, d), lambda i, f: (0, 0)),
        ] + dec_in_specs,
        out_specs=[pl.BlockSpec((tm, d), lambda i, f: (i, 0)),
                   pl.BlockSpec((None, 2, tf), lambda i, f: (i, 0, f))] + dec_out_specs,
        out_shape=[jax.ShapeDtypeStruct((m, d), F32),
                   jax.ShapeDtypeStruct((m // tm, 2, D_FF), F32)] + dec_out_shape,
        scratch_shapes=[pltpu.VMEM((tm, d), BF16), pltpu.VMEM((tm, d), F32)],
        compiler_params=_params("arbitrary", "arbitrary"),
        name="ffn_prompt_gla_decode",
    )(x1, x1, w_gate, w_up, w_down, w_conv, b_conv, g, b, *dec_args)
    tiles_per_seq = seq // tm
    return y, tile_state[tiles_per_seq - 1::tiles_per_seq], o_dec, s_dec


def _ffn_dec_kernel(x_ref, s0_ref, s1_ref, wg_ref, wu_ref, wd_ref, wc_ref, bc_ref, g_ref, b_ref,
                    o_ref, hg_ref, xb_scr, acc_scr):
    fi = pl.program_id(1)

    @pl.when(fi == 0)
    def _():
        xb_scr[...] = x_ref[...].astype(BF16)
        acc_scr[...] = jnp.zeros_like(acc_scr)

    xb = xb_scr[...]
    hg = _dot(xb, wg_ref[...])
    hg_ref[...] = hg
    wc = wc_ref[...]
    hc = wc[0:1, :] * s0_ref[...] + wc[1:2, :] * s1_ref[...] + wc[2:3, :] * hg + bc_ref[...]
    valid = D_FF - fi * hc.shape[1]
    _ffn_down(_ffn_act(hc, _dot(xb, wu_ref[...]), valid), valid, wd_ref, acc_scr, x_ref, g_ref, b_ref, o_ref, fi)


def _ffn_decode(x1, s0, s1, w_gate, w_up, w_down, w_conv, b_conv, g, b):
    m, d = x1.shape
    tf = FF_TILE
    return pl.pallas_call(
        _ffn_dec_kernel,
        grid=(1, FF_TILES),
        in_specs=[
            pl.BlockSpec((m, d), lambda i, f: (0, 0)),
            pl.BlockSpec((m, tf), lambda i, f: (0, f)),
            pl.BlockSpec((m, tf), lambda i, f: (0, f)),
            pl.BlockSpec((d, tf), lambda i, f: (0, f)),
            pl.BlockSpec((d, tf), lambda i, f: (0, f)),
            pl.BlockSpec((tf, d), lambda i, f: (f, 0)),
            pl.BlockSpec((3, tf), lambda i, f: (0, f)),
            pl.BlockSpec((1, tf), lambda i, f: (0, f)),
            pl.BlockSpec((1, d), lambda i, f: (0, 0)),
            pl.BlockSpec((1, d), lambda i, f: (0, 0)),
        ],
        out_specs=[pl.BlockSpec((m, d), lambda i, f: (0, 0)), pl.BlockSpec((m, tf), lambda i, f: (0, f))],
        out_shape=[jax.ShapeDtypeStruct((m, d), F32), jax.ShapeDtypeStruct((m, D_FF), F32)],
        scratch_shapes=[pltpu.VMEM((m, d), BF16), pltpu.VMEM((m, d), F32)],
        compiler_params=_params("parallel", "arbitrary"),
        name="ffn_decode",
    )(x1, s0, s1, w_gate, w_up, w_down, w_conv, b_conv, g, b)


def kernel(x_prompt, x_sample, mem_prompt, cache_mem_k, cache_mem_v, state_gla, state_conv, state_ffn_conv, w_in, w_gla_a2, b_gla_a2, g_gla_norm, w_gla_out, w_conv, w_conv_out, w_mem_k, w_mem_v, w_mem_out, w_o, ln1_g, ln1_b, w_ffn_gate, w_ffn_up, w_ffn_conv, b_ffn_conv, w_ffn_down, ln2_g, ln2_b):
    nb, seq, d = x_prompt.shape
    ns = x_sample.shape[0]

    w_in_t = jnp.swapaxes(w_in[0], 0, 1)
    wa2p = jnp.pad(w_gla_a2[0], ((0, LANES - GLA_RANK), (0, 0)))
    ba2 = b_gla_a2
    gn = g_gla_norm
    w_fc = w_ffn_conv[0]
    b_fc = b_ffn_conv
    w_cv = w_conv[0]

    def mix_in(x2d, tag, dtype, riders_b=(), riders_a=()):
        pr, xb = _proj(x2d, w_in_t, "proj_r_" + tag, COLS_A, LANES, LANES, transposed=True, emit_x=True)
        pb = _proj(xb, w_in_t, "proj_b_" + tag, COLS_A, COLS_B, shift=GLA_RANK, transposed=True,
                   riders=riders_b, out_dtype=dtype)
        pa = _proj(xb, w_in_t, "proj_a_" + tag, 0, COLS_A, transposed=True, riders=riders_a, out_dtype=dtype)
        return pa, pr, pb

    xs = x_sample.reshape(ns, d)
    sa, sr, sb = mix_in(xs, "s", F32)
    mq_s = sb[:, 3 * CONV_DIM:3 * CONV_DIM + MEM_DIM].reshape(ns, MEM_HEADS, MEM_HD)
    memattn_dec = (_memattn_dec_item, _memattn_decode_specs,
                   (_heads_to_tile(mq_s), _heads_to_tile(cache_mem_k[0]), _heads_to_tile(cache_mem_v[0])))

    xp = x_prompt.reshape(nb * seq, d)
    memb = mem_prompt.reshape(nb * N_MEM, d).astype(BF16)
    mk = _proj(memb, w_mem_k[0], "mem_k", tn=MEM_HD)
    mv = _proj(memb, w_mem_v[0], "mem_v", tn=MEM_HD)
    steps_a, steps_b = _proj_steps(nb * seq, COLS_A), _proj_steps(nb * seq, COLS_B)
    riders_b = (memattn_dec, _cast_rider(w_ffn_gate[0], steps_b), _cast_rider(w_ffn_up[0], steps_b))
    riders_a = tuple(_cast_rider(w[0], steps_a) for w in (w_ffn_down, w_gla_out, w_conv_out, w_mem_out, w_o))
    (pa, w_fd, w_go, w_co, w_mo, w_oo), pr, (pb, om_s, w_fg, w_fu) = mix_in(xp, "p", BF16, riders_b, riders_a)

    def mix_out(x2d, o_gla, cbu, om, proj_b):
        merged = _merge(o_gla, cbu, om, w_go, w_co, w_mo, proj_b)
        return _wo_ln(x2d, merged, w_oo, ln1_g, ln1_b)

    o_gla, p_gla = _gla_prompt(pa, pr, wa2p, ba2, gn, nb, seq)
    cbu, p_conv = _conv_prompt(pb, w_cv, seq)
    om = _memattn_prompt(pb, mk, mv, nb, seq)
    x1 = mix_out(xp, o_gla, cbu, om, pb)
    yp, p_ffn, o_gla_s, s_gla = _ffn_prompt(x1, w_fg, w_fu, w_fd, w_fc, b_fc, ln2_g, ln2_b, seq,
                                            (sa, sr, wa2p, ba2, gn, state_gla[0]))
    o_gla_s = o_gla_s.reshape(ns, GLA_HEADS * GLA_HV)

    cbu_s, s_conv = _conv_decode(sb, w_cv, state_conv[0].reshape(ns, 2 * CONV_DIM))
    om_s = _tile_to_heads(om_s).reshape(ns, MEM_DIM)
    x1s = mix_out(xs, o_gla_s, cbu_s, om_s, sb)
    f0 = state_ffn_conv[0][:, 0, :]
    f1 = state_ffn_conv[0][:, 1, :]
    ys, hg_s = _ffn_decode(x1s, f0, f1, w_fg, w_fu, w_fd, w_fc, b_fc, ln2_g, ln2_b)
    s_ffn = jnp.stack([f1, hg_s], axis=1)

    return (yp.reshape(nb, seq, d), ys.reshape(ns, 1, d),
            mk.reshape(1, nb, N_MEM, MEM_HEADS, MEM_HD), mv.reshape(1, nb, N_MEM, MEM_HEADS, MEM_HD),
            p_gla[None], p_conv[None], p_ffn.reshape(1, nb, 2, D_FF),
            s_gla[None], s_conv.reshape(1, ns, 2, CONV_DIM), s_ffn[None])
```

```python
import functools

import jax
import jax.numpy as jnp
from jax import lax
from jax.experimental import pallas as pl
from jax.experimental.pallas import tpu as pltpu

F32 = jnp.float32
BF16 = jnp.bfloat16

D_MODEL = 2048
GLA_HEADS = 4
GLA_HK = 256
GLA_HV = 512
GLA_RANK = 16
GLA_GATE_NORM = 16.0
CONV_DIM = 1024
N_MEM = 256
MEM_HEADS = 4
MEM_HD = 256
MEM_DIM = 1024
D_FF = 5504
DEPTH = 1
DN_ALPHA = (2 * DEPTH) ** 0.25
LN_EPS = 1e-5
RMS_EPS = 1e-6

LANES = 128
SUBLANES = 8
LOG2_E = 1.4426950408889634
FF_TILE = 512
FF_TILES = -(-D_FF // FF_TILE)
FFN_GATE_CHUNK = FF_TILE
COLS_A = 2 * GLA_HEADS * GLA_HK + 2 * GLA_HEADS * GLA_HV
COLS_B = 3 * CONV_DIM + MEM_DIM + 3 * D_MODEL
GLA_CHUNK = 256
VMEM_LIMIT = 56 * 1024 * 1024

NN = (((1,), (0,)), ((), ()))
NT = (((1,), (1,)), ((), ()))
TN = (((0,), (0,)), ((), ()))


def _dot(a, b, dims=NN):
    return lax.dot_general(a, b, dims, preferred_element_type=F32)


def _params(*sem):
    return pltpu.CompilerParams(dimension_semantics=sem, vmem_limit_bytes=VMEM_LIMIT)


def _split2(x):
    hi = x.astype(BF16)
    lo = (x - hi.astype(F32)).astype(BF16)
    return hi, lo


def _split3(x):
    hi = x.astype(BF16)
    r = x - hi.astype(F32)
    mid = r.astype(BF16)
    lo = (r - mid.astype(F32)).astype(BF16)
    return hi, mid, lo


def _sigmoid(z):
    return 0.5 * jnp.tanh(0.5 * z) + 0.5


def _silu(z):
    h = 0.5 * z
    return h * jnp.tanh(h) + h


def _layer_norm(y, g, b):
    mu = jnp.mean(y, axis=-1, keepdims=True)
    d = y - mu
    var = jnp.mean(d * d, axis=-1, keepdims=True)
    return d * lax.rsqrt(var + LN_EPS) * g + b


def _log2_decay(z):
    return (jnp.minimum(z, 0.0) - jnp.log(1.0 + jnp.exp(-jnp.abs(z)))) * (LOG2_E / GLA_GATE_NORM)


def _log_decay(alr, wa2, ba2):
    ah, al = _split2(alr)
    wh, wl = _split2(wa2)
    return _log2_decay(_dot(ah, wh) + _dot(ah, wl) + _dot(al, wh) + ba2)


def _rms_gate(o, gn, g):
    o = o * lax.rsqrt(jnp.mean(o * o, axis=-1, keepdims=True) + RMS_EPS) * gn
    return o * _silu(g)


PROJ_SHIFT_ROWS = 16
PROJ_TILE = 1024
BF16_SUBLANES = 16


def _proj_kernel(transposed, shift, riders, emit_x, x_ref, w_ref, *rest):
    rest = list(rest)
    wx_ref = rest.pop(0) if shift else None
    wb_scr = rest.pop()
    rider_in = [[rest.pop(0) for _ in range(n_in)] for _, n_in, _ in riders]
    o_ref = rest.pop(0)
    ox_ref = rest.pop(0) if emit_x else None
    rider_out = [[rest.pop(0) for _ in range(n_out)] for _, _, n_out in riders]
    item = pl.program_id(0) * pl.num_programs(1) + pl.program_id(1)

    @pl.when(pl.program_id(1) == 0)
    def _():
        w = w_ref[...]
        if shift:
            w = jnp.concatenate([w[shift:], wx_ref[:shift]], axis=0)
        w = w.astype(BF16)
        wb_scr[...] = w

    xb = x_ref[...].astype(BF16)
    if emit_x:
        ox_ref[...] = xb
    o_ref[...] = _dot(xb, wb_scr[...], NT if transposed else NN).astype(o_ref.dtype)
    for (body, _, _), r_in, r_out in zip(riders, rider_in, rider_out):
        body(item, *r_in, *r_out)


def _proj(x, w, name, col0=0, n=None, tn=PROJ_TILE, shift=0, transposed=False, riders=(), out_dtype=F32,
          emit_x=False):
    m, k = x.shape
    if n is None:
        n = w.shape[0] if transposed else w.shape[1]
    tm = min(m, PROJ_TILE)
    tn = min(n, tn)
    c0 = col0 // tn
    if transposed:
        w_spec = pl.BlockSpec((tn, k), lambda j, i: (c0 + j, 0))
    else:
        assert not shift
        w_spec = pl.BlockSpec((k, tn), lambda j, i: (0, c0 + j))
    in_specs = [pl.BlockSpec((tm, k), lambda j, i: (i, 0)), w_spec]
    args = [x, w]
    if shift:
        r = PROJ_SHIFT_ROWS
        assert shift <= r and shift % 8 == 0 and tn % r == 0 and col0 % r == 0
        in_specs.append(pl.BlockSpec((r, k), lambda j, i: (col0 // r + (j + 1) * (tn // r), 0)))
        args.append(w)
    out_specs = [pl.BlockSpec((tm, tn), lambda j, i: (i, j))]
    out_shape = [jax.ShapeDtypeStruct((m, n), out_dtype)]
    if emit_x:
        assert n == tn, "each x tile must be visited once"
        out_specs.append(pl.BlockSpec((tm, k), lambda j, i: (i, 0)))
        out_shape.append(jax.ShapeDtypeStruct((m, k), BF16))
    kernel_riders = []
    for body, specs_fn, operands in riders:
        n_items, r_in_specs, r_args, r_out_specs, r_out_shape = specs_fn(lambda j, i: j * (m // tm) + i, *operands)
        assert n_items <= (n // tn) * (m // tm), "not enough projection steps to carry the rider's work items"
        in_specs += r_in_specs
        args += list(r_args)
        out_specs += r_out_specs
        out_shape += r_out_shape
        kernel_riders.append((body, len(r_in_specs), len(r_out_specs)))
    outs = pl.pallas_call(
        functools.partial(_proj_kernel, transposed, shift, tuple(kernel_riders), emit_x),
        grid=(n // tn, m // tm),
        in_specs=in_specs,
        out_specs=out_specs,
        out_shape=out_shape,
        scratch_shapes=[pltpu.VMEM((tn, k) if transposed else (k, tn), BF16)],
        compiler_params=_params("arbitrary" if riders else "parallel", "arbitrary"),
        name=name,
    )(*args)
    return outs if len(outs) > 1 else outs[0]


def _proj_steps(m, n):
    return (n // min(n, PROJ_TILE)) * (m // min(m, PROJ_TILE))


def _cast_item(item, x_ref, o_ref):
    o_ref[...] = x_ref[...].astype(o_ref.dtype)


def _cast_specs(item, w, n_steps):
    r, c = w.shape
    rows = -(-r // n_steps)
    rows = -(-rows // BF16_SUBLANES) * BF16_SUBLANES
    n_items = -(-r // rows)
    at = lambda *g: (jnp.minimum(item(*g), n_items - 1), 0)
    return (n_items, [pl.BlockSpec((rows, c), at)], (w,), [pl.BlockSpec((rows, c), at)],
            [jax.ShapeDtypeStruct((r, c), BF16)])


def _cast_rider(w, n_steps):
    return (_cast_item, _cast_specs, (w, n_steps))


def _gla_levels(c):
    return c.bit_length() - 1


def _init_gla_masks(mask_scr, tri_scr):
    c = tri_scr.shape[0]
    rr = lax.broadcasted_iota(jnp.int32, (c, c), 0)
    cc = lax.broadcasted_iota(jnp.int32, (c, c), 1)
    tri_scr[...] = (rr >= cc).astype(BF16)
    mask_scr[0] = (rr == cc).astype(F32)
    for l in range(_gla_levels(c)):
        b = 1 << l
        pair = (rr // (2 * b) == cc // (2 * b)) & ((rr // b) % 2 == 1) & ((cc // b) % 2 == 0)
        mask_scr[1 + l] = pair.astype(F32)


def _intra_scores(q, k, la, bc, mask_scr):
    c, dk = q.shape
    hc = c // 2
    ri = lax.broadcasted_iota(jnp.int32, (c, 1), 0)
    diag = jnp.sum(q * k, axis=1, keepdims=True)
    eye = mask_scr[0, :hc, :hc]
    a0, a1 = diag[:hc] * eye, diag[hc:] * eye
    for l in range(_gla_levels(c)):
        b = 1 << l
        if b < SUBLANES:
            upper = (ri // b) % 2 == 1
            if b == 1:
                e = jnp.where(upper, la, 0.0)
            elif b == 2:
                m4 = ri % 4
                la_prev = pltpu.roll(la, 1, 0)
                la_next = pltpu.roll(la, c - 1, 0)
                e = jnp.where(m4 == 2, la, jnp.where(m4 == 3, la + la_prev, jnp.where(m4 == 0, la_next, 0.0)))
            else:
                ref = bc.reshape(c // (2 * b), 2 * b, dk)[:, b - 1:b, :]
                ref = jnp.broadcast_to(ref, (c // (2 * b), 2 * b, dk)).reshape(c, dk)
                dlt = bc - ref
                e = jnp.minimum(dlt, -dlt)
            w = jnp.where(upper, q, k) * jnp.exp2(e)
        else:
            parts = []
            for r0 in range(0, c, 2 * b):
                ref = bc[r0 + b - 1:r0 + b, :]
                parts.append(k[r0:r0 + b] * jnp.exp2(ref - bc[r0:r0 + b]))
                parts.append(q[r0 + b:r0 + 2 * b] * jnp.exp2(bc[r0 + b:r0 + 2 * b] - ref))
            w = jnp.concatenate(parts, axis=0)
        w = w.astype(BF16)
        w0, w1 = w[:hc], w[hc:]
        if b == hc:
            return a0, a1, _dot(w1, w0, NT)
        mask = mask_scr[1 + l, :hc, :hc]
        a0 = a0 + _dot(w0, w0, NT) * mask
        a1 = a1 + _dot(w1, w1, NT) * mask


def _gla_kernel(q_ref, k_ref, v_ref, g_ref, alr_ref, wa2_ref, ba2_ref, gn_ref, o_ref, sfin_ref,
                s_scr, mask_scr, tri_scr):
    ci = pl.program_id(2)

    @pl.when(ci == 0)
    def _():
        s_scr[...] = jnp.zeros_like(s_scr)
        _init_gla_masks(mask_scr, tri_scr)

    c = q_ref.shape[0]
    tri = tri_scr[...]
    ones = jnp.ones((c, LANES), BF16)
    alr = alr_ref[...]
    gn = gn_ref[...]
    for hh in range(s_scr.shape[0]):
        ks = slice(hh * GLA_HK, (hh + 1) * GLA_HK)
        vs = slice(hh * GLA_HV, (hh + 1) * GLA_HV)
        la = _log_decay(alr, wa2_ref[:, ks], ba2_ref[:, ks])
        lh, lm, ll = _split3(la)
        bc = _dot(tri, lh) + _dot(tri, lm) + _dot(tri, ll)
        b_last = bc[c - 1:c, :]
        bl_col = _dot(lh, ones, TN) + _dot(lm, ones, TN) + _dot(ll, ones, TN)
        dec_col = jnp.exp2(bl_col)
        dec_col = jnp.concatenate([dec_col] * (GLA_HV // LANES), axis=1)

        q = q_ref[:, ks].astype(F32) * (GLA_HK ** -0.5)
        k = k_ref[:, ks].astype(F32)
        vb = v_ref[:, vs].astype(BF16)
        s = s_scr[hh]
        o = _dot((q * jnp.exp2(bc)).astype(BF16), s.astype(BF16))
        a0, a1, a10 = _intra_scores(q, k, la, bc, mask_scr)
        hc = c // 2
        v0, v1 = vb[:hc], vb[hc:]
        o = o + jnp.concatenate([_dot(a0.astype(BF16), v0),
                                 _dot(a10.astype(BF16), v0) + _dot(a1.astype(BF16), v1)], axis=0)
        kd = (k * jnp.exp2(b_last - bc)).astype(BF16)
        s_new = dec_col * s + _dot(kd, vb, TN)
        s_scr[hh] = s_new
        o_ref[:, vs] = _rms_gate(o, gn, g_ref[:, vs].astype(F32)).astype(o_ref.dtype)

    @pl.when(ci == pl.num_programs(2) - 1)
    def _():
        sfin_ref[...] = s_scr[...]


GLA_HEADS_PER_STEP = 2


def _gla_prompt(proj_a, alr, wa2p, ba2, gn, nb, seq):
    c = min(GLA_CHUNK, seq)
    nc = seq // c
    hp = GLA_HEADS_PER_STEP
    ng = GLA_HEADS // hp
    wk, wv = hp * GLA_HK, hp * GLA_HV
    row = lambda b, hg, ci: b * nc + ci
    return pl.pallas_call(
        _gla_kernel,
        grid=(nb, ng, nc),
        in_specs=[
            pl.BlockSpec((c, wk), lambda b, hg, ci: (row(b, hg, ci), hg)),
            pl.BlockSpec((c, wk), lambda b, hg, ci: (row(b, hg, ci), ng + hg)),
            pl.BlockSpec((c, wv), lambda b, hg, ci: (row(b, hg, ci), ng + hg)),
            pl.BlockSpec((c, wv), lambda b, hg, ci: (row(b, hg, ci), 2 * ng + hg)),
            pl.BlockSpec((c, LANES), lambda b, hg, ci: (row(b, hg, ci), 0)),
            pl.BlockSpec((LANES, wk), lambda b, hg, ci: (0, hg)),
            pl.BlockSpec((1, wk), lambda b, hg, ci: (0, hg)),
            pl.BlockSpec((1, GLA_HV), lambda b, hg, ci: (0, 0)),
        ],
        out_specs=[
            pl.BlockSpec((c, wv), lambda b, hg, ci: (row(b, hg, ci), hg)),
            pl.BlockSpec((None, hp, GLA_HK, GLA_HV), lambda b, hg, ci: (b, hg, 0, 0)),
        ],
        out_shape=[
            jax.ShapeDtypeStruct((nb * seq, GLA_HEADS * GLA_HV), BF16),
            jax.ShapeDtypeStruct((nb, GLA_HEADS, GLA_HK, GLA_HV), F32),
        ],
        scratch_shapes=[pltpu.VMEM((hp, GLA_HK, GLA_HV), F32),
                        pltpu.VMEM((1 + _gla_levels(c), c, c), F32),
                        pltpu.VMEM((c, c), BF16)],
        compiler_params=_params("parallel", "parallel", "arbitrary"),
        name="gla_prompt",
    )(proj_a, proj_a, proj_a, proj_a, alr, wa2p, ba2, gn)


GLA_DEC_ROWS = 4


def _gla_dec_item(q_ref, k_ref, v_ref, g_ref, alr_ref, wa2_ref, ba2_ref, gn_ref, s_ref, o_ref, so_ref):
    nb = q_ref.shape[0]
    alr = alr_ref[...]
    wa2 = wa2_ref[...]
    z = ba2_ref[...]
    for r in range(GLA_RANK):
        z = z + alr[:, r:r + 1] * wa2[r:r + 1, :]
    la = _log2_decay(z)
    rep = LANES // nb
    to_cols = lambda x: jnp.concatenate([x] * rep, axis=0).T
    a_t = to_cols(jnp.exp2(la))
    k_t = to_cols(k_ref[...])
    q_t = to_cols(q_ref[...] * (GLA_HK ** -0.5))
    v = v_ref[...]
    rows = []
    for j in range(nb):
        s_new = a_t[:, j:j + 1] * s_ref[j] + k_t[:, j:j + 1] * v[j:j + 1, :]
        so_ref[j] = s_new
        rows.append(jnp.sum(q_t[:, j:j + 1] * s_new, axis=0, keepdims=True))
    o = jnp.concatenate(rows, axis=0)
    o_ref[...] = _rms_gate(o, gn_ref[...], g_ref[...]).astype(o_ref.dtype)


GLA_DEC_INPUTS = 9


def _gla_decode_specs(item, proj_a, alr, wa2p, ba2, gn, state):
    nseq = proj_a.shape[0]
    h = GLA_HEADS
    r = GLA_DEC_ROWS
    n_items = (nseq // r) * h
    proj3 = proj_a.reshape(nseq // r, r, proj_a.shape[1])
    alr3 = alr.reshape(nseq // r, r, LANES)

    def at(fn):
        def index_map(*grid_idx):
            it = jnp.minimum(item(*grid_idx), n_items - 1)
            return fn(it // h, it % h)
        return index_map

    in_specs = [
        pl.BlockSpec((None, r, GLA_HK), at(lambda b, hh: (b, 0, hh))),
        pl.BlockSpec((None, r, GLA_HK), at(lambda b, hh: (b, 0, h + hh))),
        pl.BlockSpec((None, r, GLA_HV), at(lambda b, hh: (b, 0, h + hh))),
        pl.BlockSpec((None, r, GLA_HV), at(lambda b, hh: (b, 0, 2 * h + hh))),
        pl.BlockSpec((None, r, LANES), at(lambda b, hh: (b, 0, 0))),
        pl.BlockSpec((LANES, GLA_HK), at(lambda b, hh: (0, hh))),
        pl.BlockSpec((1, GLA_HK), at(lambda b, hh: (0, hh))),
        pl.BlockSpec((1, GLA_HV), at(lambda b, hh: (0, 0))),
        pl.BlockSpec((r, None, GLA_HK, GLA_HV), at(lambda b, hh: (b, hh, 0, 0))),
    ]
    out_specs = [
        pl.BlockSpec((None, r, GLA_HV), at(lambda b, hh: (b, 0, hh))),
        pl.BlockSpec((r, None, GLA_HK, GLA_HV), at(lambda b, hh: (b, hh, 0, 0))),
    ]
    out_shape = [
        jax.ShapeDtypeStruct((nseq // r, r, h * GLA_HV), F32),
        jax.ShapeDtypeStruct(state.shape, F32),
    ]
    args = (proj3, proj3, proj3, proj3, alr3, wa2p, ba2, gn, state)
    assert len(in_specs) == GLA_DEC_INPUTS
    return n_items, in_specs, args, out_specs, out_shape


HALO_ROWS = 16


def _shift_rows(x, halo, ri):
    h1 = halo[HALO_ROWS - 1:HALO_ROWS, :]
    h2 = halo[HALO_ROWS - 2:HALO_ROWS - 1, :]
    p1 = jnp.where(ri == 0, h1, pltpu.roll(x, 1, 0))
    p2 = jnp.where(ri == 0, h2, jnp.where(ri == 1, h1, pltpu.roll(x, 2, 0)))
    return p1, p2


CONV_ITEM_ROWS = 512


def _conv_item(tiles_per_seq, n_items, item, cb_ref, cc_ref, ch_ref, cch_ref, chh_ref, w_ref, o_ref, st_ref):
    tl, cd = cb_ref.shape
    tile = jnp.minimum(item, n_items - 1)
    seq_start = tile % tiles_per_seq == 0
    ri = lax.broadcasted_iota(jnp.int32, (tl, 1), 0)
    for c0 in range(0, cd, LANES):
        cs = slice(c0, c0 + LANES)
        cch = cc_ref[:, cs].astype(F32) * ch_ref[:, cs].astype(F32)
        halo = cch_ref[:, cs].astype(F32) * chh_ref[:, cs].astype(F32)
        halo = jnp.where(seq_start, 0.0, halo)
        p1, p2 = _shift_rows(cch, halo, ri)
        w = w_ref[:, cs]
        u = w[0:1, :] * p2 + w[1:2, :] * p1 + w[2:3, :] * cch
        o_ref[:, cs] = (cb_ref[:, cs].astype(F32) * u).astype(o_ref.dtype)
        st_ref[:, cs] = cch[tl - 2:tl, :]


def _conv_prompt_specs(item, proj_b, w_conv, seq):
    m = proj_b.shape[0]
    tl = CONV_ITEM_ROWS
    assert seq % tl == 0
    n_items = m // tl
    tiles_per_seq = seq // tl
    th = tl // HALO_ROWS
    cd = CONV_DIM
    tile = lambda *g: jnp.minimum(item(*g), n_items - 1)
    in_specs = [
        pl.BlockSpec((tl, cd), lambda *g: (tile(*g), 0)),
        pl.BlockSpec((tl, cd), lambda *g: (tile(*g), 1)),
        pl.BlockSpec((tl, cd), lambda *g: (tile(*g), 2)),
        pl.BlockSpec((HALO_ROWS, cd), lambda *g: (jnp.maximum(tile(*g) * th - 1, 0), 1)),
        pl.BlockSpec((HALO_ROWS, cd), lambda *g: (jnp.maximum(tile(*g) * th - 1, 0), 2)),
        pl.BlockSpec((3, cd), lambda *g: (0, 0)),
    ]
    out_specs = [
        pl.BlockSpec((tl, cd), lambda *g: (tile(*g), 0)),
        pl.BlockSpec((None, 2, cd), lambda *g: (tile(*g) // tiles_per_seq, 0, 0)),
    ]
    out_shape = [
        jax.ShapeDtypeStruct((m, cd), BF16),
        jax.ShapeDtypeStruct((m // seq, 2, cd), F32),
    ]
    return n_items, in_specs, (proj_b, proj_b, proj_b, proj_b, proj_b, w_conv), out_specs, out_shape


def _conv_prompt(proj_b, w_conv, seq):
    n_items, in_specs, args, out_specs, out_shape = _conv_prompt_specs(lambda t: t, proj_b, w_conv, seq)
    tiles_per_seq = seq // CONV_ITEM_ROWS

    def conv_kernel(*refs):
        _conv_item(tiles_per_seq, n_items, pl.program_id(0), *refs)

    return pl.pallas_call(
        conv_kernel,
        grid=(n_items,),
        in_specs=in_specs,
        out_specs=out_specs,
        out_shape=out_shape,
        compiler_params=_params("arbitrary"),
        name="conv_prompt",
    )(*args)


def _conv_dec_kernel(cb_ref, cc_ref, ch_ref, s0_ref, s1_ref, w_ref, o_ref, st_ref):
    cd = cb_ref.shape[1]
    cch = cc_ref[...] * ch_ref[...]
    s1 = s1_ref[...]
    w = w_ref[...]
    u = w[0:1, :] * s0_ref[...] + w[1:2, :] * s1 + w[2:3, :] * cch
    o_ref[...] = (cb_ref[...] * u).astype(o_ref.dtype)
    st_ref[:, :cd] = s1
    st_ref[:, cd:] = cch


def _conv_decode(proj_b, w_conv, state2d):
    n = proj_b.shape[0]
    cd = CONV_DIM
    blk = lambda col: pl.BlockSpec((n, cd), lambda i: (0, col))
    return pl.pallas_call(
        _conv_dec_kernel,
        grid=(1,),
        in_specs=[blk(0), blk(1), blk(2), blk(0), blk(1), pl.BlockSpec((3, cd), lambda i: (0, 0))],
        out_specs=[pl.BlockSpec((n, cd), lambda i: (0, 0)), pl.BlockSpec((n, 2 * cd), lambda i: (0, 0))],
        out_shape=[jax.ShapeDtypeStruct((n, cd), BF16), jax.ShapeDtypeStruct((n, 2 * cd), F32)],
        compiler_params=_params("arbitrary"),
        name="conv_decode",
    )(proj_b, proj_b, proj_b, state2d, state2d, w_conv)


def _softmax_rows(logits):
    m = jnp.max(logits, axis=-1, keepdims=True)
    p = jnp.exp(logits - m)
    return p / jnp.sum(p, axis=-1, keepdims=True)


def _memattn_kernel(q_ref, k_ref, v_ref, o_ref):
    logits = _dot(q_ref[...].astype(BF16), k_ref[...].astype(BF16), NT) * (MEM_HD ** -0.5)
    p = _softmax_rows(logits)
    o_ref[...] = _dot(p.astype(BF16), v_ref[...].astype(BF16)).astype(o_ref.dtype)


def _memattn_prompt(proj_b, mk, mv, nb, seq):
    tl = min(2048, seq)
    nt = seq // tl
    h = MEM_HEADS
    qcol = 3 * CONV_DIM // MEM_HD
    return pl.pallas_call(
        _memattn_kernel,
        grid=(nb, h, nt),
        in_specs=[
            pl.BlockSpec((tl, MEM_HD), lambda b, hh, t: (b * nt + t, qcol + hh)),
            pl.BlockSpec((N_MEM, MEM_HD), lambda b, hh, t: (b, hh)),
            pl.BlockSpec((N_MEM, MEM_HD), lambda b, hh, t: (b, hh)),
        ],
        out_specs=pl.BlockSpec((tl, MEM_HD), lambda b, hh, t: (b * nt + t, hh)),
        out_shape=jax.ShapeDtypeStruct((nb * seq, MEM_DIM), BF16),
        compiler_params=_params("parallel", "parallel", "parallel"),
        name="memattn_prompt",
    )(proj_b, mk, mv)


MEMATTN_DEC_ROWS = 2


MEM_HD_CHUNKS = MEM_HD // LANES
MEM_TILE_ROWS = MEM_HD_CHUNKS * MEM_HEADS
assert MEM_HD_CHUNKS == 2 and MEM_TILE_ROWS == 8


def _heads_to_tile(x):
    lead = x.shape[:-2]
    x = x.reshape(lead + (MEM_HEADS, MEM_HD_CHUNKS, LANES))
    return jnp.swapaxes(x, -3, -2).reshape(lead + (MEM_TILE_ROWS, LANES))


def _tile_to_heads(x):
    lead = x.shape[:-2]
    x = x.reshape(lead + (MEM_HD_CHUNKS, MEM_HEADS, LANES))
    return jnp.swapaxes(x, -3, -2).reshape(lead + (MEM_HEADS, MEM_HD))


def _memattn_dec_item(item, q_ref, k_ref, v_ref, o_ref):
    def body(i, carry):
        q = q_ref[i] * (MEM_HD ** -0.5)
        prod = k_ref[i] * q[None]
        prod = prod + pltpu.roll(prod, MEM_HEADS, 1)
        logits = jnp.sum(prod, axis=-1, keepdims=True)
        e = jnp.exp(logits - jnp.max(logits, axis=0, keepdims=True))
        o_ref[i] = jnp.sum(e * v_ref[i], axis=0) / jnp.sum(e, axis=0)
        return carry

    lax.fori_loop(0, q_ref.shape[0], body, 0, unroll=True)


def _memattn_decode_specs(item, q_t, k_t, v_t):
    n = q_t.shape[0]
    r = MEMATTN_DEC_ROWS
    n_items = n // r
    tile = (MEM_TILE_ROWS, LANES)

    def at(ndim):
        return lambda *grid_idx: (jnp.minimum(item(*grid_idx), n_items - 1),) + (0,) * (ndim - 1)

    in_specs = [
        pl.BlockSpec((r,) + tile, at(3)),
        pl.BlockSpec((r, N_MEM) + tile, at(4)),
        pl.BlockSpec((r, N_MEM) + tile, at(4)),
    ]
    out_specs = [pl.BlockSpec((r,) + tile, at(3))]
    out_shape = [jax.ShapeDtypeStruct((n,) + tile, F32)]
    return n_items, in_specs, (q_t, k_t, v_t), out_specs, out_shape


def _merge_kernel(a_ref, b_ref, m_ref, wa_ref, wb_ref, wm_ref, za_ref, zb_ref, zm_ref, o_ref):
    ya = _dot(a_ref[...].astype(BF16), wa_ref[...])
    yb = _dot(b_ref[...].astype(BF16), wb_ref[...])
    ym = _dot(m_ref[...].astype(BF16), wm_ref[...])
    gate = lambda z_ref: _sigmoid(z_ref[...].astype(F32))
    merged = gate(za_ref) * ya + gate(zb_ref) * yb + gate(zm_ref) * ym
    o_ref[...] = merged.astype(o_ref.dtype)


def _merge(o_gla, cbu, om, w_gla_out, w_conv_out, w_mem_out, proj_b):
    m = o_gla.shape[0]
    tm = min(1024, m)
    tn = 512
    d = D_MODEL
    zoff = (3 * CONV_DIM + MEM_DIM) // tn
    zspec = lambda g: pl.BlockSpec((tm, tn), lambda i, j: (i, zoff + g * (d // tn) + j))
    return pl.pallas_call(
        _merge_kernel,
        grid=(m // tm, d // tn),
        in_specs=[
            pl.BlockSpec((tm, o_gla.shape[1]), lambda i, j: (i, 0)),
            pl.BlockSpec((tm, cbu.shape[1]), lambda i, j: (i, 0)),
            pl.BlockSpec((tm, om.shape[1]), lambda i, j: (i, 0)),
            pl.BlockSpec((w_gla_out.shape[0], tn), lambda i, j: (0, j)),
            pl.BlockSpec((w_conv_out.shape[0], tn), lambda i, j: (0, j)),
            pl.BlockSpec((w_mem_out.shape[0], tn), lambda i, j: (0, j)),
            zspec(0), zspec(1), zspec(2),
        ],
        out_specs=pl.BlockSpec((tm, tn), lambda i, j: (i, j)),
        out_shape=jax.ShapeDtypeStruct((m, d), BF16),
        compiler_params=_params("parallel", "parallel"),
        name="merge",
    )(o_gla, cbu, om, w_gla_out, w_conv_out, w_mem_out, proj_b, proj_b, proj_b)


def _wo_ln_kernel(x_ref, m_ref, w_ref, g_ref, b_ref, o_ref):
    tm = x_ref.shape[0]
    half = tm // 4 if tm % 512 == 0 else tm
    for r in range(0, tm, half):
        rows = pl.ds(r, half)
        y = DN_ALPHA * x_ref[rows, :] + _dot(m_ref[rows, :], w_ref[...])
        o_ref[rows, :] = _layer_norm(y, g_ref[...], b_ref[...])


def _wo_ln(x, merged, w_o, g, b):
    m, d = x.shape
    tm = min(512, m)
    return pl.pallas_call(
        _wo_ln_kernel,
        grid=(m // tm,),
        in_specs=[
            pl.BlockSpec((tm, d), lambda i: (i, 0)),
            pl.BlockSpec((tm, d), lambda i: (i, 0)),
            pl.BlockSpec((d, d), lambda i: (0, 0)),
            pl.BlockSpec((1, d), lambda i: (0, 0)),
            pl.BlockSpec((1, d), lambda i: (0, 0)),
        ],
        out_specs=pl.BlockSpec((tm, d), lambda i: (i, 0)),
        out_shape=jax.ShapeDtypeStruct((m, d), F32),
        compiler_params=_params("parallel"),
        name="wo_ln",
    )(x, merged, w_o, g, b)


def _ffn_act(hc, hu, valid, c0=0):
    h = hc * _sigmoid(hc) * hu
    return jnp.where(c0 + lax.broadcasted_iota(jnp.int32, (1, hc.shape[1]), 1) < valid, h, 0.0).astype(BF16)


def _ffn_down(h, valid, wd_ref, acc_ref, x_ref, g_ref, b_ref, o_ref, fi):
    wd = jnp.where(lax.broadcasted_iota(jnp.int32, (h.shape[1], 1), 0) < valid, wd_ref[...], 0.0)
    acc_ref[...] += _dot(h, wd)

    @pl.when(fi == pl.num_programs(1) - 1)
    def _():
        o_ref[...] = _layer_norm(DN_ALPHA * x_ref[...] + acc_ref[...], g_ref[...], b_ref[...])


def _ffn_kernel(seq, x_ref, xh_ref, wg_ref, wu_ref, wd_ref, wc_ref, bc_ref, g_ref, b_ref, *rest):
    dec_in, (o_ref, st_ref, od_ref, so_ref, xb_scr, acc_scr) = rest[:GLA_DEC_INPUTS], rest[GLA_DEC_INPUTS:]
    fi = pl.program_id(1)
    tm = x_ref.shape[0]

    @pl.when(fi == 0)
    def _():
        xb_scr[...] = x_ref[...].astype(BF16)
        acc_scr[...] = jnp.zeros_like(acc_scr)

    _gla_dec_item(*dec_in, od_ref, so_ref)
    xb = xb_scr[...]
    xh = xh_ref[...].astype(BF16)
    tf = wg_ref.shape[1]
    valid = D_FF - fi * tf
    seq_start = (pl.program_id(0) * tm) % seq == 0
    ri = lax.broadcasted_iota(jnp.int32, (tm, 1), 0)
    hs = []
    for c0 in range(0, tf, FFN_GATE_CHUNK):
        cs = slice(c0, c0 + FFN_GATE_CHUNK)
        wg = wg_ref[:, cs]
        hg = _dot(xb, wg)
        st_ref[:, cs] = hg[tm - 2:tm, :]
        halo = jnp.where(seq_start, 0.0, _dot(xh, wg))
        p1, p2 = _shift_rows(hg, halo, ri)
        wc = wc_ref[:, cs]
        hc = wc[0:1, :] * p2 + wc[1:2, :] * p1 + wc[2:3, :] * hg + bc_ref[:, cs]
        hs.append(_ffn_act(hc, _dot(xb, wu_ref[:, cs]), valid, c0))
    _ffn_down(jnp.concatenate(hs, axis=1), valid, wd_ref, acc_scr, x_ref, g_ref, b_ref, o_ref, fi)


def _ffn_prompt(x1, w_gate, w_up, w_down, w_conv, b_conv, g, b, seq, gla_dec_operands):
    m, d = x1.shape
    tm = min(512, seq)
    tf = FF_TILE
    t8 = tm // HALO_ROWS
    n_items, dec_in_specs, dec_args, dec_out_specs, dec_out_shape = _gla_decode_specs(
        lambda i, f: i * FF_TILES + f, *gla_dec_operands)
    assert n_items <= (m // tm) * FF_TILES, "not enough FFN steps to carry the decode GLA work items"
    y, tile_state, o_dec, s_dec = pl.pallas_call(
        functools.partial(_ffn_kernel, seq),
        grid=(m // tm, FF_TILES),
        in_specs=[
            pl.BlockSpec((tm, d), lambda i, f: (i, 0)),
            pl.BlockSpec((HALO_ROWS, d), lambda i, f: (jnp.maximum(i * t8 - 1, 0), 0)),
            pl.BlockSpec((d, tf), lambda i, f: (0, f)),
            pl.BlockSpec((d, tf), lambda i, f: (0, f)),
            pl.BlockSpec((tf, d), lambda i, f: (f, 0)),
            pl.BlockSpec((3, tf), lambda i, f: (0, f)),
            pl.BlockSpec((1, tf), lambda i, f: (0, f)),
            pl.BlockSpec((1, d), lambda i, f: (0, 0)),
            pl.BlockSpec((1, d), lambda i, f: (0, 0)),
        ] + dec_in_specs,
        out_specs=[pl.BlockSpec((tm, d), lambda i, f: (i, 0)),
                   pl.BlockSpec((None, 2, tf), lambda i, f: (i, 0, f))] + dec_out_specs,
        out_shape=[jax.ShapeDtypeStruct((m, d), F32),
                   jax.ShapeDtypeStruct((m // tm, 2, D_FF), F32)] + dec_out_shape,
        scratch_shapes=[pltpu.VMEM((tm, d), BF16), pltpu.VMEM((tm, d), F32)],
        compiler_params=_params("arbitrary", "arbitrary"),
        name="ffn_prompt_gla_decode",
    )(x1, x1, w_gate, w_up, w_down, w_conv, b_conv, g, b, *dec_args)
    tiles_per_seq = seq // tm
    return y, tile_state[tiles_per_seq - 1::tiles_per_seq], o_dec, s_dec


def _ffn_dec_kernel(x_ref, s0_ref, s1_ref, wg_ref, wu_ref, wd_ref, wc_ref, bc_ref, g_ref, b_ref,
                    o_ref, hg_ref, xb_scr, acc_scr):
    fi = pl.program_id(1)

    @pl.when(fi == 0)
    def _():
        xb_scr[...] = x_ref[...].astype(BF16)
        acc_scr[...] = jnp.zeros_like(acc_scr)

    xb = xb_scr[...]
    hg = _dot(xb, wg_ref[...])
    hg_ref[...] = hg
    wc = wc_ref[...]
    hc = wc[0:1, :] * s0_ref[...] + wc[1:2, :] * s1_ref[...] + wc[2:3, :] * hg + bc_ref[...]
    valid = D_FF - fi * hc.shape[1]
    _ffn_down(_ffn_act(hc, _dot(xb, wu_ref[...]), valid), valid, wd_ref, acc_scr, x_ref, g_ref, b_ref, o_ref, fi)


def _ffn_decode(x1, s0, s1, w_gate, w_up, w_down, w_conv, b_conv, g, b):
    m, d = x1.shape
    tf = FF_TILE
    return pl.pallas_call(
        _ffn_dec_kernel,
        grid=(1, FF_TILES),
        in_specs=[
            pl.BlockSpec((m, d), lambda i, f: (0, 0)),
            pl.BlockSpec((m, tf), lambda i, f: (0, f)),
            pl.BlockSpec((m, tf), lambda i, f: (0, f)),
            pl.BlockSpec((d, tf), lambda i, f: (0, f)),
            pl.BlockSpec((d, tf), lambda i, f: (0, f)),
            pl.BlockSpec((tf, d), lambda i, f: (f, 0)),
            pl.BlockSpec((3, tf), lambda i, f: (0, f)),
            pl.BlockSpec((1, tf), lambda i, f: (0, f)),
            pl.BlockSpec((1, d), lambda i, f: (0, 0)),
            pl.BlockSpec((1, d), lambda i, f: (0, 0)),
        ],
        out_specs=[pl.BlockSpec((m, d), lambda i, f: (0, 0)), pl.BlockSpec((m, tf), lambda i, f: (0, f))],
        out_shape=[jax.ShapeDtypeStruct((m, d), F32), jax.ShapeDtypeStruct((m, D_FF), F32)],
        scratch_shapes=[pltpu.VMEM((m, d), BF16), pltpu.VMEM((m, d), F32)],
        compiler_params=_params("parallel", "arbitrary"),
        name="ffn_decode",
    )(x1, s0, s1, w_gate, w_up, w_down, w_conv, b_conv, g, b)


def kernel(x_prompt, x_sample, mem_prompt, cache_mem_k, cache_mem_v, state_gla, state_conv, state_ffn_conv, w_in, w_gla_a2, b_gla_a2, g_gla_norm, w_gla_out, w_conv, w_conv_out, w_mem_k, w_mem_v, w_mem_out, w_o, ln1_g, ln1_b, w_ffn_gate, w_ffn_up, w_ffn_conv, b_ffn_conv, w_ffn_down, ln2_g, ln2_b):
    nb, seq, d = x_prompt.shape
    ns = x_sample.shape[0]

    w_in_t = jnp.swapaxes(w_in[0], 0, 1)
    wa2p = jnp.pad(w_gla_a2[0], ((0, LANES - GLA_RANK), (0, 0)))
    ba2 = b_gla_a2
    gn = g_gla_norm
    w_fc = w_ffn_conv[0]
    b_fc = b_ffn_conv
    w_cv = w_conv[0]

    def mix_in(x2d, tag, dtype, riders_b=(), riders_a=()):
        pr, xb = _proj(x2d, w_in_t, "proj_r_" + tag, COLS_A, LANES, LANES, transposed=True, emit_x=True)
        pb = _proj(xb, w_in_t, "proj_b_" + tag, COLS_A, COLS_B, shift=GLA_RANK, transposed=True,
                   riders=riders_b, out_dtype=dtype)
        pa = _proj(xb, w_in_t, "proj_a_" + tag, 0, COLS_A, transposed=True, riders=riders_a, out_dtype=dtype)
        return pa, pr, pb

    xs = x_sample.reshape(ns, d)
    sa, sr, sb = mix_in(xs, "s", F32)
    mq_s = sb[:, 3 * CONV_DIM:3 * CONV_DIM + MEM_DIM].reshape(ns, MEM_HEADS, MEM_HD)
    memattn_dec = (_memattn_dec_item, _memattn_decode_specs,
                   (_heads_to_tile(mq_s), _heads_to_tile(cache_mem_k[0]), _heads_to_tile(cache_mem_v[0])))

    xp = x_prompt.reshape(nb * seq, d)
    memb = mem_prompt.reshape(nb * N_MEM, d).astype(BF16)
    mk = _proj(memb, w_mem_k[0], "mem_k", tn=MEM_HD)
    mv = _proj(memb, w_mem_v[0], "mem_v", tn=MEM_HD)
    steps_a, steps_b = _proj_steps(nb * seq, COLS_A), _proj_steps(nb * seq, COLS_B)
    riders_b = (memattn_dec, _cast_rider(w_ffn_gate[0], steps_b), _cast_rider(w_ffn_up[0], steps_b))
    riders_a = tuple(_cast_rider(w[0], steps_a) for w in (w_ffn_down, w_gla_out, w_conv_out, w_mem_out, w_o))
    (pa, w_fd, w_go, w_co, w_mo, w_oo), pr, (pb, om_s, w_fg, w_fu) = mix_in(xp, "p", BF16, riders_b, riders_a)

    def mix_out(x2d, o_gla, cbu, om, proj_b):
        merged = _merge(o_gla, cbu, om, w_go, w_co, w_mo, proj_b)
        return _wo_ln(x2d, merged, w_oo, ln1_g, ln1_b)

    o_gla, p_gla = _gla_prompt(pa, pr, wa2p, ba2, gn, nb, seq)
    cbu, p_conv = _conv_prompt(pb, w_cv, seq)
    om = _memattn_prompt(pb, mk, mv, nb, seq)
    x1 = mix_out(xp, o_gla, cbu, om, pb)
    yp, p_ffn, o_gla_s, s_gla = _ffn_prompt(x1, w_fg, w_fu, w_fd, w_fc, b_fc, ln2_g, ln2_b, seq,
                                            (sa, sr, wa2p, ba2, gn, state_gla[0]))
    o_gla_s = o_gla_s.reshape(ns, GLA_HEADS * GLA_HV)

    cbu_s, s_conv = _conv_decode(sb, w_cv, state_conv[0].reshape(ns, 2 * CONV_DIM))
    om_s = _tile_to_heads(om_s).reshape(ns, MEM_DIM)
    x1s = mix_out(xs, o_gla_s, cbu_s, om_s, sb)
    f0 = state_ffn_conv[0][:, 0, :]
    f1 = state_ffn_conv[0][:, 1, :]
    ys, hg_s = _ffn_decode(x1s, f0, f1, w_fg, w_fu, w_fd, w_fc, b_fc, ln2_g, ln2_b)
    s_ffn = jnp.stack([f1, hg_s], axis=1)

    return (yp.reshape(nb, seq, d), ys.reshape(ns, 1, d),
            mk.reshape(1, nb, N_MEM, MEM_HEADS, MEM_HD), mv.reshape(1, nb, N_MEM, MEM_HEADS, MEM_HD),
            p_gla[None], p_conv[None], p_ffn.reshape(1, nb, 2, D_FF),
            s_gla[None], s_conv.reshape(1, ns, 2, CONV_DIM), s_ffn[None])
```

```python
import functools

import jax
import jax.numpy as jnp
from jax import lax
from jax.experimental import pallas as pl
from jax.experimental.pallas import tpu as pltpu

F32 = jnp.float32
BF16 = jnp.bfloat16

D_MODEL = 2048
GLA_HEADS = 4
GLA_HK = 256
GLA_HV = 512
GLA_RANK = 16
GLA_GATE_NORM = 16.0
CONV_DIM = 1024
N_MEM = 256
MEM_HEADS = 4
MEM_HD = 256
MEM_DIM = 1024
D_FF = 5504
DEPTH = 1
DN_ALPHA = (2 * DEPTH) ** 0.25
LN_EPS = 1e-5
RMS_EPS = 1e-6

LANES = 128
SUBLANES = 8
LOG2_E = 1.4426950408889634
FF_TILE = 512
FF_TILES = -(-D_FF // FF_TILE)
FFN_GATE_CHUNK = FF_TILE
COLS_A = 2 * GLA_HEADS * GLA_HK + 2 * GLA_HEADS * GLA_HV
COLS_B = 3 * CONV_DIM + MEM_DIM + 3 * D_MODEL
GLA_CHUNK = 256
VMEM_LIMIT = 56 * 1024 * 1024

NN = (((1,), (0,)), ((), ()))
NT = (((1,), (1,)), ((), ()))
TN = (((0,), (0,)), ((), ()))


def _dot(a, b, dims=NN):
    return lax.dot_general(a, b, dims, preferred_element_type=F32)


def _params(*sem):
    return pltpu.CompilerParams(dimension_semantics=sem, vmem_limit_bytes=VMEM_LIMIT)


def _split2(x):
    hi = x.astype(BF16)
    lo = (x - hi.astype(F32)).astype(BF16)
    return hi, lo


def _split3(x):
    hi = x.astype(BF16)
    r = x - hi.astype(F32)
    mid = r.astype(BF16)
    lo = (r - mid.astype(F32)).astype(BF16)
    return hi, mid, lo


def _sigmoid(z):
    return 0.5 * jnp.tanh(0.5 * z) + 0.5


def _silu(z):
    h = 0.5 * z
    return h * jnp.tanh(h) + h


def _layer_norm(y, g, b):
    mu = jnp.mean(y, axis=-1, keepdims=True)
    d = y - mu
    var = jnp.mean(d * d, axis=-1, keepdims=True)
    return d * lax.rsqrt(var + LN_EPS) * g + b


def _log2_decay(z):
    return (jnp.minimum(z, 0.0) - jnp.log(1.0 + jnp.exp(-jnp.abs(z)))) * (LOG2_E / GLA_GATE_NORM)


def _log_decay(alr, wa2, ba2):
    ah, al = _split2(alr)
    wh, wl = _split2(wa2)
    return _log2_decay(_dot(ah, wh) + _dot(ah, wl) + _dot(al, wh) + ba2)


def _rms_gate(o, gn, g):
    o = o * lax.rsqrt(jnp.mean(o * o, axis=-1, keepdims=True) + RMS_EPS) * gn
    return o * _silu(g)


PROJ_SHIFT_ROWS = 16
PROJ_TILE = 1024
BF16_SUBLANES = 16


def _proj_kernel(transposed, shift, riders, emit_x, x_ref, w_ref, *rest):
    rest = list(rest)
    wx_ref = rest.pop(0) if shift else None
    wb_scr = rest.pop()
    rider_in = [[rest.pop(0) for _ in range(n_in)] for _, n_in, _ in riders]
    o_ref = rest.pop(0)
    ox_ref = rest.pop(0) if emit_x else None
    rider_out = [[rest.pop(0) for _ in range(n_out)] for _, _, n_out in riders]
    item = pl.program_id(0) * pl.num_programs(1) + pl.program_id(1)

    @pl.when(pl.program_id(1) == 0)
    def _():
        w = w_ref[...]
        if shift:
            w = jnp.concatenate([w[shift:], wx_ref[:shift]], axis=0)
        w = w.astype(BF16)
        wb_scr[...] = w

    xb = x_ref[...].astype(BF16)
    if emit_x:
        ox_ref[...] = xb
    o_ref[...] = _dot(xb, wb_scr[...], NT if transposed else NN).astype(o_ref.dtype)
    for (body, _, _), r_in, r_out in zip(riders, rider_in, rider_out):
        body(item, *r_in, *r_out)


def _proj(x, w, name, col0=0, n=None, tn=PROJ_TILE, shift=0, transposed=False, riders=(), out_dtype=F32,
          emit_x=False):
    m, k = x.shape
    if n is None:
        n = w.shape[0] if transposed else w.shape[1]
    tm = min(m, PROJ_TILE)
    tn = min(n, tn)
    c0 = col0 // tn
    if transposed:
        w_spec = pl.BlockSpec((tn, k), lambda j, i: (c0 + j, 0))
    else:
        assert not shift
        w_spec = pl.BlockSpec((k, tn), lambda j, i: (0, c0 + j))
    in_specs = [pl.BlockSpec((tm, k), lambda j, i: (i, 0)), w_spec]
    args = [x, w]
    if shift:
        r = PROJ_SHIFT_ROWS
        assert shift <= r and shift % 8 == 0 and tn % r == 0 and col0 % r == 0
        in_specs.append(pl.BlockSpec((r, k), lambda j, i: (col0 // r + (j + 1) * (tn // r), 0)))
        args.append(w)
    out_specs = [pl.BlockSpec((tm, tn), lambda j, i: (i, j))]
    out_shape = [jax.ShapeDtypeStruct((m, n), out_dtype)]
    if emit_x:
        assert n == tn, "each x tile must be visited once"
        out_specs.append(pl.BlockSpec((tm, k), lambda j, i: (i, 0)))
        out_shape.append(jax.ShapeDtypeStruct((m, k), BF16))
    kernel_riders = []
    for body, specs_fn, operands in riders:
        n_items, r_in_specs, r_args, r_out_specs, r_out_shape = specs_fn(lambda j, i: j * (m // tm) + i, *operands)
        assert n_items <= (n // tn) * (m // tm), "not enough projection steps to carry the rider's work items"
        in_specs += r_in_specs
        args += list(r_args)
        out_specs += r_out_specs
        out_shape += r_out_shape
        kernel_riders.append((body, len(r_in_specs), len(r_out_specs)))
    outs = pl.pallas_call(
        functools.partial(_proj_kernel, transposed, shift, tuple(kernel_riders), emit_x),
        grid=(n // tn, m // tm),
        in_specs=in_specs,
        out_specs=out_specs,
        out_shape=out_shape,
        scratch_shapes=[pltpu.VMEM((tn, k) if transposed else (k, tn), BF16)],
        compiler_params=_params("arbitrary" if riders else "parallel", "arbitrary"),
        name=name,
    )(*args)
    return outs if len(outs) > 1 else outs[0]


def _proj_steps(m, n):
    return (n // min(n, PROJ_TILE)) * (m // min(m, PROJ_TILE))


def _cast_item(item, x_ref, o_ref):
    o_ref[...] = x_ref[...].astype(o_ref.dtype)


def _cast_specs(item, w, n_steps):
    r, c = w.shape
    rows = -(-r // n_steps)
    rows = -(-rows // BF16_SUBLANES) * BF16_SUBLANES
    n_items = -(-r // rows)
    at = lambda *g: (jnp.minimum(item(*g), n_items - 1), 0)
    return (n_items, [pl.BlockSpec((rows, c), at)], (w,), [pl.BlockSpec((rows, c), at)],
            [jax.ShapeDtypeStruct((r, c), BF16)])


def _cast_rider(w, n_steps):
    return (_cast_item, _cast_specs, (w, n_steps))


def _gla_levels(c):
    return c.bit_length() - 1


def _init_gla_masks(mask_scr, tri_scr):
    c = tri_scr.shape[0]
    rr = lax.broadcasted_iota(jnp.int32, (c, c), 0)
    cc = lax.broadcasted_iota(jnp.int32, (c, c), 1)
    tri_scr[...] = (rr >= cc).astype(BF16)
    mask_scr[0] = (rr == cc).astype(F32)
    for l in range(_gla_levels(c)):
        b = 1 << l
        pair = (rr // (2 * b) == cc // (2 * b)) & ((rr // b) % 2 == 1) & ((cc // b) % 2 == 0)
        mask_scr[1 + l] = pair.astype(F32)


def _intra_scores(q, k, la, bc, mask_scr):
    c, dk = q.shape
    hc = c // 2
    ri = lax.broadcasted_iota(jnp.int32, (c, 1), 0)
    diag = jnp.sum(q * k, axis=1, keepdims=True)
    eye = mask_scr[0, :hc, :hc]
    a0, a1 = diag[:hc] * eye, diag[hc:] * eye
    for l in range(_gla_levels(c)):
        b = 1 << l
        if b < SUBLANES:
            upper = (ri // b) % 2 == 1
            if b == 1:
                e = jnp.where(upper, la, 0.0)
            elif b == 2:
                m4 = ri % 4
                la_prev = pltpu.roll(la, 1, 0)
                la_next = pltpu.roll(la, c - 1, 0)
                e = jnp.where(m4 == 2, la, jnp.where(m4 == 3, la + la_prev, jnp.where(m4 == 0, la_next, 0.0)))
            else:
                ref = bc.reshape(c // (2 * b), 2 * b, dk)[:, b - 1:b, :]
                ref = jnp.broadcast_to(ref, (c // (2 * b), 2 * b, dk)).reshape(c, dk)
                dlt = bc - ref
                e = jnp.minimum(dlt, -dlt)
            w = jnp.where(upper, q, k) * jnp.exp2(e)
        else:
            parts = []
            for r0 in range(0, c, 2 * b):
                ref = bc[r0 + b - 1:r0 + b, :]
                parts.append(k[r0:r0 + b] * jnp.exp2(ref - bc[r0:r0 + b]))
                parts.append(q[r0 + b:r0 + 2 * b] * jnp.exp2(bc[r0 + b:r0 + 2 * b] - ref))
            w = jnp.concatenate(parts, axis=0)
        w = w.astype(BF16)
        w0, w1 = w[:hc], w[hc:]
        if b == hc:
            return a0, a1, _dot(w1, w0, NT)
        mask = mask_scr[1 + l, :hc, :hc]
        a0 = a0 + _dot(w0, w0, NT) * mask
        a1 = a1 + _dot(w1, w1, NT) * mask


def _gla_kernel(q_ref, k_ref, v_ref, g_ref, alr_ref, wa2_ref, ba2_ref, gn_ref, o_ref, sfin_ref,
                s_scr, mask_scr, tri_scr):
    ci = pl.program_id(2)

    @pl.when(ci == 0)
    def _():
        s_scr[...] = jnp.zeros_like(s_scr)
        _init_gla_masks(mask_scr, tri_scr)

    c = q_ref.shape[0]
    tri = tri_scr[...]
    ones = jnp.ones((c, LANES), BF16)
    alr = alr_ref[...]
    gn = gn_ref[...]
    for hh in range(s_scr.shape[0]):
        ks = slice(hh * GLA_HK, (hh + 1) * GLA_HK)
        vs = slice(hh * GLA_HV, (hh + 1) * GLA_HV)
        la = _log_decay(alr, wa2_ref[:, ks], ba2_ref[:, ks])
        lh, lm, ll = _split3(la)
        bc = _dot(tri, lh) + _dot(tri, lm) + _dot(tri, ll)
        b_last = bc[c - 1:c, :]
        bl_col = _dot(lh, ones, TN) + _dot(lm, ones, TN) + _dot(ll, ones, TN)
        dec_col = jnp.exp2(bl_col)
        dec_col = jnp.concatenate([dec_col] * (GLA_HV // LANES), axis=1)

        q = q_ref[:, ks].astype(F32) * (GLA_HK ** -0.5)
        k = k_ref[:, ks].astype(F32)
        vb = v_ref[:, vs].astype(BF16)
        s = s_scr[hh]
        o = _dot((q * jnp.exp2(bc)).astype(BF16), s.astype(BF16))
        a0, a1, a10 = _intra_scores(q, k, la, bc, mask_scr)
        hc = c // 2
        v0, v1 = vb[:hc], vb[hc:]
        o = o + jnp.concatenate([_dot(a0.astype(BF16), v0),
                                 _dot(a10.astype(BF16), v0) + _dot(a1.astype(BF16), v1)], axis=0)
        kd = (k * jnp.exp2(b_last - bc)).astype(BF16)
        s_new = dec_col * s + _dot(kd, vb, TN)
        s_scr[hh] = s_new
        o_ref[:, vs] = _rms_gate(o, gn, g_ref[:, vs].astype(F32)).astype(o_ref.dtype)

    @pl.when(ci == pl.num_programs(2) - 1)
    def _():
        sfin_ref[...] = s_scr[...]


GLA_HEADS_PER_STEP = 2


def _gla_prompt(proj_a, alr, wa2p, ba2, gn, nb, seq):
    c = min(GLA_CHUNK, seq)
    nc = seq // c
    hp = GLA_HEADS_PER_STEP
    ng = GLA_HEADS // hp
    wk, wv = hp * GLA_HK, hp * GLA_HV
    row = lambda b, hg, ci: b * nc + ci
    return pl.pallas_call(
        _gla_kernel,
        grid=(nb, ng, nc),
        in_specs=[
            pl.BlockSpec((c, wk), lambda b, hg, ci: (row(b, hg, ci), hg)),
            pl.BlockSpec((c, wk), lambda b, hg, ci: (row(b, hg, ci), ng + hg)),
            pl.BlockSpec((c, wv), lambda b, hg, ci: (row(b, hg, ci), ng + hg)),
            pl.BlockSpec((c, wv), lambda b, hg, ci: (row(b, hg, ci), 2 * ng + hg)),
            pl.BlockSpec((c, LANES), lambda b, hg, ci: (row(b, hg, ci), 0)),
            pl.BlockSpec((LANES, wk), lambda b, hg, ci: (0, hg)),
            pl.BlockSpec((1, wk), lambda b, hg, ci: (0, hg)),
            pl.BlockSpec((1, GLA_HV), lambda b, hg, ci: (0, 0)),
        ],
        out_specs=[
            pl.BlockSpec((c, wv), lambda b, hg, ci: (row(b, hg, ci), hg)),
            pl.BlockSpec((None, hp, GLA_HK, GLA_HV), lambda b, hg, ci: (b, hg, 0, 0)),
        ],
        out_shape=[
            jax.ShapeDtypeStruct((nb * seq, GLA_HEADS * GLA_HV), BF16),
            jax.ShapeDtypeStruct((nb, GLA_HEADS, GLA_HK, GLA_HV), F32),
        ],
        scratch_shapes=[pltpu.VMEM((hp, GLA_HK, GLA_HV), F32),
                        pltpu.VMEM((1 + _gla_levels(c), c, c), F32),
                        pltpu.VMEM((c, c), BF16)],
        compiler_params=_params("parallel", "parallel", "arbitrary"),
        name="gla_prompt",
    )(proj_a, proj_a, proj_a, proj_a, alr, wa2p, ba2, gn)


GLA_DEC_ROWS = 4


def _gla_dec_item(q_ref, k_ref, v_ref, g_ref, alr_ref, wa2_ref, ba2_ref, gn_ref, s_ref, o_ref, so_ref):
    nb = q_ref.shape[0]
    alr = alr_ref[...]
    wa2 = wa2_ref[...]
    z = ba2_ref[...]
    for r in range(GLA_RANK):
        z = z + alr[:, r:r + 1] * wa2[r:r + 1, :]
    la = _log2_decay(z)
    rep = LANES // nb
    to_cols = lambda x: jnp.concatenate([x] * rep, axis=0).T
    a_t = to_cols(jnp.exp2(la))
    k_t = to_cols(k_ref[...])
    q_t = to_cols(q_ref[...] * (GLA_HK ** -0.5))
    v = v_ref[...]
    rows = []
    for j in range(nb):
        s_new = a_t[:, j:j + 1] * s_ref[j] + k_t[:, j:j + 1] * v[j:j + 1, :]
        so_ref[j] = s_new
        rows.append(jnp.sum(q_t[:, j:j + 1] * s_new, axis=0, keepdims=True))
    o = jnp.concatenate(rows, axis=0)
    o_ref[...] = _rms_gate(o, gn_ref[...], g_ref[...]).astype(o_ref.dtype)


GLA_DEC_INPUTS = 9


def _gla_decode_specs(item, proj_a, alr, wa2p, ba2, gn, state):
    nseq = proj_a.shape[0]
    h = GLA_HEADS
    r = GLA_DEC_ROWS
    n_items = (nseq // r) * h
    proj3 = proj_a.reshape(nseq // r, r, proj_a.shape[1])
    alr3 = alr.reshape(nseq // r, r, LANES)

    def at(fn):
        def index_map(*grid_idx):
            it = jnp.minimum(item(*grid_idx), n_items - 1)
            return fn(it // h, it % h)
        return index_map

    in_specs = [
        pl.BlockSpec((None, r, GLA_HK), at(lambda b, hh: (b, 0, hh))),
        pl.BlockSpec((None, r, GLA_HK), at(lambda b, hh: (b, 0, h + hh))),
        pl.BlockSpec((None, r, GLA_HV), at(lambda b, hh: (b, 0, h + hh))),
        pl.BlockSpec((None, r, GLA_HV), at(lambda b, hh: (b, 0, 2 * h + hh))),
        pl.BlockSpec((None, r, LANES), at(lambda b, hh: (b, 0, 0))),
        pl.BlockSpec((LANES, GLA_HK), at(lambda b, hh: (0, hh))),
        pl.BlockSpec((1, GLA_HK), at(lambda b, hh: (0, hh))),
        pl.BlockSpec((1, GLA_HV), at(lambda b, hh: (0, 0))),
        pl.BlockSpec((r, None, GLA_HK, GLA_HV), at(lambda b, hh: (b, hh, 0, 0))),
    ]
    out_specs = [
        pl.BlockSpec((None, r, GLA_HV), at(lambda b, hh: (b, 0, hh))),
        pl.BlockSpec((r, None, GLA_HK, GLA_HV), at(lambda b, hh: (b, hh, 0, 0))),
    ]
    out_shape = [
        jax.ShapeDtypeStruct((nseq // r, r, h * GLA_HV), F32),
        jax.ShapeDtypeStruct(state.shape, F32),
    ]
    args = (proj3, proj3, proj3, proj3, alr3, wa2p, ba2, gn, state)
    assert len(in_specs) == GLA_DEC_INPUTS
    return n_items, in_specs, args, out_specs, out_shape


HALO_ROWS = 16


def _shift_rows(x, halo, ri):
    h1 = halo[HALO_ROWS - 1:HALO_ROWS, :]
    h2 = halo[HALO_ROWS - 2:HALO_ROWS - 1, :]
    p1 = jnp.where(ri == 0, h1, pltpu.roll(x, 1, 0))
    p2 = jnp.where(ri == 0, h2, jnp.where(ri == 1, h1, pltpu.roll(x, 2, 0)))
    return p1, p2


CONV_ITEM_ROWS = 1024


def _conv_item(tiles_per_seq, n_items, item, cb_ref, cc_ref, ch_ref, cch_ref, chh_ref, w_ref, o_ref, st_ref):
    tl, cd = cb_ref.shape
    tile = jnp.minimum(item, n_items - 1)
    seq_start = tile % tiles_per_seq == 0
    ri = lax.broadcasted_iota(jnp.int32, (tl, 1), 0)
    for c0 in range(0, cd, LANES):
        cs = slice(c0, c0 + LANES)
        cch = cc_ref[:, cs].astype(F32) * ch_ref[:, cs].astype(F32)
        halo = cch_ref[:, cs].astype(F32) * chh_ref[:, cs].astype(F32)
        halo = jnp.where(seq_start, 0.0, halo)
        p1, p2 = _shift_rows(cch, halo, ri)
        w = w_ref[:, cs]
        u = w[0:1, :] * p2 + w[1:2, :] * p1 + w[2:3, :] * cch
        o_ref[:, cs] = (cb_ref[:, cs].astype(F32) * u).astype(o_ref.dtype)
        st_ref[:, cs] = cch[tl - 2:tl, :]


def _conv_prompt_specs(item, proj_b, w_conv, seq):
    m = proj_b.shape[0]
    tl = min(CONV_ITEM_ROWS, seq)
    assert seq % tl == 0
    n_items = m // tl
    tiles_per_seq = seq // tl
    th = tl // HALO_ROWS
    cd = CONV_DIM
    tile = lambda *g: jnp.minimum(item(*g), n_items - 1)
    in_specs = [
        pl.BlockSpec((tl, cd), lambda *g: (tile(*g), 0)),
        pl.BlockSpec((tl, cd), lambda *g: (tile(*g), 1)),
        pl.BlockSpec((tl, cd), lambda *g: (tile(*g), 2)),
        pl.BlockSpec((HALO_ROWS, cd), lambda *g: (jnp.maximum(tile(*g) * th - 1, 0), 1)),
        pl.BlockSpec((HALO_ROWS, cd), lambda *g: (jnp.maximum(tile(*g) * th - 1, 0), 2)),
        pl.BlockSpec((3, cd), lambda *g: (0, 0)),
    ]
    out_specs = [
        pl.BlockSpec((tl, cd), lambda *g: (tile(*g), 0)),
        pl.BlockSpec((None, 2, cd), lambda *g: (tile(*g) // tiles_per_seq, 0, 0)),
    ]
    out_shape = [
        jax.ShapeDtypeStruct((m, cd), BF16),
        jax.ShapeDtypeStruct((m // seq, 2, cd), F32),
    ]
    return n_items, in_specs, (proj_b, proj_b, proj_b, proj_b, proj_b, w_conv), out_specs, out_shape


def _conv_prompt(proj_b, w_conv, seq):
    n_items, in_specs, args, out_specs, out_shape = _conv_prompt_specs(lambda t: t, proj_b, w_conv, seq)
    tiles_per_seq = seq // min(CONV_ITEM_ROWS, seq)

    def conv_kernel(*refs):
        _conv_item(tiles_per_seq, n_items, pl.program_id(0), *refs)

    return pl.pallas_call(
        conv_kernel,
        grid=(n_items,),
        in_specs=in_specs,
        out_specs=out_specs,
        out_shape=out_shape,
        compiler_params=_params("arbitrary"),
        name="conv_prompt",
    )(*args)


def _conv_dec_kernel(cb_ref, cc_ref, ch_ref, s0_ref, s1_ref, w_ref, o_ref, st_ref):
    cd = cb_ref.shape[1]
    cch = cc_ref[...] * ch_ref[...]
    s1 = s1_ref[...]
    w = w_ref[...]
    u = w[0:1, :] * s0_ref[...] + w[1:2, :] * s1 + w[2:3, :] * cch
    o_ref[...] = (cb_ref[...] * u).astype(o_ref.dtype)
    st_ref[:, :cd] = s1
    st_ref[:, cd:] = cch


def _conv_decode(proj_b, w_conv, state2d):
    n = proj_b.shape[0]
    cd = CONV_DIM
    blk = lambda col: pl.BlockSpec((n, cd), lambda i: (0, col))
    return pl.pallas_call(
        _conv_dec_kernel,
        grid=(1,),
        in_specs=[blk(0), blk(1), blk(2), blk(0), blk(1), pl.BlockSpec((3, cd), lambda i: (0, 0))],
        out_specs=[pl.BlockSpec((n, cd), lambda i: (0, 0)), pl.BlockSpec((n, 2 * cd), lambda i: (0, 0))],
        out_shape=[jax.ShapeDtypeStruct((n, cd), BF16), jax.ShapeDtypeStruct((n, 2 * cd), F32)],
        compiler_params=_params("arbitrary"),
        name="conv_decode",
    )(proj_b, proj_b, proj_b, state2d, state2d, w_conv)


def _softmax_rows(logits):
    m = jnp.max(logits, axis=-1, keepdims=True)
    p = jnp.exp(logits - m)
    return p / jnp.sum(p, axis=-1, keepdims=True)


def _memattn_kernel(q_ref, k_ref, v_ref, o_ref):
    logits = _dot(q_ref[...].astype(BF16), k_ref[...].astype(BF16), NT) * (MEM_HD ** -0.5)
    p = _softmax_rows(logits)
    o_ref[...] = _dot(p.astype(BF16), v_ref[...].astype(BF16)).astype(o_ref.dtype)


def _memattn_prompt(proj_b, mk, mv, nb, seq):
    tl = min(2048, seq)
    nt = seq // tl
    h = MEM_HEADS
    qcol = 3 * CONV_DIM // MEM_HD
    return pl.pallas_call(
        _memattn_kernel,
        grid=(nb, h, nt),
        in_specs=[
            pl.BlockSpec((tl, MEM_HD), lambda b, hh, t: (b * nt + t, qcol + hh)),
            pl.BlockSpec((N_MEM, MEM_HD), lambda b, hh, t: (b, hh)),
            pl.BlockSpec((N_MEM, MEM_HD), lambda b, hh, t: (b, hh)),
        ],
        out_specs=pl.BlockSpec((tl, MEM_HD), lambda b, hh, t: (b * nt + t, hh)),
        out_shape=jax.ShapeDtypeStruct((nb * seq, MEM_DIM), BF16),
        compiler_params=_params("parallel", "parallel", "parallel"),
        name="memattn_prompt",
    )(proj_b, mk, mv)


MEMATTN_DEC_ROWS = 2


MEM_HD_CHUNKS = MEM_HD // LANES
MEM_TILE_ROWS = MEM_HD_CHUNKS * MEM_HEADS
assert MEM_HD_CHUNKS == 2 and MEM_TILE_ROWS == 8


def _heads_to_tile(x):
    lead = x.shape[:-2]
    x = x.reshape(lead + (MEM_HEADS, MEM_HD_CHUNKS, LANES))
    return jnp.swapaxes(x, -3, -2).reshape(lead + (MEM_TILE_ROWS, LANES))


def _tile_to_heads(x):
    lead = x.shape[:-2]
    x = x.reshape(lead + (MEM_HD_CHUNKS, MEM_HEADS, LANES))
    return jnp.swapaxes(x, -3, -2).reshape(lead + (MEM_HEADS, MEM_HD))


def _memattn_dec_item(item, q_ref, k_ref, v_ref, o_ref):
    def body(i, carry):
        q = q_ref[i] * (MEM_HD ** -0.5)
        prod = k_ref[i] * q[None]
        prod = prod + pltpu.roll(prod, MEM_HEADS, 1)
        logits = jnp.sum(prod, axis=-1, keepdims=True)
        e = jnp.exp(logits - jnp.max(logits, axis=0, keepdims=True))
        o_ref[i] = jnp.sum(e * v_ref[i], axis=0) / jnp.sum(e, axis=0)
        return carry

    lax.fori_loop(0, q_ref.shape[0], body, 0, unroll=True)


def _memattn_decode_specs(item, q_t, k_t, v_t):
    n = q_t.shape[0]
    r = MEMATTN_DEC_ROWS
    n_items = n // r
    tile = (MEM_TILE_ROWS, LANES)

    def at(ndim):
        return lambda *grid_idx: (jnp.minimum(item(*grid_idx), n_items - 1),) + (0,) * (ndim - 1)

    in_specs = [
        pl.BlockSpec((r,) + tile, at(3)),
        pl.BlockSpec((r, N_MEM) + tile, at(4)),
        pl.BlockSpec((r, N_MEM) + tile, at(4)),
    ]
    out_specs = [pl.BlockSpec((r,) + tile, at(3))]
    out_shape = [jax.ShapeDtypeStruct((n,) + tile, F32)]
    return n_items, in_specs, (q_t, k_t, v_t), out_specs, out_shape


def _merge_kernel(a_ref, b_ref, m_ref, wa_ref, wb_ref, wm_ref, za_ref, zb_ref, zm_ref, o_ref):
    ya = _dot(a_ref[...].astype(BF16), wa_ref[...])
    yb = _dot(b_ref[...].astype(BF16), wb_ref[...])
    ym = _dot(m_ref[...].astype(BF16), wm_ref[...])
    gate = lambda z_ref: _sigmoid(z_ref[...].astype(F32))
    merged = gate(za_ref) * ya + gate(zb_ref) * yb + gate(zm_ref) * ym
    o_ref[...] = merged.astype(o_ref.dtype)


def _merge(o_gla, cbu, om, w_gla_out, w_conv_out, w_mem_out, proj_b):
    m = o_gla.shape[0]
    tm = min(1024, m)
    tn = 1024
    d = D_MODEL
    zoff = (3 * CONV_DIM + MEM_DIM) // tn
    zspec = lambda g: pl.BlockSpec((tm, tn), lambda i, j: (i, zoff + g * (d // tn) + j))
    return pl.pallas_call(
        _merge_kernel,
        grid=(m // tm, d // tn),
        in_specs=[
            pl.BlockSpec((tm, o_gla.shape[1]), lambda i, j: (i, 0)),
            pl.BlockSpec((tm, cbu.shape[1]), lambda i, j: (i, 0)),
            pl.BlockSpec((tm, om.shape[1]), lambda i, j: (i, 0)),
            pl.BlockSpec((w_gla_out.shape[0], tn), lambda i, j: (0, j)),
            pl.BlockSpec((w_conv_out.shape[0], tn), lambda i, j: (0, j)),
            pl.BlockSpec((w_mem_out.shape[0], tn), lambda i, j: (0, j)),
            zspec(0), zspec(1), zspec(2),
        ],
        out_specs=pl.BlockSpec((tm, tn), lambda i, j: (i, j)),
        out_shape=jax.ShapeDtypeStruct((m, d), BF16),
        compiler_params=_params("parallel", "parallel"),
        name="merge",
    )(o_gla, cbu, om, w_gla_out, w_conv_out, w_mem_out, proj_b, proj_b, proj_b)


def _wo_ln_kernel(x_ref, m_ref, w_ref, g_ref, b_ref, o_ref):
    tm = x_ref.shape[0]
    half = tm // 4 if tm % 512 == 0 else tm
    for r in range(0, tm, half):
        rows = pl.ds(r, half)
        y = DN_ALPHA * x_ref[rows, :] + _dot(m_ref[rows, :], w_ref[...])
        o_ref[rows, :] = _layer_norm(y, g_ref[...], b_ref[...])


def _wo_ln(x, merged, w_o, g, b):
    m, d = x.shape
    tm = min(512, m)
    return pl.pallas_call(
        _wo_ln_kernel,
        grid=(m // tm,),
        in_specs=[
            pl.BlockSpec((tm, d), lambda i: (i, 0)),
            pl.BlockSpec((tm, d), lambda i: (i, 0)),
            pl.BlockSpec((d, d), lambda i: (0, 0)),
            pl.BlockSpec((1, d), lambda i: (0, 0)),
            pl.BlockSpec((1, d), lambda i: (0, 0)),
        ],
        out_specs=pl.BlockSpec((tm, d), lambda i: (i, 0)),
        out_shape=jax.ShapeDtypeStruct((m, d), F32),
        compiler_params=_params("parallel"),
        name="wo_ln",
    )(x, merged, w_o, g, b)


def _ffn_act(hc, hu, valid, c0=0):
    h = hc * _sigmoid(hc) * hu
    return jnp.where(c0 + lax.broadcasted_iota(jnp.int32, (1, hc.shape[1]), 1) < valid, h, 0.0).astype(BF16)


def _ffn_down(h, valid, wd_ref, acc_ref, x_ref, g_ref, b_ref, o_ref, fi):
    wd = jnp.where(lax.broadcasted_iota(jnp.int32, (h.shape[1], 1), 0) < valid, wd_ref[...], 0.0)
    acc_ref[...] += _dot(h, wd)

    @pl.when(fi == pl.num_programs(1) - 1)
    def _():
        o_ref[...] = _layer_norm(DN_ALPHA * x_ref[...] + acc_ref[...], g_ref[...], b_ref[...])


def _ffn_kernel(seq, x_ref, xh_ref, wg_ref, wu_ref, wd_ref, wc_ref, bc_ref, g_ref, b_ref, *rest):
    dec_in, (o_ref, st_ref, od_ref, so_ref, xb_scr, acc_scr) = rest[:GLA_DEC_INPUTS], rest[GLA_DEC_INPUTS:]
    fi = pl.program_id(1)
    tm = x_ref.shape[0]

    @pl.when(fi == 0)
    def _():
        xb_scr[...] = x_ref[...].astype(BF16)
        acc_scr[...] = jnp.zeros_like(acc_scr)

    _gla_dec_item(*dec_in, od_ref, so_ref)
    xb = xb_scr[...]
    xh = xh_ref[...].astype(BF16)
    tf = wg_ref.shape[1]
    valid = D_FF - fi * tf
    seq_start = (pl.program_id(0) * tm) % seq == 0
    ri = lax.broadcasted_iota(jnp.int32, (tm, 1), 0)
    hs = []
    for c0 in range(0, tf, FFN_GATE_CHUNK):
        cs = slice(c0, c0 + FFN_GATE_CHUNK)
        wg = wg_ref[:, cs]
        hg = _dot(xb, wg)
        st_ref[:, cs] = hg[tm - 2:tm, :]
        halo = jnp.where(seq_start, 0.0, _dot(xh, wg))
        p1, p2 = _shift_rows(hg, halo, ri)
        wc = wc_ref[:, cs]
        hc = wc[0:1, :] * p2 + wc[1:2, :] * p1 + wc[2:3, :] * hg + bc_ref[:, cs]
        hs.append(_ffn_act(hc, _dot(xb, wu_ref[:, cs]), valid, c0))
    _ffn_down(jnp.concatenate(hs, axis=1), valid, wd_ref, acc_scr, x_ref, g_ref, b_ref, o_ref, fi)


def _ffn_prompt(x1, w_gate, w_up, w_down, w_conv, b_conv, g, b, seq, gla_dec_operands):
    m, d = x1.shape
    tm = min(512, seq)
    tf = FF_TILE
    t8 = tm // HALO_ROWS
    n_items, dec_in_specs, dec_args, dec_out_specs, dec_out_shape = _gla_decode_specs(
        lambda i, f: i * FF_TILES + f, *gla_dec_operands)
    assert n_items <= (m // tm) * FF_TILES, "not enough FFN steps to carry the decode GLA work items"
    y, tile_state, o_dec, s_dec = pl.pallas_call(
        functools.partial(_ffn_kernel, seq),
        grid=(m // tm, FF_TILES),
        in_specs=[
            pl.BlockSpec((tm, d), lambda i, f: (i, 0)),
            pl.BlockSpec((HALO_ROWS, d), lambda i, f: (jnp.maximum(i * t8 - 1, 0), 0)),
            pl.BlockSpec((d, tf), lambda i, f: (0, f)),
            pl.BlockSpec((d, tf), lambda i, f: (0, f)),
            pl.BlockSpec((tf, d), lambda i, f: (f, 0)),
            pl.BlockSpec((3, tf), lambda i, f: (0, f)),
            pl.BlockSpec((1, tf), lambda i, f: (0, f)),
            pl.BlockSpec((1, d), lambda i, f: (0, 0)),
            pl.BlockSpec((1, d), lambda i, f: (0, 0)),
        ] + dec_in_specs,
        out_specs=[pl.BlockSpec((tm, d), lambda i, f: (i, 0)),
                   pl.BlockSpec((None, 2, tf), lambda i, f: (i, 0, f))] + dec_out_specs,
        out_shape=[jax.ShapeDtypeStruct((m, d), F32),
                   jax.ShapeDtypeStruct((m // tm, 2, D_FF), F32)] + dec_out_shape,
        scratch_shapes=[pltpu.VMEM((tm, d), BF16), pltpu.VMEM((tm, d), F32)],
        compiler_params=_params("arbitrary", "arbitrary"),
        name="ffn_prompt_gla_decode",
    )(x1, x1, w_gate, w_up, w_down, w_conv, b_conv, g, b, *dec_args)
    tiles_per_seq = seq // tm
    return y, tile_state[tiles_per_seq - 1::tiles_per_seq], o_dec, s_dec


def _ffn_dec_kernel(x_ref, s0_ref, s1_ref, wg_ref, wu_ref, wd_ref, wc_ref, bc_ref, g_ref, b_ref,
                    o_ref, hg_ref, xb_scr, acc_scr):
    fi = pl.program_id(1)

    @pl.when(fi == 0)
    def _():
        xb_scr[...] = x_ref[...].astype(BF16)
        acc_scr[...] = jnp.zeros_like(acc_scr)

    xb = xb_scr[...]
    hg = _dot(xb, wg_ref[...])
    hg_ref[...] = hg
    wc = wc_ref[...]
    hc = wc[0:1, :] * s0_ref[...] + wc[1:2, :] * s1_ref[...] + wc[2:3, :] * hg + bc_ref[...]
    valid = D_FF - fi * hc.shape[1]
    _ffn_down(_ffn_act(hc, _dot(xb, wu_ref[...]), valid), valid, wd_ref, acc_scr, x_ref, g_ref, b_ref, o_ref, fi)


def _ffn_decode(x1, s0, s1, w_gate, w_up, w_down, w_conv, b_conv, g, b):
    m, d = x1.shape
    tf = FF_TILE
    return pl.pallas_call(
        _ffn_dec_kernel,
        grid=(1, FF_TILES),
        in_specs=[
            pl.BlockSpec((m, d), lambda i, f: (0, 0)),
            pl.BlockSpec((m, tf), lambda i, f: (0, f)),
            pl.BlockSpec((m, tf), lambda i, f: (0, f)),
            pl.BlockSpec((d, tf), lambda i, f: (0, f)),
            pl.BlockSpec((d, tf), lambda i, f: (0, f)),
            pl.BlockSpec((tf, d), lambda i, f: (f, 0)),
            pl.BlockSpec((3, tf), lambda i, f: (0, f)),
            pl.BlockSpec((1, tf), lambda i, f: (0, f)),
            pl.BlockSpec((1, d), lambda i, f: (0, 0)),
            pl.BlockSpec((1, d), lambda i, f: (0, 0)),
        ],
        out_specs=[pl.BlockSpec((m, d), lambda i, f: (0, 0)), pl.BlockSpec((m, tf), lambda i, f: (0, f))],
        out_shape=[jax.ShapeDtypeStruct((m, d), F32), jax.ShapeDtypeStruct((m, D_FF), F32)],
        scratch_shapes=[pltpu.VMEM((m, d), BF16), pltpu.VMEM((m, d), F32)],
        compiler_params=_params("parallel", "arbitrary"),
        name="ffn_decode",
    )(x1, s0, s1, w_gate, w_up, w_down, w_conv, b_conv, g, b)


def kernel(x_prompt, x_sample, mem_prompt, cache_mem_k, cache_mem_v, state_gla, state_conv, state_ffn_conv, w_in, w_gla_a2, b_gla_a2, g_gla_norm, w_gla_out, w_conv, w_conv_out, w_mem_k, w_mem_v, w_mem_out, w_o, ln1_g, ln1_b, w_ffn_gate, w_ffn_up, w_ffn_conv, b_ffn_conv, w_ffn_down, ln2_g, ln2_b):
    nb, seq, d = x_prompt.shape
    ns = x_sample.shape[0]

    w_in_t = jnp.swapaxes(w_in[0], 0, 1)
    wa2p = jnp.pad(w_gla_a2[0], ((0, LANES - GLA_RANK), (0, 0)))
    ba2 = b_gla_a2
    gn = g_gla_norm
    w_fc = w_ffn_conv[0]
    b_fc = b_ffn_conv
    w_cv = w_conv[0]

    def mix_in(x2d, tag, dtype, riders_b=(), riders_a=()):
        pr, xb = _proj(x2d, w_in_t, "proj_r_" + tag, COLS_A, LANES, LANES, transposed=True, emit_x=True)
        pb = _proj(xb, w_in_t, "proj_b_" + tag, COLS_A, COLS_B, shift=GLA_RANK, transposed=True,
                   riders=riders_b, out_dtype=dtype)
        pa = _proj(xb, w_in_t, "proj_a_" + tag, 0, COLS_A, transposed=True, riders=riders_a, out_dtype=dtype)
        return pa, pr, pb

    xs = x_sample.reshape(ns, d)
    sa, sr, sb = mix_in(xs, "s", F32)
    mq_s = sb[:, 3 * CONV_DIM:3 * CONV_DIM + MEM_DIM].reshape(ns, MEM_HEADS, MEM_HD)
    memattn_dec = (_memattn_dec_item, _memattn_decode_specs,
                   (_heads_to_tile(mq_s), _heads_to_tile(cache_mem_k[0]), _heads_to_tile(cache_mem_v[0])))

    xp = x_prompt.reshape(nb * seq, d)
    memb = mem_prompt.reshape(nb * N_MEM, d).astype(BF16)
    mk = _proj(memb, w_mem_k[0], "mem_k", tn=MEM_HD)
    mv = _proj(memb, w_mem_v[0], "mem_v", tn=MEM_HD)
    steps_a, steps_b = _proj_steps(nb * seq, COLS_A), _proj_steps(nb * seq, COLS_B)
    riders_b = (memattn_dec, _cast_rider(w_ffn_gate[0], steps_b), _cast_rider(w_ffn_up[0], steps_b))
    riders_a = tuple(_cast_rider(w[0], steps_a) for w in (w_ffn_down, w_gla_out, w_conv_out, w_mem_out, w_o))
    (pa, w_fd, w_go, w_co, w_mo, w_oo), pr, (pb, om_s, w_fg, w_fu) = mix_in(xp, "p", BF16, riders_b, riders_a)

    def mix_out(x2d, o_gla, cbu, om, proj_b):
        merged = _merge(o_gla, cbu, om, w_go, w_co, w_mo, proj_b)
        return _wo_ln(x2d, merged, w_oo, ln1_g, ln1_b)

    o_gla, p_gla = _gla_prompt(pa, pr, wa2p, ba2, gn, nb, seq)
    cbu, p_conv = _conv_prompt(pb, w_cv, seq)
    om = _memattn_prompt(pb, mk, mv, nb, seq)
    x1 = mix_out(xp, o_gla, cbu, om, pb)
    yp, p_ffn, o_gla_s, s_gla = _ffn_prompt(x1, w_fg, w_fu, w_fd, w_fc, b_fc, ln2_g, ln2_b, seq,
                                            (sa, sr, wa2p, ba2, gn, state_gla[0]))
    o_gla_s = o_gla_s.reshape(ns, GLA_HEADS * GLA_HV)

    cbu_s, s_conv = _conv_decode(sb, w_cv, state_conv[0].reshape(ns, 2 * CONV_DIM))
    om_s = _tile_to_heads(om_s).reshape(ns, MEM_DIM)
    x1s = mix_out(xs, o_gla_s, cbu_s, om_s, sb)
    f0 = state_ffn_conv[0][:, 0, :]
    f1 = state_ffn_conv[0][:, 1, :]
    ys, hg_s = _ffn_decode(x1s, f0, f1, w_fg, w_fu, w_fd, w_fc, b_fc, ln2_g, ln2_b)
    s_ffn = jnp.stack([f1, hg_s], axis=1)

    return (yp.reshape(nb, seq, d), ys.reshape(ns, 1, d),
            mk.reshape(1, nb, N_MEM, MEM_HEADS, MEM_HD), mv.reshape(1, nb, N_MEM, MEM_HEADS, MEM_HD),
            p_gla[None], p_conv[None], p_ffn.reshape(1, nb, 2, D_FF),
            s_gla[None], s_conv.reshape(1, ns, 2, CONV_DIM), s_ffn[None])
```

```python
import functools

import jax
import jax.numpy as jnp
from jax import lax
from jax.experimental import pallas as pl
from jax.experimental.pallas import tpu as pltpu

F32 = jnp.float32
BF16 = jnp.bfloat16

D_MODEL = 2048
GLA_HEADS = 4
GLA_HK = 256
GLA_HV = 512
GLA_RANK = 16
GLA_GATE_NORM = 16.0
CONV_DIM = 1024
N_MEM = 256
MEM_HEADS = 4
MEM_HD = 256
MEM_DIM = 1024
D_FF = 5504
DEPTH = 1
DN_ALPHA = (2 * DEPTH) ** 0.25
LN_EPS = 1e-5
RMS_EPS = 1e-6

LANES = 128
SUBLANES = 8
LOG2_E = 1.4426950408889634
FF_TILE = 512
FF_TILES = -(-D_FF // FF_TILE)
FFN_GATE_CHUNK = FF_TILE
COLS_A = 2 * GLA_HEADS * GLA_HK + 2 * GLA_HEADS * GLA_HV
COLS_B = 3 * CONV_DIM + MEM_DIM + 3 * D_MODEL
GLA_CHUNK = 256
VMEM_LIMIT = 56 * 1024 * 1024

NN = (((1,), (0,)), ((), ()))
NT = (((1,), (1,)), ((), ()))
TN = (((0,), (0,)), ((), ()))


def _dot(a, b, dims=NN):
    return lax.dot_general(a, b, dims, preferred_element_type=F32)


def _params(*sem):
    return pltpu.CompilerParams(dimension_semantics=sem, vmem_limit_bytes=VMEM_LIMIT)


def _split2(x):
    hi = x.astype(BF16)
    lo = (x - hi.astype(F32)).astype(BF16)
    return hi, lo


def _split3(x):
    hi = x.astype(BF16)
    r = x - hi.astype(F32)
    mid = r.astype(BF16)
    lo = (r - mid.astype(F32)).astype(BF16)
    return hi, mid, lo


def _sigmoid(z):
    return 0.5 * jnp.tanh(0.5 * z) + 0.5


def _silu(z):
    h = 0.5 * z
    return h * jnp.tanh(h) + h


def _layer_norm(y, g, b):
    mu = jnp.mean(y, axis=-1, keepdims=True)
    d = y - mu
    var = jnp.mean(d * d, axis=-1, keepdims=True)
    return d * lax.rsqrt(var + LN_EPS) * g + b


def _log2_decay(z):
    return (jnp.minimum(z, 0.0) - jnp.log(1.0 + jnp.exp(-jnp.abs(z)))) * (LOG2_E / GLA_GATE_NORM)


def _log_decay(alr, wa2, ba2):
    ah, al = _split2(alr)
    wh, wl = _split2(wa2)
    return _log2_decay(_dot(ah, wh) + _dot(ah, wl) + _dot(al, wh) + ba2)


def _rms_gate(o, gn, g):
    o = o * lax.rsqrt(jnp.mean(o * o, axis=-1, keepdims=True) + RMS_EPS) * gn
    return o * _silu(g)


PROJ_SHIFT_ROWS = 16
PROJ_TILE = 1024
BF16_SUBLANES = 16


def _proj_kernel(transposed, shift, riders, emit_x, x_ref, w_ref, *rest):
    rest = list(rest)
    wx_ref = rest.pop(0) if shift else None
    wb_scr = rest.pop()
    rider_in = [[rest.pop(0) for _ in range(n_in)] for _, n_in, _ in riders]
    o_ref = rest.pop(0)
    ox_ref = rest.pop(0) if emit_x else None
    rider_out = [[rest.pop(0) for _ in range(n_out)] for _, _, n_out in riders]
    item = pl.program_id(0) * pl.num_programs(1) + pl.program_id(1)

    @pl.when(pl.program_id(1) == 0)
    def _():
        w = w_ref[...]
        if shift:
            w = jnp.concatenate([w[shift:], wx_ref[:shift]], axis=0)
        w = w.astype(BF16)
        wb_scr[...] = w

    xb = x_ref[...].astype(BF16)
    if emit_x:
        ox_ref[...] = xb
    o_ref[...] = _dot(xb, wb_scr[...], NT if transposed else NN).astype(o_ref.dtype)
    for (body, _, _), r_in, r_out in zip(riders, rider_in, rider_out):
        body(item, *r_in, *r_out)


def _proj(x, w, name, col0=0, n=None, tn=PROJ_TILE, shift=0, transposed=False, riders=(), out_dtype=F32,
          emit_x=False):
    m, k = x.shape
    if n is None:
        n = w.shape[0] if transposed else w.shape[1]
    tm = min(m, PROJ_TILE)
    tn = min(n, tn)
    c0 = col0 // tn
    if transposed:
        w_spec = pl.BlockSpec((tn, k), lambda j, i: (c0 + j, 0))
    else:
        assert not shift
        w_spec = pl.BlockSpec((k, tn), lambda j, i: (0, c0 + j))
    in_specs = [pl.BlockSpec((tm, k), lambda j, i: (i, 0)), w_spec]
    args = [x, w]
    if shift:
        r = PROJ_SHIFT_ROWS
        assert shift <= r and shift % 8 == 0 and tn % r == 0 and col0 % r == 0
        in_specs.append(pl.BlockSpec((r, k), lambda j, i: (col0 // r + (j + 1) * (tn // r), 0)))
        args.append(w)
    out_specs = [pl.BlockSpec((tm, tn), lambda j, i: (i, j))]
    out_shape = [jax.ShapeDtypeStruct((m, n), out_dtype)]
    if emit_x:
        assert n == tn, "each x tile must be visited once"
        out_specs.append(pl.BlockSpec((tm, k), lambda j, i: (i, 0)))
        out_shape.append(jax.ShapeDtypeStruct((m, k), BF16))
    kernel_riders = []
    for body, specs_fn, operands in riders:
        n_items, r_in_specs, r_args, r_out_specs, r_out_shape = specs_fn(lambda j, i: j * (m // tm) + i, *operands)
        assert n_items <= (n // tn) * (m // tm), "not enough projection steps to carry the rider's work items"
        in_specs += r_in_specs
        args += list(r_args)
        out_specs += r_out_specs
        out_shape += r_out_shape
        kernel_riders.append((body, len(r_in_specs), len(r_out_specs)))
    outs = pl.pallas_call(
        functools.partial(_proj_kernel, transposed, shift, tuple(kernel_riders), emit_x),
        grid=(n // tn, m // tm),
        in_specs=in_specs,
        out_specs=out_specs,
        out_shape=out_shape,
        scratch_shapes=[pltpu.VMEM((tn, k) if transposed else (k, tn), BF16)],
        compiler_params=_params("arbitrary" if riders else "parallel", "arbitrary"),
        name=name,
    )(*args)
    return outs if len(outs) > 1 else outs[0]


def _proj_steps(m, n):
    return (n // min(n, PROJ_TILE)) * (m // min(m, PROJ_TILE))


def _cast_item(item, x_ref, o_ref):
    o_ref[...] = x_ref[...].astype(o_ref.dtype)


def _cast_specs(item, w, n_steps):
    r, c = w.shape
    rows = -(-r // n_steps)
    rows = -(-rows // BF16_SUBLANES) * BF16_SUBLANES
    n_items = -(-r // rows)
    at = lambda *g: (jnp.minimum(item(*g), n_items - 1), 0)
    return (n_items, [pl.BlockSpec((rows, c), at)], (w,), [pl.BlockSpec((rows, c), at)],
            [jax.ShapeDtypeStruct((r, c), BF16)])


def _cast_rider(w, n_steps):
    return (_cast_item, _cast_specs, (w, n_steps))


def _gla_levels(c):
    return c.bit_length() - 1


def _init_gla_masks(mask_scr, tri_scr):
    c = tri_scr.shape[0]
    rr = lax.broadcasted_iota(jnp.int32, (c, c), 0)
    cc = lax.broadcasted_iota(jnp.int32, (c, c), 1)
    tri_scr[...] = (rr >= cc).astype(BF16)
    mask_scr[0] = (rr == cc).astype(F32)
    for l in range(_gla_levels(c)):
        b = 1 << l
        pair = (rr // (2 * b) == cc // (2 * b)) & ((rr // b) % 2 == 1) & ((cc // b) % 2 == 0)
        mask_scr[1 + l] = pair.astype(F32)


def _intra_scores(q, k, la, bc, mask_scr):
    c, dk = q.shape
    hc = c // 2
    ri = lax.broadcasted_iota(jnp.int32, (c, 1), 0)
    diag = jnp.sum(q * k, axis=1, keepdims=True)
    eye = mask_scr[0, :hc, :hc]
    a0, a1 = diag[:hc] * eye, diag[hc:] * eye
    for l in range(_gla_levels(c)):
        b = 1 << l
        if b < SUBLANES:
            upper = (ri // b) % 2 == 1
            if b == 1:
                e = jnp.where(upper, la, 0.0)
            elif b == 2:
                m4 = ri % 4
                la_prev = pltpu.roll(la, 1, 0)
                la_next = pltpu.roll(la, c - 1, 0)
                e = jnp.where(m4 == 2, la, jnp.where(m4 == 3, la + la_prev, jnp.where(m4 == 0, la_next, 0.0)))
            else:
                ref = bc.reshape(c // (2 * b), 2 * b, dk)[:, b - 1:b, :]
                ref = jnp.broadcast_to(ref, (c // (2 * b), 2 * b, dk)).reshape(c, dk)
                dlt = bc - ref
                e = jnp.minimum(dlt, -dlt)
            w = jnp.where(upper, q, k) * jnp.exp2(e)
        else:
            parts = []
            for r0 in range(0, c, 2 * b):
                ref = bc[r0 + b - 1:r0 + b, :]
                parts.append(k[r0:r0 + b] * jnp.exp2(ref - bc[r0:r0 + b]))
                parts.append(q[r0 + b:r0 + 2 * b] * jnp.exp2(bc[r0 + b:r0 + 2 * b] - ref))
            w = jnp.concatenate(parts, axis=0)
        w = w.astype(BF16)
        w0, w1 = w[:hc], w[hc:]
        if b == hc:
            return a0, a1, _dot(w1, w0, NT)
        mask = mask_scr[1 + l, :hc, :hc]
        a0 = a0 + _dot(w0, w0, NT) * mask
        a1 = a1 + _dot(w1, w1, NT) * mask


def _gla_kernel(q_ref, k_ref, v_ref, g_ref, alr_ref, wa2_ref, ba2_ref, gn_ref, o_ref, sfin_ref,
                s_scr, mask_scr, tri_scr):
    ci = pl.program_id(2)

    @pl.when(ci == 0)
    def _():
        s_scr[...] = jnp.zeros_like(s_scr)
        _init_gla_masks(mask_scr, tri_scr)

    c = q_ref.shape[0]
    tri = tri_scr[...]
    ones = jnp.ones((c, LANES), BF16)
    alr = alr_ref[...]
    gn = gn_ref[...]
    for hh in range(s_scr.shape[0]):
        ks = slice(hh * GLA_HK, (hh + 1) * GLA_HK)
        vs = slice(hh * GLA_HV, (hh + 1) * GLA_HV)
        la = _log_decay(alr, wa2_ref[:, ks], ba2_ref[:, ks])
        lh, lm, ll = _split3(la)
        bc = _dot(tri, lh) + _dot(tri, lm) + _dot(tri, ll)
        b_last = bc[c - 1:c, :]
        bl_col = _dot(lh, ones, TN) + _dot(lm, ones, TN) + _dot(ll, ones, TN)
        dec_col = jnp.exp2(bl_col)
        dec_col = jnp.concatenate([dec_col] * (GLA_HV // LANES), axis=1)

        q = q_ref[:, ks].astype(F32) * (GLA_HK ** -0.5)
        k = k_ref[:, ks].astype(F32)
        vb = v_ref[:, vs].astype(BF16)
        s = s_scr[hh]
        o = _dot((q * jnp.exp2(bc)).astype(BF16), s.astype(BF16))
        a0, a1, a10 = _intra_scores(q, k, la, bc, mask_scr)
        hc = c // 2
        v0, v1 = vb[:hc], vb[hc:]
        o = o + jnp.concatenate([_dot(a0.astype(BF16), v0),
                                 _dot(a10.astype(BF16), v0) + _dot(a1.astype(BF16), v1)], axis=0)
        kd = (k * jnp.exp2(b_last - bc)).astype(BF16)
        s_new = dec_col * s + _dot(kd, vb, TN)
        s_scr[hh] = s_new
        o_ref[:, vs] = _rms_gate(o, gn, g_ref[:, vs].astype(F32)).astype(o_ref.dtype)

    @pl.when(ci == pl.num_programs(2) - 1)
    def _():
        sfin_ref[...] = s_scr[...]


GLA_HEADS_PER_STEP = 4


def _gla_prompt(proj_a, alr, wa2p, ba2, gn, nb, seq):
    c = min(GLA_CHUNK, seq)
    nc = seq // c
    hp = GLA_HEADS_PER_STEP
    ng = GLA_HEADS // hp
    wk, wv = hp * GLA_HK, hp * GLA_HV
    row = lambda b, hg, ci: b * nc + ci
    return pl.pallas_call(
        _gla_kernel,
        grid=(nb, ng, nc),
        in_specs=[
            pl.BlockSpec((c, wk), lambda b, hg, ci: (row(b, hg, ci), hg)),
            pl.BlockSpec((c, wk), lambda b, hg, ci: (row(b, hg, ci), ng + hg)),
            pl.BlockSpec((c, wv), lambda b, hg, ci: (row(b, hg, ci), ng + hg)),
            pl.BlockSpec((c, wv), lambda b, hg, ci: (row(b, hg, ci), 2 * ng + hg)),
            pl.BlockSpec((c, LANES), lambda b, hg, ci: (row(b, hg, ci), 0)),
            pl.BlockSpec((LANES, wk), lambda b, hg, ci: (0, hg)),
            pl.BlockSpec((1, wk), lambda b, hg, ci: (0, hg)),
            pl.BlockSpec((1, GLA_HV), lambda b, hg, ci: (0, 0)),
        ],
        out_specs=[
            pl.BlockSpec((c, wv), lambda b, hg, ci: (row(b, hg, ci), hg)),
            pl.BlockSpec((None, hp, GLA_HK, GLA_HV), lambda b, hg, ci: (b, hg, 0, 0)),
        ],
        out_shape=[
            jax.ShapeDtypeStruct((nb * seq, GLA_HEADS * GLA_HV), BF16),
            jax.ShapeDtypeStruct((nb, GLA_HEADS, GLA_HK, GLA_HV), F32),
        ],
        scratch_shapes=[pltpu.VMEM((hp, GLA_HK, GLA_HV), F32),
                        pltpu.VMEM((1 + _gla_levels(c), c, c), F32),
                        pltpu.VMEM((c, c), BF16)],
        compiler_params=_params("parallel", "parallel", "arbitrary"),
        name="gla_prompt",
    )(proj_a, proj_a, proj_a, proj_a, alr, wa2p, ba2, gn)


GLA_DEC_ROWS = 4


def _gla_dec_item(q_ref, k_ref, v_ref, g_ref, alr_ref, wa2_ref, ba2_ref, gn_ref, s_ref, o_ref, so_ref):
    nb = q_ref.shape[0]
    alr = alr_ref[...]
    wa2 = wa2_ref[...]
    z = ba2_ref[...]
    for r in range(GLA_RANK):
        z = z + alr[:, r:r + 1] * wa2[r:r + 1, :]
    la = _log2_decay(z)
    rep = LANES // nb
    to_cols = lambda x: jnp.concatenate([x] * rep, axis=0).T
    a_t = to_cols(jnp.exp2(la))
    k_t = to_cols(k_ref[...])
    q_t = to_cols(q_ref[...] * (GLA_HK ** -0.5))
    v = v_ref[...]
    rows = []
    for j in range(nb):
        s_new = a_t[:, j:j + 1] * s_ref[j] + k_t[:, j:j + 1] * v[j:j + 1, :]
        so_ref[j] = s_new
        rows.append(jnp.sum(q_t[:, j:j + 1] * s_new, axis=0, keepdims=True))
    o = jnp.concatenate(rows, axis=0)
    o_ref[...] = _rms_gate(o, gn_ref[...], g_ref[...]).astype(o_ref.dtype)


GLA_DEC_INPUTS = 9


def _gla_decode_specs(item, proj_a, alr, wa2p, ba2, gn, state):
    nseq = proj_a.shape[0]
    h = GLA_HEADS
    r = GLA_DEC_ROWS
    n_items = (nseq // r) * h
    proj3 = proj_a.reshape(nseq // r, r, proj_a.shape[1])
    alr3 = alr.reshape(nseq // r, r, LANES)

    def at(fn):
        def index_map(*grid_idx):
            it = jnp.minimum(item(*grid_idx), n_items - 1)
            return fn(it // h, it % h)
        return index_map

    in_specs = [
        pl.BlockSpec((None, r, GLA_HK), at(lambda b, hh: (b, 0, hh))),
        pl.BlockSpec((None, r, GLA_HK), at(lambda b, hh: (b, 0, h + hh))),
        pl.BlockSpec((None, r, GLA_HV), at(lambda b, hh: (b, 0, h + hh))),
        pl.BlockSpec((None, r, GLA_HV), at(lambda b, hh: (b, 0, 2 * h + hh))),
        pl.BlockSpec((None, r, LANES), at(lambda b, hh: (b, 0, 0))),
        pl.BlockSpec((LANES, GLA_HK), at(lambda b, hh: (0, hh))),
        pl.BlockSpec((1, GLA_HK), at(lambda b, hh: (0, hh))),
        pl.BlockSpec((1, GLA_HV), at(lambda b, hh: (0, 0))),
        pl.BlockSpec((r, None, GLA_HK, GLA_HV), at(lambda b, hh: (b, hh, 0, 0))),
    ]
    out_specs = [
        pl.BlockSpec((None, r, GLA_HV), at(lambda b, hh: (b, 0, hh))),
        pl.BlockSpec((r, None, GLA_HK, GLA_HV), at(lambda b, hh: (b, hh, 0, 0))),
    ]
    out_shape = [
        jax.ShapeDtypeStruct((nseq // r, r, h * GLA_HV), F32),
        jax.ShapeDtypeStruct(state.shape, F32),
    ]
    args = (proj3, proj3, proj3, proj3, alr3, wa2p, ba2, gn, state)
    assert len(in_specs) == GLA_DEC_INPUTS
    return n_items, in_specs, args, out_specs, out_shape


HALO_ROWS = 16


def _shift_rows(x, halo, ri):
    h1 = halo[HALO_ROWS - 1:HALO_ROWS, :]
    h2 = halo[HALO_ROWS - 2:HALO_ROWS - 1, :]
    p1 = jnp.where(ri == 0, h1, pltpu.roll(x, 1, 0))
    p2 = jnp.where(ri == 0, h2, jnp.where(ri == 1, h1, pltpu.roll(x, 2, 0)))
    return p1, p2


CONV_ITEM_ROWS = 1024


def _conv_item(tiles_per_seq, n_items, item, cb_ref, cc_ref, ch_ref, cch_ref, chh_ref, w_ref, o_ref, st_ref):
    tl, cd = cb_ref.shape
    tile = jnp.minimum(item, n_items - 1)
    seq_start = tile % tiles_per_seq == 0
    ri = lax.broadcasted_iota(jnp.int32, (tl, 1), 0)
    for c0 in range(0, cd, LANES):
        cs = slice(c0, c0 + LANES)
        cch = cc_ref[:, cs].astype(F32) * ch_ref[:, cs].astype(F32)
        halo = cch_ref[:, cs].astype(F32) * chh_ref[:, cs].astype(F32)
        halo = jnp.where(seq_start, 0.0, halo)
        p1, p2 = _shift_rows(cch, halo, ri)
        w = w_ref[:, cs]
        u = w[0:1, :] * p2 + w[1:2, :] * p1 + w[2:3, :] * cch
        o_ref[:, cs] = (cb_ref[:, cs].astype(F32) * u).astype(o_ref.dtype)
        st_ref[:, cs] = cch[tl - 2:tl, :]


def _conv_prompt_specs(item, proj_b, w_conv, seq):
    m = proj_b.shape[0]
    tl = min(CONV_ITEM_ROWS, seq)
    assert seq % tl == 0
    n_items = m // tl
    tiles_per_seq = seq // tl
    th = tl // HALO_ROWS
    cd = CONV_DIM
    tile = lambda *g: jnp.minimum(item(*g), n_items - 1)
    in_specs = [
        pl.BlockSpec((tl, cd), lambda *g: (tile(*g), 0)),
        pl.BlockSpec((tl, cd), lambda *g: (tile(*g), 1)),
        pl.BlockSpec((tl, cd), lambda *g: (tile(*g), 2)),
        pl.BlockSpec((HALO_ROWS, cd), lambda *g: (jnp.maximum(tile(*g) * th - 1, 0), 1)),
        pl.BlockSpec((HALO_ROWS, cd), lambda *g: (jnp.maximum(tile(*g) * th - 1, 0), 2)),
        pl.BlockSpec((3, cd), lambda *g: (0, 0)),
    ]
    out_specs = [
        pl.BlockSpec((tl, cd), lambda *g: (tile(*g), 0)),
        pl.BlockSpec((None, 2, cd), lambda *g: (tile(*g) // tiles_per_seq, 0, 0)),
    ]
    out_shape = [
        jax.ShapeDtypeStruct((m, cd), BF16),
        jax.ShapeDtypeStruct((m // seq, 2, cd), F32),
    ]
    return n_items, in_specs, (proj_b, proj_b, proj_b, proj_b, proj_b, w_conv), out_specs, out_shape


def _conv_prompt(proj_b, w_conv, seq):
    n_items, in_specs, args, out_specs, out_shape = _conv_prompt_specs(lambda t: t, proj_b, w_conv, seq)
    tiles_per_seq = seq // min(CONV_ITEM_ROWS, seq)

    def conv_kernel(*refs):
        _conv_item(tiles_per_seq, n_items, pl.program_id(0), *refs)

    return pl.pallas_call(
        conv_kernel,
        grid=(n_items,),
        in_specs=in_specs,
        out_specs=out_specs,
        out_shape=out_shape,
        compiler_params=_params("arbitrary"),
        name="conv_prompt",
    )(*args)


def _conv_dec_kernel(cb_ref, cc_ref, ch_ref, s0_ref, s1_ref, w_ref, o_ref, st_ref):
    cd = cb_ref.shape[1]
    cch = cc_ref[...] * ch_ref[...]
    s1 = s1_ref[...]
    w = w_ref[...]
    u = w[0:1, :] * s0_ref[...] + w[1:2, :] * s1 + w[2:3, :] * cch
    o_ref[...] = (cb_ref[...] * u).astype(o_ref.dtype)
    st_ref[:, :cd] = s1
    st_ref[:, cd:] = cch


def _conv_decode(proj_b, w_conv, state2d):
    n = proj_b.shape[0]
    cd = CONV_DIM
    blk = lambda col: pl.BlockSpec((n, cd), lambda i: (0, col))
    return pl.pallas_call(
        _conv_dec_kernel,
        grid=(1,),
        in_specs=[blk(0), blk(1), blk(2), blk(0), blk(1), pl.BlockSpec((3, cd), lambda i: (0, 0))],
        out_specs=[pl.BlockSpec((n, cd), lambda i: (0, 0)), pl.BlockSpec((n, 2 * cd), lambda i: (0, 0))],
        out_shape=[jax.ShapeDtypeStruct((n, cd), BF16), jax.ShapeDtypeStruct((n, 2 * cd), F32)],
        compiler_params=_params("arbitrary"),
        name="conv_decode",
    )(proj_b, proj_b, proj_b, state2d, state2d, w_conv)


def _softmax_rows(logits):
    m = jnp.max(logits, axis=-1, keepdims=True)
    p = jnp.exp(logits - m)
    return p / jnp.sum(p, axis=-1, keepdims=True)


def _memattn_kernel(q_ref, k_ref, v_ref, o_ref):
    logits = _dot(q_ref[...].astype(BF16), k_ref[...].astype(BF16), NT) * (MEM_HD ** -0.5)
    p = _softmax_rows(logits)
    o_ref[...] = _dot(p.astype(BF16), v_ref[...].astype(BF16)).astype(o_ref.dtype)


def _memattn_prompt(proj_b, mk, mv, nb, seq):
    tl = min(2048, seq)
    nt = seq // tl
    h = MEM_HEADS
    qcol = 3 * CONV_DIM // MEM_HD
    return pl.pallas_call(
        _memattn_kernel,
        grid=(nb, h, nt),
        in_specs=[
            pl.BlockSpec((tl, MEM_HD), lambda b, hh, t: (b * nt + t, qcol + hh)),
            pl.BlockSpec((N_MEM, MEM_HD), lambda b, hh, t: (b, hh)),
            pl.BlockSpec((N_MEM, MEM_HD), lambda b, hh, t: (b, hh)),
        ],
        out_specs=pl.BlockSpec((tl, MEM_HD), lambda b, hh, t: (b * nt + t, hh)),
        out_shape=jax.ShapeDtypeStruct((nb * seq, MEM_DIM), BF16),
        compiler_params=_params("parallel", "parallel", "parallel"),
        name="memattn_prompt",
    )(proj_b, mk, mv)


MEMATTN_DEC_ROWS = 2


MEM_HD_CHUNKS = MEM_HD // LANES
MEM_TILE_ROWS = MEM_HD_CHUNKS * MEM_HEADS
assert MEM_HD_CHUNKS == 2 and MEM_TILE_ROWS == 8


def _heads_to_tile(x):
    lead = x.shape[:-2]
    x = x.reshape(lead + (MEM_HEADS, MEM_HD_CHUNKS, LANES))
    return jnp.swapaxes(x, -3, -2).reshape(lead + (MEM_TILE_ROWS, LANES))


def _tile_to_heads(x):
    lead = x.shape[:-2]
    x = x.reshape(lead + (MEM_HD_CHUNKS, MEM_HEADS, LANES))
    return jnp.swapaxes(x, -3, -2).reshape(lead + (MEM_HEADS, MEM_HD))


def _memattn_dec_item(item, q_ref, k_ref, v_ref, o_ref):
    def body(i, carry):
        q = q_ref[i] * (MEM_HD ** -0.5)
        prod = k_ref[i] * q[None]
        prod = prod + pltpu.roll(prod, MEM_HEADS, 1)
        logits = jnp.sum(prod, axis=-1, keepdims=True)
        e = jnp.exp(logits - jnp.max(logits, axis=0, keepdims=True))
        o_ref[i] = jnp.sum(e * v_ref[i], axis=0) / jnp.sum(e, axis=0)
        return carry

    lax.fori_loop(0, q_ref.shape[0], body, 0, unroll=True)


def _memattn_decode_specs(item, q_t, k_t, v_t):
    n = q_t.shape[0]
    r = MEMATTN_DEC_ROWS
    n_items = n // r
    tile = (MEM_TILE_ROWS, LANES)

    def at(ndim):
        return lambda *grid_idx: (jnp.minimum(item(*grid_idx), n_items - 1),) + (0,) * (ndim - 1)

    in_specs = [
        pl.BlockSpec((r,) + tile, at(3)),
        pl.BlockSpec((r, N_MEM) + tile, at(4)),
        pl.BlockSpec((r, N_MEM) + tile, at(4)),
    ]
    out_specs = [pl.BlockSpec((r,) + tile, at(3))]
    out_shape = [jax.ShapeDtypeStruct((n,) + tile, F32)]
    return n_items, in_specs, (q_t, k_t, v_t), out_specs, out_shape


def _merge_kernel(a_ref, b_ref, m_ref, wa_ref, wb_ref, wm_ref, za_ref, zb_ref, zm_ref, o_ref):
    ya = _dot(a_ref[...].astype(BF16), wa_ref[...])
    yb = _dot(b_ref[...].astype(BF16), wb_ref[...])
    ym = _dot(m_ref[...].astype(BF16), wm_ref[...])
    gate = lambda z_ref: _sigmoid(z_ref[...].astype(F32))
    merged = gate(za_ref) * ya + gate(zb_ref) * yb + gate(zm_ref) * ym
    o_ref[...] = merged.astype(o_ref.dtype)


def _merge(o_gla, cbu, om, w_gla_out, w_conv_out, w_mem_out, proj_b):
    m = o_gla.shape[0]
    tm = min(1024, m)
    tn = 1024
    d = D_MODEL
    zoff = (3 * CONV_DIM + MEM_DIM) // tn
    zspec = lambda g: pl.BlockSpec((tm, tn), lambda i, j: (i, zoff + g * (d // tn) + j))
    return pl.pallas_call(
        _merge_kernel,
        grid=(m // tm, d // tn),
        in_specs=[
            pl.BlockSpec((tm, o_gla.shape[1]), lambda i, j: (i, 0)),
            pl.BlockSpec((tm, cbu.shape[1]), lambda i, j: (i, 0)),
            pl.BlockSpec((tm, om.shape[1]), lambda i, j: (i, 0)),
            pl.BlockSpec((w_gla_out.shape[0], tn), lambda i, j: (0, j)),
            pl.BlockSpec((w_conv_out.shape[0], tn), lambda i, j: (0, j)),
            pl.BlockSpec((w_mem_out.shape[0], tn), lambda i, j: (0, j)),
            zspec(0), zspec(1), zspec(2),
        ],
        out_specs=pl.BlockSpec((tm, tn), lambda i, j: (i, j)),
        out_shape=jax.ShapeDtypeStruct((m, d), BF16),
        compiler_params=_params("parallel", "parallel"),
        name="merge",
    )(o_gla, cbu, om, w_gla_out, w_conv_out, w_mem_out, proj_b, proj_b, proj_b)


def _wo_ln_kernel(x_ref, m_ref, w_ref, g_ref, b_ref, o_ref):
    tm = x_ref.shape[0]
    half = tm // 4 if tm % 512 == 0 else tm
    for r in range(0, tm, half):
        rows = pl.ds(r, half)
        y = DN_ALPHA * x_ref[rows, :] + _dot(m_ref[rows, :], w_ref[...])
        o_ref[rows, :] = _layer_norm(y, g_ref[...], b_ref[...])


def _wo_ln(x, merged, w_o, g, b):
    m, d = x.shape
    tm = min(512, m)
    return pl.pallas_call(
        _wo_ln_kernel,
        grid=(m // tm,),
        in_specs=[
            pl.BlockSpec((tm, d), lambda i: (i, 0)),
            pl.BlockSpec((tm, d), lambda i: (i, 0)),
            pl.BlockSpec((d, d), lambda i: (0, 0)),
            pl.BlockSpec((1, d), lambda i: (0, 0)),
            pl.BlockSpec((1, d), lambda i: (0, 0)),
        ],
        out_specs=pl.BlockSpec((tm, d), lambda i: (i, 0)),
        out_shape=jax.ShapeDtypeStruct((m, d), F32),
        compiler_params=_params("parallel"),
        name="wo_ln",
    )(x, merged, w_o, g, b)


def _ffn_act(hc, hu, valid, c0=0):
    h = hc * _sigmoid(hc) * hu
    return jnp.where(c0 + lax.broadcasted_iota(jnp.int32, (1, hc.shape[1]), 1) < valid, h, 0.0).astype(BF16)


def _ffn_down(h, valid, wd_ref, acc_ref, x_ref, g_ref, b_ref, o_ref, fi):
    wd = jnp.where(lax.broadcasted_iota(jnp.int32, (h.shape[1], 1), 0) < valid, wd_ref[...], 0.0)
    acc_ref[...] += _dot(h, wd)

    @pl.when(fi == pl.num_programs(1) - 1)
    def _():
        o_ref[...] = _layer_norm(DN_ALPHA * x_ref[...] + acc_ref[...], g_ref[...], b_ref[...])


def _ffn_kernel(seq, x_ref, xh_ref, wg_ref, wu_ref, wd_ref, wc_ref, bc_ref, g_ref, b_ref, *rest):
    dec_in, (o_ref, st_ref, od_ref, so_ref, xb_scr, acc_scr) = rest[:GLA_DEC_INPUTS], rest[GLA_DEC_INPUTS:]
    fi = pl.program_id(1)
    tm = x_ref.shape[0]

    @pl.when(fi == 0)
    def _():
        xb_scr[...] = x_ref[...].astype(BF16)
        acc_scr[...] = jnp.zeros_like(acc_scr)

    _gla_dec_item(*dec_in, od_ref, so_ref)
    xb = xb_scr[...]
    xh = xh_ref[...].astype(BF16)
    tf = wg_ref.shape[1]
    valid = D_FF - fi * tf
    seq_start = (pl.program_id(0) * tm) % seq == 0
    ri = lax.broadcasted_iota(jnp.int32, (tm, 1), 0)
    hs = []
    for c0 in range(0, tf, FFN_GATE_CHUNK):
        cs = slice(c0, c0 + FFN_GATE_CHUNK)
        wg = wg_ref[:, cs]
        hg = _dot(xb, wg)
        st_ref[:, cs] = hg[tm - 2:tm, :]
        halo = jnp.where(seq_start, 0.0, _dot(xh, wg))
        p1, p2 = _shift_rows(hg, halo, ri)
        wc = wc_ref[:, cs]
        hc = wc[0:1, :] * p2 + wc[1:2, :] * p1 + wc[2:3, :] * hg + bc_ref[:, cs]
        hs.append(_ffn_act(hc, _dot(xb, wu_ref[:, cs]), valid, c0))
    _ffn_down(jnp.concatenate(hs, axis=1), valid, wd_ref, acc_scr, x_ref, g_ref, b_ref, o_ref, fi)


def _ffn_prompt(x1, w_gate, w_up, w_down, w_conv, b_conv, g, b, seq, gla_dec_operands):
    m, d = x1.shape
    tm = min(512, seq)
    tf = FF_TILE
    t8 = tm // HALO_ROWS
    n_items, dec_in_specs, dec_args, dec_out_specs, dec_out_shape = _gla_decode_specs(
        lambda i, f: i * FF_TILES + f, *gla_dec_operands)
    assert n_items <= (m // tm) * FF_TILES, "not enough FFN steps to carry the decode GLA work items"
    y, tile_state, o_dec, s_dec = pl.pallas_call(
        functools.partial(_ffn_kernel, seq),
        grid=(m // tm, FF_TILES),
        in_specs=[
            pl.BlockSpec((tm, d), lambda i, f: (i, 0)),
            pl.BlockSpec((HALO_ROWS, d), lambda i, f: (jnp.maximum(i * t8 - 1, 0), 0)),
            pl.BlockSpec((d, tf), lambda i, f: (0, f)),
            pl.BlockSpec((d, tf), lambda i, f: (0, f)),
            pl.BlockSpec((tf, d), lambda i, f: (f, 0)),
            pl.BlockSpec((3, tf), lambda i, f: (0, f)),
            pl.BlockSpec((1, tf), lambda i, f: (0, f)),
            pl.BlockSpec((1, d), lambda i, f: (0, 0)),
            pl.BlockSpec((1, d), lambda i, f: (0, 0)),
        ] + dec_in_specs,
        out_specs=[pl.BlockSpec((tm, d), lambda i, f: (i, 0)),
                   pl.BlockSpec((None, 2, tf), lambda i, f: (i, 0, f))] + dec_out_specs,
        out_shape=[jax.ShapeDtypeStruct((m, d), F32),
                   jax.ShapeDtypeStruct((m // tm, 2, D_FF), F32)] + dec_out_shape,
        scratch_shapes=[pltpu.VMEM((tm, d), BF16), pltpu.VMEM((tm, d), F32)],
        compiler_params=_params("arbitrary", "arbitrary"),
        name="ffn_prompt_gla_decode",
    )(x1, x1, w_gate, w_up, w_down, w_conv, b_conv, g, b, *dec_args)
    tiles_per_seq = seq // tm
    return y, tile_state[tiles_per_seq - 1::tiles_per_seq], o_dec, s_dec


def _ffn_dec_kernel(x_ref, s0_ref, s1_ref, wg_ref, wu_ref, wd_ref, wc_ref, bc_ref, g_ref, b_ref,
                    o_ref, hg_ref, xb_scr, acc_scr):
    fi = pl.program_id(1)

    @pl.when(fi == 0)
    def _():
        xb_scr[...] = x_ref[...].astype(BF16)
        acc_scr[...] = jnp.zeros_like(acc_scr)

    xb = xb_scr[...]
    hg = _dot(xb, wg_ref[...])
    hg_ref[...] = hg
    wc = wc_ref[...]
    hc = wc[0:1, :] * s0_ref[...] + wc[1:2, :] * s1_ref[...] + wc[2:3, :] * hg + bc_ref[...]
    valid = D_FF - fi * hc.shape[1]
    _ffn_down(_ffn_act(hc, _dot(xb, wu_ref[...]), valid), valid, wd_ref, acc_scr, x_ref, g_ref, b_ref, o_ref, fi)


def _ffn_decode(x1, s0, s1, w_gate, w_up, w_down, w_conv, b_conv, g, b):
    m, d = x1.shape
    tf = FF_TILE
    return pl.pallas_call(
        _ffn_dec_kernel,
        grid=(1, FF_TILES),
        in_specs=[
            pl.BlockSpec((m, d), lambda i, f: (0, 0)),
            pl.BlockSpec((m, tf), lambda i, f: (0, f)),
            pl.BlockSpec((m, tf), lambda i, f: (0, f)),
            pl.BlockSpec((d, tf), lambda i, f: (0, f)),
            pl.BlockSpec((d, tf), lambda i, f: (0, f)),
            pl.BlockSpec((tf, d), lambda i, f: (f, 0)),
            pl.BlockSpec((3, tf), lambda i, f: (0, f)),
            pl.BlockSpec((1, tf), lambda i, f: (0, f)),
            pl.BlockSpec((1, d), lambda i, f: (0, 0)),
            pl.BlockSpec((1, d), lambda i, f: (0, 0)),
        ],
        out_specs=[pl.BlockSpec((m, d), lambda i, f: (0, 0)), pl.BlockSpec((m, tf), lambda i, f: (0, f))],
        out_shape=[jax.ShapeDtypeStruct((m, d), F32), jax.ShapeDtypeStruct((m, D_FF), F32)],
        scratch_shapes=[pltpu.VMEM((m, d), BF16), pltpu.VMEM((m, d), F32)],
        compiler_params=_params("parallel", "arbitrary"),
        name="ffn_decode",
    )(x1, s0, s1, w_gate, w_up, w_down, w_conv, b_conv, g, b)


def kernel(x_prompt, x_sample, mem_prompt, cache_mem_k, cache_mem_v, state_gla, state_conv, state_ffn_conv, w_in, w_gla_a2, b_gla_a2, g_gla_norm, w_gla_out, w_conv, w_conv_out, w_mem_k, w_mem_v, w_mem_out, w_o, ln1_g, ln1_b, w_ffn_gate, w_ffn_up, w_ffn_conv, b_ffn_conv, w_ffn_down, ln2_g, ln2_b):
    nb, seq, d = x_prompt.shape
    ns = x_sample.shape[0]

    w_in_t = jnp.swapaxes(w_in[0], 0, 1)
    wa2p = jnp.pad(w_gla_a2[0], ((0, LANES - GLA_RANK), (0, 0)))
    ba2 = b_gla_a2
    gn = g_gla_norm
    w_fc = w_ffn_conv[0]
    b_fc = b_ffn_conv
    w_cv = w_conv[0]

    def mix_in(x2d, tag, dtype, riders_b=(), riders_a=()):
        pr, xb = _proj(x2d, w_in_t, "proj_r_" + tag, COLS_A, LANES, LANES, transposed=True, emit_x=True)
        pb = _proj(xb, w_in_t, "proj_b_" + tag, COLS_A, COLS_B, shift=GLA_RANK, transposed=True,
                   riders=riders_b, out_dtype=dtype)
        pa = _proj(xb, w_in_t, "proj_a_" + tag, 0, COLS_A, transposed=True, riders=riders_a, out_dtype=dtype)
        return pa, pr, pb

    xs = x_sample.reshape(ns, d)
    sa, sr, sb = mix_in(xs, "s", F32)
    mq_s = sb[:, 3 * CONV_DIM:3 * CONV_DIM + MEM_DIM].reshape(ns, MEM_HEADS, MEM_HD)
    memattn_dec = (_memattn_dec_item, _memattn_decode_specs,
                   (_heads_to_tile(mq_s), _heads_to_tile(cache_mem_k[0]), _heads_to_tile(cache_mem_v[0])))

    xp = x_prompt.reshape(nb * seq, d)
    memb = mem_prompt.reshape(nb * N_MEM, d).astype(BF16)
    mk = _proj(memb, w_mem_k[0], "mem_k", tn=MEM_HD)
    mv = _proj(memb, w_mem_v[0], "mem_v", tn=MEM_HD)
    steps_a, steps_b = _proj_steps(nb * seq, COLS_A), _proj_steps(nb * seq, COLS_B)
    riders_b = (memattn_dec, _cast_rider(w_ffn_gate[0], steps_b), _cast_rider(w_ffn_up[0], steps_b))
    riders_a = tuple(_cast_rider(w[0], steps_a) for w in (w_ffn_down, w_gla_out, w_conv_out, w_mem_out, w_o))
    (pa, w_fd, w_go, w_co, w_mo, w_oo), pr, (pb, om_s, w_fg, w_fu) = mix_in(xp, "p", BF16, riders_b, riders_a)

    def mix_out(x2d, o_gla, cbu, om, proj_b):
        merged = _merge(o_gla, cbu, om, w_go, w_co, w_mo, proj_b)
        return _wo_ln(x2d, merged, w_oo, ln1_g, ln1_b)

    o_gla, p_gla = _gla_prompt(pa, pr, wa2p, ba2, gn, nb, seq)
    cbu, p_conv = _conv_prompt(pb, w_cv, seq)
    om = _memattn_prompt(pb, mk, mv, nb, seq)
    x1 = mix_out(xp, o_gla, cbu, om, pb)
    yp, p_ffn, o_gla_s, s_gla = _ffn_prompt(x1, w_fg, w_fu, w_fd, w_fc, b_fc, ln2_g, ln2_b, seq,
                                            (sa, sr, wa2p, ba2, gn, state_gla[0]))
    o_gla_s = o_gla_s.reshape(ns, GLA_HEADS * GLA_HV)

    cbu_s, s_conv = _conv_decode(sb, w_cv, state_conv[0].reshape(ns, 2 * CONV_DIM))
    om_s = _tile_to_heads(om_s).reshape(ns, MEM_DIM)
    x1s = mix_out(xs, o_gla_s, cbu_s, om_s, sb)
    f0 = state_ffn_conv[0][:, 0, :]
    f1 = state_ffn_conv[0][:, 1, :]
    ys, hg_s = _ffn_decode(x1s, f0, f1, w_fg, w_fu, w_fd, w_fc, b_fc, ln2_g, ln2_b)
    s_ffn = jnp.stack([f1, hg_s], axis=1)

    return (yp.reshape(nb, seq, d), ys.reshape(ns, 1, d),
            mk.reshape(1, nb, N_MEM, MEM_HEADS, MEM_HD), mv.reshape(1, nb, N_MEM, MEM_HEADS, MEM_HD),
            p_gla[None], p_conv[None], p_ffn.reshape(1, nb, 2, D_FF),
            s_gla[None], s_conv.reshape(1, ns, 2, CONV_DIM), s_ffn[None])
```

```python
import functools

import jax
import jax.numpy as jnp
from jax import lax
from jax.experimental import pallas as pl
from jax.experimental.pallas import tpu as pltpu

F32 = jnp.float32
BF16 = jnp.bfloat16

D_MODEL = 2048
GLA_HEADS = 4
GLA_HK = 256
GLA_HV = 512
GLA_RANK = 16
GLA_GATE_NORM = 16.0
CONV_DIM = 1024
N_MEM = 256
MEM_HEADS = 4
MEM_HD = 256
MEM_DIM = 1024
D_FF = 5504
DEPTH = 1
DN_ALPHA = (2 * DEPTH) ** 0.25
LN_EPS = 1e-5
RMS_EPS = 1e-6

LANES = 128
SUBLANES = 8
LOG2_E = 1.4426950408889634
FF_TILE = 512
FF_TILES = -(-D_FF // FF_TILE)
FFN_GATE_CHUNK = FF_TILE
COLS_A = 2 * GLA_HEADS * GLA_HK + 2 * GLA_HEADS * GLA_HV
COLS_B = 3 * CONV_DIM + MEM_DIM + 3 * D_MODEL
GLA_CHUNK = 256
VMEM_LIMIT = 56 * 1024 * 1024

NN = (((1,), (0,)), ((), ()))
NT = (((1,), (1,)), ((), ()))
TN = (((0,), (0,)), ((), ()))


def _dot(a, b, dims=NN):
    return lax.dot_general(a, b, dims, preferred_element_type=F32)


def _params(*sem):
    return pltpu.CompilerParams(dimension_semantics=sem, vmem_limit_bytes=VMEM_LIMIT)


def _split2(x):
    hi = x.astype(BF16)
    lo = (x - hi.astype(F32)).astype(BF16)
    return hi, lo


def _split3(x):
    hi = x.astype(BF16)
    r = x - hi.astype(F32)
    mid = r.astype(BF16)
    lo = (r - mid.astype(F32)).astype(BF16)
    return hi, mid, lo


def _sigmoid(z):
    return 0.5 * jnp.tanh(0.5 * z) + 0.5


def _silu(z):
    h = 0.5 * z
    return h * jnp.tanh(h) + h


def _layer_norm(y, g, b):
    mu = jnp.mean(y, axis=-1, keepdims=True)
    d = y - mu
    var = jnp.mean(d * d, axis=-1, keepdims=True)
    return d * lax.rsqrt(var + LN_EPS) * g + b


def _log2_decay(z):
    return (jnp.minimum(z, 0.0) - jnp.log(1.0 + jnp.exp(-jnp.abs(z)))) * (LOG2_E / GLA_GATE_NORM)


def _log_decay(alr, wa2, ba2):
    ah, al = _split2(alr)
    wh, wl = _split2(wa2)
    return _log2_decay(_dot(ah, wh) + _dot(ah, wl) + _dot(al, wh) + ba2)


def _rms_gate(o, gn, g):
    o = o * lax.rsqrt(jnp.mean(o * o, axis=-1, keepdims=True) + RMS_EPS) * gn
    return o * _silu(g)


PROJ_SHIFT_ROWS = 16
PROJ_TILE = 1024
BF16_SUBLANES = 16


def _proj_kernel(transposed, shift, riders, emit_x, x_ref, w_ref, *rest):
    rest = list(rest)
    wx_ref = rest.pop(0) if shift else None
    wb_scr = rest.pop()
    rider_in = [[rest.pop(0) for _ in range(n_in)] for _, n_in, _ in riders]
    o_ref = rest.pop(0)
    ox_ref = rest.pop(0) if emit_x else None
    rider_out = [[rest.pop(0) for _ in range(n_out)] for _, _, n_out in riders]
    item = pl.program_id(0) * pl.num_programs(1) + pl.program_id(1)

    @pl.when(pl.program_id(1) == 0)
    def _():
        w = w_ref[...]
        if shift:
            w = jnp.concatenate([w[shift:], wx_ref[:shift]], axis=0)
        w = w.astype(BF16)
        wb_scr[...] = w

    xb = x_ref[...].astype(BF16)
    if emit_x:
        ox_ref[...] = xb
    o_ref[...] = _dot(xb, wb_scr[...], NT if transposed else NN).astype(o_ref.dtype)
    for (body, _, _), r_in, r_out in zip(riders, rider_in, rider_out):
        body(item, *r_in, *r_out)


def _proj(x, w, name, col0=0, n=None, tn=PROJ_TILE, shift=0, transposed=False, riders=(), out_dtype=F32,
          emit_x=False):
    m, k = x.shape
    if n is None:
        n = w.shape[0] if transposed else w.shape[1]
    tm = min(m, PROJ_TILE)
    tn = min(n, tn)
    c0 = col0 // tn
    if transposed:
        w_spec = pl.BlockSpec((tn, k), lambda j, i: (c0 + j, 0))
    else:
        assert not shift
        w_spec = pl.BlockSpec((k, tn), lambda j, i: (0, c0 + j))
    in_specs = [pl.BlockSpec((tm, k), lambda j, i: (i, 0)), w_spec]
    args = [x, w]
    if shift:
        r = PROJ_SHIFT_ROWS
        assert shift <= r and shift % 8 == 0 and tn % r == 0 and col0 % r == 0
        in_specs.append(pl.BlockSpec((r, k), lambda j, i: (col0 // r + (j + 1) * (tn // r), 0)))
        args.append(w)
    out_specs = [pl.BlockSpec((tm, tn), lambda j, i: (i, j))]
    out_shape = [jax.ShapeDtypeStruct((m, n), out_dtype)]
    if emit_x:
        assert n == tn, "each x tile must be visited once"
        out_specs.append(pl.BlockSpec((tm, k), lambda j, i: (i, 0)))
        out_shape.append(jax.ShapeDtypeStruct((m, k), BF16))
    kernel_riders = []
    for body, specs_fn, operands in riders:
        n_items, r_in_specs, r_args, r_out_specs, r_out_shape = specs_fn(lambda j, i: j * (m // tm) + i, *operands)
        assert n_items <= (n // tn) * (m // tm), "not enough projection steps to carry the rider's work items"
        in_specs += r_in_specs
        args += list(r_args)
        out_specs += r_out_specs
        out_shape += r_out_shape
        kernel_riders.append((body, len(r_in_specs), len(r_out_specs)))
    outs = pl.pallas_call(
        functools.partial(_proj_kernel, transposed, shift, tuple(kernel_riders), emit_x),
        grid=(n // tn, m // tm),
        in_specs=in_specs,
        out_specs=out_specs,
        out_shape=out_shape,
        scratch_shapes=[pltpu.VMEM((tn, k) if transposed else (k, tn), BF16)],
        compiler_params=_params("arbitrary" if riders else "parallel", "arbitrary"),
        name=name,
    )(*args)
    return outs if len(outs) > 1 else outs[0]


def _proj_steps(m, n):
    return (n // min(n, PROJ_TILE)) * (m // min(m, PROJ_TILE))


def _cast_item(item, x_ref, o_ref):
    o_ref[...] = x_ref[...].astype(o_ref.dtype)


def _cast_specs(item, w, n_steps):
    r, c = w.shape
    rows = -(-r // n_steps)
    rows = -(-rows // BF16_SUBLANES) * BF16_SUBLANES
    n_items = -(-r // rows)
    at = lambda *g: (jnp.minimum(item(*g), n_items - 1), 0)
    return (n_items, [pl.BlockSpec((rows, c), at)], (w,), [pl.BlockSpec((rows, c), at)],
            [jax.ShapeDtypeStruct((r, c), BF16)])


def _cast_rider(w, n_steps):
    return (_cast_item, _cast_specs, (w, n_steps))


def _gla_levels(c):
    return c.bit_length() - 1


def _init_gla_masks(mask_scr, tri_scr):
    c = tri_scr.shape[0]
    rr = lax.broadcasted_iota(jnp.int32, (c, c), 0)
    cc = lax.broadcasted_iota(jnp.int32, (c, c), 1)
    tri_scr[...] = (rr >= cc).astype(BF16)
    mask_scr[0] = (rr == cc).astype(F32)
    for l in range(_gla_levels(c)):
        b = 1 << l
        pair = (rr // (2 * b) == cc // (2 * b)) & ((rr // b) % 2 == 1) & ((cc // b) % 2 == 0)
        mask_scr[1 + l] = pair.astype(F32)


def _intra_scores(q, k, la, bc, mask_scr):
    c, dk = q.shape
    hc = c // 2
    ri = lax.broadcasted_iota(jnp.int32, (c, 1), 0)
    diag = jnp.sum(q * k, axis=1, keepdims=True)
    eye = mask_scr[0, :hc, :hc]
    a0, a1 = diag[:hc] * eye, diag[hc:] * eye
    for l in range(_gla_levels(c)):
        b = 1 << l
        if b < SUBLANES:
            upper = (ri // b) % 2 == 1
            if b == 1:
                e = jnp.where(upper, la, 0.0)
            elif b == 2:
                m4 = ri % 4
                la_prev = pltpu.roll(la, 1, 0)
                la_next = pltpu.roll(la, c - 1, 0)
                e = jnp.where(m4 == 2, la, jnp.where(m4 == 3, la + la_prev, jnp.where(m4 == 0, la_next, 0.0)))
            else:
                ref = bc.reshape(c // (2 * b), 2 * b, dk)[:, b - 1:b, :]
                ref = jnp.broadcast_to(ref, (c // (2 * b), 2 * b, dk)).reshape(c, dk)
                dlt = bc - ref
                e = jnp.minimum(dlt, -dlt)
            w = jnp.where(upper, q, k) * jnp.exp2(e)
        else:
            parts = []
            for r0 in range(0, c, 2 * b):
                ref = bc[r0 + b - 1:r0 + b, :]
                parts.append(k[r0:r0 + b] * jnp.exp2(ref - bc[r0:r0 + b]))
                parts.append(q[r0 + b:r0 + 2 * b] * jnp.exp2(bc[r0 + b:r0 + 2 * b] - ref))
            w = jnp.concatenate(parts, axis=0)
        w = w.astype(BF16)
        w0, w1 = w[:hc], w[hc:]
        if b == hc:
            return a0, a1, _dot(w1, w0, NT)
        mask = mask_scr[1 + l, :hc, :hc]
        a0 = a0 + _dot(w0, w0, NT) * mask
        a1 = a1 + _dot(w1, w1, NT) * mask


def _gla_kernel(q_ref, k_ref, v_ref, g_ref, alr_ref, wa2_ref, ba2_ref, gn_ref, o_ref, sfin_ref,
                s_scr, mask_scr, tri_scr):
    ci = pl.program_id(2)

    @pl.when(ci == 0)
    def _():
        s_scr[...] = jnp.zeros_like(s_scr)
        _init_gla_masks(mask_scr, tri_scr)

    c = q_ref.shape[0]
    tri = tri_scr[...]
    ones = jnp.ones((c, LANES), BF16)
    alr = alr_ref[...]
    gn = gn_ref[...]
    for hh in range(s_scr.shape[0]):
        ks = slice(hh * GLA_HK, (hh + 1) * GLA_HK)
        vs = slice(hh * GLA_HV, (hh + 1) * GLA_HV)
        la = _log_decay(alr, wa2_ref[:, ks], ba2_ref[:, ks])
        lh, lm, ll = _split3(la)
        bc = _dot(tri, lh) + _dot(tri, lm) + _dot(tri, ll)
        b_last = bc[c - 1:c, :]
        bl_col = _dot(lh, ones, TN) + _dot(lm, ones, TN) + _dot(ll, ones, TN)
        dec_col = jnp.exp2(bl_col)
        dec_col = jnp.concatenate([dec_col] * (GLA_HV // LANES), axis=1)

        q = q_ref[:, ks].astype(F32) * (GLA_HK ** -0.5)
        k = k_ref[:, ks].astype(F32)
        vb = v_ref[:, vs].astype(BF16)
        s = s_scr[hh]
        o = _dot((q * jnp.exp2(bc)).astype(BF16), s.astype(BF16))
        a0, a1, a10 = _intra_scores(q, k, la, bc, mask_scr)
        hc = c // 2
        v0, v1 = vb[:hc], vb[hc:]
        o = o + jnp.concatenate([_dot(a0.astype(BF16), v0),
                                 _dot(a10.astype(BF16), v0) + _dot(a1.astype(BF16), v1)], axis=0)
        kd = (k * jnp.exp2(b_last - bc)).astype(BF16)
        s_new = dec_col * s + _dot(kd, vb, TN)
        s_scr[hh] = s_new
        o_ref[:, vs] = _rms_gate(o, gn, g_ref[:, vs].astype(F32)).astype(o_ref.dtype)

    @pl.when(ci == pl.num_programs(2) - 1)
    def _():
        sfin_ref[...] = s_scr[...]


GLA_HEADS_PER_STEP = 4


def _gla_prompt(proj_a, alr, wa2p, ba2, gn, nb, seq):
    c = min(GLA_CHUNK, seq)
    nc = seq // c
    hp = GLA_HEADS_PER_STEP
    ng = GLA_HEADS // hp
    wk, wv = hp * GLA_HK, hp * GLA_HV
    row = lambda b, hg, ci: b * nc + ci
    return pl.pallas_call(
        _gla_kernel,
        grid=(nb, ng, nc),
        in_specs=[
            pl.BlockSpec((c, wk), lambda b, hg, ci: (row(b, hg, ci), hg)),
            pl.BlockSpec((c, wk), lambda b, hg, ci: (row(b, hg, ci), ng + hg)),
            pl.BlockSpec((c, wv), lambda b, hg, ci: (row(b, hg, ci), ng + hg)),
            pl.BlockSpec((c, wv), lambda b, hg, ci: (row(b, hg, ci), 2 * ng + hg)),
            pl.BlockSpec((c, LANES), lambda b, hg, ci: (row(b, hg, ci), 0)),
            pl.BlockSpec((LANES, wk), lambda b, hg, ci: (0, hg)),
            pl.BlockSpec((1, wk), lambda b, hg, ci: (0, hg)),
            pl.BlockSpec((1, GLA_HV), lambda b, hg, ci: (0, 0)),
        ],
        out_specs=[
            pl.BlockSpec((c, wv), lambda b, hg, ci: (row(b, hg, ci), hg)),
            pl.BlockSpec((None, hp, GLA_HK, GLA_HV), lambda b, hg, ci: (b, hg, 0, 0)),
        ],
        out_shape=[
            jax.ShapeDtypeStruct((nb * seq, GLA_HEADS * GLA_HV), BF16),
            jax.ShapeDtypeStruct((nb, GLA_HEADS, GLA_HK, GLA_HV), F32),
        ],
        scratch_shapes=[pltpu.VMEM((hp, GLA_HK, GLA_HV), F32),
                        pltpu.VMEM((1 + _gla_levels(c), c, c), F32),
                        pltpu.VMEM((c, c), BF16)],
        compiler_params=_params("parallel", "parallel", "arbitrary"),
        name="gla_prompt",
    )(proj_a, proj_a, proj_a, proj_a, alr, wa2p, ba2, gn)


GLA_DEC_ROWS = 4


def _gla_dec_item(q_ref, k_ref, v_ref, g_ref, alr_ref, wa2_ref, ba2_ref, gn_ref, s_ref, o_ref, so_ref):
    nb = q_ref.shape[0]
    alr = alr_ref[...]
    wa2 = wa2_ref[...]
    z = ba2_ref[...]
    for r in range(GLA_RANK):
        z = z + alr[:, r:r + 1] * wa2[r:r + 1, :]
    la = _log2_decay(z)
    rep = LANES // nb
    to_cols = lambda x: jnp.concatenate([x] * rep, axis=0).T
    a_t = to_cols(jnp.exp2(la))
    k_t = to_cols(k_ref[...])
    q_t = to_cols(q_ref[...] * (GLA_HK ** -0.5))
    v = v_ref[...]
    rows = []
    for j in range(nb):
        s_new = a_t[:, j:j + 1] * s_ref[j] + k_t[:, j:j + 1] * v[j:j + 1, :]
        so_ref[j] = s_new
        rows.append(jnp.sum(q_t[:, j:j + 1] * s_new, axis=0, keepdims=True))
    o = jnp.concatenate(rows, axis=0)
    o_ref[...] = _rms_gate(o, gn_ref[...], g_ref[...]).astype(o_ref.dtype)


GLA_DEC_INPUTS = 9


def _gla_decode_specs(item, proj_a, alr, wa2p, ba2, gn, state):
    nseq = proj_a.shape[0]
    h = GLA_HEADS
    r = GLA_DEC_ROWS
    n_items = (nseq // r) * h
    proj3 = proj_a.reshape(nseq // r, r, proj_a.shape[1])
    alr3 = alr.reshape(nseq // r, r, LANES)

    def at(fn):
        def index_map(*grid_idx):
            it = jnp.minimum(item(*grid_idx), n_items - 1)
            return fn(it // h, it % h)
        return index_map

    in_specs = [
        pl.BlockSpec((None, r, GLA_HK), at(lambda b, hh: (b, 0, hh))),
        pl.BlockSpec((None, r, GLA_HK), at(lambda b, hh: (b, 0, h + hh))),
        pl.BlockSpec((None, r, GLA_HV), at(lambda b, hh: (b, 0, h + hh))),
        pl.BlockSpec((None, r, GLA_HV), at(lambda b, hh: (b, 0, 2 * h + hh))),
        pl.BlockSpec((None, r, LANES), at(lambda b, hh: (b, 0, 0))),
        pl.BlockSpec((LANES, GLA_HK), at(lambda b, hh: (0, hh))),
        pl.BlockSpec((1, GLA_HK), at(lambda b, hh: (0, hh))),
        pl.BlockSpec((1, GLA_HV), at(lambda b, hh: (0, 0))),
        pl.BlockSpec((r, None, GLA_HK, GLA_HV), at(lambda b, hh: (b, hh, 0, 0))),
    ]
    out_specs = [
        pl.BlockSpec((None, r, GLA_HV), at(lambda b, hh: (b, 0, hh))),
        pl.BlockSpec((r, None, GLA_HK, GLA_HV), at(lambda b, hh: (b, hh, 0, 0))),
    ]
    out_shape = [
        jax.ShapeDtypeStruct((nseq // r, r, h * GLA_HV), F32),
        jax.ShapeDtypeStruct(state.shape, F32),
    ]
    args = (proj3, proj3, proj3, proj3, alr3, wa2p, ba2, gn, state)
    assert len(in_specs) == GLA_DEC_INPUTS
    return n_items, in_specs, args, out_specs, out_shape


HALO_ROWS = 16


def _shift_rows(x, halo, ri):
    h1 = halo[HALO_ROWS - 1:HALO_ROWS, :]
    h2 = halo[HALO_ROWS - 2:HALO_ROWS - 1, :]
    p1 = jnp.where(ri == 0, h1, pltpu.roll(x, 1, 0))
    p2 = jnp.where(ri == 0, h2, jnp.where(ri == 1, h1, pltpu.roll(x, 2, 0)))
    return p1, p2


CONV_ITEM_ROWS = 1024


def _conv_item(tiles_per_seq, n_items, item, cb_ref, cc_ref, ch_ref, cch_ref, chh_ref, w_ref, o_ref, st_ref):
    tl, cd = cb_ref.shape
    tile = jnp.minimum(item, n_items - 1)
    seq_start = tile % tiles_per_seq == 0
    ri = lax.broadcasted_iota(jnp.int32, (tl, 1), 0)
    for c0 in range(0, cd, LANES):
        cs = slice(c0, c0 + LANES)
        cch = cc_ref[:, cs].astype(F32) * ch_ref[:, cs].astype(F32)
        halo = cch_ref[:, cs].astype(F32) * chh_ref[:, cs].astype(F32)
        halo = jnp.where(seq_start, 0.0, halo)
        p1, p2 = _shift_rows(cch, halo, ri)
        w = w_ref[:, cs]
        u = w[0:1, :] * p2 + w[1:2, :] * p1 + w[2:3, :] * cch
        o_ref[:, cs] = (cb_ref[:, cs].astype(F32) * u).astype(o_ref.dtype)
        st_ref[:, cs] = cch[tl - 2:tl, :]


def _conv_prompt_specs(item, proj_b, w_conv, seq):
    m = proj_b.shape[0]
    tl = min(CONV_ITEM_ROWS, seq)
    assert seq % tl == 0
    n_items = m // tl
    tiles_per_seq = seq // tl
    th = tl // HALO_ROWS
    cd = CONV_DIM
    tile = lambda *g: jnp.minimum(item(*g), n_items - 1)
    in_specs = [
        pl.BlockSpec((tl, cd), lambda *g: (tile(*g), 0)),
        pl.BlockSpec((tl, cd), lambda *g: (tile(*g), 1)),
        pl.BlockSpec((tl, cd), lambda *g: (tile(*g), 2)),
        pl.BlockSpec((HALO_ROWS, cd), lambda *g: (jnp.maximum(tile(*g) * th - 1, 0), 1)),
        pl.BlockSpec((HALO_ROWS, cd), lambda *g: (jnp.maximum(tile(*g) * th - 1, 0), 2)),
        pl.BlockSpec((3, cd), lambda *g: (0, 0)),
    ]
    out_specs = [
        pl.BlockSpec((tl, cd), lambda *g: (tile(*g), 0)),
        pl.BlockSpec((None, 2, cd), lambda *g: (tile(*g) // tiles_per_seq, 0, 0)),
    ]
    out_shape = [
        jax.ShapeDtypeStruct((m, cd), BF16),
        jax.ShapeDtypeStruct((m // seq, 2, cd), F32),
    ]
    return n_items, in_specs, (proj_b, proj_b, proj_b, proj_b, proj_b, w_conv), out_specs, out_shape


def _conv_prompt(proj_b, w_conv, seq):
    n_items, in_specs, args, out_specs, out_shape = _conv_prompt_specs(lambda t: t, proj_b, w_conv, seq)
    tiles_per_seq = seq // min(CONV_ITEM_ROWS, seq)

    def conv_kernel(*refs):
        _conv_item(tiles_per_seq, n_items, pl.program_id(0), *refs)

    return pl.pallas_call(
        conv_kernel,
        grid=(n_items,),
        in_specs=in_specs,
        out_specs=out_specs,
        out_shape=out_shape,
        compiler_params=_params("arbitrary"),
        name="conv_prompt",
    )(*args)


def _conv_dec_kernel(cb_ref, cc_ref, ch_ref, s0_ref, s1_ref, w_ref, o_ref, st_ref):
    cd = cb_ref.shape[1]
    cch = cc_ref[...] * ch_ref[...]
    s1 = s1_ref[...]
    w = w_ref[...]
    u = w[0:1, :] * s0_ref[...] + w[1:2, :] * s1 + w[2:3, :] * cch
    o_ref[...] = (cb_ref[...] * u).astype(o_ref.dtype)
    st_ref[:, :cd] = s1
    st_ref[:, cd:] = cch


def _conv_decode(proj_b, w_conv, state2d):
    n = proj_b.shape[0]
    cd = CONV_DIM
    blk = lambda col: pl.BlockSpec((n, cd), lambda i: (0, col))
    return pl.pallas_call(
        _conv_dec_kernel,
        grid=(1,),
        in_specs=[blk(0), blk(1), blk(2), blk(0), blk(1), pl.BlockSpec((3, cd), lambda i: (0, 0))],
        out_specs=[pl.BlockSpec((n, cd), lambda i: (0, 0)), pl.BlockSpec((n, 2 * cd), lambda i: (0, 0))],
        out_shape=[jax.ShapeDtypeStruct((n, cd), BF16), jax.ShapeDtypeStruct((n, 2 * cd), F32)],
        compiler_params=_params("arbitrary"),
        name="conv_decode",
    )(proj_b, proj_b, proj_b, state2d, state2d, w_conv)


def _softmax_rows(logits):
    m = jnp.max(logits, axis=-1, keepdims=True)
    p = jnp.exp(logits - m)
    return p / jnp.sum(p, axis=-1, keepdims=True)


def _memattn_kernel(q_ref, k_ref, v_ref, o_ref):
    logits = _dot(q_ref[...].astype(BF16), k_ref[...].astype(BF16), NT) * (MEM_HD ** -0.5)
    p = _softmax_rows(logits)
    o_ref[...] = _dot(p.astype(BF16), v_ref[...].astype(BF16)).astype(o_ref.dtype)


def _memattn_prompt(proj_b, mk, mv, nb, seq):
    tl = min(2048, seq)
    nt = seq // tl
    h = MEM_HEADS
    qcol = 3 * CONV_DIM // MEM_HD
    return pl.pallas_call(
        _memattn_kernel,
        grid=(nb, h, nt),
        in_specs=[
            pl.BlockSpec((tl, MEM_HD), lambda b, hh, t: (b * nt + t, qcol + hh)),
            pl.BlockSpec((N_MEM, MEM_HD), lambda b, hh, t: (b, hh)),
            pl.BlockSpec((N_MEM, MEM_HD), lambda b, hh, t: (b, hh)),
        ],
        out_specs=pl.BlockSpec((tl, MEM_HD), lambda b, hh, t: (b * nt + t, hh)),
        out_shape=jax.ShapeDtypeStruct((nb * seq, MEM_DIM), BF16),
        compiler_params=_params("parallel", "parallel", "parallel"),
        name="memattn_prompt",
    )(proj_b, mk, mv)


MEMATTN_DEC_ROWS = 2


MEM_HD_CHUNKS = MEM_HD // LANES
MEM_TILE_ROWS = MEM_HD_CHUNKS * MEM_HEADS
assert MEM_HD_CHUNKS == 2 and MEM_TILE_ROWS == 8


def _heads_to_tile(x):
    lead = x.shape[:-2]
    x = x.reshape(lead + (MEM_HEADS, MEM_HD_CHUNKS, LANES))
    return jnp.swapaxes(x, -3, -2).reshape(lead + (MEM_TILE_ROWS, LANES))


def _tile_to_heads(x):
    lead = x.shape[:-2]
    x = x.reshape(lead + (MEM_HD_CHUNKS, MEM_HEADS, LANES))
    return jnp.swapaxes(x, -3, -2).reshape(lead + (MEM_HEADS, MEM_HD))


def _memattn_dec_item(item, q_ref, k_ref, v_ref, o_ref):
    def body(i, carry):
        q = q_ref[i] * (MEM_HD ** -0.5)
        prod = k_ref[i] * q[None]
        prod = prod + pltpu.roll(prod, MEM_HEADS, 1)
        logits = jnp.sum(prod, axis=-1, keepdims=True)
        e = jnp.exp(logits - jnp.max(logits, axis=0, keepdims=True))
        o_ref[i] = jnp.sum(e * v_ref[i], axis=0) / jnp.sum(e, axis=0)
        return carry

    lax.fori_loop(0, q_ref.shape[0], body, 0, unroll=True)


def _memattn_decode_specs(item, q_t, k_t, v_t):
    n = q_t.shape[0]
    r = MEMATTN_DEC_ROWS
    n_items = n // r
    tile = (MEM_TILE_ROWS, LANES)

    def at(ndim):
        return lambda *grid_idx: (jnp.minimum(item(*grid_idx), n_items - 1),) + (0,) * (ndim - 1)

    in_specs = [
        pl.BlockSpec((r,) + tile, at(3)),
        pl.BlockSpec((r, N_MEM) + tile, at(4)),
        pl.BlockSpec((r, N_MEM) + tile, at(4)),
    ]
    out_specs = [pl.BlockSpec((r,) + tile, at(3))]
    out_shape = [jax.ShapeDtypeStruct((n,) + tile, F32)]
    return n_items, in_specs, (q_t, k_t, v_t), out_specs, out_shape


def _merge_kernel(a_ref, b_ref, m_ref, wa_ref, wb_ref, wm_ref, za_ref, zb_ref, zm_ref, o_ref):
    ya = _dot(a_ref[...].astype(BF16), wa_ref[...])
    yb = _dot(b_ref[...].astype(BF16), wb_ref[...])
    ym = _dot(m_ref[...].astype(BF16), wm_ref[...])
    gate = lambda z_ref: _sigmoid(z_ref[...].astype(F32))
    merged = gate(za_ref) * ya + gate(zb_ref) * yb + gate(zm_ref) * ym
    o_ref[...] = merged.astype(o_ref.dtype)


def _merge(o_gla, cbu, om, w_gla_out, w_conv_out, w_mem_out, proj_b):
    m = o_gla.shape[0]
    tm = min(1024, m)
    tn = 1024
    d = D_MODEL
    zoff = (3 * CONV_DIM + MEM_DIM) // tn
    zspec = lambda g: pl.BlockSpec((tm, tn), lambda i, j: (i, zoff + g * (d // tn) + j))
    return pl.pallas_call(
        _merge_kernel,
        grid=(m // tm, d // tn),
        in_specs=[
            pl.BlockSpec((tm, o_gla.shape[1]), lambda i, j: (i, 0)),
            pl.BlockSpec((tm, cbu.shape[1]), lambda i, j: (i, 0)),
            pl.BlockSpec((tm, om.shape[1]), lambda i, j: (i, 0)),
            pl.BlockSpec((w_gla_out.shape[0], tn), lambda i, j: (0, j)),
            pl.BlockSpec((w_conv_out.shape[0], tn), lambda i, j: (0, j)),
            pl.BlockSpec((w_mem_out.shape[0], tn), lambda i, j: (0, j)),
            zspec(0), zspec(1), zspec(2),
        ],
        out_specs=pl.BlockSpec((tm, tn), lambda i, j: (i, j)),
        out_shape=jax.ShapeDtypeStruct((m, d), BF16),
        compiler_params=_params("parallel", "parallel"),
        name="merge",
    )(o_gla, cbu, om, w_gla_out, w_conv_out, w_mem_out, proj_b, proj_b, proj_b)


def _wo_ln_kernel(x_ref, m_ref, w_ref, g_ref, b_ref, o_ref):
    tm = x_ref.shape[0]
    half = tm // 4 if tm % 512 == 0 else tm
    for r in range(0, tm, half):
        rows = pl.ds(r, half)
        y = DN_ALPHA * x_ref[rows, :] + _dot(m_ref[rows, :], w_ref[...])
        o_ref[rows, :] = _layer_norm(y, g_ref[...], b_ref[...])


def _wo_ln(x, merged, w_o, g, b):
    m, d = x.shape
    tm = min(512, m)
    return pl.pallas_call(
        _wo_ln_kernel,
        grid=(m // tm,),
        in_specs=[
            pl.BlockSpec((tm, d), lambda i: (i, 0)),
            pl.BlockSpec((tm, d), lambda i: (i, 0)),
            pl.BlockSpec((d, d), lambda i: (0, 0)),
            pl.BlockSpec((1, d), lambda i: (0, 0)),
            pl.BlockSpec((1, d), lambda i: (0, 0)),
        ],
        out_specs=pl.BlockSpec((tm, d), lambda i: (i, 0)),
        out_shape=jax.ShapeDtypeStruct((m, d), F32),
        compiler_params=_params("parallel"),
        name="wo_ln",
    )(x, merged, w_o, g, b)


def _ffn_act(hc, hu, valid, c0=0):
    h = hc * _sigmoid(hc) * hu
    return jnp.where(c0 + lax.broadcasted_iota(jnp.int32, (1, hc.shape[1]), 1) < valid, h, 0.0).astype(BF16)


def _ffn_down(h, valid, wd_ref, acc_ref, g_ref, b_ref, o_ref, fi):
    wd = jnp.where(lax.broadcasted_iota(jnp.int32, (h.shape[1], 1), 0) < valid, wd_ref[...], 0.0)
    acc_ref[...] += _dot(h, wd)

    @pl.when(fi == pl.num_programs(1) - 1)
    def _():
        o_ref[...] = _layer_norm(acc_ref[...], g_ref[...], b_ref[...])


def _ffn_kernel(seq, x_ref, xh_ref, wg_ref, wu_ref, wd_ref, wc_ref, bc_ref, g_ref, b_ref, *rest):
    dec_in, (o_ref, st_ref, od_ref, so_ref, xb_scr, acc_scr) = rest[:GLA_DEC_INPUTS], rest[GLA_DEC_INPUTS:]
    fi = pl.program_id(1)
    tm = x_ref.shape[0]

    @pl.when(fi == 0)
    def _():
        x = x_ref[...]
        xb_scr[...] = x.astype(BF16)
        acc_scr[...] = DN_ALPHA * x

    _gla_dec_item(*dec_in, od_ref, so_ref)
    xb = xb_scr[...]
    xh = xh_ref[...].astype(BF16)
    tf = wg_ref.shape[1]
    valid = D_FF - fi * tf
    seq_start = (pl.program_id(0) * tm) % seq == 0
    ri = lax.broadcasted_iota(jnp.int32, (tm, 1), 0)
    hs = []
    for c0 in range(0, tf, FFN_GATE_CHUNK):
        cs = slice(c0, c0 + FFN_GATE_CHUNK)
        wg = wg_ref[:, cs]
        hg = _dot(xb, wg)
        st_ref[:, cs] = hg[tm - 2:tm, :]
        halo = jnp.where(seq_start, 0.0, _dot(xh, wg))
        p1, p2 = _shift_rows(hg, halo, ri)
        wc = wc_ref[:, cs]
        hc = wc[0:1, :] * p2 + wc[1:2, :] * p1 + wc[2:3, :] * hg + bc_ref[:, cs]
        hs.append(_ffn_act(hc, _dot(xb, wu_ref[:, cs]), valid, c0))
    _ffn_down(jnp.concatenate(hs, axis=1), valid, wd_ref, acc_scr, g_ref, b_ref, o_ref, fi)


def _ffn_prompt(x1, w_gate, w_up, w_down, w_conv, b_conv, g, b, seq, gla_dec_operands):
    m, d = x1.shape
    tm = min(512, seq)
    tf = FF_TILE
    t8 = tm // HALO_ROWS
    n_items, dec_in_specs, dec_args, dec_out_specs, dec_out_shape = _gla_decode_specs(
        lambda i, f: i * FF_TILES + f, *gla_dec_operands)
    assert n_items <= (m // tm) * FF_TILES, "not enough FFN steps to carry the decode GLA work items"
    y, tile_state, o_dec, s_dec = pl.pallas_call(
        functools.partial(_ffn_kernel, seq),
        grid=(m // tm, FF_TILES),
        in_specs=[
            pl.BlockSpec((tm, d), lambda i, f: (i, 0)),
            pl.BlockSpec((HALO_ROWS, d), lambda i, f: (jnp.maximum(i * t8 - 1, 0), 0)),
            pl.BlockSpec((d, tf), lambda i, f: (0, f)),
            pl.BlockSpec((d, tf), lambda i, f: (0, f)),
            pl.BlockSpec((tf, d), lambda i, f: (f, 0)),
            pl.BlockSpec((3, tf), lambda i, f: (0, f)),
            pl.BlockSpec((1, tf), lambda i, f: (0, f)),
            pl.BlockSpec((1, d), lambda i, f: (0, 0)),
            pl.BlockSpec((1, d), lambda i, f: (0, 0)),
        ] + dec_in_specs,
        out_specs=[pl.BlockSpec((tm, d), lambda i, f: (i, 0)),
                   pl.BlockSpec((None, 2, tf), lambda i, f: (i, 0, f))] + dec_out_specs,
        out_shape=[jax.ShapeDtypeStruct((m, d), F32),
                   jax.ShapeDtypeStruct((m // tm, 2, D_FF), F32)] + dec_out_shape,
        scratch_shapes=[pltpu.VMEM((tm, d), BF16), pltpu.VMEM((tm, d), F32)],
        compiler_params=_params("arbitrary", "arbitrary"),
        name="ffn_prompt_gla_decode",
    )(x1, x1, w_gate, w_up, w_down, w_conv, b_conv, g, b, *dec_args)
    tiles_per_seq = seq // tm
    return y, tile_state[tiles_per_seq - 1::tiles_per_seq], o_dec, s_dec


def _ffn_dec_kernel(x_ref, s0_ref, s1_ref, wg_ref, wu_ref, wd_ref, wc_ref, bc_ref, g_ref, b_ref,
                    o_ref, hg_ref, xb_scr, acc_scr):
    fi = pl.program_id(1)

    @pl.when(fi == 0)
    def _():
        x = x_ref[...]
        xb_scr[...] = x.astype(BF16)
        acc_scr[...] = DN_ALPHA * x

    xb = xb_scr[...]
    hg = _dot(xb, wg_ref[...])
    hg_ref[...] = hg
    wc = wc_ref[...]
    hc = wc[0:1, :] * s0_ref[...] + wc[1:2, :] * s1_ref[...] + wc[2:3, :] * hg + bc_ref[...]
    valid = D_FF - fi * hc.shape[1]
    _ffn_down(_ffn_act(hc, _dot(xb, wu_ref[...]), valid), valid, wd_ref, acc_scr, g_ref, b_ref, o_ref, fi)


def _ffn_decode(x1, s0, s1, w_gate, w_up, w_down, w_conv, b_conv, g, b):
    m, d = x1.shape
    tf = FF_TILE
    return pl.pallas_call(
        _ffn_dec_kernel,
        grid=(1, FF_TILES),
        in_specs=[
            pl.BlockSpec((m, d), lambda i, f: (0, 0)),
            pl.BlockSpec((m, tf), lambda i, f: (0, f)),
            pl.BlockSpec((m, tf), lambda i, f: (0, f)),
            pl.BlockSpec((d, tf), lambda i, f: (0, f)),
            pl.BlockSpec((d, tf), lambda i, f: (0, f)),
            pl.BlockSpec((tf, d), lambda i, f: (f, 0)),
            pl.BlockSpec((3, tf), lambda i, f: (0, f)),
            pl.BlockSpec((1, tf), lambda i, f: (0, f)),
            pl.BlockSpec((1, d), lambda i, f: (0, 0)),
            pl.BlockSpec((1, d), lambda i, f: (0, 0)),
        ],
        out_specs=[pl.BlockSpec((m, d), lambda i, f: (0, 0)), pl.BlockSpec((m, tf), lambda i, f: (0, f))],
        out_shape=[jax.ShapeDtypeStruct((m, d), F32), jax.ShapeDtypeStruct((m, D_FF), F32)],
        scratch_shapes=[pltpu.VMEM((m, d), BF16), pltpu.VMEM((m, d), F32)],
        compiler_params=_params("parallel", "arbitrary"),
        name="ffn_decode",
    )(x1, s0, s1, w_gate, w_up, w_down, w_conv, b_conv, g, b)


def kernel(x_prompt, x_sample, mem_prompt, cache_mem_k, cache_mem_v, state_gla, state_conv, state_ffn_conv, w_in, w_gla_a2, b_gla_a2, g_gla_norm, w_gla_out, w_conv, w_conv_out, w_mem_k, w_mem_v, w_mem_out, w_o, ln1_g, ln1_b, w_ffn_gate, w_ffn_up, w_ffn_conv, b_ffn_conv, w_ffn_down, ln2_g, ln2_b):
    nb, seq, d = x_prompt.shape
    ns = x_sample.shape[0]

    w_in_t = jnp.swapaxes(w_in[0], 0, 1)
    wa2p = jnp.pad(w_gla_a2[0], ((0, LANES - GLA_RANK), (0, 0)))
    ba2 = b_gla_a2
    gn = g_gla_norm
    w_fc = w_ffn_conv[0]
    b_fc = b_ffn_conv
    w_cv = w_conv[0]

    def mix_in(x2d, tag, dtype, riders_b=(), riders_a=()):
        pr, xb = _proj(x2d, w_in_t, "proj_r_" + tag, COLS_A, LANES, LANES, transposed=True, emit_x=True)
        pb = _proj(xb, w_in_t, "proj_b_" + tag, COLS_A, COLS_B, shift=GLA_RANK, transposed=True,
                   riders=riders_b, out_dtype=dtype)
        pa = _proj(xb, w_in_t, "proj_a_" + tag, 0, COLS_A, transposed=True, riders=riders_a, out_dtype=dtype)
        return pa, pr, pb

    xs = x_sample.reshape(ns, d)
    sa, sr, sb = mix_in(xs, "s", F32)
    mq_s = sb[:, 3 * CONV_DIM:3 * CONV_DIM + MEM_DIM].reshape(ns, MEM_HEADS, MEM_HD)
    memattn_dec = (_memattn_dec_item, _memattn_decode_specs,
                   (_heads_to_tile(mq_s), _heads_to_tile(cache_mem_k[0]), _heads_to_tile(cache_mem_v[0])))

    xp = x_prompt.reshape(nb * seq, d)
    memb = mem_prompt.reshape(nb * N_MEM, d).astype(BF16)
    mk = _proj(memb, w_mem_k[0], "mem_k", tn=MEM_HD)
    mv = _proj(memb, w_mem_v[0], "mem_v", tn=MEM_HD)
    steps_a, steps_b = _proj_steps(nb * seq, COLS_A), _proj_steps(nb * seq, COLS_B)
    riders_b = (memattn_dec, _cast_rider(w_ffn_gate[0], steps_b), _cast_rider(w_ffn_up[0], steps_b))
    riders_a = tuple(_cast_rider(w[0], steps_a) for w in (w_ffn_down, w_gla_out, w_conv_out, w_mem_out, w_o))
    (pa, w_fd, w_go, w_co, w_mo, w_oo), pr, (pb, om_s, w_fg, w_fu) = mix_in(xp, "p", BF16, riders_b, riders_a)

    def mix_out(x2d, o_gla, cbu, om, proj_b):
        merged = _merge(o_gla, cbu, om, w_go, w_co, w_mo, proj_b)
        return _wo_ln(x2d, merged, w_oo, ln1_g, ln1_b)

    o_gla, p_gla = _gla_prompt(pa, pr, wa2p, ba2, gn, nb, seq)
    cbu, p_conv = _conv_prompt(pb, w_cv, seq)
    om = _memattn_prompt(pb, mk, mv, nb, seq)
    x1 = mix_out(xp, o_gla, cbu, om, pb)
    yp, p_ffn, o_gla_s, s_gla = _ffn_prompt(x1, w_fg, w_fu, w_fd, w_fc, b_fc, ln2_g, ln2_b, seq,
                                            (sa, sr, wa2p, ba2, gn, state_gla[0]))
    o_gla_s = o_gla_s.reshape(ns, GLA_HEADS * GLA_HV)

    cbu_s, s_conv = _conv_decode(sb, w_cv, state_conv[0].reshape(ns, 2 * CONV_DIM))
    om_s = _tile_to_heads(om_s).reshape(ns, MEM_DIM)
    x1s = mix_out(xs, o_gla_s, cbu_s, om_s, sb)
    f0 = state_ffn_conv[0][:, 0, :]
    f1 = state_ffn_conv[0][:, 1, :]
    ys, hg_s = _ffn_decode(x1s, f0, f1, w_fg, w_fu, w_fd, w_fc, b_fc, ln2_g, ln2_b)
    s_ffn = jnp.stack([f1, hg_s], axis=1)

    return (yp.reshape(nb, seq, d), ys.reshape(ns, 1, d),
            mk.reshape(1, nb, N_MEM, MEM_HEADS, MEM_HD), mv.reshape(1, nb, N_MEM, MEM_HEADS, MEM_HD),
            p_gla[None], p_conv[None], p_ffn.reshape(1, nb, 2, D_FF),
            s_gla[None], s_conv.reshape(1, ns, 2, CONV_DIM), s_ffn[None])
```
